```python
import math
import jax
import jax.numpy as jnp
from jax import lax
import numpy as np

D_MODEL = 2048
BATCH = 4
SEQ = 2048
DEPTH = 1
DEC_BATCH = 128
DEC_SEQ = 8
PAST_LEN = 16384
PAGE_SIZE = 128

HA_HEADS = 8
HA_DK = 128
HA_DV = 128
HA_KDIM = HA_HEADS * HA_DK
HA_VDIM = HA_HEADS * HA_DV

GD_KHEADS = 4
GD_VHEADS = 8
GD_DK = 128
GD_DV = 128
GD_KDIM = GD_KHEADS * GD_DK
GD_VDIM = GD_VHEADS * GD_DV
CONV_K = 4
CONV_DIM = 2 * GD_KDIM + GD_VDIM

CHUNK = 64

N_GROUPS = 4
EXP_PER_GROUP = 8
N_EXPERTS = N_GROUPS * EXP_PER_GROUP
TOP_K = 2
D_EXPERT = 512
MOE_BLOCK = 128

EPS = 1e-6

OFF_HA_F = HA_KDIM
OFF_HA_I = OFF_HA_F + HA_KDIM
OFF_HA_G = OFF_HA_I + HA_VDIM
OFF_GD_QKV = OFF_HA_G + HA_VDIM
OFF_GD_Z = OFF_GD_QKV + CONV_DIM
OFF_GD_B = OFF_GD_Z + GD_VDIM
OFF_GD_A = OFF_GD_B + GD_VHEADS
OFF_GATE_A = OFF_GD_A + GD_VHEADS
OFF_GATE_B = OFF_GATE_A + D_MODEL
D_IN = OFF_GATE_B + D_MODEL

kernel_name = 'hgrn2_gdn_parallel_hier_moe_step'


def rms_norm(x, w):
    xf = x.astype(jnp.float32)
    y = xf * lax.rsqrt(jnp.mean(xf * xf, axis=-1, keepdims=True) + EPS)
    return (y * w.astype(jnp.float32)).astype(x.dtype)


def l2_normalize(x):
    xf = x.astype(jnp.float32)
    return xf * lax.rsqrt(jnp.sum(xf * xf, axis=-1, keepdims=True) + EPS)


def _chunk(x, c):
    b, l = x.shape[:2]
    n = -(-l // c)
    x = jnp.pad(x, [(0, 0), (0, n * c - l)] + [(0, 0)] * (x.ndim - 2))
    x = x.reshape((b, n, c) + x.shape[2:])
    return jnp.moveaxis(x, (1, 2), (0, 3))


def _unchunk(o, l):
    n, b, h, c, dv = o.shape
    o = jnp.moveaxis(o, (0, 3), (1, 2)).reshape(b, n * c, h, dv)
    return o[:, :l]


def hgrn2_recurrence(q, k, v, logf, s0):
    l = q.shape[1]
    c = min(CHUNK, l)
    qc, kc, vc, gc = (_chunk(t.astype(jnp.float32), c) for t in (q, k, v, logf))
    causal = jnp.tril(jnp.ones((c, c), dtype=bool))

    def step(s, inp):
        qi, ki, vi, gi = inp
        g = jnp.cumsum(gi, axis=2)
        diff = g[:, :, :, None, :] - g[:, :, None, :, :]
        decay = jnp.exp(jnp.where(causal[:, :, None], diff, -jnp.inf))
        scores = jnp.einsum('bhid,bhjd,bhijd->bhij', qi, ki, decay)
        o = (jnp.einsum('bhij,bhjv->bhiv', scores, vi)
             + jnp.einsum('bhid,bhdv->bhiv', qi * jnp.exp(g), s))
        g_end = g[:, :, -1:, :]
        s = (jnp.exp(g_end[:, :, 0, :, None]) * s
             + jnp.einsum('bhjd,bhjv->bhdv', ki * jnp.exp(g_end - g), vi))
        return s, o

    s, o = lax.scan(step, s0.astype(jnp.float32), (qc, kc, vc, gc))
    return _unchunk(o, l), s


def gated_delta_recurrence(q, k, v, g, beta, s0):
    l = q.shape[1]
    dv = v.shape[-1]
    c = min(CHUNK, l)
    qc, kc, vc, gc, bc = (_chunk(t.astype(jnp.float32), c) for t in (q, k, v, g, beta))
    causal = jnp.tril(jnp.ones((c, c), dtype=bool))
    strict = jnp.tril(jnp.ones((c, c), dtype=bool), -1)
    eye = jnp.eye(c, dtype=jnp.float32)

    def step(s, inp):
        qi, ki, vi, gi, bi = inp
        gcum = jnp.cumsum(gi, axis=-1)
        decay = jnp.exp(jnp.where(causal, gcum[..., :, None] - gcum[..., None, :], -jnp.inf))
        kk = jnp.einsum('bhid,bhjd->bhij', ki, ki)
        m = jnp.where(strict, bi[..., :, None] * kk * decay, 0.0) + eye
        rhs = jnp.concatenate([vi * bi[..., None], ki * (bi * jnp.exp(gcum))[..., None]], axis=-1)
        sol = lax.linalg.triangular_solve(m, rhs, left_side=True, lower=True, unit_diagonal=True)
        u = sol[..., :dv] - jnp.einsum('bhik,bhkv->bhiv', sol[..., dv:], s)
        qk = jnp.einsum('bhid,bhjd->bhij', qi, ki) * decay
        o = (jnp.einsum('bhid,bhdv->bhiv', qi * jnp.exp(gcum)[..., None], s)
             + jnp.einsum('bhij,bhjv->bhiv', qk, u))
        s = (jnp.exp(gcum[..., -1])[..., None, None] * s
             + jnp.einsum('bhjd,bhjv->bhdv', ki * jnp.exp(gcum[..., -1:] - gcum)[..., None], u))
        return s, o

    s, o = lax.scan(step, s0.astype(jnp.float32), (qc, kc, vc, gc, bc))
    return _unchunk(o, l), s


def hybrid_mixer(xn, s_h, s_g, s_c, lb, w_in, conv_w, ha_onorm, w_pa,
                 gd_A_log, gd_dt_bias, gd_onorm, w_pb, w_out):
    b, l, _ = xn.shape
    dt = xn.dtype
    proj = jnp.einsum('bld,de->ble', xn, w_in)
    (ha_q, ha_f, ha_i, ha_g, gd_qkv, gd_z, gd_b, gd_a, gate_a, gate_b) = jnp.split(
        proj, [OFF_HA_F, OFF_HA_I, OFF_HA_G, OFF_GD_QKV, OFF_GD_Z, OFF_GD_B, OFF_GD_A,
               OFF_GATE_A, OFF_GATE_B], axis=-1)

    q_a = ha_q.reshape(b, l, HA_HEADS, HA_DK) * (HA_DK ** -0.5)
    lb_h = lb.reshape(HA_HEADS, HA_DK)
    f = lb_h + (1.0 - lb_h) * jax.nn.sigmoid(ha_f.astype(jnp.float32).reshape(b, l, HA_HEADS, HA_DK))
    o_a, s_h_new = hgrn2_recurrence(q_a, 1.0 - f, ha_i.reshape(b, l, HA_HEADS, HA_DV), jnp.log(f), s_h)
    o_a = rms_norm(o_a, ha_onorm) * jax.nn.silu(ha_g.astype(jnp.float32).reshape(b, l, HA_HEADS, HA_DV))
    y_a = jnp.einsum('blv,vd->bld', o_a.reshape(b, l, HA_VDIM).astype(dt), w_pa)

    xx = jnp.concatenate([s_c.astype(dt), gd_qkv], axis=1)
    conv = sum(xx[:, j:j + l] * conv_w[j] for j in range(CONV_K))
    s_c_new = xx[:, xx.shape[1] - (CONV_K - 1):]
    conv = jax.nn.silu(conv)
    q_b, k_b, v_b = jnp.split(conv, [GD_KDIM, 2 * GD_KDIM], axis=-1)
    rep = GD_VHEADS // GD_KHEADS
    q_b = jnp.repeat(l2_normalize(q_b.reshape(b, l, GD_KHEADS, GD_DK)) * (GD_DK ** -0.5), rep, axis=2)
    k_b = jnp.repeat(l2_normalize(k_b.reshape(b, l, GD_KHEADS, GD_DK)), rep, axis=2)
    v_b = v_b.reshape(b, l, GD_VHEADS, GD_DV)
    beta = jax.nn.sigmoid(gd_b.astype(jnp.float32))
    g = -jnp.exp(gd_A_log.astype(jnp.float32)) * jax.nn.softplus(
        gd_a.astype(jnp.float32) + gd_dt_bias.astype(jnp.float32))
    o_b, s_g_new = gated_delta_recurrence(q_b, k_b, v_b, g, beta, s_g)
    o_b = rms_norm(o_b, gd_onorm) * jax.nn.silu(gd_z.astype(jnp.float32).reshape(b, l, GD_VHEADS, GD_DV))
    y_b = jnp.einsum('blv,vd->bld', o_b.reshape(b, l, GD_VDIM).astype(dt), w_pb)

    merged = (jax.nn.sigmoid(gate_a.astype(jnp.float32)) * y_a.astype(jnp.float32)
              + jax.nn.sigmoid(gate_b.astype(jnp.float32)) * y_b.astype(jnp.float32))
    out = jnp.einsum('bld,de->ble', merged.astype(dt), w_out)
    return out, s_h_new.astype(s_h.dtype), s_g_new.astype(s_g.dtype), s_c_new.astype(s_c.dtype)


def hierarchical_moe(x, w_rg, b_rg, w_re, b_re, w_eg, w_eu, w_ed):
    t, d = x.shape
    g_logits = jnp.dot(x, w_rg).astype(jnp.float32) + b_rg.astype(jnp.float32)
    g_idx = jnp.argmax(g_logits, axis=-1)
    g_w = jnp.take_along_axis(jax.nn.softmax(g_logits, axis=-1), g_idx[:, None], axis=-1)
    e_logits = (jnp.dot(x, w_re).astype(jnp.float32) + b_re.astype(jnp.float32)).reshape(
        t, N_GROUPS, EXP_PER_GROUP)
    e_logits = jnp.take_along_axis(e_logits, g_idx[:, None, None], axis=1)[:, 0]
    top_v, top_i = lax.top_k(e_logits, TOP_K)
    gate = g_w * jax.nn.softmax(top_v, axis=-1)
    eid = (g_idx[:, None] * EXP_PER_GROUP + top_i).reshape(-1)

    a = t * TOP_K
    order = jnp.argsort(eid)
    e_sorted = eid[order]
    tok_sorted = order // TOP_K
    w_sorted = gate.reshape(-1)[order]
    counts = jnp.bincount(eid, length=N_EXPERTS)
    padded = ((counts + MOE_BLOCK - 1) // MOE_BLOCK) * MOE_BLOCK
    start = jnp.cumsum(counts) - counts
    pad_end = jnp.cumsum(padded)
    pad_start = pad_end - padded
    dest = pad_start[e_sorted] + jnp.arange(a) - start[e_sorted]
    n_blocks = -(-a // MOE_BLOCK) + N_EXPERTS
    buf_tok = jnp.full((n_blocks * MOE_BLOCK,), t, dtype=jnp.int32).at[dest].set(tok_sorted.astype(jnp.int32))
    x_pad = jnp.concatenate([x, jnp.zeros((1, d), x.dtype)], axis=0)
    xb = x_pad[buf_tok].reshape(n_blocks, MOE_BLOCK, d)
    block_e = jnp.minimum(jnp.searchsorted(pad_end, jnp.arange(n_blocks) * MOE_BLOCK, side='right'),
                          N_EXPERTS - 1)

    def expert_block(args):
        xblk, e = args
        hid = jax.nn.silu(jnp.dot(xblk, w_eg[e])) * jnp.dot(xblk, w_eu[e])
        return jnp.dot(hid, w_ed[e])

    yb = lax.map(expert_block, (xb, block_e)).reshape(n_blocks * MOE_BLOCK, d)
    y_assign = yb[dest] * w_sorted[:, None].astype(x.dtype)
    return jax.ops.segment_sum(y_assign, tok_sorted, num_segments=t)


def decoder_layer(x, s_h, s_g, s_c, lb, norm_mix, w_in, conv_w, ha_onorm, w_pa,
                  gd_A_log, gd_dt_bias, gd_onorm, w_pb, w_out, norm_ffn,
                  w_rg, b_rg, w_re, b_re, w_eg, w_eu, w_ed):
    b, l, d = x.shape
    mix, s_h, s_g, s_c = hybrid_mixer(rms_norm(x, norm_mix), s_h, s_g, s_c, lb, w_in, conv_w, ha_onorm,
                                      w_pa, gd_A_log, gd_dt_bias, gd_onorm, w_pb, w_out)
    h = x + mix
    ffn = hierarchical_moe(rms_norm(h, norm_ffn).reshape(b * l, d), w_rg, b_rg, w_re, b_re,
                           w_eg, w_eu, w_ed).reshape(b, l, d)
    return h + ffn, s_h, s_g, s_c


def setup_inputs(seed: int = 0) -> dict:
    key = jax.random.key(seed)
    ks = jax.random.split(key, 28)
    f32 = jnp.float32

    def nrm(k, shape, scale):
        return jax.random.normal(k, shape, f32) * scale

    def gain(k, shape):
        return 1.0 + 0.02 * jax.random.normal(k, shape, f32)

    dt = jnp.exp(jax.random.uniform(ks[12], (DEPTH, GD_VHEADS), f32, math.log(1e-3), math.log(1e-1)))
    return {
        'x_prompt': nrm(ks[0], (BATCH, SEQ, D_MODEL), 1.0),
        'x_sample': nrm(ks[1], (DEC_BATCH, DEC_SEQ, D_MODEL), 1.0),
        'state_hgrn': nrm(ks[2], (DEPTH, DEC_BATCH, HA_HEADS, HA_DK, HA_DV), 0.5),
        'state_gdn': nrm(ks[3], (DEPTH, DEC_BATCH, GD_VHEADS, GD_DK, GD_DV), 0.1),
        'state_conv': nrm(ks[4], (DEPTH, DEC_BATCH, CONV_K - 1, CONV_DIM), 1.0),
        'norm_mix': gain(ks[5], (DEPTH, D_MODEL)),
        'w_in': nrm(ks[6], (DEPTH, D_MODEL, D_IN), D_MODEL ** -0.5),
        'conv_w': nrm(ks[7], (DEPTH, CONV_K, CONV_DIM), CONV_K ** -0.5),
        'ha_lb_logits': nrm(ks[8], (DEPTH + 1, HA_KDIM), 0.5),
        'ha_onorm': gain(ks[9], (DEPTH, HA_DV)),
        'w_pa': nrm(ks[10], (DEPTH, HA_VDIM, D_MODEL), HA_VDIM ** -0.5),
        'gd_A_log': jnp.log(jax.random.uniform(ks[11], (DEPTH, GD_VHEADS), f32, 1.0, 16.0)),
        'gd_dt_bias': dt + jnp.log(-jnp.expm1(-dt)),
        'gd_onorm': gain(ks[13], (DEPTH, GD_DV)),
        'w_pb': nrm(ks[14], (DEPTH, GD_VDIM, D_MODEL), GD_VDIM ** -0.5),
        'w_out': nrm(ks[15], (DEPTH, D_MODEL, D_MODEL), D_MODEL ** -0.5),
        'norm_ffn': gain(ks[16], (DEPTH, D_MODEL)),
        'w_router_group': nrm(ks[17], (DEPTH, D_MODEL, N_GROUPS), D_MODEL ** -0.5),
        'b_router_group': nrm(ks[18], (DEPTH, N_GROUPS), 0.01),
        'w_router_expert': nrm(ks[19], (DEPTH, D_MODEL, N_EXPERTS), D_MODEL ** -0.5),
        'b_router_expert': nrm(ks[20], (DEPTH, N_EXPERTS), 0.01),
        'w_exp_gate': nrm(ks[21], (DEPTH, N_EXPERTS, D_MODEL, D_EXPERT), D_MODEL ** -0.5),
        'w_exp_up': nrm(ks[22], (DEPTH, N_EXPERTS, D_MODEL, D_EXPERT), D_MODEL ** -0.5),
        'w_exp_down': nrm(ks[23], (DEPTH, N_EXPERTS, D_EXPERT, D_MODEL), D_EXPERT ** -0.5),
        'norm_final': gain(ks[24], (D_MODEL,)),
    }


def reference(x_prompt, x_sample, state_hgrn, state_gdn, state_conv, norm_mix, w_in, conv_w,
              ha_lb_logits, ha_onorm, w_pa, gd_A_log, gd_dt_bias, gd_onorm, w_pb, w_out, norm_ffn,
              w_router_group, b_router_group, w_router_expert, b_router_expert,
              w_exp_gate, w_exp_up, w_exp_down, norm_final):
    lb_all = jnp.cumsum(jax.nn.softmax(ha_lb_logits.astype(jnp.float32), axis=0), axis=0)
    bp = x_prompt.shape[0]
    pdt = x_prompt.dtype
    yp, ys = x_prompt, x_sample
    hp, gp, cp, hs, gs, cs = [], [], [], [], [], []
    for l in range(DEPTH):
        weights = (lb_all[l], norm_mix[l], w_in[l], conv_w[l], ha_onorm[l], w_pa[l], gd_A_log[l],
                   gd_dt_bias[l], gd_onorm[l], w_pb[l], w_out[l], norm_ffn[l], w_router_group[l],
                   b_router_group[l], w_router_expert[l], b_router_expert[l], w_exp_gate[l],
                   w_exp_up[l], w_exp_down[l])
        yp, h1, g1, c1 = decoder_layer(
            yp, jnp.zeros((bp, HA_HEADS, HA_DK, HA_DV), pdt), jnp.zeros((bp, GD_VHEADS, GD_DK, GD_DV), pdt),
            jnp.zeros((bp, CONV_K - 1, CONV_DIM), pdt), *weights)
        ys, h2, g2, c2 = decoder_layer(ys, state_hgrn[l], state_gdn[l], state_conv[l], *weights)
        hp.append(h1)
        gp.append(g1)
        cp.append(c1)
        hs.append(h2)
        gs.append(g2)
        cs.append(c2)
    y_prompt = rms_norm(yp, norm_final)
    y_sample = rms_norm(ys, norm_final)
    return (y_prompt, y_sample, jnp.stack(hp), jnp.stack(gp), jnp.stack(cp),
            jnp.stack(hs), jnp.stack(gs), jnp.stack(cs))
```

```python
import functools
import math

import jax
import jax.numpy as jnp
from jax import lax
from jax.experimental import pallas as pl
from jax.experimental.pallas import tpu as pltpu

F32 = jnp.float32
BF = jnp.bfloat16
I32 = jnp.int32
EPS = 1e-6
CONV_K = 4
TOP_K = 2
MOE_BLOCK = 128
GDN_CHUNK = 64
HGRN_SUB = 16
LANES = 128
NEG = -3.0e38
HI = lax.Precision.HIGHEST
MIB = 1024 * 1024


def _cp(sem, vmem_mib=48):
    return pltpu.CompilerParams(dimension_semantics=sem, vmem_limit_bytes=vmem_mib * MIB)


def _pick(n, pref, mult=8):
    best = None
    for t in range(mult, min(n, pref) + 1, mult):
        if n % t == 0:
            best = t
    return best if best is not None else n


def _sigmoid(x):
    return 1.0 / (1.0 + jnp.exp(-x))


def _silu(x):
    return x * _sigmoid(x)


def _iota(shape, dim):
    return lax.broadcasted_iota(I32, shape, dim)


def _dot(a, b):
    return jnp.dot(a, b, preferred_element_type=F32)


def _dot_nt(a, b):
    return lax.dot_general(a, b, (((1,), (1,)), ((), ())), preferred_element_type=F32)


def _dot_tn(a, b):
    return lax.dot_general(a, b, (((0,), (0,)), ((), ())), preferred_element_type=F32)


def _split(a):
    hi = a.astype(BF)
    lo = (a - hi.astype(F32)).astype(BF)
    return hi, lo


def _dot3(a, b):
    ah, al = _split(a)
    bh, bl = _split(b)
    return _dot(ah, bh) + (_dot(ah, bl) + _dot(al, bh))


def _two_source_specs(rows_a, rows_b, tm, width):
    na, nb = rows_a // tm, rows_b // tm
    spec_a = pl.BlockSpec((tm, width), lambda i: (jnp.minimum(i, na - 1), 0))
    spec_b = pl.BlockSpec((tm, width), lambda i: (jnp.maximum(i - na, 0), 0))
    return na, nb, spec_a, spec_b


def _rmsnorm_kernel(xa_ref, xb_ref, w_ref, o_ref, *, na):
    i = pl.program_id(0)

    def body(x_ref):
        x = x_ref[...]
        ms = jnp.mean(x * x, axis=-1, keepdims=True)
        o_ref[...] = (x * lax.rsqrt(ms + EPS) * w_ref[...]).astype(o_ref.dtype)

    @pl.when(i < na)
    def _():
        body(xa_ref)

    @pl.when(i >= na)
    def _():
        body(xb_ref)


def _rmsnorm_bf16(xa, xb, w):
    d = xa.shape[1]
    tm = _pick(math.gcd(xa.shape[0], xb.shape[0]), 512)
    na, nb, sa, sb = _two_source_specs(xa.shape[0], xb.shape[0], tm, d)
    return pl.pallas_call(
        functools.partial(_rmsnorm_kernel, na=na),
        grid=(na + nb,),
        in_specs=[sa, sb, pl.BlockSpec((1, d), lambda i: (0, 0))],
        out_specs=pl.BlockSpec((tm, d), lambda i: (i, 0)),
        out_shape=jax.ShapeDtypeStruct((xa.shape[0] + xb.shape[0], d), BF),
        compiler_params=_cp(("arbitrary",)),
        name="rmsnorm_mix",
    )(xa, xb, w.reshape(1, d))


def _proj_kernel(x_ref, w_ref, *rest, n_extra, n_out, epilogue):
    extra = rest[:n_extra]
    outs = rest[n_extra:n_extra + n_out]
    wbf = rest[n_extra + n_out]

    @pl.when(pl.program_id(1) == 0)
    def _():
        wbf[...] = w_ref[...].astype(BF)

    acc = _dot(x_ref[...], wbf[...])
    vals = epilogue(acc, *[e[...] for e in extra])
    for o_ref, v in zip(outs, vals):
        o_ref[...] = v.astype(o_ref.dtype)


def _proj(x_bf, w, col0, ncols, epilogue, out_dtypes, extras=(), name="proj"):
    t, k = x_bf.shape
    tn = _pick(math.gcd(ncols, col0) if col0 else ncols, 512, LANES)
    tm = _pick(t, 1024)
    j0 = col0 // tn
    n_out = len(out_dtypes)
    kern = functools.partial(_proj_kernel, n_extra=len(extras), n_out=n_out, epilogue=epilogue)
    return pl.pallas_call(
        kern,
        grid=(ncols // tn, t // tm),
        in_specs=[pl.BlockSpec((tm, k), lambda j, i: (i, 0)),
                  pl.BlockSpec((k, tn), lambda j, i: (0, j0 + j))]
                 + [pl.BlockSpec((e.shape[0], tn), lambda j, i: (0, j)) for e in extras],
        out_specs=[pl.BlockSpec((tm, tn), lambda j, i: (i, j)) for _ in out_dtypes],
        out_shape=[jax.ShapeDtypeStruct((t, ncols), dt) for dt in out_dtypes],
        scratch_shapes=[pltpu.VMEM((k, tn), BF)],
        compiler_params=_cp(("arbitrary", "arbitrary")),
        name=name,
    )(x_bf, w, *extras)


def _forget_epilogue(acc, lb_logits):
    m = jnp.max(lb_logits, axis=0, keepdims=True)
    e = jnp.exp(lb_logits - m)
    lb = e[0:1, :] / jnp.sum(e, axis=0, keepdims=True)
    f = lb + (1.0 - lb) * _sigmoid(acc)
    return jnp.log(f), 1.0 - f


def _beta_decay_kernel(x_ref, w_ref, wt_ref, prow_ref, pcol_ref, o_ref, ot_ref, *, vh):
    x = x_ref[...]
    acc = _dot(x, w_ref[...])
    acct = _dot_nt(wt_ref[...], x)

    def act(a, is_beta, a_neg_exp, dt_bias):
        z = a + dt_bias
        softplus = jnp.maximum(z, 0.0) + jnp.log(1.0 + jnp.exp(-jnp.abs(z)))
        return jnp.where(is_beta, _sigmoid(a), a_neg_exp * softplus)

    prow = prow_ref[...]
    pcol = pcol_ref[...]
    o_ref[...] = act(acc, _iota(acc.shape, 1) < vh, prow[0:1, :], prow[1:2, :])
    ot_ref[...] = act(acct, _iota(acct.shape, 0) < vh, pcol[:, 0:1], pcol[:, 1:2])


def _beta_decay(x_bf, w_ba, a_log, dt_bias):
    t, k = x_bf.shape
    vh = a_log.shape[0]
    tm = t if t <= 2048 else _pick(t, 1024, LANES)
    zeros = jnp.zeros((vh,), F32)
    prow = jnp.stack([jnp.concatenate([zeros, -jnp.exp(a_log)]), jnp.concatenate([zeros, dt_bias])])
    return pl.pallas_call(
        functools.partial(_beta_decay_kernel, vh=vh),
        grid=(t // tm,),
        in_specs=[pl.BlockSpec((tm, k), lambda i: (i, 0)),
                  pl.BlockSpec((k, 2 * vh), lambda i: (0, 0)),
                  pl.BlockSpec((2 * vh, k), lambda i: (0, 0)),
                  pl.BlockSpec((2, 2 * vh), lambda i: (0, 0)),
                  pl.BlockSpec((2 * vh, 2), lambda i: (0, 0))],
        out_specs=[pl.BlockSpec((tm, 2 * vh), lambda i: (i, 0)),
                   pl.BlockSpec((2 * vh, tm), lambda i: (0, i))],
        out_shape=[jax.ShapeDtypeStruct((t, 2 * vh), F32), jax.ShapeDtypeStruct((2 * vh, t), F32)],
        compiler_params=_cp(("arbitrary",)),
        name="proj_beta_decay",
    )(x_bf, w_ba.astype(BF), w_ba.T.astype(BF), prow, prow.T)


def _hgrn2_kernel(q_ref, lf_ref, k_ref, v_ref, g_ref, on_ref, s0_ref, o_ref, so_ref, s_scr,
                  *, heads, dk, dv, sub, nsub, nchunks):
    c = pl.program_id(1)

    @pl.when(c == 0)
    def _():
        s_scr[...] = s0_ref[0]

    r_i = _iota((sub, sub), 0)
    c_i = _iota((sub, sub), 1)
    tril = (r_i >= c_i).astype(F32)
    row = _iota((sub, 1), 0)
    onorm = on_ref[...]

    def block(sb, carry):
        r0 = pl.multiple_of(sb * sub, sub)
        rows = pl.ds(r0, sub)
        gcum_all = jnp.dot(tril, lf_ref[rows, :], precision=HI, preferred_element_type=F32)
        for h in range(heads):
            kc = pl.ds(h * dk, dk)
            vc = pl.ds(h * dv, dv)
            gc = gcum_all[:, h * dk:(h + 1) * dk]
            q = q_ref[rows, kc]
            k = k_ref[rows, kc]
            v = v_ref[rows, vc]
            gend = gc[sub - 1:sub, :]
            s_old = s_scr[h]
            o = _dot((q * jnp.exp(gc)).astype(BF), s_old.astype(BF))
            for j in range(sub):
                p = (q * k[j:j + 1, :]) * jnp.exp(gc - gc[j:j + 1, :])
                a_col = jnp.sum(p, axis=-1, keepdims=True)
                a_col = jnp.where(row >= j, a_col, 0.0)
                o = o + a_col * v[j:j + 1, :]
            kdec = (k * jnp.exp(gend - gc)).astype(BF)
            kv = _dot_tn(kdec, v.astype(BF))
            dmat = jnp.transpose(jnp.broadcast_to(jnp.exp(gend), (dk, dk)))
            s_scr[h] = (dmat if dk == dv else dmat[:, 0:1]) * s_old + kv
            ms = jnp.mean(o * o, axis=-1, keepdims=True)
            gate = g_ref[rows, vc]
            o_ref[rows, vc] = (o * lax.rsqrt(ms + EPS) * onorm) * _silu(gate)
        return carry

    lax.fori_loop(0, nsub, block, 0)

    @pl.when(c == nchunks - 1)
    def _():
        so_ref[0] = s_scr[...]


def _hgrn2(q, lf, kf, v, gate, onorm, s0, *, row0, seq_len, rows_per_step, sub):
    nseq, heads, dk, dv = s0.shape
    width = q.shape[1]
    cg = rows_per_step
    nchunks = seq_len // cg
    b0 = row0 // cg
    row_spec = pl.BlockSpec((cg, width), lambda b, c: (b0 + b * nchunks + c, 0))
    st_spec = pl.BlockSpec((1, heads, dk, dv), lambda b, c: (b, 0, 0, 0))
    kern = functools.partial(_hgrn2_kernel, heads=heads, dk=dk, dv=dv, sub=sub, nsub=cg // sub,
                             nchunks=nchunks)
    return pl.pallas_call(
        kern,
        grid=(nseq, nchunks),
        in_specs=[row_spec, row_spec, row_spec, row_spec, row_spec,
                  pl.BlockSpec((1, dv), lambda b, c: (0, 0)), st_spec],
        out_specs=[pl.BlockSpec((cg, heads * dv), lambda b, c: (b * nchunks + c, 0)), st_spec],
        out_shape=[jax.ShapeDtypeStruct((nseq * seq_len, heads * dv), F32),
                   jax.ShapeDtypeStruct(s0.shape, F32)],
        scratch_shapes=[pltpu.VMEM((heads, dk, dv), F32)],
        compiler_params=_cp(("arbitrary", "arbitrary")),
        name="hgrn2_recurrence",
    )(q, lf, kf, v, gate, onorm.reshape(1, dv), s0)


def _gdn_kernel(qkv_ref, z_ref, bg_ref, bgt_ref, cw_ref, on_ref, s0_ref, c0_ref,
                o_ref, so_ref, co_ref, s_scr, xx,
                *, kheads, vheads, dk, dv, chunk, nsub, nchunks):
    c = pl.program_id(1)
    cg = chunk * nsub
    kdim = kheads * dk
    rep = vheads // kheads

    @pl.when(c == 0)
    def _():
        s_scr[...] = s0_ref[0]
        xx[0:8, :] = c0_ref[0]

    xx[8:8 + cg, :] = qkv_ref[...]
    conv = xx[pl.ds(8 - (CONV_K - 1), cg), :] * cw_ref[0:1, :]
    for j in range(1, CONV_K):
        conv = conv + xx[pl.ds(8 - (CONV_K - 1) + j, cg), :] * cw_ref[j:j + 1, :]
    xs = _silu(conv)

    @pl.when(c == nchunks - 1)
    def _():
        co_ref[0] = xx[pl.ds(8 + cg - (CONV_K - 1), CONV_K - 1), :]

    xx[0:8, :] = xx[cg:cg + 8, :]

    r_i = _iota((chunk, chunk), 0)
    c_i = _iota((chunk, chunk), 1)
    causal = r_i >= c_i
    strict = r_i > c_i
    tril = causal.astype(F32)
    triu = (r_i <= c_i).astype(F32)
    eye = (r_i == c_i).astype(F32)
    onorm = on_ref[...]
    n_sq = max(int(math.log2(chunk)) - 1, 0)

    for s in range(nsub):
        r0 = s * chunk
        bg = bg_ref[r0:r0 + chunk, :]
        gt = bgt_ref[0, vheads:2 * vheads, r0:r0 + chunk]
        gcol_all = jnp.dot(tril, bg[:, vheads:2 * vheads], precision=HI, preferred_element_type=F32)
        grow_all = jnp.dot(gt, triu, precision=HI, preferred_element_type=F32)
        for kh in range(kheads):
            qh = xs[r0:r0 + chunk, kh * dk:(kh + 1) * dk]
            kk_ = xs[r0:r0 + chunk, kdim + kh * dk:kdim + (kh + 1) * dk]
            qn = qh * lax.rsqrt(jnp.sum(qh * qh, axis=-1, keepdims=True) + EPS) * (dk ** -0.5)
            kn = kk_ * lax.rsqrt(jnp.sum(kk_ * kk_, axis=-1, keepdims=True) + EPS)
            kn_bf = kn.astype(BF)
            kk = _dot_nt(kn_bf, kn_bf)
            qk = _dot_nt(qn.astype(BF), kn_bf)
            for r in range(rep):
                h = kh * rep + r
                vcols = slice(2 * kdim + h * dv, 2 * kdim + (h + 1) * dv)
                vh_ = xs[r0:r0 + chunk, vcols]
                gcol = gcol_all[:, h:h + 1]
                grow = grow_all[h:h + 1, :]
                beta = bg[:, h:h + 1]
                decay = jnp.where(causal, jnp.exp(jnp.minimum(gcol - grow, 0.0)), 0.0)
                x = jnp.where(strict, -(beta * kk * decay), 0.0)
                inv = eye + x
                p = x
                for _ in range(n_sq):
                    p = _dot3(p, p)
                    inv = inv + _dot3(inv, p)
                eg = jnp.exp(gcol)
                sol_v = _dot3(inv, vh_ * beta)
                sol_k = _dot3(inv, kn * (beta * eg))
                s_old = s_scr[h]
                s_bf = s_old.astype(BF)
                u = sol_v - _dot(sol_k.astype(BF), s_bf)
                o = _dot((qn * eg).astype(BF), s_bf) + _dot((qk * decay).astype(BF), u.astype(BF))
                gend = gcol[chunk - 1:chunk, :]
                kdec = (kn * jnp.exp(gend - gcol)).astype(BF)
                s_scr[h] = jnp.exp(gend) * s_old + _dot_tn(kdec, u.astype(BF))
                ms = jnp.mean(o * o, axis=-1, keepdims=True)
                zc = slice(h * dv, (h + 1) * dv)
                o_ref[r0:r0 + chunk, zc] = (o * lax.rsqrt(ms + EPS) * onorm) * _silu(z_ref[r0:r0 + chunk, zc])

    @pl.when(c == nchunks - 1)
    def _():
        so_ref[0] = s_scr[...]


def _gdn(qkv, z, bg, bgt3, conv_w, onorm, s0, conv0, *, row0, seq_len, rows_per_step, chunk):
    nseq, vheads, dk, dv = s0.shape
    conv_dim = qkv.shape[1]
    kheads = (conv_dim - vheads * dv) // (2 * dk)
    cg = rows_per_step
    nchunks = seq_len // cg
    b0 = row0 // cg
    conv0p = jnp.pad(conv0, ((0, 0), (8 - (CONV_K - 1), 0), (0, 0)))
    rmap = lambda b, c: (b0 + b * nchunks + c, 0)
    st_spec = pl.BlockSpec((1, vheads, dk, dv), lambda b, c: (b, 0, 0, 0))
    kern = functools.partial(_gdn_kernel, kheads=kheads, vheads=vheads, dk=dk, dv=dv, chunk=chunk,
                             nsub=cg // chunk, nchunks=nchunks)
    return pl.pallas_call(
        kern,
        grid=(nseq, nchunks),
        in_specs=[pl.BlockSpec((cg, conv_dim), rmap),
                  pl.BlockSpec((cg, vheads * dv), rmap),
                  pl.BlockSpec((cg, 2 * vheads), rmap),
                  pl.BlockSpec((1, 2 * vheads, cg), lambda b, c: (b0 + b * nchunks + c, 0, 0)),
                  pl.BlockSpec((CONV_K, conv_dim), lambda b, c: (0, 0)),
                  pl.BlockSpec((1, dv), lambda b, c: (0, 0)),
                  st_spec,
                  pl.BlockSpec((1, 8, conv_dim), lambda b, c: (b, 0, 0))],
        out_specs=[pl.BlockSpec((cg, vheads * dv), lambda b, c: (b * nchunks + c, 0)),
                   st_spec,
                   pl.BlockSpec((1, CONV_K - 1, conv_dim), lambda b, c: (b, 0, 0))],
        out_shape=[jax.ShapeDtypeStruct((nseq * seq_len, vheads * dv), F32),
                   jax.ShapeDtypeStruct(s0.shape, F32),
                   jax.ShapeDtypeStruct((nseq, CONV_K - 1, conv_dim), F32)],
        scratch_shapes=[pltpu.VMEM((vheads, dk, dv), F32), pltpu.VMEM((8 + cg, conv_dim), F32)],
        compiler_params=_cp(("arbitrary", "arbitrary")),
        name="gdn_recurrence",
    )(qkv, z, bg, bgt3, conv_w, onorm.reshape(1, dv), s0, conv0p)


def _merge_kernel(oa1_ref, oa2_ref, ob1_ref, ob2_ref, ga_ref, gb_ref, wa_ref, wb_ref, o_ref, *, na):
    i = pl.program_id(1)

    def body(oa_ref, ob_ref):
        ya = _dot(oa_ref[...].astype(BF), wa_ref[...])
        yb = _dot(ob_ref[...].astype(BF), wb_ref[...])
        o_ref[...] = (_sigmoid(ga_ref[...]) * ya + _sigmoid(gb_ref[...]) * yb).astype(o_ref.dtype)

    @pl.when(i < na)
    def _():
        body(oa1_ref, ob1_ref)

    @pl.when(i >= na)
    def _():
        body(oa2_ref, ob2_ref)


def _merge(oa_p, oa_s, ob_p, ob_s, gates, w_pa, w_pb):
    rp, rs = oa_p.shape[0], oa_s.shape[0]
    ka, kb = oa_p.shape[1], ob_p.shape[1]
    d = w_pa.shape[1]
    tm = _pick(math.gcd(rp, rs), 512)
    tn = _pick(d, 512, LANES)
    na, nb = rp // tm, rs // tm
    nj = d // tn
    amap = lambda j, i: (jnp.minimum(i, na - 1), 0)
    bmap = lambda j, i: (jnp.maximum(i - na, 0), 0)
    return pl.pallas_call(
        functools.partial(_merge_kernel, na=na),
        grid=(nj, na + nb),
        in_specs=[pl.BlockSpec((tm, ka), amap), pl.BlockSpec((tm, ka), bmap),
                  pl.BlockSpec((tm, kb), amap), pl.BlockSpec((tm, kb), bmap),
                  pl.BlockSpec((tm, tn), lambda j, i: (i, j)),
                  pl.BlockSpec((tm, tn), lambda j, i: (i, nj + j)),
                  pl.BlockSpec((ka, tn), lambda j, i: (0, j)),
                  pl.BlockSpec((kb, tn), lambda j, i: (0, j))],
        out_specs=pl.BlockSpec((tm, tn), lambda j, i: (i, j)),
        out_shape=jax.ShapeDtypeStruct((rp + rs, d), BF),
        compiler_params=_cp(("arbitrary", "arbitrary")),
        name="branch_proj_merge",
    )(oa_p, oa_s, ob_p, ob_s, gates, gates, w_pa.astype(BF), w_pb.astype(BF))


def _outproj_kernel(m_ref, xa_ref, xb_ref, wo_ref, nw_ref, wr_ref, h_ref, xf_ref, lg_ref, *, na):
    i = pl.program_id(0)

    def body(x_ref):
        h = x_ref[...] + _dot(m_ref[...], wo_ref[...])
        h_ref[...] = h
        ms = jnp.mean(h * h, axis=-1, keepdims=True)
        xf = (h * lax.rsqrt(ms + EPS) * nw_ref[...]).astype(BF)
        xf_ref[...] = xf
        lg_ref[...] = _dot(xf, wr_ref[...])

    @pl.when(i < na)
    def _():
        body(xa_ref)

    @pl.when(i >= na)
    def _():
        body(xb_ref)


def _outproj(merged, xa, xb, w_out, norm_ffn, w_router):
    t, d = merged.shape
    tm = _pick(math.gcd(xa.shape[0], xb.shape[0]), 512)
    na, nb, sa, sb = _two_source_specs(xa.shape[0], xb.shape[0], tm, d)
    nr = w_router.shape[1]
    const = lambda i: (0, 0)
    return pl.pallas_call(
        functools.partial(_outproj_kernel, na=na),
        grid=(na + nb,),
        in_specs=[pl.BlockSpec((tm, d), lambda i: (i, 0)), sa, sb,
                  pl.BlockSpec((d, d), const, pipeline_mode=pl.Buffered(1)),
                  pl.BlockSpec((1, d), const),
                  pl.BlockSpec((d, nr), const, pipeline_mode=pl.Buffered(1))],
        out_specs=[pl.BlockSpec((tm, d), lambda i: (i, 0)),
                   pl.BlockSpec((tm, d), lambda i: (i, 0)),
                   pl.BlockSpec((tm, nr), lambda i: (i, 0))],
        out_shape=[jax.ShapeDtypeStruct((t, d), F32), jax.ShapeDtypeStruct((t, d), BF),
                   jax.ShapeDtypeStruct((t, nr), F32)],
        compiler_params=_cp(("arbitrary",), 56),
        name="out_proj_ffn_norm",
    )(merged, xa, xb, w_out.astype(BF), norm_ffn.reshape(1, d), w_router)


def _route_kernel(lg_ref, b_ref, eid_ref, gate_ref, rank_ref, cnt_ref, carry,
                  *, ngroups, nexp, tm):
    i = pl.program_id(0)

    @pl.when(i == 0)
    def _():
        carry[...] = jnp.zeros_like(carry)

    per_group = nexp // ngroups
    lg = lg_ref[...] + b_ref[...]
    lane = _iota(lg.shape, 1)
    big = jnp.int32(1 << 20)
    is_g = lane < ngroups
    gl = jnp.where(is_g, lg, NEG)
    gmax = jnp.max(gl, axis=-1, keepdims=True)
    gidx = jnp.min(jnp.where(gl == gmax, lane, big), axis=-1, keepdims=True)
    gsum = jnp.sum(jnp.where(is_g, jnp.exp(gl - gmax), 0.0), axis=-1, keepdims=True)
    gw = 1.0 / gsum
    elane = lane - ngroups
    in_grp = (elane >= gidx * per_group) & (elane < (gidx + 1) * per_group)
    el = jnp.where(in_grp, lg, NEG)
    v1 = jnp.max(el, axis=-1, keepdims=True)
    i1 = jnp.min(jnp.where(in_grp & (el == v1), elane, big), axis=-1, keepdims=True)
    in2 = in_grp & (elane != i1)
    el2 = jnp.where(in2, lg, NEG)
    v2 = jnp.max(el2, axis=-1, keepdims=True)
    i2 = jnp.min(jnp.where(in2 & (el2 == v2), elane, big), axis=-1, keepdims=True)
    p2 = jnp.exp(v2 - v1)
    den = 1.0 + p2
    lane2 = _iota((tm, TOP_K), 1)
    eid_ref[...] = jnp.where(lane2 == 0, i1, i2)
    gate_ref[...] = jnp.where(lane2 == 0, gw / den, gw * p2 / den)

    oh1 = (elane == i1).astype(F32)
    oh2 = (elane == i2).astype(F32)
    lower = (_iota((tm, tm), 0) > _iota((tm, tm), 1)).astype(BF)
    cs1 = _dot(lower, oh1.astype(BF))
    cs2 = _dot(lower, oh2.astype(BF))
    tot1 = jnp.sum(oh1, axis=0, keepdims=True)
    tot2 = jnp.sum(oh2, axis=0, keepdims=True)
    base = carry[0:1, :]
    r1 = jnp.sum(oh1 * (base + cs1), axis=-1, keepdims=True)
    r2 = jnp.sum(oh2 * (base + tot1 + cs2), axis=-1, keepdims=True)
    rank_ref[...] = jnp.where(lane2 == 0, r1, r2).astype(I32)
    new = base + tot1 + tot2
    carry[...] = jnp.broadcast_to(new, carry.shape)
    cnt_ref[...] = jnp.broadcast_to(new, cnt_ref.shape)


def _route(logits, bias_row, ngroups, nexp):
    t, nr = logits.shape
    tm = _pick(t, 512)
    kern = functools.partial(_route_kernel, ngroups=ngroups, nexp=nexp, tm=tm)
    return pl.pallas_call(
        kern,
        grid=(t // tm,),
        in_specs=[pl.BlockSpec((tm, nr), lambda i: (i, 0)), pl.BlockSpec((1, nr), lambda i: (0, 0))],
        out_specs=[pl.BlockSpec((tm, TOP_K), lambda i: (i, 0)),
                   pl.BlockSpec((tm, TOP_K), lambda i: (i, 0)),
                   pl.BlockSpec((tm, TOP_K), lambda i: (i, 0)),
                   pl.BlockSpec((8, nr), lambda i: (0, 0))],
        out_shape=[jax.ShapeDtypeStruct((t, TOP_K), I32), jax.ShapeDtypeStruct((t, TOP_K), F32),
                   jax.ShapeDtypeStruct((t, TOP_K), I32), jax.ShapeDtypeStruct((8, nr), F32)],
        scratch_shapes=[pltpu.VMEM((8, nr), F32)],
        compiler_params=_cp(("arbitrary",)),
        name="route_topk_rank",
    )(logits, bias_row)


def _plan_kernel(cnt_ref, eid_ref, rank_ref, pos_ref, be_ref, *, ngroups, nexp, nblocks_pad, tm):
    nr = cnt_ref.shape[1]
    cnt = cnt_ref[0:1, :]
    padded = jnp.floor((cnt + (MOE_BLOCK - 1)) * (1.0 / MOE_BLOCK)) * MOE_BLOCK
    r_i = _iota((nr, nr), 0)
    c_i = _iota((nr, nr), 1)
    padded_col = jnp.sum(jnp.where(r_i == c_i, jnp.broadcast_to(padded, (nr, nr)), 0.0), axis=1, keepdims=True)
    start = jnp.sum(jnp.where(r_i < c_i, jnp.broadcast_to(padded_col, (nr, nr)), 0.0), axis=0, keepdims=True)
    end = start + padded
    lane = _iota((tm, nr), 1)
    eid = eid_ref[...]
    rank = rank_ref[...]
    lane2 = _iota((tm, TOP_K), 1)
    pos = jnp.zeros((tm, TOP_K), F32)
    for k in range(TOP_K):
        oh = (lane - ngroups) == eid[:, k:k + 1]
        st = jnp.sum(jnp.where(oh, start, 0.0), axis=-1, keepdims=True)
        pos = jnp.where(lane2 == k, st, pos)
    pos_ref[...] = pos.astype(I32) + rank

    @pl.when(pl.program_id(0) == 0)
    def _():
        blk_row = _iota((nblocks_pad, nr), 0).astype(F32) * MOE_BLOCK
        lane_b = _iota((nblocks_pad, nr), 1)
        is_e = (lane_b >= ngroups) & (lane_b < ngroups + nexp)
        n_le = jnp.sum(jnp.where(is_e & (end <= blk_row), 1.0, 0.0), axis=-1, keepdims=True)
        be_ref[...] = jnp.minimum(n_le, nexp - 1.0).astype(I32)


def _plan(cnt, eid, rank, ngroups, nexp, nblocks):
    t = eid.shape[0]
    nr = cnt.shape[1]
    tm = _pick(t, 512)
    nblocks_pad = -(-nblocks // 8) * 8
    kern = functools.partial(_plan_kernel, ngroups=ngroups, nexp=nexp, nblocks_pad=nblocks_pad, tm=tm)
    return pl.pallas_call(
        kern,
        grid=(t // tm,),
        in_specs=[pl.BlockSpec((8, nr), lambda i: (0, 0)),
                  pl.BlockSpec((tm, TOP_K), lambda i: (i, 0)),
                  pl.BlockSpec((tm, TOP_K), lambda i: (i, 0))],
        out_specs=[pl.BlockSpec((tm, TOP_K), lambda i: (i, 0)),
                   pl.BlockSpec((nblocks_pad, 1), lambda i: (0, 0))],
        out_shape=[jax.ShapeDtypeStruct((t, TOP_K), I32), jax.ShapeDtypeStruct((nblocks_pad, 1), I32)],
        compiler_params=_cp(("arbitrary",)),
        name="route_plan",
    )(cnt, eid, rank)


def _expert_kernel(be_ref, x_ref, wg_ref, wu_ref, wd_ref, y_ref, wg_bf, wu_bf, wd_bf):
    i = pl.program_id(0)
    prev = be_ref[jnp.maximum(i - 1, 0)]

    @pl.when((i == 0) | (be_ref[i] != prev))
    def _():
        wg_bf[...] = wg_ref[0].astype(BF)
        wu_bf[...] = wu_ref[0].astype(BF)
        wd_bf[...] = wd_ref[0].astype(BF)

    x = x_ref[...]
    hid = _silu(_dot(x, wg_bf[...])) * _dot(x, wu_bf[...])
    y_ref[...] = _dot(hid.astype(BF), wd_bf[...])


def _experts(block_e, xb, w_eg, w_eu, w_ed):
    nrows, d = xb.shape
    nblocks = nrows // MOE_BLOCK
    de = w_eg.shape[2]
    grid_spec = pltpu.PrefetchScalarGridSpec(
        num_scalar_prefetch=1,
        grid=(nblocks,),
        in_specs=[pl.BlockSpec((MOE_BLOCK, d), lambda i, be: (i, 0)),
                  pl.BlockSpec((1, d, de), lambda i, be: (be[i], 0, 0)),
                  pl.BlockSpec((1, d, de), lambda i, be: (be[i], 0, 0)),
                  pl.BlockSpec((1, de, d), lambda i, be: (be[i], 0, 0))],
        out_specs=pl.BlockSpec((MOE_BLOCK, d), lambda i, be: (i, 0)),
        scratch_shapes=[pltpu.VMEM((d, de), BF), pltpu.VMEM((d, de), BF), pltpu.VMEM((de, d), BF)],
    )
    return pl.pallas_call(
        _expert_kernel,
        grid_spec=grid_spec,
        out_shape=jax.ShapeDtypeStruct((nrows, d), F32),
        compiler_params=_cp(("arbitrary",), 56),
        name="expert_blocks",
    )(block_e, xb, w_eg, w_eu, w_ed)


def _final_kernel(h_ref, f_ref, w_ref, ya_ref, yb_ref, *, na):
    i = pl.program_id(0)
    y = h_ref[...] + f_ref[...]
    ms = jnp.mean(y * y, axis=-1, keepdims=True)
    out = y * lax.rsqrt(ms + EPS) * w_ref[...]

    @pl.when(i < na)
    def _():
        ya_ref[...] = out

    @pl.when(i >= na)
    def _():
        yb_ref[...] = out


def _final(h, ffn, w, rows_a, rows_b):
    t, d = h.shape
    tm = _pick(math.gcd(rows_a, rows_b), 512)
    na, nb, sa, sb = _two_source_specs(rows_a, rows_b, tm, d)
    return pl.pallas_call(
        functools.partial(_final_kernel, na=na),
        grid=(na + nb,),
        in_specs=[pl.BlockSpec((tm, d), lambda i: (i, 0)), pl.BlockSpec((tm, d), lambda i: (i, 0)),
                  pl.BlockSpec((1, d), lambda i: (0, 0))],
        out_specs=[sa, sb],
        out_shape=[jax.ShapeDtypeStruct((rows_a, d), F32), jax.ShapeDtypeStruct((rows_b, d), F32)],
        compiler_params=_cp(("arbitrary",)),
        name="combine_final_norm",
    )(h, ffn, w.reshape(1, d))


def kernel(x_prompt, x_sample, state_hgrn, state_gdn, state_conv, norm_mix, w_in, conv_w, ha_lb_logits, ha_onorm, w_pa, gd_A_log, gd_dt_bias, gd_onorm, w_pb, w_out, norm_ffn, w_router_group, b_router_group, w_router_expert, b_router_expert, w_exp_gate, w_exp_up, w_exp_down, norm_final):
    depth = state_hgrn.shape[0]
    assert depth == 1, "one decoder layer"
    bp, lp, d = x_prompt.shape
    bs, ls, _ = x_sample.shape
    _, _, ha_heads, ha_dk, ha_dv = state_hgrn.shape
    _, _, vheads, gd_dk, gd_dv = state_gdn.shape
    conv_dim = state_conv.shape[3]
    ha_kdim, ha_vdim = ha_heads * ha_dk, ha_heads * ha_dv
    gd_vdim = vheads * gd_dv
    ngroups = w_router_group.shape[2]
    nexp = w_router_expert.shape[2]
    rp, rs = bp * lp, bs * ls
    t = rp + rs

    off_f = ha_kdim
    off_i = off_f + ha_kdim
    off_g = off_i + ha_vdim
    off_qkv = off_g + ha_vdim
    off_z = off_qkv + conv_dim
    off_b = off_z + gd_vdim
    off_gate = off_b + 2 * vheads
    w_in0 = w_in[0]

    xn = _rmsnorm_bf16(x_prompt.reshape(rp, d), x_sample.reshape(rs, d), norm_mix[0])

    ident = lambda a: (a,)
    (q_a,) = _proj(xn, w_in0, 0, ha_kdim, lambda a: (a * (ha_dk ** -0.5),), [F32], name="proj_hgrn_q")
    logf, k_a = _proj(xn, w_in0, off_f, ha_kdim, _forget_epilogue, [F32, F32],
                      extras=(ha_lb_logits,), name="proj_hgrn_forget")
    (v_a,) = _proj(xn, w_in0, off_i, ha_vdim, ident, [F32], name="proj_hgrn_in")
    (g_a,) = _proj(xn, w_in0, off_g, ha_vdim, ident, [F32], name="proj_hgrn_gate")
    (qkv,) = _proj(xn, w_in0, off_qkv, conv_dim, ident, [F32], name="proj_gdn_qkv")
    (z_b,) = _proj(xn, w_in0, off_z, gd_vdim, ident, [F32], name="proj_gdn_z")
    (gates,) = _proj(xn, w_in0[:, off_gate:], 0, 2 * d, ident, [F32], name="proj_merge_gates")
    bg, bgt = _beta_decay(xn, w_in0[:, off_b:off_gate], gd_A_log[0], gd_dt_bias[0])

    cg_p = _pick(lp, 128, GDN_CHUNK) if lp >= GDN_CHUNK else lp
    sub_p = min(HGRN_SUB, lp)
    chunk_p = min(GDN_CHUNK, lp)
    cg_s = ls
    zeros_h = jnp.zeros((bp, ha_heads, ha_dk, ha_dv), F32)
    zeros_g = jnp.zeros((bp, vheads, gd_dk, gd_dv), F32)
    zeros_c = jnp.zeros((bp, CONV_K - 1, conv_dim), F32)

    oa_p, sh_p = _hgrn2(q_a, logf, k_a, v_a, g_a, ha_onorm[0], zeros_h,
                        row0=0, seq_len=lp, rows_per_step=cg_p, sub=sub_p)
    oa_s, sh_s = _hgrn2(q_a, logf, k_a, v_a, g_a, ha_onorm[0], state_hgrn[0],
                        row0=rp, seq_len=ls, rows_per_step=cg_s, sub=min(HGRN_SUB, ls))

    def time_on_lanes(rows0, nrows, cg):
        part = bgt[:, rows0:rows0 + nrows].reshape(2 * vheads, nrows // cg, cg)
        return jnp.transpose(part, (1, 0, 2))

    bgt_p = time_on_lanes(0, rp, cg_p)
    bgt_s = time_on_lanes(rp, rs, cg_s)
    ob_p, sg_p, sc_p = _gdn(qkv, z_b, bg, bgt_p, conv_w[0], gd_onorm[0], zeros_g, zeros_c,
                            row0=0, seq_len=lp, rows_per_step=cg_p, chunk=chunk_p)
    ob_s, sg_s, sc_s = _gdn(qkv[rp:], z_b[rp:], bg[rp:], bgt_s, conv_w[0], gd_onorm[0], state_gdn[0],
                            state_conv[0], row0=0, seq_len=ls, rows_per_step=cg_s, chunk=min(GDN_CHUNK, ls))

    merged = _merge(oa_p, oa_s, ob_p, ob_s, gates, w_pa[0], w_pb[0])

    nr = LANES
    w_router = jnp.concatenate([w_router_group[0], w_router_expert[0]], axis=1)
    w_router = jnp.pad(w_router, ((0, 0), (0, nr - ngroups - nexp))).astype(BF)
    b_router = jnp.pad(jnp.concatenate([b_router_group[0], b_router_expert[0]]), (0, nr - ngroups - nexp))
    h, xf, logits = _outproj(merged, x_prompt.reshape(rp, d), x_sample.reshape(rs, d), w_out[0],
                             norm_ffn[0], w_router)

    eid, gate, rank, cnt = _route(logits, b_router.reshape(1, nr), ngroups, nexp)
    nblocks = -(-(t * TOP_K) // MOE_BLOCK) + nexp
    pos, block_e = _plan(cnt, eid, rank, ngroups, nexp, nblocks)
    block_e = block_e[:nblocks, 0]

    tok = jnp.broadcast_to(jnp.arange(t, dtype=I32)[:, None], (t, TOP_K))
    buf_tok = jnp.zeros((nblocks * MOE_BLOCK,), I32).at[pos.reshape(-1)].set(tok.reshape(-1))
    xb = xf[buf_tok]
    yb = _experts(block_e, xb, w_exp_gate[0], w_exp_up[0], w_exp_down[0])
    ffn = jnp.sum(yb[pos] * gate[:, :, None], axis=1)

    y_p, y_s = _final(h, ffn, norm_final, rp, rs)
    return (y_p.reshape(bp, lp, d), y_s.reshape(bs, ls, d),
            sh_p[None], sg_p[None], sc_p[None], sh_s[None], sg_s[None], sc_s[None])
```

```python
import functools
import math

import jax
import jax.numpy as jnp
from jax import lax
from jax.experimental import pallas as pl
from jax.experimental.pallas import tpu as pltpu

F32 = jnp.float32
BF = jnp.bfloat16
I32 = jnp.int32
EPS = 1e-6
CONV_K = 4
TOP_K = 2
MOE_BLOCK = 128
GDN_CHUNK = 64
HGRN_SUB = 16
LANES = 128
NEG = -3.0e38
HI = lax.Precision.HIGHEST
MIB = 1024 * 1024


def _cp(sem, vmem_mib=48):
    return pltpu.CompilerParams(dimension_semantics=sem, vmem_limit_bytes=vmem_mib * MIB)


def _pick(n, pref, mult=8):
    best = None
    for t in range(mult, min(n, pref) + 1, mult):
        if n % t == 0:
            best = t
    return best if best is not None else n


def _sigmoid(x):
    return 1.0 / (1.0 + jnp.exp(-x))


def _silu(x):
    return x * _sigmoid(x)


def _iota(shape, dim):
    return lax.broadcasted_iota(I32, shape, dim)


def _dot(a, b):
    return jnp.dot(a, b, preferred_element_type=F32)


def _dot_nt(a, b):
    return lax.dot_general(a, b, (((1,), (1,)), ((), ())), preferred_element_type=F32)


def _dot_tn(a, b):
    return lax.dot_general(a, b, (((0,), (0,)), ((), ())), preferred_element_type=F32)


def _split(a):
    hi = a.astype(BF)
    lo = (a - hi.astype(F32)).astype(BF)
    return hi, lo


def _dot3(a, b):
    ah, al = _split(a)
    bh, bl = _split(b)
    return _dot(ah, bh) + (_dot(ah, bl) + _dot(al, bh))


def _two_source_specs(rows_a, rows_b, tm, width):
    na, nb = rows_a // tm, rows_b // tm
    spec_a = pl.BlockSpec((tm, width), lambda i: (jnp.minimum(i, na - 1), 0))
    spec_b = pl.BlockSpec((tm, width), lambda i: (jnp.maximum(i - na, 0), 0))
    return na, nb, spec_a, spec_b


def _rmsnorm_kernel(xa_ref, xb_ref, w_ref, o_ref, *, na):
    i = pl.program_id(0)

    def body(x_ref):
        x = x_ref[...]
        ms = jnp.mean(x * x, axis=-1, keepdims=True)
        o_ref[...] = (x * lax.rsqrt(ms + EPS) * w_ref[...]).astype(o_ref.dtype)

    @pl.when(i < na)
    def _():
        body(xa_ref)

    @pl.when(i >= na)
    def _():
        body(xb_ref)


def _rmsnorm_bf16(xa, xb, w):
    d = xa.shape[1]
    tm = _pick(math.gcd(xa.shape[0], xb.shape[0]), 512)
    na, nb, sa, sb = _two_source_specs(xa.shape[0], xb.shape[0], tm, d)
    return pl.pallas_call(
        functools.partial(_rmsnorm_kernel, na=na),
        grid=(na + nb,),
        in_specs=[sa, sb, pl.BlockSpec((1, d), lambda i: (0, 0))],
        out_specs=pl.BlockSpec((tm, d), lambda i: (i, 0)),
        out_shape=jax.ShapeDtypeStruct((xa.shape[0] + xb.shape[0], d), BF),
        compiler_params=_cp(("arbitrary",)),
        name="rmsnorm_mix",
    )(xa, xb, w.reshape(1, d))


def _proj_kernel(x_ref, w_ref, *rest, n_extra, n_out, epilogue):
    extra = rest[:n_extra]
    outs = rest[n_extra:n_extra + n_out]
    wbf = rest[n_extra + n_out]

    @pl.when(pl.program_id(1) == 0)
    def _():
        wbf[...] = w_ref[...].astype(BF)

    acc = _dot(x_ref[...], wbf[...])
    vals = epilogue(acc, *[e[...] for e in extra])
    for o_ref, v in zip(outs, vals):
        o_ref[...] = v.astype(o_ref.dtype)


def _proj(x_bf, w, col0, ncols, epilogue, out_dtypes, extras=(), name="proj"):
    t, k = x_bf.shape
    tn = _pick(math.gcd(ncols, col0) if col0 else ncols, 512, LANES)
    tm = _pick(t, 1024)
    j0 = col0 // tn
    n_out = len(out_dtypes)
    kern = functools.partial(_proj_kernel, n_extra=len(extras), n_out=n_out, epilogue=epilogue)
    return pl.pallas_call(
        kern,
        grid=(ncols // tn, t // tm),
        in_specs=[pl.BlockSpec((tm, k), lambda j, i: (i, 0)),
                  pl.BlockSpec((k, tn), lambda j, i: (0, j0 + j))]
                 + [pl.BlockSpec((e.shape[0], tn), lambda j, i: (0, j)) for e in extras],
        out_specs=[pl.BlockSpec((tm, tn), lambda j, i: (i, j)) for _ in out_dtypes],
        out_shape=[jax.ShapeDtypeStruct((t, ncols), dt) for dt in out_dtypes],
        scratch_shapes=[pltpu.VMEM((k, tn), BF)],
        compiler_params=_cp(("arbitrary", "arbitrary")),
        name=name,
    )(x_bf, w, *extras)


def _forget_epilogue(acc, lb_logits):
    m = jnp.max(lb_logits, axis=0, keepdims=True)
    e = jnp.exp(lb_logits - m)
    lb = e[0:1, :] / jnp.sum(e, axis=0, keepdims=True)
    f = lb + (1.0 - lb) * _sigmoid(acc)
    return jnp.log(f), 1.0 - f


def _beta_decay_kernel(x_ref, w_ref, wt_ref, prow_ref, pcol_ref, o_ref, ot_ref, *, vh):
    x = x_ref[...]
    acc = _dot(x, w_ref[...])
    acct = _dot_nt(wt_ref[...], x)

    def act(a, is_beta, a_neg_exp, dt_bias):
        z = a + dt_bias
        softplus = jnp.maximum(z, 0.0) + jnp.log(1.0 + jnp.exp(-jnp.abs(z)))
        return jnp.where(is_beta, _sigmoid(a), a_neg_exp * softplus)

    prow = prow_ref[...]
    pcol = pcol_ref[...]
    o_ref[...] = act(acc, _iota(acc.shape, 1) < vh, prow[0:1, :], prow[1:2, :])
    ot_ref[...] = act(acct, _iota(acct.shape, 0) < vh, pcol[:, 0:1], pcol[:, 1:2])


def _beta_decay(x_bf, w_ba, a_log, dt_bias):
    t, k = x_bf.shape
    vh = a_log.shape[0]
    tm = t if t <= 2048 else _pick(t, 1024, LANES)
    zeros = jnp.zeros((vh,), F32)
    prow = jnp.stack([jnp.concatenate([zeros, -jnp.exp(a_log)]), jnp.concatenate([zeros, dt_bias])])
    return pl.pallas_call(
        functools.partial(_beta_decay_kernel, vh=vh),
        grid=(t // tm,),
        in_specs=[pl.BlockSpec((tm, k), lambda i: (i, 0)),
                  pl.BlockSpec((k, 2 * vh), lambda i: (0, 0)),
                  pl.BlockSpec((2 * vh, k), lambda i: (0, 0)),
                  pl.BlockSpec((2, 2 * vh), lambda i: (0, 0)),
                  pl.BlockSpec((2 * vh, 2), lambda i: (0, 0))],
        out_specs=[pl.BlockSpec((tm, 2 * vh), lambda i: (i, 0)),
                   pl.BlockSpec((2 * vh, tm), lambda i: (0, i))],
        out_shape=[jax.ShapeDtypeStruct((t, 2 * vh), F32), jax.ShapeDtypeStruct((2 * vh, t), F32)],
        compiler_params=_cp(("arbitrary",)),
        name="proj_beta_decay",
    )(x_bf, w_ba.astype(BF), w_ba.T.astype(BF), prow, prow.T)


def _hgrn2_kernel(q_ref, lf_ref, k_ref, v_ref, g_ref, on_ref, s0_ref, o_ref, so_ref, s_scr,
                  *, heads, dk, dv, sub, nsub, nchunks):
    c = pl.program_id(1)

    @pl.when(c == 0)
    def _():
        s_scr[...] = s0_ref[0]

    r_i = _iota((sub, sub), 0)
    c_i = _iota((sub, sub), 1)
    tril = (r_i >= c_i).astype(F32)
    row = _iota((sub, 1), 0)
    onorm = on_ref[...]

    def block(sb, carry):
        r0 = pl.multiple_of(sb * sub, sub)
        rows = pl.ds(r0, sub)
        gcum_all = jnp.dot(tril, lf_ref[rows, :], precision=HI, preferred_element_type=F32)
        for h in range(heads):
            kc = pl.ds(h * dk, dk)
            vc = pl.ds(h * dv, dv)
            gc = gcum_all[:, h * dk:(h + 1) * dk]
            q = q_ref[rows, kc]
            k = k_ref[rows, kc]
            v = v_ref[rows, vc]
            gend = gc[sub - 1:sub, :]
            s_old = s_scr[h]
            o = _dot((q * jnp.exp(gc)).astype(BF), s_old.astype(BF))
            for j in range(sub):
                p = (q * k[j:j + 1, :]) * jnp.exp(gc - gc[j:j + 1, :])
                a_col = jnp.sum(p, axis=-1, keepdims=True)
                a_col = jnp.where(row >= j, a_col, 0.0)
                o = o + a_col * v[j:j + 1, :]
            kdec = (k * jnp.exp(gend - gc)).astype(BF)
            kv = _dot_tn(kdec, v.astype(BF))
            dmat = jnp.transpose(jnp.broadcast_to(jnp.exp(gend), (dk, dk)))
            s_scr[h] = (dmat if dk == dv else dmat[:, 0:1]) * s_old + kv
            ms = jnp.mean(o * o, axis=-1, keepdims=True)
            gate = g_ref[rows, vc]
            o_ref[rows, vc] = (o * lax.rsqrt(ms + EPS) * onorm) * _silu(gate)
        return carry

    lax.fori_loop(0, nsub, block, 0)

    @pl.when(c == nchunks - 1)
    def _():
        so_ref[0] = s_scr[...]


def _hgrn2(q, lf, kf, v, gate, onorm, s0, *, row0, seq_len, rows_per_step, sub):
    nseq, heads, dk, dv = s0.shape
    width = q.shape[1]
    cg = rows_per_step
    nchunks = seq_len // cg
    b0 = row0 // cg
    row_spec = pl.BlockSpec((cg, width), lambda b, c: (b0 + b * nchunks + c, 0))
    st_spec = pl.BlockSpec((1, heads, dk, dv), lambda b, c: (b, 0, 0, 0))
    kern = functools.partial(_hgrn2_kernel, heads=heads, dk=dk, dv=dv, sub=sub, nsub=cg // sub,
                             nchunks=nchunks)
    return pl.pallas_call(
        kern,
        grid=(nseq, nchunks),
        in_specs=[row_spec, row_spec, row_spec, row_spec, row_spec,
                  pl.BlockSpec((1, dv), lambda b, c: (0, 0)), st_spec],
        out_specs=[pl.BlockSpec((cg, heads * dv), lambda b, c: (b * nchunks + c, 0)), st_spec],
        out_shape=[jax.ShapeDtypeStruct((nseq * seq_len, heads * dv), F32),
                   jax.ShapeDtypeStruct(s0.shape, F32)],
        scratch_shapes=[pltpu.VMEM((heads, dk, dv), F32)],
        compiler_params=_cp(("arbitrary", "arbitrary")),
        name="hgrn2_recurrence",
    )(q, lf, kf, v, gate, onorm.reshape(1, dv), s0)


def _gdn_kernel(qkv_ref, z_ref, bg_ref, bgt_ref, cw_ref, on_ref, s0_ref, c0_ref,
                o_ref, so_ref, co_ref, s_scr, xx,
                *, kheads, vheads, dk, dv, chunk, nsub, nchunks):
    c = pl.program_id(1)
    cg = chunk * nsub
    kdim = kheads * dk
    rep = vheads // kheads

    @pl.when(c == 0)
    def _():
        s_scr[...] = s0_ref[0]
        xx[0:8, :] = c0_ref[0]

    xx[8:8 + cg, :] = qkv_ref[...]
    conv = xx[pl.ds(8 - (CONV_K - 1), cg), :] * cw_ref[0:1, :]
    for j in range(1, CONV_K):
        conv = conv + xx[pl.ds(8 - (CONV_K - 1) + j, cg), :] * cw_ref[j:j + 1, :]
    xs = _silu(conv)

    @pl.when(c == nchunks - 1)
    def _():
        co_ref[0] = xx[pl.ds(8 + cg - (CONV_K - 1), CONV_K - 1), :]

    xx[0:8, :] = xx[cg:cg + 8, :]

    r_i = _iota((chunk, chunk), 0)
    c_i = _iota((chunk, chunk), 1)
    causal = r_i >= c_i
    strict = r_i > c_i
    tril = causal.astype(F32)
    triu = (r_i <= c_i).astype(F32)
    eye = (r_i == c_i).astype(F32)
    onorm = on_ref[...]
    n_sq = max(int(math.log2(chunk)) - 1, 0)

    for s in range(nsub):
        r0 = s * chunk
        bg = bg_ref[r0:r0 + chunk, :]
        gt = bgt_ref[0, vheads:2 * vheads, r0:r0 + chunk]
        gcol_all = jnp.dot(tril, bg[:, vheads:2 * vheads], precision=HI, preferred_element_type=F32)
        grow_all = jnp.dot(gt, triu, precision=HI, preferred_element_type=F32)
        for kh in range(kheads):
            qh = xs[r0:r0 + chunk, kh * dk:(kh + 1) * dk]
            kk_ = xs[r0:r0 + chunk, kdim + kh * dk:kdim + (kh + 1) * dk]
            qn = qh * lax.rsqrt(jnp.sum(qh * qh, axis=-1, keepdims=True) + EPS) * (dk ** -0.5)
            kn = kk_ * lax.rsqrt(jnp.sum(kk_ * kk_, axis=-1, keepdims=True) + EPS)
            kn_bf = kn.astype(BF)
            kk = _dot_nt(kn_bf, kn_bf)
            qk = _dot_nt(qn.astype(BF), kn_bf)
            for r in range(rep):
                h = kh * rep + r
                vcols = slice(2 * kdim + h * dv, 2 * kdim + (h + 1) * dv)
                vh_ = xs[r0:r0 + chunk, vcols]
                gcol = gcol_all[:, h:h + 1]
                grow = grow_all[h:h + 1, :]
                beta = bg[:, h:h + 1]
                decay = jnp.where(causal, jnp.exp(jnp.minimum(gcol - grow, 0.0)), 0.0)
                x = jnp.where(strict, -(beta * kk * decay), 0.0)
                inv = eye + x
                p = x
                for _ in range(n_sq):
                    p = _dot3(p, p)
                    inv = inv + _dot3(inv, p)
                eg = jnp.exp(gcol)
                sol_v = _dot3(inv, vh_ * beta)
                sol_k = _dot3(inv, kn * (beta * eg))
                s_old = s_scr[h]
                s_bf = s_old.astype(BF)
                u = sol_v - _dot(sol_k.astype(BF), s_bf)
                o = _dot((qn * eg).astype(BF), s_bf) + _dot((qk * decay).astype(BF), u.astype(BF))
                gend = gcol[chunk - 1:chunk, :]
                kdec = (kn * jnp.exp(gend - gcol)).astype(BF)
                s_scr[h] = jnp.exp(gend) * s_old + _dot_tn(kdec, u.astype(BF))
                ms = jnp.mean(o * o, axis=-1, keepdims=True)
                zc = slice(h * dv, (h + 1) * dv)
                o_ref[r0:r0 + chunk, zc] = (o * lax.rsqrt(ms + EPS) * onorm) * _silu(z_ref[r0:r0 + chunk, zc])

    @pl.when(c == nchunks - 1)
    def _():
        so_ref[0] = s_scr[...]


def _gdn(qkv, z, bg, bgt3, conv_w, onorm, s0, conv0, *, row0, seq_len, rows_per_step, chunk):
    nseq, vheads, dk, dv = s0.shape
    conv_dim = qkv.shape[1]
    kheads = (conv_dim - vheads * dv) // (2 * dk)
    cg = rows_per_step
    nchunks = seq_len // cg
    b0 = row0 // cg
    conv0p = jnp.pad(conv0, ((0, 0), (8 - (CONV_K - 1), 0), (0, 0)))
    rmap = lambda b, c: (b0 + b * nchunks + c, 0)
    st_spec = pl.BlockSpec((1, vheads, dk, dv), lambda b, c: (b, 0, 0, 0))
    kern = functools.partial(_gdn_kernel, kheads=kheads, vheads=vheads, dk=dk, dv=dv, chunk=chunk,
                             nsub=cg // chunk, nchunks=nchunks)
    return pl.pallas_call(
        kern,
        grid=(nseq, nchunks),
        in_specs=[pl.BlockSpec((cg, conv_dim), rmap),
                  pl.BlockSpec((cg, vheads * dv), rmap),
                  pl.BlockSpec((cg, 2 * vheads), rmap),
                  pl.BlockSpec((1, 2 * vheads, cg), lambda b, c: (b0 + b * nchunks + c, 0, 0)),
                  pl.BlockSpec((CONV_K, conv_dim), lambda b, c: (0, 0)),
                  pl.BlockSpec((1, dv), lambda b, c: (0, 0)),
                  st_spec,
                  pl.BlockSpec((1, 8, conv_dim), lambda b, c: (b, 0, 0))],
        out_specs=[pl.BlockSpec((cg, vheads * dv), lambda b, c: (b * nchunks + c, 0)),
                   st_spec,
                   pl.BlockSpec((1, CONV_K - 1, conv_dim), lambda b, c: (b, 0, 0))],
        out_shape=[jax.ShapeDtypeStruct((nseq * seq_len, vheads * dv), F32),
                   jax.ShapeDtypeStruct(s0.shape, F32),
                   jax.ShapeDtypeStruct((nseq, CONV_K - 1, conv_dim), F32)],
        scratch_shapes=[pltpu.VMEM((vheads, dk, dv), F32), pltpu.VMEM((8 + cg, conv_dim), F32)],
        compiler_params=_cp(("arbitrary", "arbitrary")),
        name="gdn_recurrence",
    )(qkv, z, bg, bgt3, conv_w, onorm.reshape(1, dv), s0, conv0p)


def _merge_kernel(oa1_ref, oa2_ref, ob1_ref, ob2_ref, ga_ref, gb_ref, wa_ref, wb_ref, o_ref, *, na):
    i = pl.program_id(1)

    def body(oa_ref, ob_ref):
        ya = _dot(oa_ref[...].astype(BF), wa_ref[...])
        yb = _dot(ob_ref[...].astype(BF), wb_ref[...])
        o_ref[...] = (_sigmoid(ga_ref[...]) * ya + _sigmoid(gb_ref[...]) * yb).astype(o_ref.dtype)

    @pl.when(i < na)
    def _():
        body(oa1_ref, ob1_ref)

    @pl.when(i >= na)
    def _():
        body(oa2_ref, ob2_ref)


def _merge(oa_p, oa_s, ob_p, ob_s, gates, w_pa, w_pb):
    rp, rs = oa_p.shape[0], oa_s.shape[0]
    ka, kb = oa_p.shape[1], ob_p.shape[1]
    d = w_pa.shape[1]
    tm = _pick(math.gcd(rp, rs), 512)
    tn = _pick(d, 512, LANES)
    na, nb = rp // tm, rs // tm
    nj = d // tn
    amap = lambda j, i: (jnp.minimum(i, na - 1), 0)
    bmap = lambda j, i: (jnp.maximum(i - na, 0), 0)
    return pl.pallas_call(
        functools.partial(_merge_kernel, na=na),
        grid=(nj, na + nb),
        in_specs=[pl.BlockSpec((tm, ka), amap), pl.BlockSpec((tm, ka), bmap),
                  pl.BlockSpec((tm, kb), amap), pl.BlockSpec((tm, kb), bmap),
                  pl.BlockSpec((tm, tn), lambda j, i: (i, j)),
                  pl.BlockSpec((tm, tn), lambda j, i: (i, nj + j)),
                  pl.BlockSpec((ka, tn), lambda j, i: (0, j)),
                  pl.BlockSpec((kb, tn), lambda j, i: (0, j))],
        out_specs=pl.BlockSpec((tm, tn), lambda j, i: (i, j)),
        out_shape=jax.ShapeDtypeStruct((rp + rs, d), BF),
        compiler_params=_cp(("arbitrary", "arbitrary")),
        name="branch_proj_merge",
    )(oa_p, oa_s, ob_p, ob_s, gates, gates, w_pa.astype(BF), w_pb.astype(BF))


def _outproj_kernel(m_ref, xa_ref, xb_ref, wo_ref, nw_ref, wr_ref, h_ref, xf_ref, lg_ref, *, na):
    i = pl.program_id(0)

    def body(x_ref):
        h = x_ref[...] + _dot(m_ref[...], wo_ref[...])
        h_ref[...] = h
        ms = jnp.mean(h * h, axis=-1, keepdims=True)
        xf = h * lax.rsqrt(ms + EPS) * nw_ref[...]
        xf_ref[...] = xf
        lg_ref[...] = _dot(xf.astype(BF), wr_ref[...])

    @pl.when(i < na)
    def _():
        body(xa_ref)

    @pl.when(i >= na)
    def _():
        body(xb_ref)


def _outproj(merged, xa, xb, w_out, norm_ffn, w_router):
    t, d = merged.shape
    tm = _pick(math.gcd(xa.shape[0], xb.shape[0]), 512)
    na, nb, sa, sb = _two_source_specs(xa.shape[0], xb.shape[0], tm, d)
    nr = w_router.shape[1]
    const = lambda i: (0, 0)
    return pl.pallas_call(
        functools.partial(_outproj_kernel, na=na),
        grid=(na + nb,),
        in_specs=[pl.BlockSpec((tm, d), lambda i: (i, 0)), sa, sb,
                  pl.BlockSpec((d, d), const, pipeline_mode=pl.Buffered(1)),
                  pl.BlockSpec((1, d), const),
                  pl.BlockSpec((d, nr), const, pipeline_mode=pl.Buffered(1))],
        out_specs=[pl.BlockSpec((tm, d), lambda i: (i, 0)),
                   pl.BlockSpec((tm, d), lambda i: (i, 0)),
                   pl.BlockSpec((tm, nr), lambda i: (i, 0))],
        out_shape=[jax.ShapeDtypeStruct((t, d), F32), jax.ShapeDtypeStruct((t, d), F32),
                   jax.ShapeDtypeStruct((t, nr), F32)],
        compiler_params=_cp(("arbitrary",), 56),
        name="out_proj_ffn_norm",
    )(merged, xa, xb, w_out.astype(BF), norm_ffn.reshape(1, d), w_router)


def _route_kernel(lg_ref, b_ref, eid_ref, gate_ref, rank_ref, cnt_ref, carry,
                  *, ngroups, nexp, tm):
    i = pl.program_id(0)

    @pl.when(i == 0)
    def _():
        carry[...] = jnp.zeros_like(carry)

    per_group = nexp // ngroups
    lg = lg_ref[...] + b_ref[...]
    lane = _iota(lg.shape, 1)
    big = jnp.int32(1 << 20)
    is_g = lane < ngroups
    gl = jnp.where(is_g, lg, NEG)
    gmax = jnp.max(gl, axis=-1, keepdims=True)
    gidx = jnp.min(jnp.where(gl == gmax, lane, big), axis=-1, keepdims=True)
    gsum = jnp.sum(jnp.where(is_g, jnp.exp(gl - gmax), 0.0), axis=-1, keepdims=True)
    gw = 1.0 / gsum
    elane = lane - ngroups
    in_grp = (elane >= gidx * per_group) & (elane < (gidx + 1) * per_group)
    el = jnp.where(in_grp, lg, NEG)
    v1 = jnp.max(el, axis=-1, keepdims=True)
    i1 = jnp.min(jnp.where(in_grp & (el == v1), elane, big), axis=-1, keepdims=True)
    in2 = in_grp & (elane != i1)
    el2 = jnp.where(in2, lg, NEG)
    v2 = jnp.max(el2, axis=-1, keepdims=True)
    i2 = jnp.min(jnp.where(in2 & (el2 == v2), elane, big), axis=-1, keepdims=True)
    p2 = jnp.exp(v2 - v1)
    den = 1.0 + p2
    lane2 = _iota((tm, TOP_K), 1)
    eid_ref[...] = jnp.where(lane2 == 0, i1, i2)
    gate_ref[...] = jnp.where(lane2 == 0, gw / den, gw * p2 / den)

    oh1 = (elane == i1).astype(F32)
    oh2 = (elane == i2).astype(F32)
    lower = (_iota((tm, tm), 0) > _iota((tm, tm), 1)).astype(BF)
    cs1 = _dot(lower, oh1.astype(BF))
    cs2 = _dot(lower, oh2.astype(BF))
    tot1 = jnp.sum(oh1, axis=0, keepdims=True)
    tot2 = jnp.sum(oh2, axis=0, keepdims=True)
    base = carry[0:1, :]
    r1 = jnp.sum(oh1 * (base + cs1), axis=-1, keepdims=True)
    r2 = jnp.sum(oh2 * (base + tot1 + cs2), axis=-1, keepdims=True)
    rank_ref[...] = jnp.where(lane2 == 0, r1, r2).astype(I32)
    new = base + tot1 + tot2
    carry[...] = jnp.broadcast_to(new, carry.shape)
    cnt_ref[...] = jnp.broadcast_to(new, cnt_ref.shape)


def _route(logits, bias_row, ngroups, nexp):
    t, nr = logits.shape
    tm = _pick(t, 512)
    kern = functools.partial(_route_kernel, ngroups=ngroups, nexp=nexp, tm=tm)
    return pl.pallas_call(
        kern,
        grid=(t // tm,),
        in_specs=[pl.BlockSpec((tm, nr), lambda i: (i, 0)), pl.BlockSpec((1, nr), lambda i: (0, 0))],
        out_specs=[pl.BlockSpec((tm, TOP_K), lambda i: (i, 0)),
                   pl.BlockSpec((tm, TOP_K), lambda i: (i, 0)),
                   pl.BlockSpec((tm, TOP_K), lambda i: (i, 0)),
                   pl.BlockSpec((8, nr), lambda i: (0, 0))],
        out_shape=[jax.ShapeDtypeStruct((t, TOP_K), I32), jax.ShapeDtypeStruct((t, TOP_K), F32),
                   jax.ShapeDtypeStruct((t, TOP_K), I32), jax.ShapeDtypeStruct((8, nr), F32)],
        scratch_shapes=[pltpu.VMEM((8, nr), F32)],
        compiler_params=_cp(("arbitrary",)),
        name="route_topk_rank",
    )(logits, bias_row)


def _plan_kernel(cnt_ref, eid_ref, rank_ref, pos_ref, be_ref, *, ngroups, nexp, nblocks_pad, tm):
    nr = cnt_ref.shape[1]
    cnt = cnt_ref[0:1, :]
    padded = jnp.floor((cnt + (MOE_BLOCK - 1)) * (1.0 / MOE_BLOCK)) * MOE_BLOCK
    r_i = _iota((nr, nr), 0)
    c_i = _iota((nr, nr), 1)
    padded_col = jnp.sum(jnp.where(r_i == c_i, jnp.broadcast_to(padded, (nr, nr)), 0.0), axis=1, keepdims=True)
    start = jnp.sum(jnp.where(r_i < c_i, jnp.broadcast_to(padded_col, (nr, nr)), 0.0), axis=0, keepdims=True)
    end = start + padded
    lane = _iota((tm, nr), 1)
    eid = eid_ref[...]
    rank = rank_ref[...]
    lane2 = _iota((tm, TOP_K), 1)
    pos = jnp.zeros((tm, TOP_K), F32)
    for k in range(TOP_K):
        oh = (lane - ngroups) == eid[:, k:k + 1]
        st = jnp.sum(jnp.where(oh, start, 0.0), axis=-1, keepdims=True)
        pos = jnp.where(lane2 == k, st, pos)
    pos_ref[...] = pos.astype(I32) + rank

    @pl.when(pl.program_id(0) == 0)
    def _():
        blk_row = _iota((nblocks_pad, nr), 0).astype(F32) * MOE_BLOCK
        lane_b = _iota((nblocks_pad, nr), 1)
        is_e = (lane_b >= ngroups) & (lane_b < ngroups + nexp)
        n_le = jnp.sum(jnp.where(is_e & (end <= blk_row), 1.0, 0.0), axis=-1, keepdims=True)
        be_ref[...] = jnp.minimum(n_le, nexp - 1.0).astype(I32)


def _plan(cnt, eid, rank, ngroups, nexp, nblocks):
    t = eid.shape[0]
    nr = cnt.shape[1]
    tm = _pick(t, 512)
    nblocks_pad = -(-nblocks // 8) * 8
    kern = functools.partial(_plan_kernel, ngroups=ngroups, nexp=nexp, nblocks_pad=nblocks_pad, tm=tm)
    return pl.pallas_call(
        kern,
        grid=(t // tm,),
        in_specs=[pl.BlockSpec((8, nr), lambda i: (0, 0)),
                  pl.BlockSpec((tm, TOP_K), lambda i: (i, 0)),
                  pl.BlockSpec((tm, TOP_K), lambda i: (i, 0))],
        out_specs=[pl.BlockSpec((tm, TOP_K), lambda i: (i, 0)),
                   pl.BlockSpec((nblocks_pad, 1), lambda i: (0, 0))],
        out_shape=[jax.ShapeDtypeStruct((t, TOP_K), I32), jax.ShapeDtypeStruct((nblocks_pad, 1), I32)],
        compiler_params=_cp(("arbitrary",)),
        name="route_plan",
    )(cnt, eid, rank)


def _invert_kernel(pos_ref, tok_ref, *, n_assign, n_rows):
    def clear(r, c):
        tok_ref[r] = 0
        return c

    lax.fori_loop(0, n_rows, clear, 0)

    def put(a, c):
        tok_ref[pos_ref[a]] = lax.shift_right_logical(a, TOP_K.bit_length() - 1)
        return c

    lax.fori_loop(0, n_assign, put, 0)


def _invert(pos_flat, n_rows):
    n_assign = pos_flat.shape[0]
    return pl.pallas_call(
        functools.partial(_invert_kernel, n_assign=n_assign, n_rows=n_rows),
        in_specs=[pl.BlockSpec(memory_space=pltpu.SMEM)],
        out_specs=pl.BlockSpec(memory_space=pltpu.SMEM),
        out_shape=jax.ShapeDtypeStruct((n_rows,), I32),
        name="route_invert",
    )(pos_flat)


def _expert_kernel(be_ref, tok_ref, x_hbm, wg_ref, wu_ref, wd_ref, y_ref, xbuf, sem, wg_bf, wu_bf, wd_bf):
    i = pl.program_id(0)
    n = pl.num_programs(0)
    slot = i % 2

    def row_copy(blk, slot_, r):
        t = tok_ref[blk * MOE_BLOCK + r]
        return pltpu.make_async_copy(x_hbm.at[pl.ds(t, 1), :], xbuf.at[slot_, pl.ds(r, 1), :], sem.at[slot_])

    @pl.when(i == 0)
    def _():
        for r in range(MOE_BLOCK):
            row_copy(0, 0, r).start()

    @pl.when(i + 1 < n)
    def _():
        for r in range(MOE_BLOCK):
            row_copy(i + 1, 1 - slot, r).start()

    prev = be_ref[jnp.maximum(i - 1, 0)]

    @pl.when((i == 0) | (be_ref[i] != prev))
    def _():
        wg_bf[...] = wg_ref[0].astype(BF)
        wu_bf[...] = wu_ref[0].astype(BF)
        wd_bf[...] = wd_ref[0].astype(BF)

    for r in range(MOE_BLOCK):
        row_copy(i, slot, r).wait()
    x = xbuf[slot].astype(BF)
    hid = _silu(_dot(x, wg_bf[...])) * _dot(x, wu_bf[...])
    y_ref[...] = _dot(hid.astype(BF), wd_bf[...])


def _experts(block_e, row_tok, xf, w_eg, w_eu, w_ed):
    nrows = row_tok.shape[0]
    d = xf.shape[1]
    nblocks = nrows // MOE_BLOCK
    de = w_eg.shape[2]
    grid_spec = pltpu.PrefetchScalarGridSpec(
        num_scalar_prefetch=2,
        grid=(nblocks,),
        in_specs=[pl.BlockSpec(memory_space=pl.ANY),
                  pl.BlockSpec((1, d, de), lambda i, be, tok: (be[i], 0, 0)),
                  pl.BlockSpec((1, d, de), lambda i, be, tok: (be[i], 0, 0)),
                  pl.BlockSpec((1, de, d), lambda i, be, tok: (be[i], 0, 0))],
        out_specs=pl.BlockSpec((MOE_BLOCK, d), lambda i, be, tok: (i, 0)),
        scratch_shapes=[pltpu.VMEM((2, MOE_BLOCK, d), F32), pltpu.SemaphoreType.DMA((2,)),
                        pltpu.VMEM((d, de), BF), pltpu.VMEM((d, de), BF), pltpu.VMEM((de, d), BF)],
    )
    return pl.pallas_call(
        _expert_kernel,
        grid_spec=grid_spec,
        out_shape=jax.ShapeDtypeStruct((nrows, d), F32),
        compiler_params=_cp(("arbitrary",), 56),
        name="expert_blocks",
    )(block_e, row_tok, xf, w_eg, w_eu, w_ed)


def _final_kernel(pos_ref, h_ref, gate_ref, w_ref, yb_hbm, ya_ref, ys_ref, gbuf, sem, *, na, tm):
    i = pl.program_id(0)
    n = pl.num_programs(0)
    slot = i % 2

    def row_copy(tile, slot_, r, k):
        p = pos_ref[(tile * tm + r) * TOP_K + k]
        return pltpu.make_async_copy(yb_hbm.at[pl.ds(p, 1), :], gbuf.at[slot_, k, pl.ds(r, 1), :], sem.at[slot_])

    def start_tile(tile, slot_):
        for r in range(tm):
            for k in range(TOP_K):
                row_copy(tile, slot_, r, k).start()

    @pl.when(i == 0)
    def _():
        start_tile(0, 0)

    @pl.when(i + 1 < n)
    def _():
        start_tile(i + 1, 1 - slot)

    for r in range(tm):
        for k in range(TOP_K):
            row_copy(i, slot, r, k).wait()

    gate = gate_ref[...]
    y = h_ref[...]
    ffn = gbuf[slot, 0] * gate[:, 0:1]
    for k in range(1, TOP_K):
        ffn = ffn + gbuf[slot, k] * gate[:, k:k + 1]
    y = y + ffn
    ms = jnp.mean(y * y, axis=-1, keepdims=True)
    out = y * lax.rsqrt(ms + EPS) * w_ref[...]

    @pl.when(i < na)
    def _():
        ya_ref[...] = out

    @pl.when(i >= na)
    def _():
        ys_ref[...] = out


def _final(pos_flat, h, gate, yb, w, rows_a, rows_b):
    t, d = h.shape
    tm = _pick(math.gcd(rows_a, rows_b), 128)
    na, nb = rows_a // tm, rows_b // tm
    grid_spec = pltpu.PrefetchScalarGridSpec(
        num_scalar_prefetch=1,
        grid=(na + nb,),
        in_specs=[pl.BlockSpec((tm, d), lambda i, pos: (i, 0)),
                  pl.BlockSpec((tm, TOP_K), lambda i, pos: (i, 0)),
                  pl.BlockSpec((1, d), lambda i, pos: (0, 0)),
                  pl.BlockSpec(memory_space=pl.ANY)],
        out_specs=[pl.BlockSpec((tm, d), lambda i, pos: (jnp.minimum(i, na - 1), 0)),
                   pl.BlockSpec((tm, d), lambda i, pos: (jnp.maximum(i - na, 0), 0))],
        scratch_shapes=[pltpu.VMEM((2, TOP_K, tm, d), F32), pltpu.SemaphoreType.DMA((2,))],
    )
    return pl.pallas_call(
        functools.partial(_final_kernel, na=na, tm=tm),
        grid_spec=grid_spec,
        out_shape=[jax.ShapeDtypeStruct((rows_a, d), F32), jax.ShapeDtypeStruct((rows_b, d), F32)],
        compiler_params=_cp(("arbitrary",)),
        name="combine_final_norm",
    )(pos_flat, h, gate, w.reshape(1, d), yb)


def kernel(x_prompt, x_sample, state_hgrn, state_gdn, state_conv, norm_mix, w_in, conv_w, ha_lb_logits, ha_onorm, w_pa, gd_A_log, gd_dt_bias, gd_onorm, w_pb, w_out, norm_ffn, w_router_group, b_router_group, w_router_expert, b_router_expert, w_exp_gate, w_exp_up, w_exp_down, norm_final):
    depth = state_hgrn.shape[0]
    assert depth == 1, "one decoder layer"
    bp, lp, d = x_prompt.shape
    bs, ls, _ = x_sample.shape
    _, _, ha_heads, ha_dk, ha_dv = state_hgrn.shape
    _, _, vheads, gd_dk, gd_dv = state_gdn.shape
    conv_dim = state_conv.shape[3]
    ha_kdim, ha_vdim = ha_heads * ha_dk, ha_heads * ha_dv
    gd_vdim = vheads * gd_dv
    ngroups = w_router_group.shape[2]
    nexp = w_router_expert.shape[2]
    rp, rs = bp * lp, bs * ls
    t = rp + rs

    off_f = ha_kdim
    off_i = off_f + ha_kdim
    off_g = off_i + ha_vdim
    off_qkv = off_g + ha_vdim
    off_z = off_qkv + conv_dim
    off_b = off_z + gd_vdim
    off_gate = off_b + 2 * vheads
    w_in0 = w_in[0]

    xn = _rmsnorm_bf16(x_prompt.reshape(rp, d), x_sample.reshape(rs, d), norm_mix[0])

    ident = lambda a: (a,)
    (q_a,) = _proj(xn, w_in0, 0, ha_kdim, lambda a: (a * (ha_dk ** -0.5),), [F32], name="proj_hgrn_q")
    logf, k_a = _proj(xn, w_in0, off_f, ha_kdim, _forget_epilogue, [F32, F32],
                      extras=(ha_lb_logits,), name="proj_hgrn_forget")
    (v_a,) = _proj(xn, w_in0, off_i, ha_vdim, ident, [F32], name="proj_hgrn_in")
    (g_a,) = _proj(xn, w_in0, off_g, ha_vdim, ident, [F32], name="proj_hgrn_gate")
    (qkv,) = _proj(xn, w_in0, off_qkv, conv_dim, ident, [F32], name="proj_gdn_qkv")
    (z_b,) = _proj(xn, w_in0, off_z, gd_vdim, ident, [F32], name="proj_gdn_z")
    (gates,) = _proj(xn, w_in0[:, off_gate:], 0, 2 * d, ident, [F32], name="proj_merge_gates")
    bg, bgt = _beta_decay(xn, w_in0[:, off_b:off_gate], gd_A_log[0], gd_dt_bias[0])

    cg_p = _pick(lp, 128, GDN_CHUNK) if lp >= GDN_CHUNK else lp
    sub_p = min(HGRN_SUB, lp)
    chunk_p = min(GDN_CHUNK, lp)
    cg_s = ls
    zeros_h = jnp.zeros((bp, ha_heads, ha_dk, ha_dv), F32)
    zeros_g = jnp.zeros((bp, vheads, gd_dk, gd_dv), F32)
    zeros_c = jnp.zeros((bp, CONV_K - 1, conv_dim), F32)

    oa_p, sh_p = _hgrn2(q_a, logf, k_a, v_a, g_a, ha_onorm[0], zeros_h,
                        row0=0, seq_len=lp, rows_per_step=cg_p, sub=sub_p)
    oa_s, sh_s = _hgrn2(q_a, logf, k_a, v_a, g_a, ha_onorm[0], state_hgrn[0],
                        row0=rp, seq_len=ls, rows_per_step=cg_s, sub=min(HGRN_SUB, ls))

    def time_on_lanes(rows0, nrows, cg):
        part = bgt[:, rows0:rows0 + nrows].reshape(2 * vheads, nrows // cg, cg)
        return jnp.transpose(part, (1, 0, 2))

    bgt_p = time_on_lanes(0, rp, cg_p)
    bgt_s = time_on_lanes(rp, rs, cg_s)
    ob_p, sg_p, sc_p = _gdn(qkv, z_b, bg, bgt_p, conv_w[0], gd_onorm[0], zeros_g, zeros_c,
                            row0=0, seq_len=lp, rows_per_step=cg_p, chunk=chunk_p)
    ob_s, sg_s, sc_s = _gdn(qkv[rp:], z_b[rp:], bg[rp:], bgt_s, conv_w[0], gd_onorm[0], state_gdn[0],
                            state_conv[0], row0=0, seq_len=ls, rows_per_step=cg_s, chunk=min(GDN_CHUNK, ls))

    merged = _merge(oa_p, oa_s, ob_p, ob_s, gates, w_pa[0], w_pb[0])

    nr = LANES
    w_router = jnp.concatenate([w_router_group[0], w_router_expert[0]], axis=1)
    w_router = jnp.pad(w_router, ((0, 0), (0, nr - ngroups - nexp))).astype(BF)
    b_router = jnp.pad(jnp.concatenate([b_router_group[0], b_router_expert[0]]), (0, nr - ngroups - nexp))
    h, xf, logits = _outproj(merged, x_prompt.reshape(rp, d), x_sample.reshape(rs, d), w_out[0],
                             norm_ffn[0], w_router)

    eid, gate, rank, cnt = _route(logits, b_router.reshape(1, nr), ngroups, nexp)
    nblocks = -(-(t * TOP_K) // MOE_BLOCK) + nexp
    pos, block_e = _plan(cnt, eid, rank, ngroups, nexp, nblocks)
    block_e = block_e[:nblocks, 0]

    pos_flat = pos.reshape(-1)
    row_tok = _invert(pos_flat, nblocks * MOE_BLOCK)
    yb = _experts(block_e, row_tok, xf, w_exp_gate[0], w_exp_up[0], w_exp_down[0])
    y_p, y_s = _final(pos_flat, h, gate, yb, norm_final, rp, rs)
    return (y_p.reshape(bp, lp, d), y_s.reshape(bs, ls, d),
            sh_p[None], sg_p[None], sc_p[None], sh_s[None], sg_s[None], sc_s[None])
```

```python
import functools
import math

import jax
import jax.numpy as jnp
from jax import lax
from jax.experimental import pallas as pl
from jax.experimental.pallas import tpu as pltpu

F32 = jnp.float32
BF = jnp.bfloat16
I32 = jnp.int32
EPS = 1e-6
CONV_K = 4
TOP_K = 2
MOE_BLOCK = 128
GDN_CHUNK = 64
GDN_SAMPLE_SEQS = 4
HGRN_SUB = 16
LANES = 128
NEG = -3.0e38
HI = lax.Precision.HIGHEST
MIB = 1024 * 1024


def _cp(sem, vmem_mib=48):
    return pltpu.CompilerParams(dimension_semantics=sem, vmem_limit_bytes=vmem_mib * MIB)


def _pick(n, pref, mult=8):
    best = None
    for t in range(mult, min(n, pref) + 1, mult):
        if n % t == 0:
            best = t
    return best if best is not None else n


def _sigmoid(x):
    return 1.0 / (1.0 + jnp.exp(-x))


def _silu(x):
    return x * _sigmoid(x)


def _iota(shape, dim):
    return lax.broadcasted_iota(I32, shape, dim)


def _dot(a, b):
    return jnp.dot(a, b, preferred_element_type=F32)


def _dot_nt(a, b):
    return lax.dot_general(a, b, (((1,), (1,)), ((), ())), preferred_element_type=F32)


def _dot_tn(a, b):
    return lax.dot_general(a, b, (((0,), (0,)), ((), ())), preferred_element_type=F32)


def _split(a):
    hi = a.astype(BF)
    lo = (a - hi.astype(F32)).astype(BF)
    return hi, lo


def _dot3(a, b):
    ah, al = _split(a)
    bh, bl = _split(b)
    return _dot(ah, bh) + (_dot(ah, bl) + _dot(al, bh))


def _two_source_specs(rows_a, rows_b, tm, width):
    na, nb = rows_a // tm, rows_b // tm
    spec_a = pl.BlockSpec((tm, width), lambda i: (jnp.minimum(i, na - 1), 0))
    spec_b = pl.BlockSpec((tm, width), lambda i: (jnp.maximum(i - na, 0), 0))
    return na, nb, spec_a, spec_b


def _rmsnorm_kernel(xa_ref, xb_ref, w_ref, o_ref, *, na):
    i = pl.program_id(0)

    def body(x_ref):
        x = x_ref[...]
        ms = jnp.mean(x * x, axis=-1, keepdims=True)
        o_ref[...] = (x * lax.rsqrt(ms + EPS) * w_ref[...]).astype(o_ref.dtype)

    @pl.when(i < na)
    def _():
        body(xa_ref)

    @pl.when(i >= na)
    def _():
        body(xb_ref)


def _rmsnorm_bf16(xa, xb, w):
    d = xa.shape[1]
    tm = _pick(math.gcd(xa.shape[0], xb.shape[0]), 512)
    na, nb, sa, sb = _two_source_specs(xa.shape[0], xb.shape[0], tm, d)
    return pl.pallas_call(
        functools.partial(_rmsnorm_kernel, na=na),
        grid=(na + nb,),
        in_specs=[sa, sb, pl.BlockSpec((1, d), lambda i: (0, 0))],
        out_specs=pl.BlockSpec((tm, d), lambda i: (i, 0)),
        out_shape=jax.ShapeDtypeStruct((xa.shape[0] + xb.shape[0], d), BF),
        compiler_params=_cp(("arbitrary",)),
        name="rmsnorm_mix",
    )(xa, xb, w.reshape(1, d))


def _proj_kernel(x_ref, w_ref, *rest, n_extra, n_out, epilogue):
    extra = rest[:n_extra]
    outs = rest[n_extra:n_extra + n_out]
    wbf = rest[n_extra + n_out]

    @pl.when(pl.program_id(1) == 0)
    def _():
        wbf[...] = w_ref[...].astype(BF)

    acc = _dot(x_ref[...], wbf[...])
    vals = epilogue(acc, *[e[...] for e in extra])
    for o_ref, v in zip(outs, vals):
        o_ref[...] = v.astype(o_ref.dtype)


def _proj(x_bf, w, col0, ncols, epilogue, out_dtypes, extras=(), name="proj"):
    t, k = x_bf.shape
    tn = _pick(math.gcd(ncols, col0) if col0 else ncols, 512, LANES)
    tm = _pick(t, 1024)
    j0 = col0 // tn
    n_out = len(out_dtypes)
    kern = functools.partial(_proj_kernel, n_extra=len(extras), n_out=n_out, epilogue=epilogue)
    return pl.pallas_call(
        kern,
        grid=(ncols // tn, t // tm),
        in_specs=[pl.BlockSpec((tm, k), lambda j, i: (i, 0)),
                  pl.BlockSpec((k, tn), lambda j, i: (0, j0 + j))]
                 + [pl.BlockSpec((e.shape[0], tn), lambda j, i: (0, j)) for e in extras],
        out_specs=[pl.BlockSpec((tm, tn), lambda j, i: (i, j)) for _ in out_dtypes],
        out_shape=[jax.ShapeDtypeStruct((t, ncols), dt) for dt in out_dtypes],
        scratch_shapes=[pltpu.VMEM((k, tn), BF)],
        compiler_params=_cp(("arbitrary", "arbitrary")),
        name=name,
    )(x_bf, w, *extras)


def _forget_epilogue(acc, lb_logits):
    m = jnp.max(lb_logits, axis=0, keepdims=True)
    e = jnp.exp(lb_logits - m)
    lb = e[0:1, :] / jnp.sum(e, axis=0, keepdims=True)
    f = lb + (1.0 - lb) * _sigmoid(acc)
    return jnp.log(f), 1.0 - f


def _beta_decay_kernel(x_ref, w_ref, wt_ref, prow_ref, pcol_ref, o_ref, ot_ref, *, vh):
    x = x_ref[...]
    acc = _dot(x, w_ref[...])
    acct = _dot_nt(wt_ref[...], x)

    def act(a, is_beta, a_neg_exp, dt_bias):
        z = a + dt_bias
        softplus = jnp.maximum(z, 0.0) + jnp.log(1.0 + jnp.exp(-jnp.abs(z)))
        return jnp.where(is_beta, _sigmoid(a), a_neg_exp * softplus)

    prow = prow_ref[...]
    pcol = pcol_ref[...]
    o_ref[...] = act(acc, _iota(acc.shape, 1) < vh, prow[0:1, :], prow[1:2, :])
    ot_ref[...] = act(acct, _iota(acct.shape, 0) < vh, pcol[:, 0:1], pcol[:, 1:2])


def _beta_decay(x_bf, w_ba, a_log, dt_bias):
    t, k = x_bf.shape
    vh = a_log.shape[0]
    tm = t if t <= 2048 else _pick(t, 1024, LANES)
    zeros = jnp.zeros((vh,), F32)
    prow = jnp.stack([jnp.concatenate([zeros, -jnp.exp(a_log)]), jnp.concatenate([zeros, dt_bias])])
    return pl.pallas_call(
        functools.partial(_beta_decay_kernel, vh=vh),
        grid=(t // tm,),
        in_specs=[pl.BlockSpec((tm, k), lambda i: (i, 0)),
                  pl.BlockSpec((k, 2 * vh), lambda i: (0, 0)),
                  pl.BlockSpec((2 * vh, k), lambda i: (0, 0)),
                  pl.BlockSpec((2, 2 * vh), lambda i: (0, 0)),
                  pl.BlockSpec((2 * vh, 2), lambda i: (0, 0))],
        out_specs=[pl.BlockSpec((tm, 2 * vh), lambda i: (i, 0)),
                   pl.BlockSpec((2 * vh, tm), lambda i: (0, i))],
        out_shape=[jax.ShapeDtypeStruct((t, 2 * vh), F32), jax.ShapeDtypeStruct((2 * vh, t), F32)],
        compiler_params=_cp(("arbitrary",)),
        name="proj_beta_decay",
    )(x_bf, w_ba.astype(BF), w_ba.T.astype(BF), prow, prow.T)


def _hgrn2_kernel(q_ref, lf_ref, k_ref, v_ref, g_ref, on_ref, s0_ref, o_ref, so_ref, s_scr,
                  *, heads, dk, dv, sub, nsub, nchunks):
    c = pl.program_id(1)

    @pl.when(c == 0)
    def _():
        s_scr[...] = s0_ref[0]

    r_i = _iota((sub, sub), 0)
    c_i = _iota((sub, sub), 1)
    tril = (r_i >= c_i).astype(F32)
    row = _iota((sub, 1), 0)
    onorm = on_ref[...]

    def block(sb, carry):
        r0 = pl.multiple_of(sb * sub, sub)
        rows = pl.ds(r0, sub)
        gcum_all = jnp.dot(tril, lf_ref[rows, :], precision=HI, preferred_element_type=F32)
        for h in range(heads):
            kc = pl.ds(h * dk, dk)
            vc = pl.ds(h * dv, dv)
            gc = gcum_all[:, h * dk:(h + 1) * dk]
            q = q_ref[rows, kc]
            k = k_ref[rows, kc]
            v = v_ref[rows, vc]
            gend = gc[sub - 1:sub, :]
            s_old = s_scr[h]
            o = _dot((q * jnp.exp(gc)).astype(BF), s_old.astype(BF))
            for j in range(sub):
                p = (q * k[j:j + 1, :]) * jnp.exp(gc - gc[j:j + 1, :])
                a_col = jnp.sum(p, axis=-1, keepdims=True)
                a_col = jnp.where(row >= j, a_col, 0.0)
                o = o + a_col * v[j:j + 1, :]
            kdec = (k * jnp.exp(gend - gc)).astype(BF)
            kv = _dot_tn(kdec, v.astype(BF))
            dmat = jnp.transpose(jnp.broadcast_to(jnp.exp(gend), (dk, dk)))
            s_scr[h] = (dmat if dk == dv else dmat[:, 0:1]) * s_old + kv
            ms = jnp.mean(o * o, axis=-1, keepdims=True)
            gate = g_ref[rows, vc]
            o_ref[rows, vc] = (o * lax.rsqrt(ms + EPS) * onorm) * _silu(gate)
        return carry

    lax.fori_loop(0, nsub, block, 0)

    @pl.when(c == nchunks - 1)
    def _():
        so_ref[0] = s_scr[...]


def _hgrn2(q, lf, kf, v, gate, onorm, s0, *, row0, seq_len, rows_per_step, sub):
    nseq, heads, dk, dv = s0.shape
    width = q.shape[1]
    cg = rows_per_step
    nchunks = seq_len // cg
    b0 = row0 // cg
    row_spec = pl.BlockSpec((cg, width), lambda b, c: (b0 + b * nchunks + c, 0))
    st_spec = pl.BlockSpec((1, heads, dk, dv), lambda b, c: (b, 0, 0, 0))
    kern = functools.partial(_hgrn2_kernel, heads=heads, dk=dk, dv=dv, sub=sub, nsub=cg // sub,
                             nchunks=nchunks)
    return pl.pallas_call(
        kern,
        grid=(nseq, nchunks),
        in_specs=[row_spec, row_spec, row_spec, row_spec, row_spec,
                  pl.BlockSpec((1, dv), lambda b, c: (0, 0)), st_spec],
        out_specs=[pl.BlockSpec((cg, heads * dv), lambda b, c: (b * nchunks + c, 0)), st_spec],
        out_shape=[jax.ShapeDtypeStruct((nseq * seq_len, heads * dv), F32),
                   jax.ShapeDtypeStruct(s0.shape, F32)],
        scratch_shapes=[pltpu.VMEM((heads, dk, dv), F32)],
        compiler_params=_cp(("arbitrary", "arbitrary")),
        name="hgrn2_recurrence",
    )(q, lf, kf, v, gate, onorm.reshape(1, dv), s0)


def _gdn_kernel(qkv_ref, z_ref, bg_ref, bgt_ref, cw_ref, on_ref, s0_ref, c0_ref,
                o_ref, so_ref, co_ref, s_scr, xx,
                *, kheads, vheads, dk, dv, chunk, nsub, nseq, nchunks):
    c = pl.program_id(1)
    cg = chunk * nsub
    kdim = kheads * dk
    rep = vheads // kheads
    tail0 = 8 - (CONV_K - 1)

    @pl.when(c == 0)
    def _():
        s_scr[...] = s0_ref[...]
        xx[:, 0:8, :] = c0_ref[...]

    xs = []
    for q in range(nseq):
        xx[q, 8:8 + cg, :] = qkv_ref[q * cg:(q + 1) * cg, :]
        conv = xx[q, pl.ds(tail0, cg), :] * cw_ref[0:1, :]
        for j in range(1, CONV_K):
            conv = conv + xx[q, pl.ds(tail0 + j, cg), :] * cw_ref[j:j + 1, :]
        xs.append(_silu(conv))

    @pl.when(c == nchunks - 1)
    def _():
        for q in range(nseq):
            co_ref[q] = xx[q, pl.ds(8 + cg - (CONV_K - 1), CONV_K - 1), :]

    for q in range(nseq):
        xx[q, 0:8, :] = xx[q, cg:cg + 8, :]

    r_i = _iota((chunk, chunk), 0)
    c_i = _iota((chunk, chunk), 1)
    causal = r_i >= c_i
    strict = r_i > c_i
    tril = causal.astype(F32)
    triu = (r_i <= c_i).astype(F32)
    onorm = on_ref[...]

    probs = []
    for q in range(nseq):
        for s in range(nsub):
            r0 = q * cg + s * chunk
            x0 = s * chunk
            bg = bg_ref[r0:r0 + chunk, :]
            gt = bgt_ref[0, vheads:2 * vheads, r0:r0 + chunk]
            gcol_all = jnp.dot(tril, bg[:, vheads:2 * vheads], precision=HI, preferred_element_type=F32)
            grow_all = jnp.dot(gt, triu, precision=HI, preferred_element_type=F32)
            for kh in range(kheads):
                qh = xs[q][x0:x0 + chunk, kh * dk:(kh + 1) * dk]
                kk_ = xs[q][x0:x0 + chunk, kdim + kh * dk:kdim + (kh + 1) * dk]
                qn = qh * lax.rsqrt(jnp.sum(qh * qh, axis=-1, keepdims=True) + EPS) * (dk ** -0.5)
                kn = kk_ * lax.rsqrt(jnp.sum(kk_ * kk_, axis=-1, keepdims=True) + EPS)
                kn_bf = kn.astype(BF)
                kk = _dot_nt(kn_bf, kn_bf)
                qk = _dot_nt(qn.astype(BF), kn_bf)
                for r in range(rep):
                    h = kh * rep + r
                    vh_ = xs[q][x0:x0 + chunk, 2 * kdim + h * dv:2 * kdim + (h + 1) * dv]
                    gcol = gcol_all[:, h:h + 1]
                    grow = grow_all[h:h + 1, :]
                    beta = bg[:, h:h + 1]
                    decay = jnp.where(causal, jnp.exp(jnp.minimum(gcol - grow, 0.0)), 0.0)
                    eg = jnp.exp(gcol)
                    gend = gcol[chunk - 1:chunk, :]
                    probs.append(dict(
                        q=q, s=s, h=h, r0=r0,
                        x=jnp.where(strict, -(beta * kk * decay), 0.0),
                        y=jnp.concatenate([vh_ * beta, kn * (beta * eg)], axis=1),
                        a_bf=(qk * decay).astype(BF),
                        qg=qn * eg,
                        kdec_bf=(kn * jnp.exp(gend - gcol)).astype(BF),
                        send=jnp.exp(gend)))

    if chunk <= 16:
        for j in range(chunk - 1):
            for p in probs:
                p["y"] = p["y"] + p["x"][:, j:j + 1] * p["y"][j:j + 1, :]
    else:
        n_fac = int(math.log2(chunk))
        for k in range(n_fac):
            for p in probs:
                p["y"] = p["y"] + _dot3(p["x"], p["y"])
            if k + 1 < n_fac:
                for p in probs:
                    p["x"] = _dot3(p["x"], p["x"])

    for s in range(nsub):
        cur = [p for p in probs if p["s"] == s]
        for p in cur:
            p["s_old"] = s_scr[p["q"], p["h"]]
            lhs = jnp.concatenate([p["y"][:, dv:], p["qg"]], axis=0).astype(BF)
            p["ws"] = _dot(lhs, p["s_old"].astype(BF))
        for p in cur:
            p["u_bf"] = (p["y"][:, :dv] - p["ws"][:chunk]).astype(BF)
        for p in cur:
            s_scr[p["q"], p["h"]] = p["send"] * p["s_old"] + _dot_tn(p["kdec_bf"], p["u_bf"])
        for p in cur:
            o = p["ws"][chunk:] + _dot(p["a_bf"], p["u_bf"])
            ms = jnp.mean(o * o, axis=-1, keepdims=True)
            zc = slice(p["h"] * dv, (p["h"] + 1) * dv)
            rows = slice(p["r0"], p["r0"] + chunk)
            o_ref[rows, zc] = (o * lax.rsqrt(ms + EPS) * onorm) * _silu(z_ref[rows, zc])

    @pl.when(c == nchunks - 1)
    def _():
        so_ref[...] = s_scr[...]


def _gdn(qkv, z, bg, bgt3, conv_w, onorm, s0, conv0, *, row0, seq_len, rows_per_step, chunk, seqs_per_step):
    nseq_total, vheads, dk, dv = s0.shape
    conv_dim = qkv.shape[1]
    kheads = (conv_dim - vheads * dv) // (2 * dk)
    cg = rows_per_step
    nseq = seqs_per_step
    nchunks = seq_len // cg
    assert nseq == 1 or nchunks == 1, "several sequences per step only when a step covers whole sequences"
    rows = nseq * cg
    b0 = row0 // rows
    conv0p = jnp.pad(conv0, ((0, 0), (8 - (CONV_K - 1), 0), (0, 0)))
    rmap = lambda b, c: (b0 + b * nchunks + c, 0)
    st_spec = pl.BlockSpec((nseq, vheads, dk, dv), lambda b, c: (b, 0, 0, 0))
    kern = functools.partial(_gdn_kernel, kheads=kheads, vheads=vheads, dk=dk, dv=dv, chunk=chunk,
                             nsub=cg // chunk, nseq=nseq, nchunks=nchunks)
    return pl.pallas_call(
        kern,
        grid=(nseq_total // nseq, nchunks),
        in_specs=[pl.BlockSpec((rows, conv_dim), rmap),
                  pl.BlockSpec((rows, vheads * dv), rmap),
                  pl.BlockSpec((rows, 2 * vheads), rmap),
                  pl.BlockSpec((1, 2 * vheads, rows), lambda b, c: (b0 + b * nchunks + c, 0, 0)),
                  pl.BlockSpec((CONV_K, conv_dim), lambda b, c: (0, 0)),
                  pl.BlockSpec((1, dv), lambda b, c: (0, 0)),
                  st_spec,
                  pl.BlockSpec((nseq, 8, conv_dim), lambda b, c: (b, 0, 0))],
        out_specs=[pl.BlockSpec((rows, vheads * dv), lambda b, c: (b * nchunks + c, 0)),
                   st_spec,
                   pl.BlockSpec((nseq, CONV_K - 1, conv_dim), lambda b, c: (b, 0, 0))],
        out_shape=[jax.ShapeDtypeStruct((nseq_total * seq_len, vheads * dv), F32),
                   jax.ShapeDtypeStruct(s0.shape, F32),
                   jax.ShapeDtypeStruct((nseq_total, CONV_K - 1, conv_dim), F32)],
        scratch_shapes=[pltpu.VMEM((nseq, vheads, dk, dv), F32), pltpu.VMEM((nseq, 8 + cg, conv_dim), F32)],
        compiler_params=_cp(("arbitrary", "arbitrary")),
        name="gdn_recurrence",
    )(qkv, z, bg, bgt3, conv_w, onorm.reshape(1, dv), s0, conv0p)


def _merge_kernel(oa1_ref, oa2_ref, ob1_ref, ob2_ref, ga_ref, gb_ref, wa_ref, wb_ref, o_ref, *, na):
    i = pl.program_id(1)

    def body(oa_ref, ob_ref):
        ya = _dot(oa_ref[...].astype(BF), wa_ref[...])
        yb = _dot(ob_ref[...].astype(BF), wb_ref[...])
        o_ref[...] = (_sigmoid(ga_ref[...]) * ya + _sigmoid(gb_ref[...]) * yb).astype(o_ref.dtype)

    @pl.when(i < na)
    def _():
        body(oa1_ref, ob1_ref)

    @pl.when(i >= na)
    def _():
        body(oa2_ref, ob2_ref)


def _merge(oa_p, oa_s, ob_p, ob_s, gates, w_pa, w_pb):
    rp, rs = oa_p.shape[0], oa_s.shape[0]
    ka, kb = oa_p.shape[1], ob_p.shape[1]
    d = w_pa.shape[1]
    tm = _pick(math.gcd(rp, rs), 512)
    tn = _pick(d, 512, LANES)
    na, nb = rp // tm, rs // tm
    nj = d // tn
    amap = lambda j, i: (jnp.minimum(i, na - 1), 0)
    bmap = lambda j, i: (jnp.maximum(i - na, 0), 0)
    return pl.pallas_call(
        functools.partial(_merge_kernel, na=na),
        grid=(nj, na + nb),
        in_specs=[pl.BlockSpec((tm, ka), amap), pl.BlockSpec((tm, ka), bmap),
                  pl.BlockSpec((tm, kb), amap), pl.BlockSpec((tm, kb), bmap),
                  pl.BlockSpec((tm, tn), lambda j, i: (i, j)),
                  pl.BlockSpec((tm, tn), lambda j, i: (i, nj + j)),
                  pl.BlockSpec((ka, tn), lambda j, i: (0, j)),
                  pl.BlockSpec((kb, tn), lambda j, i: (0, j))],
        out_specs=pl.BlockSpec((tm, tn), lambda j, i: (i, j)),
        out_shape=jax.ShapeDtypeStruct((rp + rs, d), BF),
        compiler_params=_cp(("arbitrary", "arbitrary")),
        name="branch_proj_merge",
    )(oa_p, oa_s, ob_p, ob_s, gates, gates, w_pa.astype(BF), w_pb.astype(BF))


def _outproj_kernel(m_ref, xa_ref, xb_ref, wo_ref, nw_ref, wr_ref, h_ref, xf_ref, lg_ref, *, na):
    i = pl.program_id(0)

    def body(x_ref):
        h = x_ref[...] + _dot(m_ref[...], wo_ref[...])
        h_ref[...] = h
        ms = jnp.mean(h * h, axis=-1, keepdims=True)
        xf = h * lax.rsqrt(ms + EPS) * nw_ref[...]
        xf_ref[...] = xf
        lg_ref[...] = _dot(xf.astype(BF), wr_ref[...])

    @pl.when(i < na)
    def _():
        body(xa_ref)

    @pl.when(i >= na)
    def _():
        body(xb_ref)


def _outproj(merged, xa, xb, w_out, norm_ffn, w_router):
    t, d = merged.shape
    tm = _pick(math.gcd(xa.shape[0], xb.shape[0]), 512)
    na, nb, sa, sb = _two_source_specs(xa.shape[0], xb.shape[0], tm, d)
    nr = w_router.shape[1]
    const = lambda i: (0, 0)
    return pl.pallas_call(
        functools.partial(_outproj_kernel, na=na),
        grid=(na + nb,),
        in_specs=[pl.BlockSpec((tm, d), lambda i: (i, 0)), sa, sb,
                  pl.BlockSpec((d, d), const, pipeline_mode=pl.Buffered(1)),
                  pl.BlockSpec((1, d), const),
                  pl.BlockSpec((d, nr), const, pipeline_mode=pl.Buffered(1))],
        out_specs=[pl.BlockSpec((tm, d), lambda i: (i, 0)),
                   pl.BlockSpec((tm, d), lambda i: (i, 0)),
                   pl.BlockSpec((tm, nr), lambda i: (i, 0))],
        out_shape=[jax.ShapeDtypeStruct((t, d), F32), jax.ShapeDtypeStruct((t, d), F32),
                   jax.ShapeDtypeStruct((t, nr), F32)],
        compiler_params=_cp(("arbitrary",), 56),
        name="out_proj_ffn_norm",
    )(merged, xa, xb, w_out.astype(BF), norm_ffn.reshape(1, d), w_router)


def _route_kernel(lg_ref, b_ref, eid_ref, gate_ref, rank_ref, cnt_ref, carry,
                  *, ngroups, nexp, tm):
    i = pl.program_id(0)

    @pl.when(i == 0)
    def _():
        carry[...] = jnp.zeros_like(carry)

    per_group = nexp // ngroups
    lg = lg_ref[...] + b_ref[...]
    lane = _iota(lg.shape, 1)
    big = jnp.int32(1 << 20)
    is_g = lane < ngroups
    gl = jnp.where(is_g, lg, NEG)
    gmax = jnp.max(gl, axis=-1, keepdims=True)
    gidx = jnp.min(jnp.where(gl == gmax, lane, big), axis=-1, keepdims=True)
    gsum = jnp.sum(jnp.where(is_g, jnp.exp(gl - gmax), 0.0), axis=-1, keepdims=True)
    gw = 1.0 / gsum
    elane = lane - ngroups
    in_grp = (elane >= gidx * per_group) & (elane < (gidx + 1) * per_group)
    el = jnp.where(in_grp, lg, NEG)
    v1 = jnp.max(el, axis=-1, keepdims=True)
    i1 = jnp.min(jnp.where(in_grp & (el == v1), elane, big), axis=-1, keepdims=True)
    in2 = in_grp & (elane != i1)
    el2 = jnp.where(in2, lg, NEG)
    v2 = jnp.max(el2, axis=-1, keepdims=True)
    i2 = jnp.min(jnp.where(in2 & (el2 == v2), elane, big), axis=-1, keepdims=True)
    p2 = jnp.exp(v2 - v1)
    den = 1.0 + p2
    lane2 = _iota((tm, TOP_K), 1)
    eid_ref[...] = jnp.where(lane2 == 0, i1, i2)
    gate_ref[...] = jnp.where(lane2 == 0, gw / den, gw * p2 / den)

    oh1 = (elane == i1).astype(F32)
    oh2 = (elane == i2).astype(F32)
    lower = (_iota((tm, tm), 0) > _iota((tm, tm), 1)).astype(BF)
    cs1 = _dot(lower, oh1.astype(BF))
    cs2 = _dot(lower, oh2.astype(BF))
    tot1 = jnp.sum(oh1, axis=0, keepdims=True)
    tot2 = jnp.sum(oh2, axis=0, keepdims=True)
    base = carry[0:1, :]
    r1 = jnp.sum(oh1 * (base + cs1), axis=-1, keepdims=True)
    r2 = jnp.sum(oh2 * (base + tot1 + cs2), axis=-1, keepdims=True)
    rank_ref[...] = jnp.where(lane2 == 0, r1, r2).astype(I32)
    new = base + tot1 + tot2
    carry[...] = jnp.broadcast_to(new, carry.shape)
    cnt_ref[...] = jnp.broadcast_to(new, cnt_ref.shape)


def _route(logits, bias_row, ngroups, nexp):
    t, nr = logits.shape
    tm = _pick(t, 512)
    kern = functools.partial(_route_kernel, ngroups=ngroups, nexp=nexp, tm=tm)
    return pl.pallas_call(
        kern,
        grid=(t // tm,),
        in_specs=[pl.BlockSpec((tm, nr), lambda i: (i, 0)), pl.BlockSpec((1, nr), lambda i: (0, 0))],
        out_specs=[pl.BlockSpec((tm, TOP_K), lambda i: (i, 0)),
                   pl.BlockSpec((tm, TOP_K), lambda i: (i, 0)),
                   pl.BlockSpec((tm, TOP_K), lambda i: (i, 0)),
                   pl.BlockSpec((8, nr), lambda i: (0, 0))],
        out_shape=[jax.ShapeDtypeStruct((t, TOP_K), I32), jax.ShapeDtypeStruct((t, TOP_K), F32),
                   jax.ShapeDtypeStruct((t, TOP_K), I32), jax.ShapeDtypeStruct((8, nr), F32)],
        scratch_shapes=[pltpu.VMEM((8, nr), F32)],
        compiler_params=_cp(("arbitrary",)),
        name="route_topk_rank",
    )(logits, bias_row)


def _plan_kernel(cnt_ref, eid_ref, rank_ref, pos_ref, be_ref, *, ngroups, nexp, nblocks_pad, tm):
    nr = cnt_ref.shape[1]
    cnt = cnt_ref[0:1, :]
    padded = jnp.floor((cnt + (MOE_BLOCK - 1)) * (1.0 / MOE_BLOCK)) * MOE_BLOCK
    r_i = _iota((nr, nr), 0)
    c_i = _iota((nr, nr), 1)
    padded_col = jnp.sum(jnp.where(r_i == c_i, jnp.broadcast_to(padded, (nr, nr)), 0.0), axis=1, keepdims=True)
    start = jnp.sum(jnp.where(r_i < c_i, jnp.broadcast_to(padded_col, (nr, nr)), 0.0), axis=0, keepdims=True)
    end = start + padded
    lane = _iota((tm, nr), 1)
    eid = eid_ref[...]
    rank = rank_ref[...]
    lane2 = _iota((tm, TOP_K), 1)
    pos = jnp.zeros((tm, TOP_K), F32)
    for k in range(TOP_K):
        oh = (lane - ngroups) == eid[:, k:k + 1]
        st = jnp.sum(jnp.where(oh, start, 0.0), axis=-1, keepdims=True)
        pos = jnp.where(lane2 == k, st, pos)
    pos_ref[...] = pos.astype(I32) + rank

    @pl.when(pl.program_id(0) == 0)
    def _():
        blk_row = _iota((nblocks_pad, nr), 0).astype(F32) * MOE_BLOCK
        lane_b = _iota((nblocks_pad, nr), 1)
        is_e = (lane_b >= ngroups) & (lane_b < ngroups + nexp)
        n_le = jnp.sum(jnp.where(is_e & (end <= blk_row), 1.0, 0.0), axis=-1, keepdims=True)
        be_ref[...] = jnp.minimum(n_le, nexp - 1.0).astype(I32)


def _plan(cnt, eid, rank, ngroups, nexp, nblocks):
    t = eid.shape[0]
    nr = cnt.shape[1]
    tm = _pick(t, 512)
    nblocks_pad = -(-nblocks // 8) * 8
    kern = functools.partial(_plan_kernel, ngroups=ngroups, nexp=nexp, nblocks_pad=nblocks_pad, tm=tm)
    return pl.pallas_call(
        kern,
        grid=(t // tm,),
        in_specs=[pl.BlockSpec((8, nr), lambda i: (0, 0)),
                  pl.BlockSpec((tm, TOP_K), lambda i: (i, 0)),
                  pl.BlockSpec((tm, TOP_K), lambda i: (i, 0))],
        out_specs=[pl.BlockSpec((tm, TOP_K), lambda i: (i, 0)),
                   pl.BlockSpec((nblocks_pad, 1), lambda i: (0, 0))],
        out_shape=[jax.ShapeDtypeStruct((t, TOP_K), I32), jax.ShapeDtypeStruct((nblocks_pad, 1), I32)],
        compiler_params=_cp(("arbitrary",)),
        name="route_plan",
    )(cnt, eid, rank)


def _invert_kernel(pos_ref, tok_ref, *, n_assign, n_rows):
    def clear(r, c):
        tok_ref[r] = 0
        return c

    lax.fori_loop(0, n_rows, clear, 0, unroll=16)

    def put(a, c):
        tok_ref[pos_ref[a]] = lax.shift_right_logical(a, TOP_K.bit_length() - 1)
        return c

    lax.fori_loop(0, n_assign, put, 0, unroll=16)


def _invert(pos_flat, n_rows):
    n_assign = pos_flat.shape[0]
    return pl.pallas_call(
        functools.partial(_invert_kernel, n_assign=n_assign, n_rows=n_rows),
        in_specs=[pl.BlockSpec(memory_space=pltpu.SMEM)],
        out_specs=pl.BlockSpec(memory_space=pltpu.SMEM),
        out_shape=jax.ShapeDtypeStruct((n_rows,), I32),
        name="route_invert",
    )(pos_flat)


def _expert_kernel(be_ref, tok_ref, x_hbm, wg_ref, wu_ref, wd_ref, y_ref, xbuf, sem, wg_bf, wu_bf, wd_bf):
    i = pl.program_id(0)
    n = pl.num_programs(0)
    slot = i % 2

    def row_copy(blk, slot_, r):
        t = tok_ref[blk * MOE_BLOCK + r]
        return pltpu.make_async_copy(x_hbm.at[pl.ds(t, 1), :], xbuf.at[slot_, pl.ds(r, 1), :], sem.at[slot_])

    @pl.when(i == 0)
    def _():
        for r in range(MOE_BLOCK):
            row_copy(0, 0, r).start()

    @pl.when(i + 1 < n)
    def _():
        for r in range(MOE_BLOCK):
            row_copy(i + 1, 1 - slot, r).start()

    prev = be_ref[jnp.maximum(i - 1, 0)]

    @pl.when((i == 0) | (be_ref[i] != prev))
    def _():
        wg_bf[...] = wg_ref[0].astype(BF)
        wu_bf[...] = wu_ref[0].astype(BF)
        wd_bf[...] = wd_ref[0].astype(BF)

    for r in range(MOE_BLOCK):
        row_copy(i, slot, r).wait()
    x = xbuf[slot].astype(BF)
    hid = _silu(_dot(x, wg_bf[...])) * _dot(x, wu_bf[...])
    y_ref[...] = _dot(hid.astype(BF), wd_bf[...])


def _experts(block_e, row_tok, xf, w_eg, w_eu, w_ed):
    nrows = row_tok.shape[0]
    d = xf.shape[1]
    nblocks = nrows // MOE_BLOCK
    de = w_eg.shape[2]
    grid_spec = pltpu.PrefetchScalarGridSpec(
        num_scalar_prefetch=2,
        grid=(nblocks,),
        in_specs=[pl.BlockSpec(memory_space=pl.ANY),
                  pl.BlockSpec((1, d, de), lambda i, be, tok: (be[i], 0, 0)),
                  pl.BlockSpec((1, d, de), lambda i, be, tok: (be[i], 0, 0)),
                  pl.BlockSpec((1, de, d), lambda i, be, tok: (be[i], 0, 0))],
        out_specs=pl.BlockSpec((MOE_BLOCK, d), lambda i, be, tok: (i, 0)),
        scratch_shapes=[pltpu.VMEM((2, MOE_BLOCK, d), F32), pltpu.SemaphoreType.DMA((2,)),
                        pltpu.VMEM((d, de), BF), pltpu.VMEM((d, de), BF), pltpu.VMEM((de, d), BF)],
    )
    return pl.pallas_call(
        _expert_kernel,
        grid_spec=grid_spec,
        out_shape=jax.ShapeDtypeStruct((nrows, d), F32),
        compiler_params=_cp(("arbitrary",), 56),
        name="expert_blocks",
    )(block_e, row_tok, xf, w_eg, w_eu, w_ed)


def _final_kernel(pos_ref, h_ref, gate_ref, w_ref, yb_hbm, ya_ref, ys_ref, gbuf, sem, *, na, tm):
    i = pl.program_id(0)
    n = pl.num_programs(0)
    slot = i % 2

    def row_copy(tile, slot_, r, k):
        p = pos_ref[(tile * tm + r) * TOP_K + k]
        return pltpu.make_async_copy(yb_hbm.at[pl.ds(p, 1), :], gbuf.at[slot_, k, pl.ds(r, 1), :], sem.at[slot_])

    def start_tile(tile, slot_):
        for r in range(tm):
            for k in range(TOP_K):
                row_copy(tile, slot_, r, k).start()

    @pl.when(i == 0)
    def _():
        start_tile(0, 0)

    @pl.when(i + 1 < n)
    def _():
        start_tile(i + 1, 1 - slot)

    for r in range(tm):
        for k in range(TOP_K):
            row_copy(i, slot, r, k).wait()

    gate = gate_ref[...]
    y = h_ref[...]
    ffn = gbuf[slot, 0] * gate[:, 0:1]
    for k in range(1, TOP_K):
        ffn = ffn + gbuf[slot, k] * gate[:, k:k + 1]
    y = y + ffn
    ms = jnp.mean(y * y, axis=-1, keepdims=True)
    out = y * lax.rsqrt(ms + EPS) * w_ref[...]

    @pl.when(i < na)
    def _():
        ya_ref[...] = out

    @pl.when(i >= na)
    def _():
        ys_ref[...] = out


def _final(pos_flat, h, gate, yb, w, rows_a, rows_b):
    t, d = h.shape
    tm = _pick(math.gcd(rows_a, rows_b), 128)
    na, nb = rows_a // tm, rows_b // tm
    grid_spec = pltpu.PrefetchScalarGridSpec(
        num_scalar_prefetch=1,
        grid=(na + nb,),
        in_specs=[pl.BlockSpec((tm, d), lambda i, pos: (i, 0)),
                  pl.BlockSpec((tm, TOP_K), lambda i, pos: (i, 0)),
                  pl.BlockSpec((1, d), lambda i, pos: (0, 0)),
                  pl.BlockSpec(memory_space=pl.ANY)],
        out_specs=[pl.BlockSpec((tm, d), lambda i, pos: (jnp.minimum(i, na - 1), 0)),
                   pl.BlockSpec((tm, d), lambda i, pos: (jnp.maximum(i - na, 0), 0))],
        scratch_shapes=[pltpu.VMEM((2, TOP_K, tm, d), F32), pltpu.SemaphoreType.DMA((2,))],
    )
    return pl.pallas_call(
        functools.partial(_final_kernel, na=na, tm=tm),
        grid_spec=grid_spec,
        out_shape=[jax.ShapeDtypeStruct((rows_a, d), F32), jax.ShapeDtypeStruct((rows_b, d), F32)],
        compiler_params=_cp(("arbitrary",)),
        name="combine_final_norm",
    )(pos_flat, h, gate, w.reshape(1, d), yb)


def kernel(x_prompt, x_sample, state_hgrn, state_gdn, state_conv, norm_mix, w_in, conv_w, ha_lb_logits, ha_onorm, w_pa, gd_A_log, gd_dt_bias, gd_onorm, w_pb, w_out, norm_ffn, w_router_group, b_router_group, w_router_expert, b_router_expert, w_exp_gate, w_exp_up, w_exp_down, norm_final):
    depth = state_hgrn.shape[0]
    assert depth == 1, "one decoder layer"
    bp, lp, d = x_prompt.shape
    bs, ls, _ = x_sample.shape
    _, _, ha_heads, ha_dk, ha_dv = state_hgrn.shape
    _, _, vheads, gd_dk, gd_dv = state_gdn.shape
    conv_dim = state_conv.shape[3]
    ha_kdim, ha_vdim = ha_heads * ha_dk, ha_heads * ha_dv
    gd_vdim = vheads * gd_dv
    ngroups = w_router_group.shape[2]
    nexp = w_router_expert.shape[2]
    rp, rs = bp * lp, bs * ls
    t = rp + rs

    off_f = ha_kdim
    off_i = off_f + ha_kdim
    off_g = off_i + ha_vdim
    off_qkv = off_g + ha_vdim
    off_z = off_qkv + conv_dim
    off_b = off_z + gd_vdim
    off_gate = off_b + 2 * vheads
    w_in0 = w_in[0]

    xn = _rmsnorm_bf16(x_prompt.reshape(rp, d), x_sample.reshape(rs, d), norm_mix[0])

    ident = lambda a: (a,)
    (q_a,) = _proj(xn, w_in0, 0, ha_kdim, lambda a: (a * (ha_dk ** -0.5),), [F32], name="proj_hgrn_q")
    logf, k_a = _proj(xn, w_in0, off_f, ha_kdim, _forget_epilogue, [F32, F32],
                      extras=(ha_lb_logits,), name="proj_hgrn_forget")
    (v_a,) = _proj(xn, w_in0, off_i, ha_vdim, ident, [F32], name="proj_hgrn_in")
    (g_a,) = _proj(xn, w_in0, off_g, ha_vdim, ident, [F32], name="proj_hgrn_gate")
    (qkv,) = _proj(xn, w_in0, off_qkv, conv_dim, ident, [F32], name="proj_gdn_qkv")
    (z_b,) = _proj(xn, w_in0, off_z, gd_vdim, ident, [F32], name="proj_gdn_z")
    (gates,) = _proj(xn, w_in0[:, off_gate:], 0, 2 * d, ident, [F32], name="proj_merge_gates")
    bg, bgt = _beta_decay(xn, w_in0[:, off_b:off_gate], gd_A_log[0], gd_dt_bias[0])

    cg_p = _pick(lp, 128, GDN_CHUNK) if lp >= GDN_CHUNK else lp
    sub_p = min(HGRN_SUB, lp)
    chunk_p = min(GDN_CHUNK, lp)
    cg_s = ls
    zeros_h = jnp.zeros((bp, ha_heads, ha_dk, ha_dv), F32)
    zeros_g = jnp.zeros((bp, vheads, gd_dk, gd_dv), F32)
    zeros_c = jnp.zeros((bp, CONV_K - 1, conv_dim), F32)

    oa_p, sh_p = _hgrn2(q_a, logf, k_a, v_a, g_a, ha_onorm[0], zeros_h,
                        row0=0, seq_len=lp, rows_per_step=cg_p, sub=sub_p)
    oa_s, sh_s = _hgrn2(q_a, logf, k_a, v_a, g_a, ha_onorm[0], state_hgrn[0],
                        row0=rp, seq_len=ls, rows_per_step=cg_s, sub=min(HGRN_SUB, ls))

    def time_on_lanes(rows0, nrows, cg):
        part = bgt[:, rows0:rows0 + nrows].reshape(2 * vheads, nrows // cg, cg)
        return jnp.transpose(part, (1, 0, 2))

    seqs_s = _pick(bs, GDN_SAMPLE_SEQS, 1)
    bgt_p = time_on_lanes(0, rp, cg_p)
    bgt_s = time_on_lanes(rp, rs, seqs_s * cg_s)
    ob_p, sg_p, sc_p = _gdn(qkv, z_b, bg, bgt_p, conv_w[0], gd_onorm[0], zeros_g, zeros_c,
                            row0=0, seq_len=lp, rows_per_step=cg_p, chunk=chunk_p, seqs_per_step=1)
    ob_s, sg_s, sc_s = _gdn(qkv[rp:], z_b[rp:], bg[rp:], bgt_s, conv_w[0], gd_onorm[0], state_gdn[0],
                            state_conv[0], row0=0, seq_len=ls, rows_per_step=cg_s, chunk=min(GDN_CHUNK, ls),
                            seqs_per_step=seqs_s)

    merged = _merge(oa_p, oa_s, ob_p, ob_s, gates, w_pa[0], w_pb[0])

    nr = LANES
    w_router = jnp.concatenate([w_router_group[0], w_router_expert[0]], axis=1)
    w_router = jnp.pad(w_router, ((0, 0), (0, nr - ngroups - nexp))).astype(BF)
    b_router = jnp.pad(jnp.concatenate([b_router_group[0], b_router_expert[0]]), (0, nr - ngroups - nexp))
    h, xf, logits = _outproj(merged, x_prompt.reshape(rp, d), x_sample.reshape(rs, d), w_out[0],
                             norm_ffn[0], w_router)

    eid, gate, rank, cnt = _route(logits, b_router.reshape(1, nr), ngroups, nexp)
    nblocks = -(-(t * TOP_K) // MOE_BLOCK) + nexp
    pos, block_e = _plan(cnt, eid, rank, ngroups, nexp, nblocks)
    block_e = block_e[:nblocks, 0]

    pos_flat = pos.reshape(-1)
    row_tok = _invert(pos_flat, nblocks * MOE_BLOCK)
    yb = _experts(block_e, row_tok, xf, w_exp_gate[0], w_exp_up[0], w_exp_down[0])
    y_p, y_s = _final(pos_flat, h, gate, yb, norm_final, rp, rs)
    return (y_p.reshape(bp, lp, d), y_s.reshape(bs, ls, d),
            sh_p[None], sg_p[None], sc_p[None], sh_s[None], sg_s[None], sc_s[None])
```

```python
import functools
import math

import jax
import jax.numpy as jnp
from jax import lax
from jax.experimental import pallas as pl
from jax.experimental.pallas import tpu as pltpu

F32 = jnp.float32
BF = jnp.bfloat16
I32 = jnp.int32
EPS = 1e-6
CONV_K = 4
TOP_K = 2
MOE_BLOCK = 128
GDN_CHUNK = 64
GDN_SAMPLE_SEQS = 4
HGRN_SUB = 16
HGRN_SAMPLE_SEQS = 4
LOG2E = 1.4426950408889634
LANES = 128
NEG = -3.0e38
HI = lax.Precision.HIGHEST
MIB = 1024 * 1024


def _cp(sem, vmem_mib=48):
    return pltpu.CompilerParams(dimension_semantics=sem, vmem_limit_bytes=vmem_mib * MIB)


def _pick(n, pref, mult=8):
    best = None
    for t in range(mult, min(n, pref) + 1, mult):
        if n % t == 0:
            best = t
    return best if best is not None else n


def _sigmoid(x):
    return 1.0 / (1.0 + jnp.exp(-x))


def _silu(x):
    return x * _sigmoid(x)


def _iota(shape, dim):
    return lax.broadcasted_iota(I32, shape, dim)


def _dot(a, b):
    return jnp.dot(a, b, preferred_element_type=F32)


def _dot_nt(a, b):
    return lax.dot_general(a, b, (((1,), (1,)), ((), ())), preferred_element_type=F32)


def _dot_tn(a, b):
    return lax.dot_general(a, b, (((0,), (0,)), ((), ())), preferred_element_type=F32)


def _split(a):
    hi = a.astype(BF)
    lo = (a - hi.astype(F32)).astype(BF)
    return hi, lo


def _split3(a, axis=0):
    hi = a.astype(BF)
    r1 = a - hi.astype(F32)
    mid = r1.astype(BF)
    lo = (r1 - mid.astype(F32)).astype(BF)
    return jnp.concatenate([hi, mid, lo], axis=axis)


def _dot3(a, b):
    ah, al = _split(a)
    bh, bl = _split(b)
    return _dot(ah, bh) + (_dot(ah, bl) + _dot(al, bh))


def _two_source_specs(rows_a, rows_b, tm, width):
    na, nb = rows_a // tm, rows_b // tm
    spec_a = pl.BlockSpec((tm, width), lambda i: (jnp.minimum(i, na - 1), 0))
    spec_b = pl.BlockSpec((tm, width), lambda i: (jnp.maximum(i - na, 0), 0))
    return na, nb, spec_a, spec_b


def _rmsnorm_kernel(xa_ref, xb_ref, w_ref, o_ref, *, na):
    i = pl.program_id(0)

    def body(x_ref):
        x = x_ref[...]
        ms = jnp.mean(x * x, axis=-1, keepdims=True)
        o_ref[...] = (x * lax.rsqrt(ms + EPS) * w_ref[...]).astype(o_ref.dtype)

    @pl.when(i < na)
    def _():
        body(xa_ref)

    @pl.when(i >= na)
    def _():
        body(xb_ref)


def _rmsnorm_bf16(xa, xb, w):
    d = xa.shape[1]
    tm = _pick(math.gcd(xa.shape[0], xb.shape[0]), 512)
    na, nb, sa, sb = _two_source_specs(xa.shape[0], xb.shape[0], tm, d)
    return pl.pallas_call(
        functools.partial(_rmsnorm_kernel, na=na),
        grid=(na + nb,),
        in_specs=[sa, sb, pl.BlockSpec((1, d), lambda i: (0, 0))],
        out_specs=pl.BlockSpec((tm, d), lambda i: (i, 0)),
        out_shape=jax.ShapeDtypeStruct((xa.shape[0] + xb.shape[0], d), BF),
        compiler_params=_cp(("arbitrary",)),
        name="rmsnorm_mix",
    )(xa, xb, w.reshape(1, d))


def _proj_kernel(x_ref, w_ref, *rest, n_extra, n_out, epilogue):
    extra = rest[:n_extra]
    outs = rest[n_extra:n_extra + n_out]
    wbf = rest[n_extra + n_out]

    @pl.when(pl.program_id(1) == 0)
    def _():
        wbf[...] = w_ref[...].astype(BF)

    acc = _dot(x_ref[...], wbf[...])
    vals = epilogue(acc, *[e[...] for e in extra])
    for o_ref, v in zip(outs, vals):
        o_ref[...] = v.astype(o_ref.dtype)


def _proj(x_bf, w, col0, ncols, epilogue, out_dtypes, extras=(), name="proj"):
    t, k = x_bf.shape
    tn = _pick(math.gcd(ncols, col0) if col0 else ncols, 512, LANES)
    tm = _pick(t, 1024)
    j0 = col0 // tn
    n_out = len(out_dtypes)
    kern = functools.partial(_proj_kernel, n_extra=len(extras), n_out=n_out, epilogue=epilogue)
    return pl.pallas_call(
        kern,
        grid=(ncols // tn, t // tm),
        in_specs=[pl.BlockSpec((tm, k), lambda j, i: (i, 0)),
                  pl.BlockSpec((k, tn), lambda j, i: (0, j0 + j))]
                 + [pl.BlockSpec((e.shape[0], tn), lambda j, i: (0, j)) for e in extras],
        out_specs=[pl.BlockSpec((tm, tn), lambda j, i: (i, j)) for _ in out_dtypes],
        out_shape=[jax.ShapeDtypeStruct((t, ncols), dt) for dt in out_dtypes],
        scratch_shapes=[pltpu.VMEM((k, tn), BF)],
        compiler_params=_cp(("arbitrary", "arbitrary")),
        name=name,
    )(x_bf, w, *extras)


def _forget_epilogue(acc, lb_logits):
    m = jnp.max(lb_logits, axis=0, keepdims=True)
    e = jnp.exp(lb_logits - m)
    lb = e[0:1, :] / jnp.sum(e, axis=0, keepdims=True)
    f = lb + (1.0 - lb) * _sigmoid(acc)
    return jnp.log(f), 1.0 - f


def _beta_decay_kernel(x_ref, w_ref, wt_ref, prow_ref, pcol_ref, o_ref, ot_ref, *, vh):
    x = x_ref[...]
    acc = _dot(x, w_ref[...])
    acct = _dot_nt(wt_ref[...], x)

    def act(a, is_beta, a_neg_exp, dt_bias):
        z = a + dt_bias
        softplus = jnp.maximum(z, 0.0) + jnp.log(1.0 + jnp.exp(-jnp.abs(z)))
        return jnp.where(is_beta, _sigmoid(a), a_neg_exp * softplus)

    prow = prow_ref[...]
    pcol = pcol_ref[...]
    o_ref[...] = act(acc, _iota(acc.shape, 1) < vh, prow[0:1, :], prow[1:2, :])
    ot_ref[...] = act(acct, _iota(acct.shape, 0) < vh, pcol[:, 0:1], pcol[:, 1:2])


def _beta_decay(x_bf, w_ba, a_log, dt_bias):
    t, k = x_bf.shape
    vh = a_log.shape[0]
    tm = t if t <= 2048 else _pick(t, 1024, LANES)
    zeros = jnp.zeros((vh,), F32)
    prow = jnp.stack([jnp.concatenate([zeros, -jnp.exp(a_log)]), jnp.concatenate([zeros, dt_bias])])
    return pl.pallas_call(
        functools.partial(_beta_decay_kernel, vh=vh),
        grid=(t // tm,),
        in_specs=[pl.BlockSpec((tm, k), lambda i: (i, 0)),
                  pl.BlockSpec((k, 2 * vh), lambda i: (0, 0)),
                  pl.BlockSpec((2 * vh, k), lambda i: (0, 0)),
                  pl.BlockSpec((2, 2 * vh), lambda i: (0, 0)),
                  pl.BlockSpec((2 * vh, 2), lambda i: (0, 0))],
        out_specs=[pl.BlockSpec((tm, 2 * vh), lambda i: (i, 0)),
                   pl.BlockSpec((2 * vh, tm), lambda i: (0, i))],
        out_shape=[jax.ShapeDtypeStruct((t, 2 * vh), F32), jax.ShapeDtypeStruct((2 * vh, t), F32)],
        compiler_params=_cp(("arbitrary",)),
        name="proj_beta_decay",
    )(x_bf, w_ba.astype(BF), w_ba.T.astype(BF), prow, prow.T)


def _hgrn2_kernel(q_ref, lf_ref, k_ref, v_ref, g_ref, on_ref, s0_ref, o_ref, so_ref, s_scr, gall_scr,
                  g_scr, k_scr, v_scr, *, heads, dk, dv, sub, nsub, nseq, nchunks, group, unroll):
    c = pl.program_id(1)
    cg = sub * nsub

    @pl.when(c == 0)
    def _():
        s_scr[...] = s0_ref[...]

    rows_all = nseq * cg
    shift = sub.bit_length() - 1
    r_i = _iota((rows_all, 3 * rows_all), 0)
    c_i = _iota((rows_all, 3 * rows_all), 1) & (rows_all - 1)
    same = lax.shift_right_logical(r_i, shift) == lax.shift_right_logical(c_i, shift)
    btril3 = jnp.where((r_i >= c_i) & same, 1.0, 0.0).astype(BF)
    gall_scr[...] = _dot(btril3, _split3(lf_ref[...])) * LOG2E
    onorm = on_ref[...]
    n_aug = 16 - sub % 16
    aug_rhs = jnp.concatenate([jnp.zeros((n_aug, dv), BF), jnp.ones((n_aug, dv), BF)], axis=1)
    zeros_v = jnp.zeros((sub, dv), BF)
    zeros_aug = jnp.zeros((n_aug - 3, dk), F32)

    nh = sub // 8
    row8 = _iota((8, 1), 0)

    def block(it, carry):
        for u in range(unroll):
            sub_chunk(it * unroll + u, g_scr.at[u], k_scr.at[u], v_scr.at[u])
        return carry

    def sub_chunk(sb, g_scr, k_scr, v_scr):
        for q in range(nseq):
            rows = pl.ds(pl.multiple_of(q * cg + sb * sub, sub), sub)
            g_scr[q] = gall_scr[rows, :]
            k_scr[q] = k_ref[rows, :]
            v_scr[q] = v_ref[rows, :]
        problems = [(q, h) for q in range(nseq) for h in range(heads)]
        for g0 in range(0, len(problems), group):
            ps = []
            for (q, h) in problems[g0:g0 + group]:
                r0 = pl.multiple_of(q * cg + sb * sub, sub)
                rows = pl.ds(r0, sub)
                kc = pl.ds(h * dk, dk)
                vc = pl.ds(h * dv, dv)
                ps.append(dict(q=q, h=h, r0=r0, rows=rows, kc=kc, vc=vc,
                               gc=g_scr[q, :, kc],
                               qv=q_ref[rows, kc], k=k_ref[rows, kc], v=v_ref[rows, vc]))
            for p in ps:
                p["s_old"] = s_scr[p["q"], p["h"]]
                p["o_state"] = _dot((p["qv"] * jnp.exp2(p["gc"])).astype(BF), p["s_old"].astype(BF))
                p["o8"] = [jnp.zeros((8, dv), F32) for _ in range(nh)]
                p["q8"] = [p["qv"][8 * i:8 * i + 8] for i in range(nh)]
                p["g8"] = [p["gc"][8 * i:8 * i + 8] for i in range(nh)]
            for j in range(sub):
                for p in ps:
                    kj = k_scr[p["q"], pl.ds(j, 1), p["kc"]]
                    vj = v_scr[p["q"], pl.ds(j, 1), p["vc"]]
                    gj = g_scr[p["q"], pl.ds(j, 1), p["kc"]]
                    for i in range(j // 8, nh):
                        pr = (p["q8"][i] * kj) * jnp.exp2(p["g8"][i] - gj)
                        a_col = jnp.sum(pr, axis=-1, keepdims=True)
                        if i == j // 8:
                            a_col = jnp.where(row8 >= j % 8, a_col, 0.0)
                        p["o8"][i] = p["o8"][i] + a_col * vj
            for p in ps:
                p["o"] = p["o_state"] + (jnp.concatenate(p["o8"], axis=0) if nh > 1 else p["o8"][0])
            for p in ps:
                gend = p["gc"][sub - 1:sub, :]
                dend = jnp.exp2(gend)
                d_hi = dend.astype(BF).astype(F32)
                d_mid = (dend - d_hi).astype(BF).astype(F32)
                d_lo = (dend - d_hi - d_mid).astype(BF).astype(F32)
                kdec = p["k"] * jnp.exp2(gend - p["gc"])
                lhs = jnp.concatenate([kdec, d_hi, d_mid, d_lo, zeros_aug], axis=0).astype(BF)
                rhs = jnp.concatenate([jnp.concatenate([p["v"].astype(BF), zeros_v], axis=1), aug_rhs], axis=0)
                p["kv"] = _dot_tn(lhs, rhs)
            for p in ps:
                s_scr[p["q"], p["h"]] = p["kv"][:, dv:] * p["s_old"] + p["kv"][:, :dv]
            for p in ps:
                o = p["o"]
                ms = jnp.mean(o * o, axis=-1, keepdims=True)
                o_ref[p["rows"], p["vc"]] = (o * lax.rsqrt(ms + EPS) * onorm) * _silu(g_ref[p["rows"], p["vc"]])

    lax.fori_loop(0, nsub // unroll, block, 0)

    @pl.when(c == nchunks - 1)
    def _():
        so_ref[...] = s_scr[...]


def _hgrn2(q, lf, kf, v, gate, onorm, s0, *, row0, seq_len, rows_per_step, sub, seqs_per_step):
    nseq_total, heads, dk, dv = s0.shape
    assert dk == dv, "state decay is applied with a (dk, dv) tile"
    width = q.shape[1]
    cg = rows_per_step
    nseq = seqs_per_step
    nchunks = seq_len // cg
    assert nseq == 1 or nchunks == 1, "several sequences per step only when a step covers whole sequences"
    rows = nseq * cg
    b0 = row0 // rows
    assert sub & (sub - 1) == 0 and sub % 8 == 0 and rows & (rows - 1) == 0
    unroll = 2 if (cg // sub) % 2 == 0 else 1
    row_spec = pl.BlockSpec((rows, width), lambda b, c: (b0 + b * nchunks + c, 0))
    st_spec = pl.BlockSpec((nseq, heads, dk, dv), lambda b, c: (b, 0, 0, 0))
    kern = functools.partial(_hgrn2_kernel, heads=heads, dk=dk, dv=dv, sub=sub, nsub=cg // sub, nseq=nseq,
                             nchunks=nchunks, group=4 if sub > 8 else 8, unroll=unroll)
    return pl.pallas_call(
        kern,
        grid=(nseq_total // nseq, nchunks),
        in_specs=[row_spec, row_spec, row_spec, row_spec, row_spec,
                  pl.BlockSpec((1, dv), lambda b, c: (0, 0)), st_spec],
        out_specs=[pl.BlockSpec((rows, heads * dv), lambda b, c: (b * nchunks + c, 0)), st_spec],
        out_shape=[jax.ShapeDtypeStruct((nseq_total * seq_len, heads * dv), F32),
                   jax.ShapeDtypeStruct(s0.shape, F32)],
        scratch_shapes=[pltpu.VMEM((nseq, heads, dk, dv), F32), pltpu.VMEM((rows, width), F32),
                        pltpu.VMEM((unroll, nseq, sub, width), F32), pltpu.VMEM((unroll, nseq, sub, width), F32),
                        pltpu.VMEM((unroll, nseq, sub, heads * dv), F32)],
        compiler_params=_cp(("arbitrary", "arbitrary")),
        name="hgrn2_recurrence",
    )(q, lf, kf, v, gate, onorm.reshape(1, dv), s0)


def _gdn_kernel(qkv_ref, z_ref, bg_ref, bgt_ref, cw_ref, on_ref, s0_ref, c0_ref,
                o_ref, so_ref, co_ref, s_scr, xx,
                *, kheads, vheads, dk, dv, chunk, nsub, nseq, nchunks):
    c = pl.program_id(1)
    cg = chunk * nsub
    kdim = kheads * dk
    rep = vheads // kheads
    tail0 = 8 - (CONV_K - 1)

    @pl.when(c == 0)
    def _():
        s_scr[...] = s0_ref[...]
        xx[:, 0:8, :] = c0_ref[...]

    xs = []
    for q in range(nseq):
        xx[q, 8:8 + cg, :] = qkv_ref[q * cg:(q + 1) * cg, :]
        conv = xx[q, pl.ds(tail0, cg), :] * cw_ref[0:1, :]
        for j in range(1, CONV_K):
            conv = conv + xx[q, pl.ds(tail0 + j, cg), :] * cw_ref[j:j + 1, :]
        xs.append(_silu(conv))

    @pl.when(c == nchunks - 1)
    def _():
        for q in range(nseq):
            co_ref[q] = xx[q, pl.ds(8 + cg - (CONV_K - 1), CONV_K - 1), :]

    for q in range(nseq):
        xx[q, 0:8, :] = xx[q, cg:cg + 8, :]

    r_i = _iota((chunk, chunk), 0)
    c_i = _iota((chunk, chunk), 1)
    causal = r_i >= c_i
    strict = r_i > c_i
    cm = chunk - 1
    tril3 = jnp.where(_iota((chunk, 3 * chunk), 0) >= (_iota((chunk, 3 * chunk), 1) & cm), 1.0, 0.0).astype(BF)
    triu3 = jnp.where((_iota((3 * chunk, chunk), 0) & cm) <= _iota((3 * chunk, chunk), 1), 1.0, 0.0).astype(BF)
    onorm = on_ref[...]

    probs = []
    for q in range(nseq):
        for s in range(nsub):
            r0 = q * cg + s * chunk
            x0 = s * chunk
            bg = bg_ref[r0:r0 + chunk, :]
            gt = bgt_ref[0, vheads:2 * vheads, r0:r0 + chunk]
            gcol_all = _dot(tril3, _split3(bg[:, vheads:2 * vheads], axis=0))
            grow_all = _dot(_split3(gt, axis=1), triu3)
            for kh in range(kheads):
                qh = xs[q][x0:x0 + chunk, kh * dk:(kh + 1) * dk]
                kk_ = xs[q][x0:x0 + chunk, kdim + kh * dk:kdim + (kh + 1) * dk]
                qn = qh * lax.rsqrt(jnp.sum(qh * qh, axis=-1, keepdims=True) + EPS) * (dk ** -0.5)
                kn = kk_ * lax.rsqrt(jnp.sum(kk_ * kk_, axis=-1, keepdims=True) + EPS)
                kn_bf = kn.astype(BF)
                kk = _dot_nt(kn_bf, kn_bf)
                qk = _dot_nt(qn.astype(BF), kn_bf)
                for r in range(rep):
                    h = kh * rep + r
                    vh_ = xs[q][x0:x0 + chunk, 2 * kdim + h * dv:2 * kdim + (h + 1) * dv]
                    gcol = gcol_all[:, h:h + 1]
                    grow = grow_all[h:h + 1, :]
                    beta = bg[:, h:h + 1]
                    decay = jnp.where(causal, jnp.exp(jnp.minimum(gcol - grow, 0.0)), 0.0)
                    eg = jnp.exp(gcol)
                    gend = gcol[chunk - 1:chunk, :]
                    probs.append(dict(
                        q=q, s=s, h=h, r0=r0,
                        x=jnp.where(strict, -(beta * kk * decay), 0.0),
                        y=jnp.concatenate([vh_ * beta, kn * (beta * eg)], axis=1),
                        a_bf=(qk * decay).astype(BF),
                        qg=qn * eg,
                        kdec_bf=(kn * jnp.exp(gend - gcol)).astype(BF),
                        send=jnp.exp(gend)))

    if chunk <= 16:
        for j in range(chunk - 1):
            for p in probs:
                p["y"] = p["y"] + p["x"][:, j:j + 1] * p["y"][j:j + 1, :]
    else:
        n_fac = int(math.log2(chunk))
        for k in range(n_fac):
            for p in probs:
                p["y"] = p["y"] + _dot3(p["x"], p["y"])
            if k + 1 < n_fac:
                for p in probs:
                    p["x"] = _dot3(p["x"], p["x"])

    for s in range(nsub):
        cur = [p for p in probs if p["s"] == s]
        for p in cur:
            p["s_old"] = s_scr[p["q"], p["h"]]
            lhs = jnp.concatenate([p["y"][:, dv:], p["qg"]], axis=0).astype(BF)
            p["ws"] = _dot(lhs, p["s_old"].astype(BF))
        for p in cur:
            p["u_bf"] = (p["y"][:, :dv] - p["ws"][:chunk]).astype(BF)
        for p in cur:
            s_scr[p["q"], p["h"]] = p["send"] * p["s_old"] + _dot_tn(p["kdec_bf"], p["u_bf"])
        for p in cur:
            o = p["ws"][chunk:] + _dot(p["a_bf"], p["u_bf"])
            ms = jnp.mean(o * o, axis=-1, keepdims=True)
            zc = slice(p["h"] * dv, (p["h"] + 1) * dv)
            rows = slice(p["r0"], p["r0"] + chunk)
            o_ref[rows, zc] = (o * lax.rsqrt(ms + EPS) * onorm) * _silu(z_ref[rows, zc])

    @pl.when(c == nchunks - 1)
    def _():
        so_ref[...] = s_scr[...]


def _gdn(qkv, z, bg, bgt3, conv_w, onorm, s0, conv0, *, row0, seq_len, rows_per_step, chunk, seqs_per_step):
    nseq_total, vheads, dk, dv = s0.shape
    conv_dim = qkv.shape[1]
    kheads = (conv_dim - vheads * dv) // (2 * dk)
    cg = rows_per_step
    nseq = seqs_per_step
    nchunks = seq_len // cg
    assert nseq == 1 or nchunks == 1, "several sequences per step only when a step covers whole sequences"
    rows = nseq * cg
    b0 = row0 // rows
    conv0p = jnp.pad(conv0, ((0, 0), (8 - (CONV_K - 1), 0), (0, 0)))
    rmap = lambda b, c: (b0 + b * nchunks + c, 0)
    st_spec = pl.BlockSpec((nseq, vheads, dk, dv), lambda b, c: (b, 0, 0, 0))
    kern = functools.partial(_gdn_kernel, kheads=kheads, vheads=vheads, dk=dk, dv=dv, chunk=chunk,
                             nsub=cg // chunk, nseq=nseq, nchunks=nchunks)
    return pl.pallas_call(
        kern,
        grid=(nseq_total // nseq, nchunks),
        in_specs=[pl.BlockSpec((rows, conv_dim), rmap),
                  pl.BlockSpec((rows, vheads * dv), rmap),
                  pl.BlockSpec((rows, 2 * vheads), rmap),
                  pl.BlockSpec((1, 2 * vheads, rows), lambda b, c: (b0 + b * nchunks + c, 0, 0)),
                  pl.BlockSpec((CONV_K, conv_dim), lambda b, c: (0, 0)),
                  pl.BlockSpec((1, dv), lambda b, c: (0, 0)),
                  st_spec,
                  pl.BlockSpec((nseq, 8, conv_dim), lambda b, c: (b, 0, 0))],
        out_specs=[pl.BlockSpec((rows, vheads * dv), lambda b, c: (b * nchunks + c, 0)),
                   st_spec,
                   pl.BlockSpec((nseq, CONV_K - 1, conv_dim), lambda b, c: (b, 0, 0))],
        out_shape=[jax.ShapeDtypeStruct((nseq_total * seq_len, vheads * dv), F32),
                   jax.ShapeDtypeStruct(s0.shape, F32),
                   jax.ShapeDtypeStruct((nseq_total, CONV_K - 1, conv_dim), F32)],
        scratch_shapes=[pltpu.VMEM((nseq, vheads, dk, dv), F32), pltpu.VMEM((nseq, 8 + cg, conv_dim), F32)],
        compiler_params=_cp(("arbitrary", "arbitrary")),
        name="gdn_recurrence",
    )(qkv, z, bg, bgt3, conv_w, onorm.reshape(1, dv), s0, conv0p)


def _merge_kernel(oa1_ref, oa2_ref, ob1_ref, ob2_ref, ga_ref, gb_ref, wa_ref, wb_ref, o_ref, *, na):
    i = pl.program_id(1)

    def body(oa_ref, ob_ref):
        ya = _dot(oa_ref[...].astype(BF), wa_ref[...])
        yb = _dot(ob_ref[...].astype(BF), wb_ref[...])
        o_ref[...] = (_sigmoid(ga_ref[...]) * ya + _sigmoid(gb_ref[...]) * yb).astype(o_ref.dtype)

    @pl.when(i < na)
    def _():
        body(oa1_ref, ob1_ref)

    @pl.when(i >= na)
    def _():
        body(oa2_ref, ob2_ref)


def _merge(oa_p, oa_s, ob_p, ob_s, gates, w_pa, w_pb):
    rp, rs = oa_p.shape[0], oa_s.shape[0]
    ka, kb = oa_p.shape[1], ob_p.shape[1]
    d = w_pa.shape[1]
    tm = _pick(math.gcd(rp, rs), 512)
    tn = _pick(d, 512, LANES)
    na, nb = rp // tm, rs // tm
    nj = d // tn
    amap = lambda j, i: (jnp.minimum(i, na - 1), 0)
    bmap = lambda j, i: (jnp.maximum(i - na, 0), 0)
    return pl.pallas_call(
        functools.partial(_merge_kernel, na=na),
        grid=(nj, na + nb),
        in_specs=[pl.BlockSpec((tm, ka), amap), pl.BlockSpec((tm, ka), bmap),
                  pl.BlockSpec((tm, kb), amap), pl.BlockSpec((tm, kb), bmap),
                  pl.BlockSpec((tm, tn), lambda j, i: (i, j)),
                  pl.BlockSpec((tm, tn), lambda j, i: (i, nj + j)),
                  pl.BlockSpec((ka, tn), lambda j, i: (0, j)),
                  pl.BlockSpec((kb, tn), lambda j, i: (0, j))],
        out_specs=pl.BlockSpec((tm, tn), lambda j, i: (i, j)),
        out_shape=jax.ShapeDtypeStruct((rp + rs, d), BF),
        compiler_params=_cp(("arbitrary", "arbitrary")),
        name="branch_proj_merge",
    )(oa_p, oa_s, ob_p, ob_s, gates, gates, w_pa.astype(BF), w_pb.astype(BF))


def _outproj_kernel(m_ref, xa_ref, xb_ref, wo_ref, nw_ref, wr_ref, h_ref, xf_ref, lg_ref, *, na):
    i = pl.program_id(0)

    def body(x_ref):
        h = x_ref[...] + _dot(m_ref[...], wo_ref[...])
        h_ref[...] = h
        ms = jnp.mean(h * h, axis=-1, keepdims=True)
        xf = h * lax.rsqrt(ms + EPS) * nw_ref[...]
        xf_ref[...] = xf
        lg_ref[...] = _dot(xf.astype(BF), wr_ref[...])

    @pl.when(i < na)
    def _():
        body(xa_ref)

    @pl.when(i >= na)
    def _():
        body(xb_ref)


def _outproj(merged, xa, xb, w_out, norm_ffn, w_router):
    t, d = merged.shape
    tm = _pick(math.gcd(xa.shape[0], xb.shape[0]), 512)
    na, nb, sa, sb = _two_source_specs(xa.shape[0], xb.shape[0], tm, d)
    nr = w_router.shape[1]
    const = lambda i: (0, 0)
    return pl.pallas_call(
        functools.partial(_outproj_kernel, na=na),
        grid=(na + nb,),
        in_specs=[pl.BlockSpec((tm, d), lambda i: (i, 0)), sa, sb,
                  pl.BlockSpec((d, d), const, pipeline_mode=pl.Buffered(1)),
                  pl.BlockSpec((1, d), const),
                  pl.BlockSpec((d, nr), const, pipeline_mode=pl.Buffered(1))],
        out_specs=[pl.BlockSpec((tm, d), lambda i: (i, 0)),
                   pl.BlockSpec((tm, d), lambda i: (i, 0)),
                   pl.BlockSpec((tm, nr), lambda i: (i, 0))],
        out_shape=[jax.ShapeDtypeStruct((t, d), F32), jax.ShapeDtypeStruct((t, d), F32),
                   jax.ShapeDtypeStruct((t, nr), F32)],
        compiler_params=_cp(("arbitrary",), 56),
        name="out_proj_ffn_norm",
    )(merged, xa, xb, w_out.astype(BF), norm_ffn.reshape(1, d), w_router)


def _route_kernel(lg_ref, b_ref, eid_ref, gate_ref, rank_ref, cnt_ref, carry,
                  *, ngroups, nexp, tm):
    i = pl.program_id(0)

    @pl.when(i == 0)
    def _():
        carry[...] = jnp.zeros_like(carry)

    per_group = nexp // ngroups
    lg = lg_ref[...] + b_ref[...]
    lane = _iota(lg.shape, 1)
    big = jnp.int32(1 << 20)
    is_g = lane < ngroups
    gl = jnp.where(is_g, lg, NEG)
    gmax = jnp.max(gl, axis=-1, keepdims=True)
    gidx = jnp.min(jnp.where(gl == gmax, lane, big), axis=-1, keepdims=True)
    gsum = jnp.sum(jnp.where(is_g, jnp.exp(gl - gmax), 0.0), axis=-1, keepdims=True)
    gw = 1.0 / gsum
    elane = lane - ngroups
    in_grp = (elane >= gidx * per_group) & (elane < (gidx + 1) * per_group)
    el = jnp.where(in_grp, lg, NEG)
    v1 = jnp.max(el, axis=-1, keepdims=True)
    i1 = jnp.min(jnp.where(in_grp & (el == v1), elane, big), axis=-1, keepdims=True)
    in2 = in_grp & (elane != i1)
    el2 = jnp.where(in2, lg, NEG)
    v2 = jnp.max(el2, axis=-1, keepdims=True)
    i2 = jnp.min(jnp.where(in2 & (el2 == v2), elane, big), axis=-1, keepdims=True)
    p2 = jnp.exp(v2 - v1)
    den = 1.0 + p2
    lane2 = _iota((tm, TOP_K), 1)
    eid_ref[...] = jnp.where(lane2 == 0, i1, i2)
    gate_ref[...] = jnp.where(lane2 == 0, gw / den, gw * p2 / den)

    oh1 = (elane == i1).astype(F32)
    oh2 = (elane == i2).astype(F32)
    lower = (_iota((tm, tm), 0) > _iota((tm, tm), 1)).astype(BF)
    cs1 = _dot(lower, oh1.astype(BF))
    cs2 = _dot(lower, oh2.astype(BF))
    tot1 = jnp.sum(oh1, axis=0, keepdims=True)
    tot2 = jnp.sum(oh2, axis=0, keepdims=True)
    base = carry[0:1, :]
    r1 = jnp.sum(oh1 * (base + cs1), axis=-1, keepdims=True)
    r2 = jnp.sum(oh2 * (base + tot1 + cs2), axis=-1, keepdims=True)
    rank_ref[...] = jnp.where(lane2 == 0, r1, r2).astype(I32)
    new = base + tot1 + tot2
    carry[...] = jnp.broadcast_to(new, carry.shape)
    cnt_ref[...] = jnp.broadcast_to(new, cnt_ref.shape)


def _route(logits, bias_row, ngroups, nexp):
    t, nr = logits.shape
    tm = _pick(t, 512)
    kern = functools.partial(_route_kernel, ngroups=ngroups, nexp=nexp, tm=tm)
    return pl.pallas_call(
        kern,
        grid=(t // tm,),
        in_specs=[pl.BlockSpec((tm, nr), lambda i: (i, 0)), pl.BlockSpec((1, nr), lambda i: (0, 0))],
        out_specs=[pl.BlockSpec((tm, TOP_K), lambda i: (i, 0)),
                   pl.BlockSpec((tm, TOP_K), lambda i: (i, 0)),
                   pl.BlockSpec((tm, TOP_K), lambda i: (i, 0)),
                   pl.BlockSpec((8, nr), lambda i: (0, 0))],
        out_shape=[jax.ShapeDtypeStruct((t, TOP_K), I32), jax.ShapeDtypeStruct((t, TOP_K), F32),
                   jax.ShapeDtypeStruct((t, TOP_K), I32), jax.ShapeDtypeStruct((8, nr), F32)],
        scratch_shapes=[pltpu.VMEM((8, nr), F32)],
        compiler_params=_cp(("arbitrary",)),
        name="route_topk_rank",
    )(logits, bias_row)


def _plan_kernel(cnt_ref, eid_ref, rank_ref, pos_ref, be_ref, *, ngroups, nexp, nblocks_pad, tm):
    nr = cnt_ref.shape[1]
    cnt = cnt_ref[0:1, :]
    padded = jnp.floor((cnt + (MOE_BLOCK - 1)) * (1.0 / MOE_BLOCK)) * MOE_BLOCK
    r_i = _iota((nr, nr), 0)
    c_i = _iota((nr, nr), 1)
    padded_col = jnp.sum(jnp.where(r_i == c_i, jnp.broadcast_to(padded, (nr, nr)), 0.0), axis=1, keepdims=True)
    start = jnp.sum(jnp.where(r_i < c_i, jnp.broadcast_to(padded_col, (nr, nr)), 0.0), axis=0, keepdims=True)
    end = start + padded
    lane = _iota((tm, nr), 1)
    eid = eid_ref[...]
    rank = rank_ref[...]
    lane2 = _iota((tm, TOP_K), 1)
    pos = jnp.zeros((tm, TOP_K), F32)
    for k in range(TOP_K):
        oh = (lane - ngroups) == eid[:, k:k + 1]
        st = jnp.sum(jnp.where(oh, start, 0.0), axis=-1, keepdims=True)
        pos = jnp.where(lane2 == k, st, pos)
    pos_ref[...] = pos.astype(I32) + rank

    @pl.when(pl.program_id(0) == 0)
    def _():
        blk_row = _iota((nblocks_pad, nr), 0).astype(F32) * MOE_BLOCK
        lane_b = _iota((nblocks_pad, nr), 1)
        is_e = (lane_b >= ngroups) & (lane_b < ngroups + nexp)
        n_le = jnp.sum(jnp.where(is_e & (end <= blk_row), 1.0, 0.0), axis=-1, keepdims=True)
        be_ref[...] = jnp.minimum(n_le, nexp - 1.0).astype(I32)


def _plan(cnt, eid, rank, ngroups, nexp, nblocks):
    t = eid.shape[0]
    nr = cnt.shape[1]
    tm = _pick(t, 512)
    nblocks_pad = -(-nblocks // 8) * 8
    kern = functools.partial(_plan_kernel, ngroups=ngroups, nexp=nexp, nblocks_pad=nblocks_pad, tm=tm)
    return pl.pallas_call(
        kern,
        grid=(t // tm,),
        in_specs=[pl.BlockSpec((8, nr), lambda i: (0, 0)),
                  pl.BlockSpec((tm, TOP_K), lambda i: (i, 0)),
                  pl.BlockSpec((tm, TOP_K), lambda i: (i, 0))],
        out_specs=[pl.BlockSpec((tm, TOP_K), lambda i: (i, 0)),
                   pl.BlockSpec((nblocks_pad, 1), lambda i: (0, 0))],
        out_shape=[jax.ShapeDtypeStruct((t, TOP_K), I32), jax.ShapeDtypeStruct((nblocks_pad, 1), I32)],
        compiler_params=_cp(("arbitrary",)),
        name="route_plan",
    )(cnt, eid, rank)


def _invert_kernel(pos_ref, tok_ref, *, n_assign, n_rows):
    def clear(r, c):
        tok_ref[r] = 0
        return c

    lax.fori_loop(0, n_rows, clear, 0, unroll=16)

    def put(a, c):
        tok_ref[pos_ref[a]] = lax.shift_right_logical(a, TOP_K.bit_length() - 1)
        return c

    lax.fori_loop(0, n_assign, put, 0, unroll=16)


def _invert(pos_flat, n_rows):
    n_assign = pos_flat.shape[0]
    return pl.pallas_call(
        functools.partial(_invert_kernel, n_assign=n_assign, n_rows=n_rows),
        in_specs=[pl.BlockSpec(memory_space=pltpu.SMEM)],
        out_specs=pl.BlockSpec(memory_space=pltpu.SMEM),
        out_shape=jax.ShapeDtypeStruct((n_rows,), I32),
        name="route_invert",
    )(pos_flat)


def _expert_kernel(be_ref, tok_ref, x_hbm, wg_ref, wu_ref, wd_ref, y_ref, xbuf, sem, wg_bf, wu_bf, wd_bf):
    i = pl.program_id(0)
    n = pl.num_programs(0)
    slot = i % 2

    def row_copy(blk, slot_, r):
        t = tok_ref[blk * MOE_BLOCK + r]
        return pltpu.make_async_copy(x_hbm.at[pl.ds(t, 1), :], xbuf.at[slot_, pl.ds(r, 1), :], sem.at[slot_])

    @pl.when(i == 0)
    def _():
        for r in range(MOE_BLOCK):
            row_copy(0, 0, r).start()

    @pl.when(i + 1 < n)
    def _():
        for r in range(MOE_BLOCK):
            row_copy(i + 1, 1 - slot, r).start()

    prev = be_ref[jnp.maximum(i - 1, 0)]

    @pl.when((i == 0) | (be_ref[i] != prev))
    def _():
        wg_bf[...] = wg_ref[0].astype(BF)
        wu_bf[...] = wu_ref[0].astype(BF)
        wd_bf[...] = wd_ref[0].astype(BF)

    for r in range(MOE_BLOCK):
        row_copy(i, slot, r).wait()
    x = xbuf[slot].astype(BF)
    hid = _silu(_dot(x, wg_bf[...])) * _dot(x, wu_bf[...])
    y_ref[...] = _dot(hid.astype(BF), wd_bf[...])


def _experts(block_e, row_tok, xf, w_eg, w_eu, w_ed):
    nrows = row_tok.shape[0]
    d = xf.shape[1]
    nblocks = nrows // MOE_BLOCK
    de = w_eg.shape[2]
    grid_spec = pltpu.PrefetchScalarGridSpec(
        num_scalar_prefetch=2,
        grid=(nblocks,),
        in_specs=[pl.BlockSpec(memory_space=pl.ANY),
                  pl.BlockSpec((1, d, de), lambda i, be, tok: (be[i], 0, 0)),
                  pl.BlockSpec((1, d, de), lambda i, be, tok: (be[i], 0, 0)),
                  pl.BlockSpec((1, de, d), lambda i, be, tok: (be[i], 0, 0))],
        out_specs=pl.BlockSpec((MOE_BLOCK, d), lambda i, be, tok: (i, 0)),
        scratch_shapes=[pltpu.VMEM((2, MOE_BLOCK, d), F32), pltpu.SemaphoreType.DMA((2,)),
                        pltpu.VMEM((d, de), BF), pltpu.VMEM((d, de), BF), pltpu.VMEM((de, d), BF)],
    )
    return pl.pallas_call(
        _expert_kernel,
        grid_spec=grid_spec,
        out_shape=jax.ShapeDtypeStruct((nrows, d), F32),
        compiler_params=_cp(("arbitrary",), 56),
        name="expert_blocks",
    )(block_e, row_tok, xf, w_eg, w_eu, w_ed)


def _final_kernel(pos_ref, h_ref, gate_ref, w_ref, yb_hbm, ya_ref, ys_ref, gbuf, sem, *, na, tm):
    i = pl.program_id(0)
    n = pl.num_programs(0)
    slot = i % 2

    def row_copy(tile, slot_, r, k):
        p = pos_ref[(tile * tm + r) * TOP_K + k]
        return pltpu.make_async_copy(yb_hbm.at[pl.ds(p, 1), :], gbuf.at[slot_, k, pl.ds(r, 1), :], sem.at[slot_])

    def start_tile(tile, slot_):
        for r in range(tm):
            for k in range(TOP_K):
                row_copy(tile, slot_, r, k).start()

    @pl.when(i == 0)
    def _():
        start_tile(0, 0)

    @pl.when(i + 1 < n)
    def _():
        start_tile(i + 1, 1 - slot)

    for r in range(tm):
        for k in range(TOP_K):
            row_copy(i, slot, r, k).wait()

    gate = gate_ref[...]
    y = h_ref[...]
    ffn = gbuf[slot, 0] * gate[:, 0:1]
    for k in range(1, TOP_K):
        ffn = ffn + gbuf[slot, k] * gate[:, k:k + 1]
    y = y + ffn
    ms = jnp.mean(y * y, axis=-1, keepdims=True)
    out = y * lax.rsqrt(ms + EPS) * w_ref[...]

    @pl.when(i < na)
    def _():
        ya_ref[...] = out

    @pl.when(i >= na)
    def _():
        ys_ref[...] = out


def _final(pos_flat, h, gate, yb, w, rows_a, rows_b):
    t, d = h.shape
    tm = _pick(math.gcd(rows_a, rows_b), 128)
    na, nb = rows_a // tm, rows_b // tm
    grid_spec = pltpu.PrefetchScalarGridSpec(
        num_scalar_prefetch=1,
        grid=(na + nb,),
        in_specs=[pl.BlockSpec((tm, d), lambda i, pos: (i, 0)),
                  pl.BlockSpec((tm, TOP_K), lambda i, pos: (i, 0)),
                  pl.BlockSpec((1, d), lambda i, pos: (0, 0)),
                  pl.BlockSpec(memory_space=pl.ANY)],
        out_specs=[pl.BlockSpec((tm, d), lambda i, pos: (jnp.minimum(i, na - 1), 0)),
                   pl.BlockSpec((tm, d), lambda i, pos: (jnp.maximum(i - na, 0), 0))],
        scratch_shapes=[pltpu.VMEM((2, TOP_K, tm, d), F32), pltpu.SemaphoreType.DMA((2,))],
    )
    return pl.pallas_call(
        functools.partial(_final_kernel, na=na, tm=tm),
        grid_spec=grid_spec,
        out_shape=[jax.ShapeDtypeStruct((rows_a, d), F32), jax.ShapeDtypeStruct((rows_b, d), F32)],
        compiler_params=_cp(("arbitrary",)),
        name="combine_final_norm",
    )(pos_flat, h, gate, w.reshape(1, d), yb)


def kernel(x_prompt, x_sample, state_hgrn, state_gdn, state_conv, norm_mix, w_in, conv_w, ha_lb_logits, ha_onorm, w_pa, gd_A_log, gd_dt_bias, gd_onorm, w_pb, w_out, norm_ffn, w_router_group, b_router_group, w_router_expert, b_router_expert, w_exp_gate, w_exp_up, w_exp_down, norm_final):
    depth = state_hgrn.shape[0]
    assert depth == 1, "one decoder layer"
    bp, lp, d = x_prompt.shape
    bs, ls, _ = x_sample.shape
    _, _, ha_heads, ha_dk, ha_dv = state_hgrn.shape
    _, _, vheads, gd_dk, gd_dv = state_gdn.shape
    conv_dim = state_conv.shape[3]
    ha_kdim, ha_vdim = ha_heads * ha_dk, ha_heads * ha_dv
    gd_vdim = vheads * gd_dv
    ngroups = w_router_group.shape[2]
    nexp = w_router_expert.shape[2]
    rp, rs = bp * lp, bs * ls
    t = rp + rs

    off_f = ha_kdim
    off_i = off_f + ha_kdim
    off_g = off_i + ha_vdim
    off_qkv = off_g + ha_vdim
    off_z = off_qkv + conv_dim
    off_b = off_z + gd_vdim
    off_gate = off_b + 2 * vheads
    w_in0 = w_in[0]

    xn = _rmsnorm_bf16(x_prompt.reshape(rp, d), x_sample.reshape(rs, d), norm_mix[0])

    ident = lambda a: (a,)
    (q_a,) = _proj(xn, w_in0, 0, ha_kdim, lambda a: (a * (ha_dk ** -0.5),), [F32], name="proj_hgrn_q")
    logf, k_a = _proj(xn, w_in0, off_f, ha_kdim, _forget_epilogue, [F32, F32],
                      extras=(ha_lb_logits,), name="proj_hgrn_forget")
    (v_a,) = _proj(xn, w_in0, off_i, ha_vdim, ident, [F32], name="proj_hgrn_in")
    (g_a,) = _proj(xn, w_in0, off_g, ha_vdim, ident, [F32], name="proj_hgrn_gate")
    (qkv,) = _proj(xn, w_in0, off_qkv, conv_dim, ident, [F32], name="proj_gdn_qkv")
    (z_b,) = _proj(xn, w_in0, off_z, gd_vdim, ident, [F32], name="proj_gdn_z")
    (gates,) = _proj(xn, w_in0[:, off_gate:], 0, 2 * d, ident, [F32], name="proj_merge_gates")
    bg, bgt = _beta_decay(xn, w_in0[:, off_b:off_gate], gd_A_log[0], gd_dt_bias[0])

    cg_p = _pick(lp, 128, GDN_CHUNK) if lp >= GDN_CHUNK else lp
    sub_p = min(HGRN_SUB, lp)
    chunk_p = min(GDN_CHUNK, lp)
    cg_s = ls
    zeros_h = jnp.zeros((bp, ha_heads, ha_dk, ha_dv), F32)
    zeros_g = jnp.zeros((bp, vheads, gd_dk, gd_dv), F32)
    zeros_c = jnp.zeros((bp, CONV_K - 1, conv_dim), F32)

    oa_p, sh_p = _hgrn2(q_a, logf, k_a, v_a, g_a, ha_onorm[0], zeros_h,
                        row0=0, seq_len=lp, rows_per_step=cg_p, sub=sub_p, seqs_per_step=1)
    oa_s, sh_s = _hgrn2(q_a, logf, k_a, v_a, g_a, ha_onorm[0], state_hgrn[0],
                        row0=rp, seq_len=ls, rows_per_step=cg_s, sub=min(HGRN_SUB, ls),
                        seqs_per_step=math.gcd(_pick(bs, HGRN_SAMPLE_SEQS, 1), max(rp // cg_s, 1)))

    def time_on_lanes(rows0, nrows, cg):
        part = bgt[:, rows0:rows0 + nrows].reshape(2 * vheads, nrows // cg, cg)
        return jnp.transpose(part, (1, 0, 2))

    seqs_s = _pick(bs, GDN_SAMPLE_SEQS, 1)
    bgt_p = time_on_lanes(0, rp, cg_p)
    bgt_s = time_on_lanes(rp, rs, seqs_s * cg_s)
    ob_p, sg_p, sc_p = _gdn(qkv, z_b, bg, bgt_p, conv_w[0], gd_onorm[0], zeros_g, zeros_c,
                            row0=0, seq_len=lp, rows_per_step=cg_p, chunk=chunk_p, seqs_per_step=1)
    ob_s, sg_s, sc_s = _gdn(qkv[rp:], z_b[rp:], bg[rp:], bgt_s, conv_w[0], gd_onorm[0], state_gdn[0],
                            state_conv[0], row0=0, seq_len=ls, rows_per_step=cg_s, chunk=min(GDN_CHUNK, ls),
                            seqs_per_step=seqs_s)

    merged = _merge(oa_p, oa_s, ob_p, ob_s, gates, w_pa[0], w_pb[0])

    nr = LANES
    w_router = jnp.concatenate([w_router_group[0], w_router_expert[0]], axis=1)
    w_router = jnp.pad(w_router, ((0, 0), (0, nr - ngroups - nexp))).astype(BF)
    b_router = jnp.pad(jnp.concatenate([b_router_group[0], b_router_expert[0]]), (0, nr - ngroups - nexp))
    h, xf, logits = _outproj(merged, x_prompt.reshape(rp, d), x_sample.reshape(rs, d), w_out[0],
                             norm_ffn[0], w_router)

    eid, gate, rank, cnt = _route(logits, b_router.reshape(1, nr), ngroups, nexp)
    nblocks = -(-(t * TOP_K) // MOE_BLOCK) + nexp
    pos, block_e = _plan(cnt, eid, rank, ngroups, nexp, nblocks)
    block_e = block_e[:nblocks, 0]

    pos_flat = pos.reshape(-1)
    row_tok = _invert(pos_flat, nblocks * MOE_BLOCK)
    yb = _experts(block_e, row_tok, xf, w_exp_gate[0], w_exp_up[0], w_exp_down[0])
    y_p, y_s = _final(pos_flat, h, gate, yb, norm_final, rp, rs)
    return (y_p.reshape(bp, lp, d), y_s.reshape(bs, ls, d),
            sh_p[None], sg_p[None], sc_p[None], sh_s[None], sg_s[None], sc_s[None])
```

```python
import functools
import math

import jax
import jax.numpy as jnp
from jax import lax
from jax.experimental import pallas as pl
from jax.experimental.pallas import tpu as pltpu

F32 = jnp.float32
BF = jnp.bfloat16
I32 = jnp.int32
EPS = 1e-6
CONV_K = 4
TOP_K = 2
MOE_BLOCK = 128
GDN_CHUNK = 64
GDN_SAMPLE_SEQS = 4
HGRN_SUB = 16
HGRN_SAMPLE_SEQS = 4
LOG2E = 1.4426950408889634
LANES = 128
NEG = -3.0e38
HI = lax.Precision.HIGHEST
MIB = 1024 * 1024


def _cp(sem, vmem_mib=48):
    return pltpu.CompilerParams(dimension_semantics=sem, vmem_limit_bytes=vmem_mib * MIB)


def _pick(n, pref, mult=8):
    best = None
    for t in range(mult, min(n, pref) + 1, mult):
        if n % t == 0:
            best = t
    return best if best is not None else n


def _sigmoid(x):
    return 1.0 / (1.0 + jnp.exp(-x))


def _silu(x):
    return x * _sigmoid(x)


def _iota(shape, dim):
    return lax.broadcasted_iota(I32, shape, dim)


def _dot(a, b):
    return jnp.dot(a, b, preferred_element_type=F32)


def _dot_nt(a, b):
    return lax.dot_general(a, b, (((1,), (1,)), ((), ())), preferred_element_type=F32)


def _dot_tn(a, b):
    return lax.dot_general(a, b, (((0,), (0,)), ((), ())), preferred_element_type=F32)


def _split(a):
    hi = a.astype(BF)
    lo = (a - hi.astype(F32)).astype(BF)
    return hi, lo


def _split3(a, axis=0):
    hi = a.astype(BF)
    r1 = a - hi.astype(F32)
    mid = r1.astype(BF)
    lo = (r1 - mid.astype(F32)).astype(BF)
    return jnp.concatenate([hi, mid, lo], axis=axis)


def _dot3(a, b):
    ah, al = _split(a)
    bh, bl = _split(b)
    return _dot(ah, bh) + (_dot(ah, bl) + _dot(al, bh))


def _two_source_specs(rows_a, rows_b, tm, width):
    na, nb = rows_a // tm, rows_b // tm
    spec_a = pl.BlockSpec((tm, width), lambda i: (jnp.minimum(i, na - 1), 0))
    spec_b = pl.BlockSpec((tm, width), lambda i: (jnp.maximum(i - na, 0), 0))
    return na, nb, spec_a, spec_b


def _rmsnorm_kernel(xa_ref, xb_ref, w_ref, o_ref, *, na):
    i = pl.program_id(0)

    def body(x_ref):
        x = x_ref[...]
        ms = jnp.mean(x * x, axis=-1, keepdims=True)
        o_ref[...] = (x * lax.rsqrt(ms + EPS) * w_ref[...]).astype(o_ref.dtype)

    @pl.when(i < na)
    def _():
        body(xa_ref)

    @pl.when(i >= na)
    def _():
        body(xb_ref)


def _rmsnorm_bf16(xa, xb, w):
    d = xa.shape[1]
    tm = _pick(math.gcd(xa.shape[0], xb.shape[0]), 512)
    na, nb, sa, sb = _two_source_specs(xa.shape[0], xb.shape[0], tm, d)
    return pl.pallas_call(
        functools.partial(_rmsnorm_kernel, na=na),
        grid=(na + nb,),
        in_specs=[sa, sb, pl.BlockSpec((1, d), lambda i: (0, 0))],
        out_specs=pl.BlockSpec((tm, d), lambda i: (i, 0)),
        out_shape=jax.ShapeDtypeStruct((xa.shape[0] + xb.shape[0], d), BF),
        compiler_params=_cp(("arbitrary",)),
        name="rmsnorm_mix",
    )(xa, xb, w.reshape(1, d))


def _proj_kernel(x_ref, w_ref, *rest, n_extra, n_out, epilogue):
    extra = rest[:n_extra]
    outs = rest[n_extra:n_extra + n_out]
    wbf = rest[n_extra + n_out]

    @pl.when(pl.program_id(1) == 0)
    def _():
        wbf[...] = w_ref[...].astype(BF)

    acc = _dot(x_ref[...], wbf[...])
    vals = epilogue(acc, *[e[...] for e in extra])
    for o_ref, v in zip(outs, vals):
        o_ref[...] = v.astype(o_ref.dtype)


def _proj(x_bf, w, col0, ncols, epilogue, out_dtypes, extras=(), name="proj"):
    t, k = x_bf.shape
    tn = _pick(math.gcd(ncols, col0) if col0 else ncols, 512, LANES)
    tm = _pick(t, 1024)
    j0 = col0 // tn
    n_out = len(out_dtypes)
    kern = functools.partial(_proj_kernel, n_extra=len(extras), n_out=n_out, epilogue=epilogue)
    return pl.pallas_call(
        kern,
        grid=(ncols // tn, t // tm),
        in_specs=[pl.BlockSpec((tm, k), lambda j, i: (i, 0)),
                  pl.BlockSpec((k, tn), lambda j, i: (0, j0 + j))]
                 + [pl.BlockSpec((e.shape[0], tn), lambda j, i: (0, j)) for e in extras],
        out_specs=[pl.BlockSpec((tm, tn), lambda j, i: (i, j)) for _ in out_dtypes],
        out_shape=[jax.ShapeDtypeStruct((t, ncols), dt) for dt in out_dtypes],
        scratch_shapes=[pltpu.VMEM((k, tn), BF)],
        compiler_params=_cp(("arbitrary", "arbitrary")),
        name=name,
    )(x_bf, w, *extras)


def _forget_epilogue(acc, lb_logits):
    m = jnp.max(lb_logits, axis=0, keepdims=True)
    e = jnp.exp(lb_logits - m)
    lb = e[0:1, :] / jnp.sum(e, axis=0, keepdims=True)
    f = lb + (1.0 - lb) * _sigmoid(acc)
    return jnp.log(f), 1.0 - f


def _beta_decay_kernel(x_ref, w_ref, wt_ref, prow_ref, pcol_ref, o_ref, ot_ref, *, vh):
    x = x_ref[...]
    acc = _dot(x, w_ref[...])
    acct = _dot_nt(wt_ref[...], x)

    def act(a, is_beta, a_neg_exp, dt_bias):
        z = a + dt_bias
        softplus = jnp.maximum(z, 0.0) + jnp.log(1.0 + jnp.exp(-jnp.abs(z)))
        return jnp.where(is_beta, _sigmoid(a), a_neg_exp * softplus)

    prow = prow_ref[...]
    pcol = pcol_ref[...]
    o_ref[...] = act(acc, _iota(acc.shape, 1) < vh, prow[0:1, :], prow[1:2, :])
    ot_ref[...] = act(acct, _iota(acct.shape, 0) < vh, pcol[:, 0:1], pcol[:, 1:2])


def _beta_decay(x_bf, w_ba, a_log, dt_bias):
    t, k = x_bf.shape
    vh = a_log.shape[0]
    tm = t if t <= 2048 else _pick(t, 1024, LANES)
    zeros = jnp.zeros((vh,), F32)
    prow = jnp.stack([jnp.concatenate([zeros, -jnp.exp(a_log)]), jnp.concatenate([zeros, dt_bias])])
    return pl.pallas_call(
        functools.partial(_beta_decay_kernel, vh=vh),
        grid=(t // tm,),
        in_specs=[pl.BlockSpec((tm, k), lambda i: (i, 0)),
                  pl.BlockSpec((k, 2 * vh), lambda i: (0, 0)),
                  pl.BlockSpec((2 * vh, k), lambda i: (0, 0)),
                  pl.BlockSpec((2, 2 * vh), lambda i: (0, 0)),
                  pl.BlockSpec((2 * vh, 2), lambda i: (0, 0))],
        out_specs=[pl.BlockSpec((tm, 2 * vh), lambda i: (i, 0)),
                   pl.BlockSpec((2 * vh, tm), lambda i: (0, i))],
        out_shape=[jax.ShapeDtypeStruct((t, 2 * vh), F32), jax.ShapeDtypeStruct((2 * vh, t), F32)],
        compiler_params=_cp(("arbitrary",)),
        name="proj_beta_decay",
    )(x_bf, w_ba.astype(BF), w_ba.T.astype(BF), prow, prow.T)


def _hgrn2_kernel(q_ref, lf_ref, k_ref, v_ref, g_ref, on_ref, s0_ref, o_ref, so_ref, s_scr, gall_scr,
                  g_scr, k_scr, v_scr, *, heads, dk, dv, sub, nsub, nseq, nchunks, group, unroll):
    c = pl.program_id(1)
    cg = sub * nsub

    @pl.when(c == 0)
    def _():
        s_scr[...] = s0_ref[...]

    rows_all = nseq * cg
    shift = sub.bit_length() - 1
    r_i = _iota((rows_all, 3 * rows_all), 0)
    c_i = _iota((rows_all, 3 * rows_all), 1) & (rows_all - 1)
    same = lax.shift_right_logical(r_i, shift) == lax.shift_right_logical(c_i, shift)
    btril3 = jnp.where((r_i >= c_i) & same, 1.0, 0.0).astype(BF)
    gall_scr[...] = _dot(btril3, _split3(lf_ref[...])) * LOG2E
    onorm = on_ref[...]
    n_aug = 16 - sub % 16
    aug_rhs = jnp.concatenate([jnp.zeros((n_aug, dv), BF), jnp.ones((n_aug, dv), BF)], axis=1)
    zeros_v = jnp.zeros((sub, dv), BF)
    zeros_aug = jnp.zeros((n_aug - 3, dk), F32)

    nh = sub // 8
    row8 = _iota((8, 1), 0)

    def block(it, carry):
        for u in range(unroll):
            sub_chunk(it * unroll + u, g_scr.at[u], k_scr.at[u], v_scr.at[u])
        return carry

    def sub_chunk(sb, g_scr, k_scr, v_scr):
        for q in range(nseq):
            rows = pl.ds(pl.multiple_of(q * cg + sb * sub, sub), sub)
            g_scr[q] = gall_scr[rows, :]
            k_scr[q] = k_ref[rows, :]
            v_scr[q] = v_ref[rows, :]
        problems = [(q, h) for q in range(nseq) for h in range(heads)]
        for g0 in range(0, len(problems), group):
            ps = []
            for (q, h) in problems[g0:g0 + group]:
                r0 = pl.multiple_of(q * cg + sb * sub, sub)
                rows = pl.ds(r0, sub)
                kc = pl.ds(h * dk, dk)
                vc = pl.ds(h * dv, dv)
                ps.append(dict(q=q, h=h, r0=r0, rows=rows, kc=kc, vc=vc,
                               gc=g_scr[q, :, kc],
                               qv=q_ref[rows, kc], k=k_ref[rows, kc], v=v_ref[rows, vc]))
            for p in ps:
                p["s_old"] = s_scr[p["q"], p["h"]]
                p["o_state"] = _dot((p["qv"] * jnp.exp2(p["gc"])).astype(BF), p["s_old"].astype(BF))
                p["o8"] = [jnp.zeros((8, dv), F32) for _ in range(nh)]
                p["q8"] = [p["qv"][8 * i:8 * i + 8] for i in range(nh)]
                p["g8"] = [p["gc"][8 * i:8 * i + 8] for i in range(nh)]
            for j in range(sub):
                for p in ps:
                    kj = k_scr[p["q"], pl.ds(j, 1), p["kc"]]
                    vj = v_scr[p["q"], pl.ds(j, 1), p["vc"]]
                    gj = g_scr[p["q"], pl.ds(j, 1), p["kc"]]
                    for i in range(j // 8, nh):
                        pr = (p["q8"][i] * kj) * jnp.exp2(p["g8"][i] - gj)
                        a_col = jnp.sum(pr, axis=-1, keepdims=True)
                        if i == j // 8:
                            a_col = jnp.where(row8 >= j % 8, a_col, 0.0)
                        p["o8"][i] = p["o8"][i] + a_col * vj
            for p in ps:
                p["o"] = p["o_state"] + (jnp.concatenate(p["o8"], axis=0) if nh > 1 else p["o8"][0])
            for p in ps:
                gend = p["gc"][sub - 1:sub, :]
                dend = jnp.exp2(gend)
                d_hi = dend.astype(BF).astype(F32)
                d_mid = (dend - d_hi).astype(BF).astype(F32)
                d_lo = (dend - d_hi - d_mid).astype(BF).astype(F32)
                kdec = p["k"] * jnp.exp2(gend - p["gc"])
                lhs = jnp.concatenate([kdec, d_hi, d_mid, d_lo, zeros_aug], axis=0).astype(BF)
                rhs = jnp.concatenate([jnp.concatenate([p["v"].astype(BF), zeros_v], axis=1), aug_rhs], axis=0)
                p["kv"] = _dot_tn(lhs, rhs)
            for p in ps:
                s_scr[p["q"], p["h"]] = p["kv"][:, dv:] * p["s_old"] + p["kv"][:, :dv]
            for p in ps:
                o = p["o"]
                ms = jnp.mean(o * o, axis=-1, keepdims=True)
                o_ref[p["rows"], p["vc"]] = (o * lax.rsqrt(ms + EPS) * onorm) * _silu(g_ref[p["rows"], p["vc"]])

    lax.fori_loop(0, nsub // unroll, block, 0)

    @pl.when(c == nchunks - 1)
    def _():
        so_ref[...] = s_scr[...]


def _hgrn2(q, lf, kf, v, gate, onorm, s0, *, row0, seq_len, rows_per_step, sub, seqs_per_step):
    nseq_total, heads, dk, dv = s0.shape
    assert dk == dv, "state decay is applied with a (dk, dv) tile"
    width = q.shape[1]
    cg = rows_per_step
    nseq = seqs_per_step
    nchunks = seq_len // cg
    assert nseq == 1 or nchunks == 1, "several sequences per step only when a step covers whole sequences"
    rows = nseq * cg
    b0 = row0 // rows
    assert sub & (sub - 1) == 0 and sub % 8 == 0 and rows & (rows - 1) == 0
    unroll = 2 if (cg // sub) % 2 == 0 else 1
    row_spec = pl.BlockSpec((rows, width), lambda b, c: (b0 + b * nchunks + c, 0))
    st_spec = pl.BlockSpec((nseq, heads, dk, dv), lambda b, c: (b, 0, 0, 0))
    kern = functools.partial(_hgrn2_kernel, heads=heads, dk=dk, dv=dv, sub=sub, nsub=cg // sub, nseq=nseq,
                             nchunks=nchunks, group=4 if sub > 8 else 8, unroll=unroll)
    return pl.pallas_call(
        kern,
        grid=(nseq_total // nseq, nchunks),
        in_specs=[row_spec, row_spec, row_spec, row_spec, row_spec,
                  pl.BlockSpec((1, dv), lambda b, c: (0, 0)), st_spec],
        out_specs=[pl.BlockSpec((rows, heads * dv), lambda b, c: (b * nchunks + c, 0)), st_spec],
        out_shape=[jax.ShapeDtypeStruct((nseq_total * seq_len, heads * dv), F32),
                   jax.ShapeDtypeStruct(s0.shape, F32)],
        scratch_shapes=[pltpu.VMEM((nseq, heads, dk, dv), F32), pltpu.VMEM((rows, width), F32),
                        pltpu.VMEM((unroll, nseq, sub, width), F32), pltpu.VMEM((unroll, nseq, sub, width), F32),
                        pltpu.VMEM((unroll, nseq, sub, heads * dv), F32)],
        compiler_params=_cp(("arbitrary", "arbitrary")),
        name="hgrn2_recurrence",
    )(q, lf, kf, v, gate, onorm.reshape(1, dv), s0)


def _gdn_kernel(qkv_ref, z_ref, bg_ref, bgt_ref, cw_ref, on_ref, s0_ref, c0_ref,
                o_ref, so_ref, co_ref, s_scr, xx,
                *, kheads, vheads, dk, dv, chunk, nsub, nseq, nchunks):
    c = pl.program_id(1)
    cg = chunk * nsub
    kdim = kheads * dk
    rep = vheads // kheads
    tail0 = 8 - (CONV_K - 1)

    @pl.when(c == 0)
    def _():
        s_scr[...] = s0_ref[...]
        xx[:, 0:8, :] = c0_ref[...]

    xs = []
    for q in range(nseq):
        xx[q, 8:8 + cg, :] = qkv_ref[q * cg:(q + 1) * cg, :]
        conv = xx[q, pl.ds(tail0, cg), :] * cw_ref[0:1, :]
        for j in range(1, CONV_K):
            conv = conv + xx[q, pl.ds(tail0 + j, cg), :] * cw_ref[j:j + 1, :]
        xs.append(_silu(conv))

    @pl.when(c == nchunks - 1)
    def _():
        for q in range(nseq):
            co_ref[q] = xx[q, pl.ds(8 + cg - (CONV_K - 1), CONV_K - 1), :]

    for q in range(nseq):
        xx[q, 0:8, :] = xx[q, cg:cg + 8, :]

    r_i = _iota((chunk, chunk), 0)
    c_i = _iota((chunk, chunk), 1)
    causal = r_i >= c_i
    strict = r_i > c_i
    cm = chunk - 1
    tril3 = jnp.where(_iota((chunk, 3 * chunk), 0) >= (_iota((chunk, 3 * chunk), 1) & cm), 1.0, 0.0).astype(BF)
    triu3 = jnp.where((_iota((3 * chunk, chunk), 0) & cm) <= _iota((3 * chunk, chunk), 1), 1.0, 0.0).astype(BF)
    onorm = on_ref[...]

    probs = []
    for q in range(nseq):
        for s in range(nsub):
            r0 = q * cg + s * chunk
            x0 = s * chunk
            bg = bg_ref[r0:r0 + chunk, :]
            gt = bgt_ref[0, vheads:2 * vheads, r0:r0 + chunk]
            gcol_all = _dot(tril3, _split3(bg[:, vheads:2 * vheads], axis=0))
            grow_all = _dot(_split3(gt, axis=1), triu3)
            for kh in range(kheads):
                qh = xs[q][x0:x0 + chunk, kh * dk:(kh + 1) * dk]
                kk_ = xs[q][x0:x0 + chunk, kdim + kh * dk:kdim + (kh + 1) * dk]
                qn = qh * lax.rsqrt(jnp.sum(qh * qh, axis=-1, keepdims=True) + EPS) * (dk ** -0.5)
                kn = kk_ * lax.rsqrt(jnp.sum(kk_ * kk_, axis=-1, keepdims=True) + EPS)
                kn_bf = kn.astype(BF)
                kk = _dot_nt(kn_bf, kn_bf)
                qk = _dot_nt(qn.astype(BF), kn_bf)
                for r in range(rep):
                    h = kh * rep + r
                    vh_ = xs[q][x0:x0 + chunk, 2 * kdim + h * dv:2 * kdim + (h + 1) * dv]
                    gcol = gcol_all[:, h:h + 1]
                    grow = grow_all[h:h + 1, :]
                    beta = bg[:, h:h + 1]
                    decay = jnp.where(causal, jnp.exp(jnp.minimum(gcol - grow, 0.0)), 0.0)
                    eg = jnp.exp(gcol)
                    gend = gcol[chunk - 1:chunk, :]
                    probs.append(dict(
                        q=q, s=s, h=h, r0=r0,
                        x=jnp.where(strict, -(beta * kk * decay), 0.0),
                        y=jnp.concatenate([vh_ * beta, kn * (beta * eg)], axis=1),
                        a_bf=(qk * decay).astype(BF),
                        qg=qn * eg,
                        kdec_bf=(kn * jnp.exp(gend - gcol)).astype(BF),
                        send=jnp.exp(gend)))

    if chunk <= 16:
        for j in range(chunk - 1):
            for p in probs:
                p["y"] = p["y"] + p["x"][:, j:j + 1] * p["y"][j:j + 1, :]
    else:
        n_fac = int(math.log2(chunk))
        for k in range(n_fac):
            for p in probs:
                p["y"] = p["y"] + _dot3(p["x"], p["y"])
            if k + 1 < n_fac:
                for p in probs:
                    p["x"] = _dot3(p["x"], p["x"])

    for s in range(nsub):
        cur = [p for p in probs if p["s"] == s]
        for p in cur:
            p["s_old"] = s_scr[p["q"], p["h"]]
            lhs = jnp.concatenate([p["y"][:, dv:], p["qg"]], axis=0).astype(BF)
            p["ws"] = _dot(lhs, p["s_old"].astype(BF))
        for p in cur:
            p["u_bf"] = (p["y"][:, :dv] - p["ws"][:chunk]).astype(BF)
        for p in cur:
            s_scr[p["q"], p["h"]] = p["send"] * p["s_old"] + _dot_tn(p["kdec_bf"], p["u_bf"])
        for p in cur:
            o = p["ws"][chunk:] + _dot(p["a_bf"], p["u_bf"])
            ms = jnp.mean(o * o, axis=-1, keepdims=True)
            zc = slice(p["h"] * dv, (p["h"] + 1) * dv)
            rows = slice(p["r0"], p["r0"] + chunk)
            o_ref[rows, zc] = (o * lax.rsqrt(ms + EPS) * onorm) * _silu(z_ref[rows, zc])

    @pl.when(c == nchunks - 1)
    def _():
        so_ref[...] = s_scr[...]


def _gdn(qkv, z, bg, bgt3, conv_w, onorm, s0, conv0, *, row0, seq_len, rows_per_step, chunk, seqs_per_step):
    nseq_total, vheads, dk, dv = s0.shape
    conv_dim = qkv.shape[1]
    kheads = (conv_dim - vheads * dv) // (2 * dk)
    cg = rows_per_step
    nseq = seqs_per_step
    nchunks = seq_len // cg
    assert nseq == 1 or nchunks == 1, "several sequences per step only when a step covers whole sequences"
    rows = nseq * cg
    b0 = row0 // rows
    conv0p = jnp.pad(conv0, ((0, 0), (8 - (CONV_K - 1), 0), (0, 0)))
    rmap = lambda b, c: (b0 + b * nchunks + c, 0)
    st_spec = pl.BlockSpec((nseq, vheads, dk, dv), lambda b, c: (b, 0, 0, 0))
    kern = functools.partial(_gdn_kernel, kheads=kheads, vheads=vheads, dk=dk, dv=dv, chunk=chunk,
                             nsub=cg // chunk, nseq=nseq, nchunks=nchunks)
    return pl.pallas_call(
        kern,
        grid=(nseq_total // nseq, nchunks),
        in_specs=[pl.BlockSpec((rows, conv_dim), rmap),
                  pl.BlockSpec((rows, vheads * dv), rmap),
                  pl.BlockSpec((rows, 2 * vheads), rmap),
                  pl.BlockSpec((1, 2 * vheads, rows), lambda b, c: (b0 + b * nchunks + c, 0, 0)),
                  pl.BlockSpec((CONV_K, conv_dim), lambda b, c: (0, 0)),
                  pl.BlockSpec((1, dv), lambda b, c: (0, 0)),
                  st_spec,
                  pl.BlockSpec((nseq, 8, conv_dim), lambda b, c: (b, 0, 0))],
        out_specs=[pl.BlockSpec((rows, vheads * dv), lambda b, c: (b * nchunks + c, 0)),
                   st_spec,
                   pl.BlockSpec((nseq, CONV_K - 1, conv_dim), lambda b, c: (b, 0, 0))],
        out_shape=[jax.ShapeDtypeStruct((nseq_total * seq_len, vheads * dv), F32),
                   jax.ShapeDtypeStruct(s0.shape, F32),
                   jax.ShapeDtypeStruct((nseq_total, CONV_K - 1, conv_dim), F32)],
        scratch_shapes=[pltpu.VMEM((nseq, vheads, dk, dv), F32), pltpu.VMEM((nseq, 8 + cg, conv_dim), F32)],
        compiler_params=_cp(("arbitrary", "arbitrary")),
        name="gdn_recurrence",
    )(qkv, z, bg, bgt3, conv_w, onorm.reshape(1, dv), s0, conv0p)


def _merge_kernel(oa1_ref, oa2_ref, ob1_ref, ob2_ref, ga_ref, gb_ref, wa_ref, wb_ref, o_ref, *, na):
    i = pl.program_id(1)

    def body(oa_ref, ob_ref):
        ya = _dot(oa_ref[...].astype(BF), wa_ref[...])
        yb = _dot(ob_ref[...].astype(BF), wb_ref[...])
        o_ref[...] = (_sigmoid(ga_ref[...]) * ya + _sigmoid(gb_ref[...]) * yb).astype(o_ref.dtype)

    @pl.when(i < na)
    def _():
        body(oa1_ref, ob1_ref)

    @pl.when(i >= na)
    def _():
        body(oa2_ref, ob2_ref)


def _merge(oa_p, oa_s, ob_p, ob_s, gates, w_pa, w_pb):
    rp, rs = oa_p.shape[0], oa_s.shape[0]
    ka, kb = oa_p.shape[1], ob_p.shape[1]
    d = w_pa.shape[1]
    tm = _pick(math.gcd(rp, rs), 512)
    tn = _pick(d, 512, LANES)
    na, nb = rp // tm, rs // tm
    nj = d // tn
    amap = lambda j, i: (jnp.minimum(i, na - 1), 0)
    bmap = lambda j, i: (jnp.maximum(i - na, 0), 0)
    return pl.pallas_call(
        functools.partial(_merge_kernel, na=na),
        grid=(nj, na + nb),
        in_specs=[pl.BlockSpec((tm, ka), amap), pl.BlockSpec((tm, ka), bmap),
                  pl.BlockSpec((tm, kb), amap), pl.BlockSpec((tm, kb), bmap),
                  pl.BlockSpec((tm, tn), lambda j, i: (i, j)),
                  pl.BlockSpec((tm, tn), lambda j, i: (i, nj + j)),
                  pl.BlockSpec((ka, tn), lambda j, i: (0, j)),
                  pl.BlockSpec((kb, tn), lambda j, i: (0, j))],
        out_specs=pl.BlockSpec((tm, tn), lambda j, i: (i, j)),
        out_shape=jax.ShapeDtypeStruct((rp + rs, d), BF),
        compiler_params=_cp(("arbitrary", "arbitrary")),
        name="branch_proj_merge",
    )(oa_p, oa_s, ob_p, ob_s, gates, gates, w_pa.astype(BF), w_pb.astype(BF))


def _outproj_kernel(m_ref, xa_ref, xb_ref, wo_ref, nw_ref, wr_ref, h_ref, xf_ref, lg_ref, *, na):
    i = pl.program_id(0)

    def body(x_ref):
        h = x_ref[...] + _dot(m_ref[...], wo_ref[...])
        h_ref[...] = h
        ms = jnp.mean(h * h, axis=-1, keepdims=True)
        xf = h * lax.rsqrt(ms + EPS) * nw_ref[...]
        xf_ref[...] = xf
        lg_ref[...] = _dot(xf.astype(BF), wr_ref[...])

    @pl.when(i < na)
    def _():
        body(xa_ref)

    @pl.when(i >= na)
    def _():
        body(xb_ref)


def _outproj(merged, xa, xb, w_out, norm_ffn, w_router):
    t, d = merged.shape
    tm = _pick(math.gcd(xa.shape[0], xb.shape[0]), 512)
    na, nb, sa, sb = _two_source_specs(xa.shape[0], xb.shape[0], tm, d)
    nr = w_router.shape[1]
    const = lambda i: (0, 0)
    return pl.pallas_call(
        functools.partial(_outproj_kernel, na=na),
        grid=(na + nb,),
        in_specs=[pl.BlockSpec((tm, d), lambda i: (i, 0)), sa, sb,
                  pl.BlockSpec((d, d), const, pipeline_mode=pl.Buffered(1)),
                  pl.BlockSpec((1, d), const),
                  pl.BlockSpec((d, nr), const, pipeline_mode=pl.Buffered(1))],
        out_specs=[pl.BlockSpec((tm, d), lambda i: (i, 0)),
                   pl.BlockSpec((tm, d), lambda i: (i, 0)),
                   pl.BlockSpec((tm, nr), lambda i: (i, 0))],
        out_shape=[jax.ShapeDtypeStruct((t, d), F32), jax.ShapeDtypeStruct((t, d), F32),
                   jax.ShapeDtypeStruct((t, nr), F32)],
        compiler_params=_cp(("arbitrary",), 56),
        name="out_proj_ffn_norm",
    )(merged, xa, xb, w_out.astype(BF), norm_ffn.reshape(1, d), w_router)


def _route_kernel(lg_ref, b_ref, eid_ref, gate_ref, rank_ref, cnt_ref, carry,
                  *, ngroups, nexp, tm):
    i = pl.program_id(0)

    @pl.when(i == 0)
    def _():
        carry[...] = jnp.zeros_like(carry)

    per_group = nexp // ngroups
    lg = lg_ref[...] + b_ref[...]
    lane = _iota(lg.shape, 1)
    big = jnp.int32(1 << 20)
    is_g = lane < ngroups
    gl = jnp.where(is_g, lg, NEG)
    gmax = jnp.max(gl, axis=-1, keepdims=True)
    gidx = jnp.min(jnp.where(gl == gmax, lane, big), axis=-1, keepdims=True)
    gsum = jnp.sum(jnp.where(is_g, jnp.exp(gl - gmax), 0.0), axis=-1, keepdims=True)
    gw = 1.0 / gsum
    elane = lane - ngroups
    in_grp = (elane >= gidx * per_group) & (elane < (gidx + 1) * per_group)
    el = jnp.where(in_grp, lg, NEG)
    v1 = jnp.max(el, axis=-1, keepdims=True)
    i1 = jnp.min(jnp.where(in_grp & (el == v1), elane, big), axis=-1, keepdims=True)
    in2 = in_grp & (elane != i1)
    el2 = jnp.where(in2, lg, NEG)
    v2 = jnp.max(el2, axis=-1, keepdims=True)
    i2 = jnp.min(jnp.where(in2 & (el2 == v2), elane, big), axis=-1, keepdims=True)
    p2 = jnp.exp(v2 - v1)
    den = 1.0 + p2
    lane2 = _iota((tm, TOP_K), 1)
    eid_ref[...] = jnp.where(lane2 == 0, i1, i2)
    gate_ref[...] = jnp.where(lane2 == 0, gw / den, gw * p2 / den)

    oh1 = (elane == i1).astype(F32)
    oh2 = (elane == i2).astype(F32)
    lower = (_iota((tm, tm), 0) > _iota((tm, tm), 1)).astype(BF)
    cs1 = _dot(lower, oh1.astype(BF))
    cs2 = _dot(lower, oh2.astype(BF))
    tot1 = jnp.sum(oh1, axis=0, keepdims=True)
    tot2 = jnp.sum(oh2, axis=0, keepdims=True)
    base = carry[0:1, :]
    r1 = jnp.sum(oh1 * (base + cs1), axis=-1, keepdims=True)
    r2 = jnp.sum(oh2 * (base + tot1 + cs2), axis=-1, keepdims=True)
    rank_ref[...] = jnp.where(lane2 == 0, r1, r2).astype(I32)
    new = base + tot1 + tot2
    carry[...] = jnp.broadcast_to(new, carry.shape)
    cnt_ref[...] = jnp.broadcast_to(new, cnt_ref.shape)


def _route(logits, bias_row, ngroups, nexp):
    t, nr = logits.shape
    tm = _pick(t, 512)
    kern = functools.partial(_route_kernel, ngroups=ngroups, nexp=nexp, tm=tm)
    return pl.pallas_call(
        kern,
        grid=(t // tm,),
        in_specs=[pl.BlockSpec((tm, nr), lambda i: (i, 0)), pl.BlockSpec((1, nr), lambda i: (0, 0))],
        out_specs=[pl.BlockSpec((tm, TOP_K), lambda i: (i, 0)),
                   pl.BlockSpec((tm, TOP_K), lambda i: (i, 0)),
                   pl.BlockSpec((tm, TOP_K), lambda i: (i, 0)),
                   pl.BlockSpec((8, nr), lambda i: (0, 0))],
        out_shape=[jax.ShapeDtypeStruct((t, TOP_K), I32), jax.ShapeDtypeStruct((t, TOP_K), F32),
                   jax.ShapeDtypeStruct((t, TOP_K), I32), jax.ShapeDtypeStruct((8, nr), F32)],
        scratch_shapes=[pltpu.VMEM((8, nr), F32)],
        compiler_params=_cp(("arbitrary",)),
        name="route_topk_rank",
    )(logits, bias_row)


def _plan_kernel(cnt_ref, eid_ref, rank_ref, pos_ref, be_ref, *, ngroups, nexp, nblocks_pad, tm):
    nr = cnt_ref.shape[1]
    cnt = cnt_ref[0:1, :]
    padded = jnp.floor((cnt + (MOE_BLOCK - 1)) * (1.0 / MOE_BLOCK)) * MOE_BLOCK
    r_i = _iota((nr, nr), 0)
    c_i = _iota((nr, nr), 1)
    padded_col = jnp.sum(jnp.where(r_i == c_i, jnp.broadcast_to(padded, (nr, nr)), 0.0), axis=1, keepdims=True)
    start = jnp.sum(jnp.where(r_i < c_i, jnp.broadcast_to(padded_col, (nr, nr)), 0.0), axis=0, keepdims=True)
    end = start + padded
    lane = _iota((tm, nr), 1)
    eid = eid_ref[...]
    rank = rank_ref[...]
    lane2 = _iota((tm, TOP_K), 1)
    pos = jnp.zeros((tm, TOP_K), F32)
    for k in range(TOP_K):
        oh = (lane - ngroups) == eid[:, k:k + 1]
        st = jnp.sum(jnp.where(oh, start, 0.0), axis=-1, keepdims=True)
        pos = jnp.where(lane2 == k, st, pos)
    pos_ref[...] = pos.astype(I32) + rank

    @pl.when(pl.program_id(0) == 0)
    def _():
        blk_row = _iota((nblocks_pad, nr), 0).astype(F32) * MOE_BLOCK
        lane_b = _iota((nblocks_pad, nr), 1)
        is_e = (lane_b >= ngroups) & (lane_b < ngroups + nexp)
        n_le = jnp.sum(jnp.where(is_e & (end <= blk_row), 1.0, 0.0), axis=-1, keepdims=True)
        e_lane = (lane_b - ngroups).astype(F32)
        nonempty = is_e & (padded > 0.0)
        e_last = jnp.max(jnp.where(nonempty, e_lane, -1.0), axis=-1, keepdims=True)
        be = jnp.minimum(n_le, e_last)
        nxt = jnp.min(jnp.where(nonempty & (e_lane > be), e_lane, 1e9), axis=-1, keepdims=True)
        nxt = jnp.where(nxt > 1e8, -1.0, nxt)
        total = jnp.sum(padded, axis=-1, keepdims=True)
        valid = jnp.where(blk_row[:, 0:1] < total, 1.0, 0.0)
        col = _iota((nblocks_pad, 4), 1)
        meta = jnp.where(col == 0, be, jnp.where(col == 1, nxt, jnp.where(col == 2, valid, 0.0)))
        be_ref[...] = meta.astype(I32)


def _plan(cnt, eid, rank, ngroups, nexp, nblocks):
    t = eid.shape[0]
    nr = cnt.shape[1]
    tm = _pick(t, 512)
    nblocks_pad = -(-nblocks // 8) * 8
    kern = functools.partial(_plan_kernel, ngroups=ngroups, nexp=nexp, nblocks_pad=nblocks_pad, tm=tm)
    return pl.pallas_call(
        kern,
        grid=(t // tm,),
        in_specs=[pl.BlockSpec((8, nr), lambda i: (0, 0)),
                  pl.BlockSpec((tm, TOP_K), lambda i: (i, 0)),
                  pl.BlockSpec((tm, TOP_K), lambda i: (i, 0))],
        out_specs=[pl.BlockSpec((tm, TOP_K), lambda i: (i, 0)),
                   pl.BlockSpec((nblocks_pad, 4), lambda i: (0, 0))],
        out_shape=[jax.ShapeDtypeStruct((t, TOP_K), I32), jax.ShapeDtypeStruct((nblocks_pad, 4), I32)],
        compiler_params=_cp(("arbitrary",)),
        name="route_plan",
    )(cnt, eid, rank)


def _invert_kernel(pos_ref, tok_ref, *, n_assign, n_rows):
    def clear(r, c):
        tok_ref[r] = 0
        return c

    lax.fori_loop(0, n_rows, clear, 0, unroll=16)

    def put(a, c):
        tok_ref[pos_ref[a]] = lax.shift_right_logical(a, TOP_K.bit_length() - 1)
        return c

    lax.fori_loop(0, n_assign, put, 0, unroll=16)


def _invert(pos_flat, n_rows):
    n_assign = pos_flat.shape[0]
    return pl.pallas_call(
        functools.partial(_invert_kernel, n_assign=n_assign, n_rows=n_rows),
        in_specs=[pl.BlockSpec(memory_space=pltpu.SMEM)],
        out_specs=pl.BlockSpec(memory_space=pltpu.SMEM),
        out_shape=jax.ShapeDtypeStruct((n_rows,), I32),
        name="route_invert",
    )(pos_flat)


def _expert_kernel(be_ref, nxt_ref, valid_ref, tok_ref, x_hbm, wg_hbm, wu_hbm, wd_hbm, y_ref,
                   xbuf0, xbuf1, xsem, wg_st, wu_st, wd_st, wsem, wg_bf, wu_bf, wd_bf):
    i = pl.program_id(0)
    n = pl.num_programs(0)
    xbufs = (xbuf0, xbuf1)

    def row_copy(blk, s, r):
        t = tok_ref[blk * MOE_BLOCK + r]
        return pltpu.make_async_copy(x_hbm.at[pl.ds(t, 1), :], xbufs[s].at[pl.ds(r, 1), :], xsem.at[s])

    def weight_copies(e):
        return (pltpu.make_async_copy(wg_hbm.at[e], wg_st, wsem.at[0]),
                pltpu.make_async_copy(wu_hbm.at[e], wu_st, wsem.at[1]),
                pltpu.make_async_copy(wd_hbm.at[e], wd_st, wsem.at[2]))

    @pl.when(i == 0)
    def _():
        for r in range(MOE_BLOCK):
            row_copy(0, 0, r).start()
        for cp in weight_copies(be_ref[0]):
            cp.start()

    prev = be_ref[jnp.maximum(i - 1, 0)]

    @pl.when((i == 0) | (be_ref[i] != prev))
    def _():
        for cp in weight_copies(be_ref[i]):
            cp.wait()
        wg_bf[...] = wg_st[...].astype(BF)
        wu_bf[...] = wu_st[...].astype(BF)
        wd_bf[...] = wd_st[...].astype(BF)

        @pl.when(nxt_ref[i] >= 0)
        def _():
            for cp in weight_copies(nxt_ref[i]):
                cp.start()

    def step(cur, oth):
        ahead = jnp.minimum(i + 1, n - 1)
        for r in range(MOE_BLOCK):
            row_copy(ahead, oth, r).start()
        for r in range(MOE_BLOCK):
            row_copy(i, cur, r).wait()
        x = xbufs[cur][...].astype(BF)
        hid = _silu(_dot(x, wg_bf[...])) * _dot(x, wu_bf[...])
        y_ref[...] = _dot(hid.astype(BF), wd_bf[...])

        @pl.when(i == n - 1)
        def _():
            for r in range(MOE_BLOCK):
                row_copy(ahead, oth, r).wait()

    @pl.when(i % 2 == 0)
    def _():
        step(0, 1)

    @pl.when(i % 2 == 1)
    def _():
        step(1, 0)


def _experts(block_e, block_next, block_valid, row_tok, xf, w_eg, w_eu, w_ed):
    nrows = row_tok.shape[0]
    d = xf.shape[1]
    nblocks = nrows // MOE_BLOCK
    de = w_eg.shape[2]
    any_spec = pl.BlockSpec(memory_space=pl.ANY)
    grid_spec = pltpu.PrefetchScalarGridSpec(
        num_scalar_prefetch=4,
        grid=(nblocks,),
        in_specs=[any_spec, any_spec, any_spec, any_spec],
        out_specs=pl.BlockSpec((MOE_BLOCK, d), lambda i, *_: (i, 0)),
        scratch_shapes=[pltpu.VMEM((MOE_BLOCK, d), F32), pltpu.VMEM((MOE_BLOCK, d), F32),
                        pltpu.SemaphoreType.DMA((2,)),
                        pltpu.VMEM((d, de), F32), pltpu.VMEM((d, de), F32), pltpu.VMEM((de, d), F32),
                        pltpu.SemaphoreType.DMA((3,)),
                        pltpu.VMEM((d, de), BF), pltpu.VMEM((d, de), BF), pltpu.VMEM((de, d), BF)],
    )
    return pl.pallas_call(
        _expert_kernel,
        grid_spec=grid_spec,
        out_shape=jax.ShapeDtypeStruct((nrows, d), F32),
        compiler_params=_cp(("arbitrary",), 48),
        name="expert_blocks",
    )(block_e, block_next, block_valid, row_tok, xf, w_eg, w_eu, w_ed)


def _final_kernel(pos_ref, h_ref, gate_ref, w_ref, yb_hbm, ya_ref, ys_ref, gbuf, sem, *, na, tm):
    i = pl.program_id(0)
    n = pl.num_programs(0)
    slot = i % 2

    def row_copy(tile, slot_, r, k):
        p = pos_ref[(tile * tm + r) * TOP_K + k]
        return pltpu.make_async_copy(yb_hbm.at[pl.ds(p, 1), :], gbuf.at[slot_, k, pl.ds(r, 1), :], sem.at[slot_])

    def start_tile(tile, slot_):
        for r in range(tm):
            for k in range(TOP_K):
                row_copy(tile, slot_, r, k).start()

    @pl.when(i == 0)
    def _():
        start_tile(0, 0)

    @pl.when(i + 1 < n)
    def _():
        start_tile(i + 1, 1 - slot)

    for r in range(tm):
        for k in range(TOP_K):
            row_copy(i, slot, r, k).wait()

    gate = gate_ref[...]
    y = h_ref[...]
    ffn = gbuf[slot, 0] * gate[:, 0:1]
    for k in range(1, TOP_K):
        ffn = ffn + gbuf[slot, k] * gate[:, k:k + 1]
    y = y + ffn
    ms = jnp.mean(y * y, axis=-1, keepdims=True)
    out = y * lax.rsqrt(ms + EPS) * w_ref[...]

    @pl.when(i < na)
    def _():
        ya_ref[...] = out

    @pl.when(i >= na)
    def _():
        ys_ref[...] = out


def _final(pos_flat, h, gate, yb, w, rows_a, rows_b):
    t, d = h.shape
    tm = _pick(math.gcd(rows_a, rows_b), 128)
    na, nb = rows_a // tm, rows_b // tm
    grid_spec = pltpu.PrefetchScalarGridSpec(
        num_scalar_prefetch=1,
        grid=(na + nb,),
        in_specs=[pl.BlockSpec((tm, d), lambda i, pos: (i, 0)),
                  pl.BlockSpec((tm, TOP_K), lambda i, pos: (i, 0)),
                  pl.BlockSpec((1, d), lambda i, pos: (0, 0)),
                  pl.BlockSpec(memory_space=pl.ANY)],
        out_specs=[pl.BlockSpec((tm, d), lambda i, pos: (jnp.minimum(i, na - 1), 0)),
                   pl.BlockSpec((tm, d), lambda i, pos: (jnp.maximum(i - na, 0), 0))],
        scratch_shapes=[pltpu.VMEM((2, TOP_K, tm, d), F32), pltpu.SemaphoreType.DMA((2,))],
    )
    return pl.pallas_call(
        functools.partial(_final_kernel, na=na, tm=tm),
        grid_spec=grid_spec,
        out_shape=[jax.ShapeDtypeStruct((rows_a, d), F32), jax.ShapeDtypeStruct((rows_b, d), F32)],
        compiler_params=_cp(("arbitrary",)),
        name="combine_final_norm",
    )(pos_flat, h, gate, w.reshape(1, d), yb)


def kernel(x_prompt, x_sample, state_hgrn, state_gdn, state_conv, norm_mix, w_in, conv_w, ha_lb_logits, ha_onorm, w_pa, gd_A_log, gd_dt_bias, gd_onorm, w_pb, w_out, norm_ffn, w_router_group, b_router_group, w_router_expert, b_router_expert, w_exp_gate, w_exp_up, w_exp_down, norm_final):
    depth = state_hgrn.shape[0]
    assert depth == 1, "one decoder layer"
    bp, lp, d = x_prompt.shape
    bs, ls, _ = x_sample.shape
    _, _, ha_heads, ha_dk, ha_dv = state_hgrn.shape
    _, _, vheads, gd_dk, gd_dv = state_gdn.shape
    conv_dim = state_conv.shape[3]
    ha_kdim, ha_vdim = ha_heads * ha_dk, ha_heads * ha_dv
    gd_vdim = vheads * gd_dv
    ngroups = w_router_group.shape[2]
    nexp = w_router_expert.shape[2]
    rp, rs = bp * lp, bs * ls
    t = rp + rs

    off_f = ha_kdim
    off_i = off_f + ha_kdim
    off_g = off_i + ha_vdim
    off_qkv = off_g + ha_vdim
    off_z = off_qkv + conv_dim
    off_b = off_z + gd_vdim
    off_gate = off_b + 2 * vheads
    w_in0 = w_in[0]

    xn = _rmsnorm_bf16(x_prompt.reshape(rp, d), x_sample.reshape(rs, d), norm_mix[0])

    ident = lambda a: (a,)
    (q_a,) = _proj(xn, w_in0, 0, ha_kdim, lambda a: (a * (ha_dk ** -0.5),), [F32], name="proj_hgrn_q")
    logf, k_a = _proj(xn, w_in0, off_f, ha_kdim, _forget_epilogue, [F32, F32],
                      extras=(ha_lb_logits,), name="proj_hgrn_forget")
    (v_a,) = _proj(xn, w_in0, off_i, ha_vdim, ident, [F32], name="proj_hgrn_in")
    (g_a,) = _proj(xn, w_in0, off_g, ha_vdim, ident, [F32], name="proj_hgrn_gate")
    (qkv,) = _proj(xn, w_in0, off_qkv, conv_dim, ident, [F32], name="proj_gdn_qkv")
    (z_b,) = _proj(xn, w_in0, off_z, gd_vdim, ident, [F32], name="proj_gdn_z")
    (gates,) = _proj(xn, w_in0[:, off_gate:], 0, 2 * d, ident, [F32], name="proj_merge_gates")
    bg, bgt = _beta_decay(xn, w_in0[:, off_b:off_gate], gd_A_log[0], gd_dt_bias[0])

    cg_p = _pick(lp, 128, GDN_CHUNK) if lp >= GDN_CHUNK else lp
    sub_p = min(HGRN_SUB, lp)
    chunk_p = min(GDN_CHUNK, lp)
    cg_s = ls
    zeros_h = jnp.zeros((bp, ha_heads, ha_dk, ha_dv), F32)
    zeros_g = jnp.zeros((bp, vheads, gd_dk, gd_dv), F32)
    zeros_c = jnp.zeros((bp, CONV_K - 1, conv_dim), F32)

    oa_p, sh_p = _hgrn2(q_a, logf, k_a, v_a, g_a, ha_onorm[0], zeros_h,
                        row0=0, seq_len=lp, rows_per_step=cg_p, sub=sub_p, seqs_per_step=1)
    oa_s, sh_s = _hgrn2(q_a, logf, k_a, v_a, g_a, ha_onorm[0], state_hgrn[0],
                        row0=rp, seq_len=ls, rows_per_step=cg_s, sub=min(HGRN_SUB, ls),
                        seqs_per_step=math.gcd(_pick(bs, HGRN_SAMPLE_SEQS, 1), max(rp // cg_s, 1)))

    def time_on_lanes(rows0, nrows, cg):
        part = bgt[:, rows0:rows0 + nrows].reshape(2 * vheads, nrows // cg, cg)
        return jnp.transpose(part, (1, 0, 2))

    seqs_s = _pick(bs, GDN_SAMPLE_SEQS, 1)
    bgt_p = time_on_lanes(0, rp, cg_p)
    bgt_s = time_on_lanes(rp, rs, seqs_s * cg_s)
    ob_p, sg_p, sc_p = _gdn(qkv, z_b, bg, bgt_p, conv_w[0], gd_onorm[0], zeros_g, zeros_c,
                            row0=0, seq_len=lp, rows_per_step=cg_p, chunk=chunk_p, seqs_per_step=1)
    ob_s, sg_s, sc_s = _gdn(qkv[rp:], z_b[rp:], bg[rp:], bgt_s, conv_w[0], gd_onorm[0], state_gdn[0],
                            state_conv[0], row0=0, seq_len=ls, rows_per_step=cg_s, chunk=min(GDN_CHUNK, ls),
                            seqs_per_step=seqs_s)

    merged = _merge(oa_p, oa_s, ob_p, ob_s, gates, w_pa[0], w_pb[0])

    nr = LANES
    w_router = jnp.concatenate([w_router_group[0], w_router_expert[0]], axis=1)
    w_router = jnp.pad(w_router, ((0, 0), (0, nr - ngroups - nexp))).astype(BF)
    b_router = jnp.pad(jnp.concatenate([b_router_group[0], b_router_expert[0]]), (0, nr - ngroups - nexp))
    h, xf, logits = _outproj(merged, x_prompt.reshape(rp, d), x_sample.reshape(rs, d), w_out[0],
                             norm_ffn[0], w_router)

    eid, gate, rank, cnt = _route(logits, b_router.reshape(1, nr), ngroups, nexp)
    nblocks = -(-(t * TOP_K) // MOE_BLOCK) + nexp
    pos, block_meta = _plan(cnt, eid, rank, ngroups, nexp, nblocks)
    block_e, block_next, block_valid = (block_meta[:nblocks, j] for j in range(3))

    pos_flat = pos.reshape(-1)
    row_tok = _invert(pos_flat, nblocks * MOE_BLOCK)
    yb = _experts(block_e, block_next, block_valid, row_tok, xf, w_exp_gate[0], w_exp_up[0], w_exp_down[0])
    y_p, y_s = _final(pos_flat, h, gate, yb, norm_final, rp, rs)
    return (y_p.reshape(bp, lp, d), y_s.reshape(bs, ls, d),
            sh_p[None], sg_p[None], sc_p[None], sh_s[None], sg_s[None], sc_s[None])
```

```python
import functools
import math

import jax
import jax.numpy as jnp
from jax import lax
from jax.experimental import pallas as pl
from jax.experimental.pallas import tpu as pltpu

F32 = jnp.float32
BF = jnp.bfloat16
I32 = jnp.int32
EPS = 1e-6
CONV_K = 4
TOP_K = 2
MOE_BLOCK = 128
GDN_CHUNK = 64
GDN_ROWS = 128
GDN_SAMPLE_SEQS = 4
HGRN_SUB = 16
HGRN_SAMPLE_SEQS = 4
LOG2E = 1.4426950408889634
LANES = 128
NEG = -3.0e38
HI = lax.Precision.HIGHEST
MIB = 1024 * 1024


def _cp(sem, vmem_mib=48):
    return pltpu.CompilerParams(dimension_semantics=sem, vmem_limit_bytes=vmem_mib * MIB)


def _pick(n, pref, mult=8):
    best = None
    for t in range(mult, min(n, pref) + 1, mult):
        if n % t == 0:
            best = t
    return best if best is not None else n


def _sigmoid(x):
    return 1.0 / (1.0 + jnp.exp(-x))


def _silu(x):
    return x * _sigmoid(x)


def _iota(shape, dim):
    return lax.broadcasted_iota(I32, shape, dim)


def _dot(a, b):
    return jnp.dot(a, b, preferred_element_type=F32)


def _dot_nt(a, b):
    return lax.dot_general(a, b, (((1,), (1,)), ((), ())), preferred_element_type=F32)


def _dot_tn(a, b):
    return lax.dot_general(a, b, (((0,), (0,)), ((), ())), preferred_element_type=F32)


def _split(a):
    hi = a.astype(BF)
    lo = (a - hi.astype(F32)).astype(BF)
    return hi, lo


def _split3(a, axis=0):
    hi = a.astype(BF)
    r1 = a - hi.astype(F32)
    mid = r1.astype(BF)
    lo = (r1 - mid.astype(F32)).astype(BF)
    return jnp.concatenate([hi, mid, lo], axis=axis)


def _dot3(a, b):
    ah, al = _split(a)
    bh, bl = _split(b)
    return _dot(ah, bh) + (_dot(ah, bl) + _dot(al, bh))


def _store_token_major(ref, x):
    rows, d = x.shape
    nch = d // LANES
    for c in range(nch):
        ref[pl.ds(c, rows, stride=nch), :] = x[:, c * LANES:(c + 1) * LANES]


def _load_token_major(ref, rows, nch):
    return jnp.concatenate([ref[pl.ds(c, rows, stride=nch), :] for c in range(nch)], axis=1)


def _two_source_specs(rows_a, rows_b, tm, width):
    na, nb = rows_a // tm, rows_b // tm
    spec_a = pl.BlockSpec((tm, width), lambda i: (jnp.minimum(i, na - 1), 0))
    spec_b = pl.BlockSpec((tm, width), lambda i: (jnp.maximum(i - na, 0), 0))
    return na, nb, spec_a, spec_b


def _rmsnorm_kernel(xa_ref, xb_ref, w_ref, o_ref, *, na):
    i = pl.program_id(0)

    def body(x_ref):
        x = x_ref[...]
        ms = jnp.mean(x * x, axis=-1, keepdims=True)
        o_ref[...] = (x * lax.rsqrt(ms + EPS) * w_ref[...]).astype(o_ref.dtype)

    @pl.when(i < na)
    def _():
        body(xa_ref)

    @pl.when(i >= na)
    def _():
        body(xb_ref)


def _rmsnorm_bf16(xa, xb, w):
    d = xa.shape[1]
    tm = _pick(math.gcd(xa.shape[0], xb.shape[0]), 512)
    na, nb, sa, sb = _two_source_specs(xa.shape[0], xb.shape[0], tm, d)
    return pl.pallas_call(
        functools.partial(_rmsnorm_kernel, na=na),
        grid=(na + nb,),
        in_specs=[sa, sb, pl.BlockSpec((1, d), lambda i: (0, 0))],
        out_specs=pl.BlockSpec((tm, d), lambda i: (i, 0)),
        out_shape=jax.ShapeDtypeStruct((xa.shape[0] + xb.shape[0], d), BF),
        compiler_params=_cp(("arbitrary",)),
        name="rmsnorm_mix",
    )(xa, xb, w.reshape(1, d))


def _proj_kernel(x_ref, w_ref, *rest, n_extra, n_out, epilogue):
    extra = rest[:n_extra]
    outs = rest[n_extra:n_extra + n_out]
    wbf = rest[n_extra + n_out]

    @pl.when(pl.program_id(1) == 0)
    def _():
        wbf[...] = w_ref[...].astype(BF)

    acc = _dot(x_ref[...], wbf[...])
    vals = epilogue(acc, *[e[...] for e in extra])
    for o_ref, v in zip(outs, vals):
        o_ref[...] = v.astype(o_ref.dtype)


def _proj(x_bf, w, col0, ncols, epilogue, out_dtypes, extras=(), name="proj"):
    t, k = x_bf.shape
    tn = _pick(math.gcd(ncols, col0) if col0 else ncols, 512, LANES)
    tm = _pick(t, 1024)
    j0 = col0 // tn
    n_out = len(out_dtypes)
    kern = functools.partial(_proj_kernel, n_extra=len(extras), n_out=n_out, epilogue=epilogue)
    return pl.pallas_call(
        kern,
        grid=(ncols // tn, t // tm),
        in_specs=[pl.BlockSpec((tm, k), lambda j, i: (i, 0)),
                  pl.BlockSpec((k, tn), lambda j, i: (0, j0 + j))]
                 + [pl.BlockSpec((e.shape[0], tn), lambda j, i: (0, j)) for e in extras],
        out_specs=[pl.BlockSpec((tm, tn), lambda j, i: (i, j)) for _ in out_dtypes],
        out_shape=[jax.ShapeDtypeStruct((t, ncols), dt) for dt in out_dtypes],
        scratch_shapes=[pltpu.VMEM((k, tn), BF)],
        compiler_params=_cp(("arbitrary", "arbitrary")),
        name=name,
    )(x_bf, w, *extras)


def _forget_epilogue(acc, lb_logits):
    m = jnp.max(lb_logits, axis=0, keepdims=True)
    e = jnp.exp(lb_logits - m)
    lb = e[0:1, :] / jnp.sum(e, axis=0, keepdims=True)
    f = lb + (1.0 - lb) * _sigmoid(acc)
    return jnp.log(f), 1.0 - f


def _beta_decay_kernel(x_ref, w_ref, wt_ref, prow_ref, pcol_ref, o_ref, ot_ref, *, vh):
    x = x_ref[...]
    acc = _dot(x, w_ref[...])
    acct = _dot_nt(wt_ref[...], x)

    def act(a, is_beta, a_neg_exp, dt_bias):
        z = a + dt_bias
        softplus = jnp.maximum(z, 0.0) + jnp.log(1.0 + jnp.exp(-jnp.abs(z)))
        return jnp.where(is_beta, _sigmoid(a), a_neg_exp * softplus)

    prow = prow_ref[...]
    pcol = pcol_ref[...]
    o_ref[...] = act(acc, _iota(acc.shape, 1) < vh, prow[0:1, :], prow[1:2, :])
    ot_ref[...] = act(acct, _iota(acct.shape, 0) < vh, pcol[:, 0:1], pcol[:, 1:2])


def _beta_decay(x_bf, w_ba, a_log, dt_bias):
    t, k = x_bf.shape
    vh = a_log.shape[0]
    tm = t if t <= 2048 else _pick(t, 1024, LANES)
    zeros = jnp.zeros((vh,), F32)
    prow = jnp.stack([jnp.concatenate([zeros, -jnp.exp(a_log)]), jnp.concatenate([zeros, dt_bias])])
    return pl.pallas_call(
        functools.partial(_beta_decay_kernel, vh=vh),
        grid=(t // tm,),
        in_specs=[pl.BlockSpec((tm, k), lambda i: (i, 0)),
                  pl.BlockSpec((k, 2 * vh), lambda i: (0, 0)),
                  pl.BlockSpec((2 * vh, k), lambda i: (0, 0)),
                  pl.BlockSpec((2, 2 * vh), lambda i: (0, 0)),
                  pl.BlockSpec((2 * vh, 2), lambda i: (0, 0))],
        out_specs=[pl.BlockSpec((tm, 2 * vh), lambda i: (i, 0)),
                   pl.BlockSpec((2 * vh, tm), lambda i: (0, i))],
        out_shape=[jax.ShapeDtypeStruct((t, 2 * vh), F32), jax.ShapeDtypeStruct((2 * vh, t), F32)],
        compiler_params=_cp(("arbitrary",)),
        name="proj_beta_decay",
    )(x_bf, w_ba.astype(BF), w_ba.T.astype(BF), prow, prow.T)


def _hgrn2_kernel(q_ref, lf_ref, k_ref, v_ref, g_ref, on_ref, s0_ref, o_ref, so_ref, s_scr, gall_scr,
                  g_scr, k_scr, v_scr, *, heads, dk, dv, sub, nsub, nseq, nchunks, group, unroll):
    c = pl.program_id(1)
    cg = sub * nsub

    @pl.when(c == 0)
    def _():
        s_scr[...] = s0_ref[...]

    rows_all = nseq * cg
    shift = sub.bit_length() - 1
    r_i = _iota((rows_all, 3 * rows_all), 0)
    c_i = _iota((rows_all, 3 * rows_all), 1) & (rows_all - 1)
    same = lax.shift_right_logical(r_i, shift) == lax.shift_right_logical(c_i, shift)
    btril3 = jnp.where((r_i >= c_i) & same, 1.0, 0.0).astype(BF)
    gall_scr[...] = _dot(btril3, _split3(lf_ref[...])) * LOG2E
    onorm = on_ref[...]
    n_aug = 16 - sub % 16
    aug_rhs = jnp.concatenate([jnp.zeros((n_aug, dv), BF), jnp.ones((n_aug, dv), BF)], axis=1)
    zeros_v = jnp.zeros((sub, dv), BF)
    zeros_aug = jnp.zeros((n_aug - 3, dk), F32)

    nh = sub // 8
    row8 = _iota((8, 1), 0)

    def block(it, carry):
        for u in range(unroll):
            sub_chunk(it * unroll + u, g_scr.at[u], k_scr.at[u], v_scr.at[u])
        return carry

    def sub_chunk(sb, g_scr, k_scr, v_scr):
        for q in range(nseq):
            rows = pl.ds(pl.multiple_of(q * cg + sb * sub, sub), sub)
            g_scr[q] = gall_scr[rows, :]
            k_scr[q] = k_ref[rows, :]
            v_scr[q] = v_ref[rows, :]
        problems = [(q, h) for q in range(nseq) for h in range(heads)]
        for g0 in range(0, len(problems), group):
            ps = []
            for (q, h) in problems[g0:g0 + group]:
                r0 = pl.multiple_of(q * cg + sb * sub, sub)
                rows = pl.ds(r0, sub)
                kc = pl.ds(h * dk, dk)
                vc = pl.ds(h * dv, dv)
                ps.append(dict(q=q, h=h, r0=r0, rows=rows, kc=kc, vc=vc,
                               gc=g_scr[q, :, kc],
                               qv=q_ref[rows, kc], k=k_ref[rows, kc], v=v_ref[rows, vc]))
            for p in ps:
                p["s_old"] = s_scr[p["q"], p["h"]]
                p["o_state"] = _dot((p["qv"] * jnp.exp2(p["gc"])).astype(BF), p["s_old"].astype(BF))
                p["o8"] = [jnp.zeros((8, dv), F32) for _ in range(nh)]
                p["q8"] = [p["qv"][8 * i:8 * i + 8] for i in range(nh)]
                p["g8"] = [p["gc"][8 * i:8 * i + 8] for i in range(nh)]
            for j in range(sub):
                for p in ps:
                    kj = k_scr[p["q"], pl.ds(j, 1), p["kc"]]
                    vj = v_scr[p["q"], pl.ds(j, 1), p["vc"]]
                    gj = g_scr[p["q"], pl.ds(j, 1), p["kc"]]
                    for i in range(j // 8, nh):
                        pr = (p["q8"][i] * kj) * jnp.exp2(p["g8"][i] - gj)
                        a_col = jnp.sum(pr, axis=-1, keepdims=True)
                        if i == j // 8:
                            a_col = jnp.where(row8 >= j % 8, a_col, 0.0)
                        p["o8"][i] = p["o8"][i] + a_col * vj
            for p in ps:
                p["o"] = p["o_state"] + (jnp.concatenate(p["o8"], axis=0) if nh > 1 else p["o8"][0])
            for p in ps:
                gend = p["gc"][sub - 1:sub, :]
                dend = jnp.exp2(gend)
                d_hi = dend.astype(BF).astype(F32)
                d_mid = (dend - d_hi).astype(BF).astype(F32)
                d_lo = (dend - d_hi - d_mid).astype(BF).astype(F32)
                kdec = p["k"] * jnp.exp2(gend - p["gc"])
                lhs = jnp.concatenate([kdec, d_hi, d_mid, d_lo, zeros_aug], axis=0).astype(BF)
                rhs = jnp.concatenate([jnp.concatenate([p["v"].astype(BF), zeros_v], axis=1), aug_rhs], axis=0)
                p["kv"] = _dot_tn(lhs, rhs)
            for p in ps:
                s_scr[p["q"], p["h"]] = p["kv"][:, dv:] * p["s_old"] + p["kv"][:, :dv]
            for p in ps:
                o = p["o"]
                ms = jnp.mean(o * o, axis=-1, keepdims=True)
                o_ref[p["rows"], p["vc"]] = (o * lax.rsqrt(ms + EPS) * onorm) * _silu(g_ref[p["rows"], p["vc"]])

    lax.fori_loop(0, nsub // unroll, block, 0)

    @pl.when(c == nchunks - 1)
    def _():
        so_ref[...] = s_scr[...]


def _hgrn2(q, lf, kf, v, gate, onorm, s0, *, row0, seq_len, rows_per_step, sub, seqs_per_step):
    nseq_total, heads, dk, dv = s0.shape
    assert dk == dv, "state decay is applied with a (dk, dv) tile"
    width = q.shape[1]
    cg = rows_per_step
    nseq = seqs_per_step
    nchunks = seq_len // cg
    assert nseq == 1 or nchunks == 1, "several sequences per step only when a step covers whole sequences"
    rows = nseq * cg
    b0 = row0 // rows
    assert sub & (sub - 1) == 0 and sub % 8 == 0 and rows & (rows - 1) == 0
    unroll = 2 if (cg // sub) % 2 == 0 else 1
    row_spec = pl.BlockSpec((rows, width), lambda b, c: (b0 + b * nchunks + c, 0))
    st_spec = pl.BlockSpec((nseq, heads, dk, dv), lambda b, c: (b, 0, 0, 0))
    kern = functools.partial(_hgrn2_kernel, heads=heads, dk=dk, dv=dv, sub=sub, nsub=cg // sub, nseq=nseq,
                             nchunks=nchunks, group=4 if sub > 8 else 8, unroll=unroll)
    return pl.pallas_call(
        kern,
        grid=(nseq_total // nseq, nchunks),
        in_specs=[row_spec, row_spec, row_spec, row_spec, row_spec,
                  pl.BlockSpec((1, dv), lambda b, c: (0, 0)), st_spec],
        out_specs=[pl.BlockSpec((rows, heads * dv), lambda b, c: (b * nchunks + c, 0)), st_spec],
        out_shape=[jax.ShapeDtypeStruct((nseq_total * seq_len, heads * dv), F32),
                   jax.ShapeDtypeStruct(s0.shape, F32)],
        scratch_shapes=[pltpu.VMEM((nseq, heads, dk, dv), F32), pltpu.VMEM((rows, width), F32),
                        pltpu.VMEM((unroll, nseq, sub, width), F32), pltpu.VMEM((unroll, nseq, sub, width), F32),
                        pltpu.VMEM((unroll, nseq, sub, heads * dv), F32)],
        compiler_params=_cp(("arbitrary", "arbitrary")),
        name="hgrn2_recurrence",
    )(q, lf, kf, v, gate, onorm.reshape(1, dv), s0)


def _gdn_kernel(qkv_ref, z_ref, bg_ref, bgt_ref, cw_ref, on_ref, s0_ref, c0_ref,
                o_ref, so_ref, co_ref, s_scr, xx, xs,
                *, kheads, vheads, dk, dv, chunk, nsub, nseq, nchunks, group_kh):
    c = pl.program_id(1)
    cg = chunk * nsub
    kdim = kheads * dk
    rep = vheads // kheads
    tail0 = 8 - (CONV_K - 1)

    @pl.when(c == 0)
    def _():
        s_scr[...] = s0_ref[...]
        xx[:, 0:8, :] = c0_ref[...]

    for q in range(nseq):
        xx[q, 8:8 + cg, :] = qkv_ref[q * cg:(q + 1) * cg, :]
        conv = xx[q, pl.ds(tail0, cg), :] * cw_ref[0:1, :]
        for j in range(1, CONV_K):
            conv = conv + xx[q, pl.ds(tail0 + j, cg), :] * cw_ref[j:j + 1, :]
        xs[q] = _silu(conv)

    @pl.when(c == nchunks - 1)
    def _():
        for q in range(nseq):
            co_ref[q] = xx[q, pl.ds(8 + cg - (CONV_K - 1), CONV_K - 1), :]

    for q in range(nseq):
        xx[q, 0:8, :] = xx[q, cg:cg + 8, :]

    wp = chunk
    r_i = _iota((chunk, wp), 0)
    c_i = _iota((chunk, wp), 1)
    causal = r_i >= c_i
    strict = r_i > c_i

    def pad_rows(a):
        if a.shape[0] == wp:
            return a
        if a.dtype == BF and a.shape[0] % 16:
            return pad_rows(a.astype(F32)).astype(BF)
        return jnp.concatenate([a, jnp.zeros((wp - a.shape[0],) + a.shape[1:], a.dtype)], axis=0)

    def dot3_padded(a, b):
        a_hi = a.astype(BF).astype(F32)
        lhs = jnp.concatenate([a_hi, a_hi, a - a_hi], axis=1).astype(BF)
        bh, bl = (pad_rows(t) for t in _split(b))
        return _dot(lhs, jnp.concatenate([bh, bl, bh], axis=0))
    cm = chunk - 1
    tril3 = jnp.where(_iota((chunk, 3 * chunk), 0) >= (_iota((chunk, 3 * chunk), 1) & cm), 1.0, 0.0).astype(BF)
    t_r = _iota((3 * chunk, wp), 0) & cm
    t_c = _iota((3 * chunk, wp), 1)
    triu3 = jnp.where((t_r <= t_c) & (t_c < chunk), 1.0, 0.0).astype(BF)
    onorm = on_ref[...]

    cums = {}

    def cumulative_decay(q, s):
        if (q, s) not in cums:
            r0 = q * cg + s * chunk
            bg = bg_ref[r0:r0 + chunk, :]
            gt = bgt_ref[0, vheads:2 * vheads, r0:r0 + chunk]
            cums[(q, s)] = (bg, _dot(tril3, _split3(bg[:, vheads:2 * vheads], axis=0)),
                            _dot(_split3(gt, axis=1), triu3))
        return cums[(q, s)]

    def run_group(items):
        probs = []
        for (q, s, kh) in items:
            r0 = q * cg + s * chunk
            x0 = s * chunk
            bg, gcol_all, grow_all = cumulative_decay(q, s)
            for kh in (kh,):
                qh = xs[q, x0:x0 + chunk, kh * dk:(kh + 1) * dk]
                kk_ = xs[q, x0:x0 + chunk, kdim + kh * dk:kdim + (kh + 1) * dk]
                qn = qh * lax.rsqrt(jnp.sum(qh * qh, axis=-1, keepdims=True) + EPS) * (dk ** -0.5)
                kn = kk_ * lax.rsqrt(jnp.sum(kk_ * kk_, axis=-1, keepdims=True) + EPS)
                kn_bf = kn.astype(BF)
                kn_pad = pad_rows(kn_bf)
                kk = _dot_nt(kn_bf, kn_pad)
                qk = _dot_nt(qn.astype(BF), kn_pad)
                for r in range(rep):
                    h = kh * rep + r
                    vh_ = xs[q, x0:x0 + chunk, 2 * kdim + h * dv:2 * kdim + (h + 1) * dv]
                    gcol = gcol_all[:, h:h + 1]
                    grow = grow_all[h:h + 1, :]
                    beta = bg[:, h:h + 1]
                    decay = jnp.where(causal, jnp.exp(jnp.minimum(gcol - grow, 0.0)), 0.0)
                    eg = jnp.exp(gcol)
                    gend = gcol[chunk - 1:chunk, :]
                    probs.append(dict(
                        q=q, s=s, h=h, r0=r0,
                        x=jnp.where(strict, -(beta * kk * decay), 0.0),
                        y=jnp.concatenate([vh_ * beta, kn * (beta * eg)], axis=1),
                        a_bf=(qk * decay).astype(BF),
                        qg=qn * eg,
                        kdec_bf=(kn * jnp.exp(gend - gcol)).astype(BF),
                        send=jnp.exp(gend)))

        if chunk <= 16:
            for j in range(chunk - 1):
                for p in probs:
                    p["y"] = p["y"] + p["x"][:, j:j + 1] * p["y"][j:j + 1, :]
        else:
            n_fac = int(math.log2(chunk))
            for k in range(n_fac):
                for p in probs:
                    p["y"] = p["y"] + dot3_padded(p["x"], p["y"])
                if k + 1 < n_fac:
                    for p in probs:
                        p["x"] = dot3_padded(p["x"], p["x"])

        for s in sorted({p["s"] for p in probs}):
            cur = [p for p in probs if p["s"] == s]
            for p in cur:
                p["s_old"] = s_scr[p["q"], p["h"]]
                lhs = jnp.concatenate([p["y"][:, dv:], p["qg"]], axis=0).astype(BF)
                p["ws"] = _dot(lhs, p["s_old"].astype(BF))
            for p in cur:
                p["u_bf"] = (p["y"][:, :dv] - p["ws"][:chunk]).astype(BF)
            for p in cur:
                s_scr[p["q"], p["h"]] = p["send"] * p["s_old"] + _dot_tn(p["kdec_bf"], p["u_bf"])
            for p in cur:
                o = p["ws"][chunk:] + _dot(p["a_bf"], pad_rows(p["u_bf"]))
                ms = jnp.mean(o * o, axis=-1, keepdims=True)
                zc = slice(p["h"] * dv, (p["h"] + 1) * dv)
                rows = slice(p["r0"], p["r0"] + chunk)
                o_ref[rows, zc] = (o * lax.rsqrt(ms + EPS) * onorm) * _silu(z_ref[rows, zc])

    for g0 in range(0, kheads, group_kh):
        run_group([(q, s, kh) for q in range(nseq) for s in range(nsub)
                   for kh in range(g0, min(g0 + group_kh, kheads))])

    @pl.when(c == nchunks - 1)
    def _():
        so_ref[...] = s_scr[...]


def _gdn(qkv, z, bg, bgt3, conv_w, onorm, s0, conv0, *, row0, seq_len, rows_per_step, chunk, seqs_per_step):
    nseq_total, vheads, dk, dv = s0.shape
    conv_dim = qkv.shape[1]
    kheads = (conv_dim - vheads * dv) // (2 * dk)
    cg = rows_per_step
    nseq = seqs_per_step
    nchunks = seq_len // cg
    assert nseq == 1 or nchunks == 1, "several sequences per step only when a step covers whole sequences"
    rows = nseq * cg
    b0 = row0 // rows
    conv0p = jnp.pad(conv0, ((0, 0), (8 - (CONV_K - 1), 0), (0, 0)))
    rmap = lambda b, c: (b0 + b * nchunks + c, 0)
    st_spec = pl.BlockSpec((nseq, vheads, dk, dv), lambda b, c: (b, 0, 0, 0))
    kern = functools.partial(_gdn_kernel, kheads=kheads, vheads=vheads, dk=dk, dv=dv, chunk=chunk,
                             nsub=cg // chunk, nseq=nseq, nchunks=nchunks, group_kh=kheads)
    return pl.pallas_call(
        kern,
        grid=(nseq_total // nseq, nchunks),
        in_specs=[pl.BlockSpec((rows, conv_dim), rmap),
                  pl.BlockSpec((rows, vheads * dv), rmap),
                  pl.BlockSpec((rows, 2 * vheads), rmap),
                  pl.BlockSpec((1, 2 * vheads, rows), lambda b, c: (b0 + b * nchunks + c, 0, 0)),
                  pl.BlockSpec((CONV_K, conv_dim), lambda b, c: (0, 0)),
                  pl.BlockSpec((1, dv), lambda b, c: (0, 0)),
                  st_spec,
                  pl.BlockSpec((nseq, 8, conv_dim), lambda b, c: (b, 0, 0))],
        out_specs=[pl.BlockSpec((rows, vheads * dv), lambda b, c: (b * nchunks + c, 0)),
                   st_spec,
                   pl.BlockSpec((nseq, CONV_K - 1, conv_dim), lambda b, c: (b, 0, 0))],
        out_shape=[jax.ShapeDtypeStruct((nseq_total * seq_len, vheads * dv), F32),
                   jax.ShapeDtypeStruct(s0.shape, F32),
                   jax.ShapeDtypeStruct((nseq_total, CONV_K - 1, conv_dim), F32)],
        scratch_shapes=[pltpu.VMEM((nseq, vheads, dk, dv), F32), pltpu.VMEM((nseq, 8 + cg, conv_dim), F32),
                        pltpu.VMEM((nseq, cg, conv_dim), F32)],
        compiler_params=_cp(("arbitrary", "arbitrary")),
        name="gdn_recurrence",
    )(qkv, z, bg, bgt3, conv_w, onorm.reshape(1, dv), s0, conv0p)


def _merge_kernel(oa1_ref, oa2_ref, ob1_ref, ob2_ref, ga_ref, gb_ref, wa_ref, wb_ref, o_ref, *, na):
    i = pl.program_id(1)

    def body(oa_ref, ob_ref):
        ya = _dot(oa_ref[...].astype(BF), wa_ref[...])
        yb = _dot(ob_ref[...].astype(BF), wb_ref[...])
        o_ref[...] = (_sigmoid(ga_ref[...]) * ya + _sigmoid(gb_ref[...]) * yb).astype(o_ref.dtype)

    @pl.when(i < na)
    def _():
        body(oa1_ref, ob1_ref)

    @pl.when(i >= na)
    def _():
        body(oa2_ref, ob2_ref)


def _merge(oa_p, oa_s, ob_p, ob_s, gates, w_pa, w_pb):
    rp, rs = oa_p.shape[0], oa_s.shape[0]
    ka, kb = oa_p.shape[1], ob_p.shape[1]
    d = w_pa.shape[1]
    tm = _pick(math.gcd(rp, rs), 512)
    tn = _pick(d, 512, LANES)
    na, nb = rp // tm, rs // tm
    nj = d // tn
    amap = lambda j, i: (jnp.minimum(i, na - 1), 0)
    bmap = lambda j, i: (jnp.maximum(i - na, 0), 0)
    return pl.pallas_call(
        functools.partial(_merge_kernel, na=na),
        grid=(nj, na + nb),
        in_specs=[pl.BlockSpec((tm, ka), amap), pl.BlockSpec((tm, ka), bmap),
                  pl.BlockSpec((tm, kb), amap), pl.BlockSpec((tm, kb), bmap),
                  pl.BlockSpec((tm, tn), lambda j, i: (i, j)),
                  pl.BlockSpec((tm, tn), lambda j, i: (i, nj + j)),
                  pl.BlockSpec((ka, tn), lambda j, i: (0, j)),
                  pl.BlockSpec((kb, tn), lambda j, i: (0, j))],
        out_specs=pl.BlockSpec((tm, tn), lambda j, i: (i, j)),
        out_shape=jax.ShapeDtypeStruct((rp + rs, d), BF),
        compiler_params=_cp(("arbitrary", "arbitrary")),
        name="branch_proj_merge",
    )(oa_p, oa_s, ob_p, ob_s, gates, gates, w_pa.astype(BF), w_pb.astype(BF))


def _outproj_kernel(m_ref, xa_ref, xb_ref, wo_ref, nw_ref, wr_ref, h_ref, xf_ref, lg_ref, *, na):
    i = pl.program_id(0)

    def body(x_ref):
        h = x_ref[...] + _dot(m_ref[...], wo_ref[...])
        h_ref[...] = h
        ms = jnp.mean(h * h, axis=-1, keepdims=True)
        xf = h * lax.rsqrt(ms + EPS) * nw_ref[...]
        _store_token_major(xf_ref, xf)
        lg_ref[...] = _dot(xf.astype(BF), wr_ref[...])

    @pl.when(i < na)
    def _():
        body(xa_ref)

    @pl.when(i >= na)
    def _():
        body(xb_ref)


def _outproj(merged, xa, xb, w_out, norm_ffn, w_router):
    t, d = merged.shape
    tm = _pick(math.gcd(xa.shape[0], xb.shape[0]), 512)
    na, nb, sa, sb = _two_source_specs(xa.shape[0], xb.shape[0], tm, d)
    nr = w_router.shape[1]
    const = lambda i: (0, 0)
    return pl.pallas_call(
        functools.partial(_outproj_kernel, na=na),
        grid=(na + nb,),
        in_specs=[pl.BlockSpec((tm, d), lambda i: (i, 0)), sa, sb,
                  pl.BlockSpec((d, d), const, pipeline_mode=pl.Buffered(1)),
                  pl.BlockSpec((1, d), const),
                  pl.BlockSpec((d, nr), const, pipeline_mode=pl.Buffered(1))],
        out_specs=[pl.BlockSpec((tm, d), lambda i: (i, 0)),
                   pl.BlockSpec((tm * (d // LANES), LANES), lambda i: (i, 0)),
                   pl.BlockSpec((tm, nr), lambda i: (i, 0))],
        out_shape=[jax.ShapeDtypeStruct((t, d), F32), jax.ShapeDtypeStruct((t * (d // LANES), LANES), F32),
                   jax.ShapeDtypeStruct((t, nr), F32)],
        compiler_params=_cp(("arbitrary",), 56),
        name="out_proj_ffn_norm",
    )(merged, xa, xb, w_out.astype(BF), norm_ffn.reshape(1, d), w_router)


def _route_kernel(lg_ref, b_ref, eid_ref, gate_ref, rank_ref, cnt_ref, carry,
                  *, ngroups, nexp, tm):
    i = pl.program_id(0)

    @pl.when(i == 0)
    def _():
        carry[...] = jnp.zeros_like(carry)

    per_group = nexp // ngroups
    lg = lg_ref[...] + b_ref[...]
    lane = _iota(lg.shape, 1)
    big = jnp.int32(1 << 20)
    is_g = lane < ngroups
    gl = jnp.where(is_g, lg, NEG)
    gmax = jnp.max(gl, axis=-1, keepdims=True)
    gidx = jnp.min(jnp.where(gl == gmax, lane, big), axis=-1, keepdims=True)
    gsum = jnp.sum(jnp.where(is_g, jnp.exp(gl - gmax), 0.0), axis=-1, keepdims=True)
    gw = 1.0 / gsum
    elane = lane - ngroups
    in_grp = (elane >= gidx * per_group) & (elane < (gidx + 1) * per_group)
    el = jnp.where(in_grp, lg, NEG)
    v1 = jnp.max(el, axis=-1, keepdims=True)
    i1 = jnp.min(jnp.where(in_grp & (el == v1), elane, big), axis=-1, keepdims=True)
    in2 = in_grp & (elane != i1)
    el2 = jnp.where(in2, lg, NEG)
    v2 = jnp.max(el2, axis=-1, keepdims=True)
    i2 = jnp.min(jnp.where(in2 & (el2 == v2), elane, big), axis=-1, keepdims=True)
    p2 = jnp.exp(v2 - v1)
    den = 1.0 + p2
    lane2 = _iota((tm, TOP_K), 1)
    eid_ref[...] = jnp.where(lane2 == 0, i1, i2)
    gate_ref[...] = jnp.where(lane2 == 0, gw / den, gw * p2 / den)

    oh1 = (elane == i1).astype(F32)
    oh2 = (elane == i2).astype(F32)
    lower = (_iota((tm, tm), 0) > _iota((tm, tm), 1)).astype(BF)
    cs1 = _dot(lower, oh1.astype(BF))
    cs2 = _dot(lower, oh2.astype(BF))
    tot1 = jnp.sum(oh1, axis=0, keepdims=True)
    tot2 = jnp.sum(oh2, axis=0, keepdims=True)
    base = carry[0:1, :]
    r1 = jnp.sum(oh1 * (base + cs1), axis=-1, keepdims=True)
    r2 = jnp.sum(oh2 * (base + tot1 + cs2), axis=-1, keepdims=True)
    rank_ref[...] = jnp.where(lane2 == 0, r1, r2).astype(I32)
    new = base + tot1 + tot2
    carry[...] = jnp.broadcast_to(new, carry.shape)
    cnt_ref[...] = jnp.broadcast_to(new, cnt_ref.shape)


def _route(logits, bias_row, ngroups, nexp):
    t, nr = logits.shape
    tm = _pick(t, 512)
    kern = functools.partial(_route_kernel, ngroups=ngroups, nexp=nexp, tm=tm)
    return pl.pallas_call(
        kern,
        grid=(t // tm,),
        in_specs=[pl.BlockSpec((tm, nr), lambda i: (i, 0)), pl.BlockSpec((1, nr), lambda i: (0, 0))],
        out_specs=[pl.BlockSpec((tm, TOP_K), lambda i: (i, 0)),
                   pl.BlockSpec((tm, TOP_K), lambda i: (i, 0)),
                   pl.BlockSpec((tm, TOP_K), lambda i: (i, 0)),
                   pl.BlockSpec((8, nr), lambda i: (0, 0))],
        out_shape=[jax.ShapeDtypeStruct((t, TOP_K), I32), jax.ShapeDtypeStruct((t, TOP_K), F32),
                   jax.ShapeDtypeStruct((t, TOP_K), I32), jax.ShapeDtypeStruct((8, nr), F32)],
        scratch_shapes=[pltpu.VMEM((8, nr), F32)],
        compiler_params=_cp(("arbitrary",)),
        name="route_topk_rank",
    )(logits, bias_row)


def _plan_kernel(cnt_ref, eid_ref, rank_ref, pos_ref, be_ref, *, ngroups, nexp, nblocks_pad, tm):
    nr = cnt_ref.shape[1]
    cnt = cnt_ref[0:1, :]
    padded = jnp.floor((cnt + (MOE_BLOCK - 1)) * (1.0 / MOE_BLOCK)) * MOE_BLOCK
    r_i = _iota((nr, nr), 0)
    c_i = _iota((nr, nr), 1)
    padded_col = jnp.sum(jnp.where(r_i == c_i, jnp.broadcast_to(padded, (nr, nr)), 0.0), axis=1, keepdims=True)
    start = jnp.sum(jnp.where(r_i < c_i, jnp.broadcast_to(padded_col, (nr, nr)), 0.0), axis=0, keepdims=True)
    end = start + padded
    lane = _iota((tm, nr), 1)
    eid = eid_ref[...]
    rank = rank_ref[...]
    lane2 = _iota((tm, TOP_K), 1)
    pos = jnp.zeros((tm, TOP_K), F32)
    for k in range(TOP_K):
        oh = (lane - ngroups) == eid[:, k:k + 1]
        st = jnp.sum(jnp.where(oh, start, 0.0), axis=-1, keepdims=True)
        pos = jnp.where(lane2 == k, st, pos)
    pos_ref[...] = pos.astype(I32) + rank

    @pl.when(pl.program_id(0) == 0)
    def _():
        blk_row = _iota((nblocks_pad, nr), 0).astype(F32) * MOE_BLOCK
        lane_b = _iota((nblocks_pad, nr), 1)
        is_e = (lane_b >= ngroups) & (lane_b < ngroups + nexp)
        n_le = jnp.sum(jnp.where(is_e & (end <= blk_row), 1.0, 0.0), axis=-1, keepdims=True)
        e_lane = (lane_b - ngroups).astype(F32)
        nonempty = is_e & (padded > 0.0)
        e_last = jnp.max(jnp.where(nonempty, e_lane, -1.0), axis=-1, keepdims=True)
        be = jnp.minimum(n_le, e_last)
        nxt = jnp.min(jnp.where(nonempty & (e_lane > be), e_lane, 1e9), axis=-1, keepdims=True)
        nxt = jnp.where(nxt > 1e8, -1.0, nxt)
        total = jnp.sum(padded, axis=-1, keepdims=True)
        valid = jnp.where(blk_row[:, 0:1] < total, 1.0, 0.0)
        col = _iota((nblocks_pad, 4), 1)
        meta = jnp.where(col == 0, be, jnp.where(col == 1, nxt, jnp.where(col == 2, valid, 0.0)))
        be_ref[...] = meta.astype(I32)


def _plan(cnt, eid, rank, ngroups, nexp, nblocks):
    t = eid.shape[0]
    nr = cnt.shape[1]
    tm = _pick(t, 512)
    nblocks_pad = -(-nblocks // 8) * 8
    kern = functools.partial(_plan_kernel, ngroups=ngroups, nexp=nexp, nblocks_pad=nblocks_pad, tm=tm)
    return pl.pallas_call(
        kern,
        grid=(t // tm,),
        in_specs=[pl.BlockSpec((8, nr), lambda i: (0, 0)),
                  pl.BlockSpec((tm, TOP_K), lambda i: (i, 0)),
                  pl.BlockSpec((tm, TOP_K), lambda i: (i, 0))],
        out_specs=[pl.BlockSpec((tm, TOP_K), lambda i: (i, 0)),
                   pl.BlockSpec((nblocks_pad, 4), lambda i: (0, 0))],
        out_shape=[jax.ShapeDtypeStruct((t, TOP_K), I32), jax.ShapeDtypeStruct((nblocks_pad, 4), I32)],
        compiler_params=_cp(("arbitrary",)),
        name="route_plan",
    )(cnt, eid, rank)


def _invert_kernel(pos_ref, tok_ref, *, n_assign, n_rows):
    def clear(r, c):
        tok_ref[r] = 0
        return c

    lax.fori_loop(0, n_rows, clear, 0, unroll=16)

    def put(a, c):
        tok_ref[pos_ref[a]] = lax.shift_right_logical(a, TOP_K.bit_length() - 1)
        return c

    lax.fori_loop(0, n_assign, put, 0, unroll=16)


def _invert(pos_flat, n_rows):
    n_assign = pos_flat.shape[0]
    return pl.pallas_call(
        functools.partial(_invert_kernel, n_assign=n_assign, n_rows=n_rows),
        in_specs=[pl.BlockSpec(memory_space=pltpu.SMEM)],
        out_specs=pl.BlockSpec(memory_space=pltpu.SMEM),
        out_shape=jax.ShapeDtypeStruct((n_rows,), I32),
        name="route_invert",
    )(pos_flat)


def _expert_kernel(be_ref, nxt_ref, valid_ref, tok_ref, x_hbm, wg_hbm, wu_hbm, wd_hbm, y_ref,
                   xbuf0, xbuf1, xsem, wg_st, wu_st, wd_st, wsem, wg_bf, wu_bf, wd_bf, *, nch):
    i = pl.program_id(0)
    n = pl.num_programs(0)
    xbufs = (xbuf0, xbuf1)

    def row_copy(blk, s, r):
        t = tok_ref[blk * MOE_BLOCK + r]
        src = x_hbm.at[pl.ds(pl.multiple_of(t * nch, nch), nch), :]
        return pltpu.make_async_copy(src, xbufs[s].at[pl.ds(r * nch, nch), :], xsem.at[s])

    def block_wait(s):
        pltpu.make_async_copy(x_hbm.at[pl.ds(0, MOE_BLOCK * nch), :], xbufs[s], xsem.at[s]).wait()

    def weight_copies(e):
        return (pltpu.make_async_copy(wg_hbm.at[e], wg_st, wsem.at[0]),
                pltpu.make_async_copy(wu_hbm.at[e], wu_st, wsem.at[1]),
                pltpu.make_async_copy(wd_hbm.at[e], wd_st, wsem.at[2]))

    @pl.when(i == 0)
    def _():
        for r in range(MOE_BLOCK):
            row_copy(0, 0, r).start()
        for cp in weight_copies(be_ref[0]):
            cp.start(priority=1)

    prev = be_ref[jnp.maximum(i - 1, 0)]

    @pl.when((i == 0) | (be_ref[i] != prev))
    def _():
        for cp in weight_copies(be_ref[i]):
            cp.wait()
        wg_bf[...] = wg_st[...].astype(BF)
        wu_bf[...] = wu_st[...].astype(BF)
        wd_bf[...] = wd_st[...].astype(BF)

        @pl.when(nxt_ref[i] >= 0)
        def _():
            for cp in weight_copies(nxt_ref[i]):
                cp.start(priority=1)

    ahead = jnp.minimum(i + 1, n - 1)
    used = valid_ref[i] != 0
    used_ahead = (i + 1 < n) & (valid_ref[ahead] != 0)

    def step(cur, oth):
        @pl.when(used_ahead)
        def _():
            for r in range(MOE_BLOCK):
                row_copy(ahead, oth, r).start()

        block_wait(cur)
        x = _load_token_major(xbufs[cur], MOE_BLOCK, nch).astype(BF)
        hid = (_silu(_dot(x, wg_bf[...])) * _dot(x, wu_bf[...])).astype(BF)
        _store_token_major(y_ref, _dot(hid, wd_bf[...]))

    @pl.when(used & (i % 2 == 0))
    def _():
        step(0, 1)

    @pl.when(used & (i % 2 == 1))
    def _():
        step(1, 0)

    @pl.when(jnp.logical_not(used))
    def _():
        y_ref[...] = jnp.zeros_like(y_ref)


def _experts(block_e, block_next, block_valid, row_tok, xf, w_eg, w_eu, w_ed):
    nrows = row_tok.shape[0]
    d = w_eg.shape[1]
    nch = d // LANES
    nblocks = nrows // MOE_BLOCK
    de = w_eg.shape[2]
    any_spec = pl.BlockSpec(memory_space=pl.ANY)
    grid_spec = pltpu.PrefetchScalarGridSpec(
        num_scalar_prefetch=4,
        grid=(nblocks,),
        in_specs=[any_spec, any_spec, any_spec, any_spec],
        out_specs=pl.BlockSpec((MOE_BLOCK * nch, LANES), lambda i, *_: (i, 0)),
        scratch_shapes=[pltpu.VMEM((MOE_BLOCK * nch, LANES), F32), pltpu.VMEM((MOE_BLOCK * nch, LANES), F32),
                        pltpu.SemaphoreType.DMA((2,)),
                        pltpu.VMEM((d, de), F32), pltpu.VMEM((d, de), F32), pltpu.VMEM((de, d), F32),
                        pltpu.SemaphoreType.DMA((3,)),
                        pltpu.VMEM((d, de), BF), pltpu.VMEM((d, de), BF), pltpu.VMEM((de, d), BF)],
    )
    return pl.pallas_call(
        functools.partial(_expert_kernel, nch=nch),
        grid_spec=grid_spec,
        out_shape=jax.ShapeDtypeStruct((nrows * nch, LANES), F32),
        compiler_params=_cp(("arbitrary",), 48),
        name="expert_blocks",
    )(block_e, block_next, block_valid, row_tok, xf, w_eg, w_eu, w_ed)


def _final_kernel(pos_ref, h_ref, gate_ref, w_ref, yb_hbm, ya_ref, ys_ref, gbuf, sem, *, na, tm, nch):
    i = pl.program_id(0)
    n = pl.num_programs(0)
    slot = i % 2

    def row_copy(tile, slot_, r, k):
        p = pos_ref[(tile * tm + r) * TOP_K + k]
        src = yb_hbm.at[pl.ds(pl.multiple_of(p * nch, nch), nch), :]
        return pltpu.make_async_copy(src, gbuf.at[slot_, k, pl.ds(r * nch, nch), :], sem.at[slot_])

    def start_tile(tile, slot_):
        for r in range(tm):
            for k in range(TOP_K):
                row_copy(tile, slot_, r, k).start()

    @pl.when(i == 0)
    def _():
        start_tile(0, 0)

    @pl.when(i + 1 < n)
    def _():
        start_tile(i + 1, 1 - slot)

    for r in range(tm):
        for k in range(TOP_K):
            row_copy(i, slot, r, k).wait()

    gate = gate_ref[...]
    y = h_ref[...]
    ffn = _load_token_major(gbuf.at[slot, 0], tm, nch) * gate[:, 0:1]
    for k in range(1, TOP_K):
        ffn = ffn + _load_token_major(gbuf.at[slot, k], tm, nch) * gate[:, k:k + 1]
    y = y + ffn
    ms = jnp.mean(y * y, axis=-1, keepdims=True)
    out = y * lax.rsqrt(ms + EPS) * w_ref[...]

    @pl.when(i < na)
    def _():
        ya_ref[...] = out

    @pl.when(i >= na)
    def _():
        ys_ref[...] = out


def _final(pos_flat, h, gate, yb, w, rows_a, rows_b):
    t, d = h.shape
    tm = _pick(math.gcd(rows_a, rows_b), 128)
    na, nb = rows_a // tm, rows_b // tm
    grid_spec = pltpu.PrefetchScalarGridSpec(
        num_scalar_prefetch=1,
        grid=(na + nb,),
        in_specs=[pl.BlockSpec((tm, d), lambda i, pos: (i, 0)),
                  pl.BlockSpec((tm, TOP_K), lambda i, pos: (i, 0)),
                  pl.BlockSpec((1, d), lambda i, pos: (0, 0)),
                  pl.BlockSpec(memory_space=pl.ANY)],
        out_specs=[pl.BlockSpec((tm, d), lambda i, pos: (jnp.minimum(i, na - 1), 0)),
                   pl.BlockSpec((tm, d), lambda i, pos: (jnp.maximum(i - na, 0), 0))],
        scratch_shapes=[pltpu.VMEM((2, TOP_K, tm * (d // LANES), LANES), F32), pltpu.SemaphoreType.DMA((2,))],
    )
    return pl.pallas_call(
        functools.partial(_final_kernel, na=na, tm=tm, nch=d // LANES),
        grid_spec=grid_spec,
        out_shape=[jax.ShapeDtypeStruct((rows_a, d), F32), jax.ShapeDtypeStruct((rows_b, d), F32)],
        compiler_params=_cp(("arbitrary",)),
        name="combine_final_norm",
    )(pos_flat, h, gate, w.reshape(1, d), yb)


def kernel(x_prompt, x_sample, state_hgrn, state_gdn, state_conv, norm_mix, w_in, conv_w, ha_lb_logits, ha_onorm, w_pa, gd_A_log, gd_dt_bias, gd_onorm, w_pb, w_out, norm_ffn, w_router_group, b_router_group, w_router_expert, b_router_expert, w_exp_gate, w_exp_up, w_exp_down, norm_final):
    depth = state_hgrn.shape[0]
    assert depth == 1, "one decoder layer"
    bp, lp, d = x_prompt.shape
    bs, ls, _ = x_sample.shape
    _, _, ha_heads, ha_dk, ha_dv = state_hgrn.shape
    _, _, vheads, gd_dk, gd_dv = state_gdn.shape
    conv_dim = state_conv.shape[3]
    ha_kdim, ha_vdim = ha_heads * ha_dk, ha_heads * ha_dv
    gd_vdim = vheads * gd_dv
    ngroups = w_router_group.shape[2]
    nexp = w_router_expert.shape[2]
    rp, rs = bp * lp, bs * ls
    t = rp + rs

    off_f = ha_kdim
    off_i = off_f + ha_kdim
    off_g = off_i + ha_vdim
    off_qkv = off_g + ha_vdim
    off_z = off_qkv + conv_dim
    off_b = off_z + gd_vdim
    off_gate = off_b + 2 * vheads
    w_in0 = w_in[0]

    xn = _rmsnorm_bf16(x_prompt.reshape(rp, d), x_sample.reshape(rs, d), norm_mix[0])

    ident = lambda a: (a,)
    (q_a,) = _proj(xn, w_in0, 0, ha_kdim, lambda a: (a * (ha_dk ** -0.5),), [F32], name="proj_hgrn_q")
    logf, k_a = _proj(xn, w_in0, off_f, ha_kdim, _forget_epilogue, [F32, F32],
                      extras=(ha_lb_logits,), name="proj_hgrn_forget")
    (v_a,) = _proj(xn, w_in0, off_i, ha_vdim, ident, [F32], name="proj_hgrn_in")
    (g_a,) = _proj(xn, w_in0, off_g, ha_vdim, ident, [F32], name="proj_hgrn_gate")
    (qkv,) = _proj(xn, w_in0, off_qkv, conv_dim, ident, [F32], name="proj_gdn_qkv")
    (z_b,) = _proj(xn, w_in0, off_z, gd_vdim, ident, [F32], name="proj_gdn_z")
    (gates,) = _proj(xn, w_in0[:, off_gate:], 0, 2 * d, ident, [F32], name="proj_merge_gates")
    bg, bgt = _beta_decay(xn, w_in0[:, off_b:off_gate], gd_A_log[0], gd_dt_bias[0])

    cg_p = _pick(lp, 128, GDN_CHUNK) if lp >= GDN_CHUNK else lp
    sub_p = min(HGRN_SUB, lp)
    chunk_p = min(GDN_CHUNK, lp)
    cg_s = ls
    zeros_h = jnp.zeros((bp, ha_heads, ha_dk, ha_dv), F32)
    zeros_g = jnp.zeros((bp, vheads, gd_dk, gd_dv), F32)
    zeros_c = jnp.zeros((bp, CONV_K - 1, conv_dim), F32)

    oa_p, sh_p = _hgrn2(q_a, logf, k_a, v_a, g_a, ha_onorm[0], zeros_h,
                        row0=0, seq_len=lp, rows_per_step=cg_p, sub=sub_p, seqs_per_step=1)
    oa_s, sh_s = _hgrn2(q_a, logf, k_a, v_a, g_a, ha_onorm[0], state_hgrn[0],
                        row0=rp, seq_len=ls, rows_per_step=cg_s, sub=min(HGRN_SUB, ls),
                        seqs_per_step=math.gcd(_pick(bs, HGRN_SAMPLE_SEQS, 1), max(rp // cg_s, 1)))

    def time_on_lanes(rows0, nrows, cg):
        part = bgt[:, rows0:rows0 + nrows].reshape(2 * vheads, nrows // cg, cg)
        return jnp.transpose(part, (1, 0, 2))

    seqs_s = _pick(bs, GDN_SAMPLE_SEQS, 1)
    cg_g = _pick(lp, GDN_ROWS, GDN_CHUNK) if lp >= GDN_CHUNK else lp
    bgt_p = time_on_lanes(0, rp, cg_g)
    bgt_s = time_on_lanes(rp, rs, seqs_s * cg_s)
    ob_p, sg_p, sc_p = _gdn(qkv, z_b, bg, bgt_p, conv_w[0], gd_onorm[0], zeros_g, zeros_c,
                            row0=0, seq_len=lp, rows_per_step=cg_g, chunk=chunk_p, seqs_per_step=1)
    ob_s, sg_s, sc_s = _gdn(qkv[rp:], z_b[rp:], bg[rp:], bgt_s, conv_w[0], gd_onorm[0], state_gdn[0],
                            state_conv[0], row0=0, seq_len=ls, rows_per_step=cg_s, chunk=min(GDN_CHUNK, ls),
                            seqs_per_step=seqs_s)

    merged = _merge(oa_p, oa_s, ob_p, ob_s, gates, w_pa[0], w_pb[0])

    nr = LANES
    w_router = jnp.concatenate([w_router_group[0], w_router_expert[0]], axis=1)
    w_router = jnp.pad(w_router, ((0, 0), (0, nr - ngroups - nexp))).astype(BF)
    b_router = jnp.pad(jnp.concatenate([b_router_group[0], b_router_expert[0]]), (0, nr - ngroups - nexp))
    h, xf, logits = _outproj(merged, x_prompt.reshape(rp, d), x_sample.reshape(rs, d), w_out[0],
                             norm_ffn[0], w_router)

    eid, gate, rank, cnt = _route(logits, b_router.reshape(1, nr), ngroups, nexp)
    nblocks = -(-(t * TOP_K) // MOE_BLOCK) + nexp
    pos, block_meta = _plan(cnt, eid, rank, ngroups, nexp, nblocks)
    block_e, block_next, block_valid = (block_meta[:nblocks, j] for j in range(3))

    pos_flat = pos.reshape(-1)
    row_tok = _invert(pos_flat, nblocks * MOE_BLOCK)
    yb = _experts(block_e, block_next, block_valid, row_tok, xf, w_exp_gate[0], w_exp_up[0], w_exp_down[0])
    y_p, y_s = _final(pos_flat, h, gate, yb, norm_final, rp, rs)
    return (y_p.reshape(bp, lp, d), y_s.reshape(bs, ls, d),
            sh_p[None], sg_p[None], sc_p[None], sh_s[None], sg_s[None], sc_s[None])
```

```python
import functools
import math

import jax
import jax.numpy as jnp
from jax import lax
from jax.experimental import pallas as pl
from jax.experimental.pallas import tpu as pltpu

F32 = jnp.float32
BF = jnp.bfloat16
I32 = jnp.int32
EPS = 1e-6
CONV_K = 4
TOP_K = 2
MOE_BLOCK = 128
GDN_CHUNK = 64
GDN_ROWS = 128
GDN_SAMPLE_SEQS = 4
HGRN_SUB = 16
HGRN_SAMPLE_SEQS = 4
LOG2E = 1.4426950408889634
LANES = 128
NEG = -3.0e38
HI = lax.Precision.HIGHEST
MIB = 1024 * 1024


def _cp(sem, vmem_mib=48):
    return pltpu.CompilerParams(dimension_semantics=sem, vmem_limit_bytes=vmem_mib * MIB)


def _pick(n, pref, mult=8):
    best = None
    for t in range(mult, min(n, pref) + 1, mult):
        if n % t == 0:
            best = t
    return best if best is not None else n


def _sigmoid(x):
    return 1.0 / (1.0 + jnp.exp(-x))


def _silu(x):
    return x * _sigmoid(x)


def _iota(shape, dim):
    return lax.broadcasted_iota(I32, shape, dim)


def _dot(a, b):
    return jnp.dot(a, b, preferred_element_type=F32)


def _dot_nt(a, b):
    return lax.dot_general(a, b, (((1,), (1,)), ((), ())), preferred_element_type=F32)


def _dot_tn(a, b):
    return lax.dot_general(a, b, (((0,), (0,)), ((), ())), preferred_element_type=F32)


def _split(a):
    hi = a.astype(BF)
    lo = (a - hi.astype(F32)).astype(BF)
    return hi, lo


def _split3(a, axis=0):
    hi = a.astype(BF)
    r1 = a - hi.astype(F32)
    mid = r1.astype(BF)
    lo = (r1 - mid.astype(F32)).astype(BF)
    return jnp.concatenate([hi, mid, lo], axis=axis)


def _dot3(a, b):
    ah, al = _split(a)
    bh, bl = _split(b)
    return _dot(ah, bh) + (_dot(ah, bl) + _dot(al, bh))


def _store_token_major(ref, x):
    rows, d = x.shape
    nch = d // LANES
    for c in range(nch):
        ref[pl.ds(c, rows, stride=nch), :] = x[:, c * LANES:(c + 1) * LANES]


def _load_token_major(ref, rows, nch):
    return jnp.concatenate([ref[pl.ds(c, rows, stride=nch), :] for c in range(nch)], axis=1)


def _two_source_specs(rows_a, rows_b, tm, width):
    na, nb = rows_a // tm, rows_b // tm
    spec_a = pl.BlockSpec((tm, width), lambda i: (jnp.minimum(i, na - 1), 0))
    spec_b = pl.BlockSpec((tm, width), lambda i: (jnp.maximum(i - na, 0), 0))
    return na, nb, spec_a, spec_b


def _rmsnorm_kernel(xa_ref, xb_ref, w_ref, o_ref, *, na):
    i = pl.program_id(0)

    def body(x_ref):
        x = x_ref[...]
        ms = jnp.mean(x * x, axis=-1, keepdims=True)
        o_ref[...] = (x * lax.rsqrt(ms + EPS) * w_ref[...]).astype(o_ref.dtype)

    @pl.when(i < na)
    def _():
        body(xa_ref)

    @pl.when(i >= na)
    def _():
        body(xb_ref)


def _rmsnorm_bf16(xa, xb, w):
    d = xa.shape[1]
    tm = _pick(math.gcd(xa.shape[0], xb.shape[0]), 512)
    na, nb, sa, sb = _two_source_specs(xa.shape[0], xb.shape[0], tm, d)
    return pl.pallas_call(
        functools.partial(_rmsnorm_kernel, na=na),
        grid=(na + nb,),
        in_specs=[sa, sb, pl.BlockSpec((1, d), lambda i: (0, 0))],
        out_specs=pl.BlockSpec((tm, d), lambda i: (i, 0)),
        out_shape=jax.ShapeDtypeStruct((xa.shape[0] + xb.shape[0], d), BF),
        compiler_params=_cp(("arbitrary",)),
        name="rmsnorm_mix",
    )(xa, xb, w.reshape(1, d))


def _inproj_kernel(x_ref, w_ref, lb_ref, p_ref, kf_ref, wbf, *, jq, jf0, jf1, q_scale):
    j = pl.program_id(0)

    @pl.when(pl.program_id(1) == 0)
    def _():
        wbf[...] = w_ref[...].astype(BF)

    acc = _dot(x_ref[...], wbf[...])
    is_forget = (j >= jf0) & (j < jf1)

    @pl.when(is_forget)
    def _():
        logf, kf = _forget_epilogue(acc, lb_ref[...])
        p_ref[...] = logf
        kf_ref[...] = kf

    @pl.when(jnp.logical_not(is_forget))
    def _():
        p_ref[...] = acc * jnp.where(j < jq, q_scale, 1.0)


def _inproj(x_bf, w, lb_logits, ncols, kdim, q_scale):
    t, k = x_bf.shape
    tn = _pick(math.gcd(ncols, kdim), 512, LANES)
    tm = _pick(t, 1024)
    ni = t // tm
    jq, jf0, jf1 = kdim // tn, kdim // tn, 2 * kdim // tn
    nf = jf1 - jf0
    fcol = lambda j: jnp.clip(j - jf0, 0, nf - 1)
    frow = lambda j, i: jnp.where(j < jf0, 0, jnp.where(j >= jf1, ni - 1, i))
    kern = functools.partial(_inproj_kernel, jq=jq, jf0=jf0, jf1=jf1, q_scale=q_scale)
    return pl.pallas_call(
        kern,
        grid=(ncols // tn, ni),
        in_specs=[pl.BlockSpec((tm, k), lambda j, i: (i, 0)),
                  pl.BlockSpec((k, tn), lambda j, i: (0, j)),
                  pl.BlockSpec((lb_logits.shape[0], tn), lambda j, i: (0, fcol(j)))],
        out_specs=[pl.BlockSpec((tm, tn), lambda j, i: (i, j)),
                   pl.BlockSpec((tm, tn), lambda j, i: (frow(j, i), fcol(j)))],
        out_shape=[jax.ShapeDtypeStruct((t, ncols), F32), jax.ShapeDtypeStruct((t, kdim), F32)],
        scratch_shapes=[pltpu.VMEM((k, tn), BF)],
        compiler_params=_cp(("arbitrary", "arbitrary")),
        name="in_proj",
    )(x_bf, w, lb_logits)


def _forget_epilogue(acc, lb_logits):
    m = jnp.max(lb_logits, axis=0, keepdims=True)
    e = jnp.exp(lb_logits - m)
    lb = e[0:1, :] / jnp.sum(e, axis=0, keepdims=True)
    f = lb + (1.0 - lb) * _sigmoid(acc)
    return jnp.log(f), 1.0 - f


def _beta_decay_kernel(x_ref, w_ref, wt_ref, prow_ref, pcol_ref, o_ref, ot_ref, *, vh):
    x = x_ref[...]
    acc = _dot(x, w_ref[...])
    acct = _dot_nt(wt_ref[...], x)

    def act(a, is_beta, a_neg_exp, dt_bias):
        z = a + dt_bias
        softplus = jnp.maximum(z, 0.0) + jnp.log(1.0 + jnp.exp(-jnp.abs(z)))
        return jnp.where(is_beta, _sigmoid(a), a_neg_exp * softplus)

    prow = prow_ref[...]
    pcol = pcol_ref[...]
    o_ref[...] = act(acc, _iota(acc.shape, 1) < vh, prow[0:1, :], prow[1:2, :])
    ot_ref[...] = act(acct, _iota(acct.shape, 0) < vh, pcol[:, 0:1], pcol[:, 1:2])


def _beta_decay(x_bf, w_ba, a_log, dt_bias):
    t, k = x_bf.shape
    vh = a_log.shape[0]
    tm = t if t <= 2048 else _pick(t, 1024, LANES)
    zeros = jnp.zeros((vh,), F32)
    prow = jnp.stack([jnp.concatenate([zeros, -jnp.exp(a_log)]), jnp.concatenate([zeros, dt_bias])])
    return pl.pallas_call(
        functools.partial(_beta_decay_kernel, vh=vh),
        grid=(t // tm,),
        in_specs=[pl.BlockSpec((tm, k), lambda i: (i, 0)),
                  pl.BlockSpec((k, 2 * vh), lambda i: (0, 0)),
                  pl.BlockSpec((2 * vh, k), lambda i: (0, 0)),
                  pl.BlockSpec((2, 2 * vh), lambda i: (0, 0)),
                  pl.BlockSpec((2 * vh, 2), lambda i: (0, 0))],
        out_specs=[pl.BlockSpec((tm, 2 * vh), lambda i: (i, 0)),
                   pl.BlockSpec((2 * vh, tm), lambda i: (0, i))],
        out_shape=[jax.ShapeDtypeStruct((t, 2 * vh), F32), jax.ShapeDtypeStruct((2 * vh, t), F32)],
        compiler_params=_cp(("arbitrary",)),
        name="proj_beta_decay",
    )(x_bf, w_ba.astype(BF), w_ba.T.astype(BF), prow, prow.T)


def _hgrn2_kernel(q_ref, lf_ref, k_ref, v_ref, g_ref, on_ref, s0_ref, o_ref, so_ref, s_scr, gall_scr,
                  g_scr, k_scr, v_scr, *, heads, dk, dv, sub, nsub, nseq, nchunks, group, unroll):
    c = pl.program_id(1)
    cg = sub * nsub

    @pl.when(c == 0)
    def _():
        s_scr[...] = s0_ref[...]

    rows_all = nseq * cg
    shift = sub.bit_length() - 1
    r_i = _iota((rows_all, 3 * rows_all), 0)
    c_i = _iota((rows_all, 3 * rows_all), 1) & (rows_all - 1)
    same = lax.shift_right_logical(r_i, shift) == lax.shift_right_logical(c_i, shift)
    btril3 = jnp.where((r_i >= c_i) & same, 1.0, 0.0).astype(BF)
    gall_scr[...] = _dot(btril3, _split3(lf_ref[...])) * LOG2E
    onorm = on_ref[...]
    n_aug = 16 - sub % 16
    aug_rhs = jnp.concatenate([jnp.zeros((n_aug, dv), BF), jnp.ones((n_aug, dv), BF)], axis=1)
    zeros_v = jnp.zeros((sub, dv), BF)
    zeros_aug = jnp.zeros((n_aug - 3, dk), F32)

    nh = sub // 8
    row8 = _iota((8, 1), 0)

    def block(it, carry):
        for u in range(unroll):
            sub_chunk(it * unroll + u, g_scr.at[u], k_scr.at[u], v_scr.at[u])
        return carry

    def sub_chunk(sb, g_scr, k_scr, v_scr):
        for q in range(nseq):
            rows = pl.ds(pl.multiple_of(q * cg + sb * sub, sub), sub)
            g_scr[q] = gall_scr[rows, :]
            k_scr[q] = k_ref[rows, :]
            v_scr[q] = v_ref[rows, :]
        problems = [(q, h) for q in range(nseq) for h in range(heads)]
        for g0 in range(0, len(problems), group):
            ps = []
            for (q, h) in problems[g0:g0 + group]:
                r0 = pl.multiple_of(q * cg + sb * sub, sub)
                rows = pl.ds(r0, sub)
                kc = pl.ds(h * dk, dk)
                vc = pl.ds(h * dv, dv)
                ps.append(dict(q=q, h=h, r0=r0, rows=rows, kc=kc, vc=vc,
                               gc=g_scr[q, :, kc],
                               qv=q_ref[rows, kc], k=k_ref[rows, kc], v=v_ref[rows, vc]))
            for p in ps:
                p["s_old"] = s_scr[p["q"], p["h"]]
                p["o_state"] = _dot((p["qv"] * jnp.exp2(p["gc"])).astype(BF), p["s_old"].astype(BF))
                p["o8"] = [jnp.zeros((8, dv), F32) for _ in range(nh)]
                p["q8"] = [p["qv"][8 * i:8 * i + 8] for i in range(nh)]
                p["g8"] = [p["gc"][8 * i:8 * i + 8] for i in range(nh)]
            for j in range(sub):
                for p in ps:
                    kj = k_scr[p["q"], pl.ds(j, 1), p["kc"]]
                    vj = v_scr[p["q"], pl.ds(j, 1), p["vc"]]
                    gj = g_scr[p["q"], pl.ds(j, 1), p["kc"]]
                    for i in range(j // 8, nh):
                        pr = (p["q8"][i] * kj) * jnp.exp2(p["g8"][i] - gj)
                        a_col = jnp.sum(pr, axis=-1, keepdims=True)
                        if i == j // 8:
                            a_col = jnp.where(row8 >= j % 8, a_col, 0.0)
                        p["o8"][i] = p["o8"][i] + a_col * vj
            for p in ps:
                p["o"] = p["o_state"] + (jnp.concatenate(p["o8"], axis=0) if nh > 1 else p["o8"][0])
            for p in ps:
                gend = p["gc"][sub - 1:sub, :]
                dend = jnp.exp2(gend)
                d_hi = dend.astype(BF).astype(F32)
                d_mid = (dend - d_hi).astype(BF).astype(F32)
                d_lo = (dend - d_hi - d_mid).astype(BF).astype(F32)
                kdec = p["k"] * jnp.exp2(gend - p["gc"])
                lhs = jnp.concatenate([kdec, d_hi, d_mid, d_lo, zeros_aug], axis=0).astype(BF)
                rhs = jnp.concatenate([jnp.concatenate([p["v"].astype(BF), zeros_v], axis=1), aug_rhs], axis=0)
                p["kv"] = _dot_tn(lhs, rhs)
            for p in ps:
                s_scr[p["q"], p["h"]] = p["kv"][:, dv:] * p["s_old"] + p["kv"][:, :dv]
            for p in ps:
                o = p["o"]
                ms = jnp.mean(o * o, axis=-1, keepdims=True)
                o_ref[p["rows"], p["vc"]] = (o * lax.rsqrt(ms + EPS) * onorm) * _silu(g_ref[p["rows"], p["vc"]])

    lax.fori_loop(0, nsub // unroll, block, 0)

    @pl.when(c == nchunks - 1)
    def _():
        so_ref[...] = s_scr[...]


def _hgrn2(q, lf, kf, v, gate, onorm, s0, *, row0, seq_len, rows_per_step, sub, seqs_per_step):
    nseq_total, heads, dk, dv = s0.shape
    assert dk == dv, "state decay is applied with a (dk, dv) tile"
    width = heads * dk
    cg = rows_per_step
    nseq = seqs_per_step
    nchunks = seq_len // cg
    assert nseq == 1 or nchunks == 1, "several sequences per step only when a step covers whole sequences"
    rows = nseq * cg
    b0 = row0 // rows
    assert sub & (sub - 1) == 0 and sub % 8 == 0 and rows & (rows - 1) == 0
    unroll = 2 if (cg // sub) % 2 == 0 else 1
    def row_spec(col_block):
        return pl.BlockSpec((rows, width), lambda b, c: (b0 + b * nchunks + c, col_block))

    st_spec = pl.BlockSpec((nseq, heads, dk, dv), lambda b, c: (b, 0, 0, 0))
    kern = functools.partial(_hgrn2_kernel, heads=heads, dk=dk, dv=dv, sub=sub, nsub=cg // sub, nseq=nseq,
                             nchunks=nchunks, group=4 if sub > 8 else 8, unroll=unroll)
    operands = (q, lf, kf, v, gate)
    return pl.pallas_call(
        kern,
        grid=(nseq_total // nseq, nchunks),
        in_specs=[row_spec(cb) for _, cb in operands]
                 + [pl.BlockSpec((1, dv), lambda b, c: (0, 0)), st_spec],
        out_specs=[pl.BlockSpec((rows, heads * dv), lambda b, c: (b * nchunks + c, 0)), st_spec],
        out_shape=[jax.ShapeDtypeStruct((nseq_total * seq_len, heads * dv), F32),
                   jax.ShapeDtypeStruct(s0.shape, F32)],
        scratch_shapes=[pltpu.VMEM((nseq, heads, dk, dv), F32), pltpu.VMEM((rows, width), F32),
                        pltpu.VMEM((unroll, nseq, sub, width), F32), pltpu.VMEM((unroll, nseq, sub, width), F32),
                        pltpu.VMEM((unroll, nseq, sub, heads * dv), F32)],
        compiler_params=_cp(("arbitrary", "arbitrary")),
        name="hgrn2_recurrence",
    )(*[a for a, _ in operands], onorm.reshape(1, dv), s0)


def _gdn_kernel(qkv_ref, z_ref, bg_ref, bgt_ref, cw_ref, on_ref, s0_ref, c0_ref,
                o_ref, so_ref, co_ref, s_scr, xx, xs,
                *, kheads, vheads, dk, dv, chunk, nsub, nseq, nchunks, group_kh):
    c = pl.program_id(1)
    cg = chunk * nsub
    kdim = kheads * dk
    rep = vheads // kheads
    tail0 = 8 - (CONV_K - 1)

    @pl.when(c == 0)
    def _():
        s_scr[...] = s0_ref[...]
        xx[:, 0:8, :] = c0_ref[...]

    for q in range(nseq):
        xx[q, 8:8 + cg, :] = qkv_ref[q * cg:(q + 1) * cg, :]
        conv = xx[q, pl.ds(tail0, cg), :] * cw_ref[0:1, :]
        for j in range(1, CONV_K):
            conv = conv + xx[q, pl.ds(tail0 + j, cg), :] * cw_ref[j:j + 1, :]
        xs[q] = _silu(conv)

    @pl.when(c == nchunks - 1)
    def _():
        for q in range(nseq):
            co_ref[q] = xx[q, pl.ds(8 + cg - (CONV_K - 1), CONV_K - 1), :]

    for q in range(nseq):
        xx[q, 0:8, :] = xx[q, cg:cg + 8, :]

    wp = chunk
    r_i = _iota((chunk, wp), 0)
    c_i = _iota((chunk, wp), 1)
    causal = r_i >= c_i
    strict = r_i > c_i

    def pad_rows(a):
        if a.shape[0] == wp:
            return a
        if a.dtype == BF and a.shape[0] % 16:
            return pad_rows(a.astype(F32)).astype(BF)
        return jnp.concatenate([a, jnp.zeros((wp - a.shape[0],) + a.shape[1:], a.dtype)], axis=0)

    def dot3_padded(a, b):
        a_hi = a.astype(BF).astype(F32)
        lhs = jnp.concatenate([a_hi, a_hi, a - a_hi], axis=1).astype(BF)
        bh, bl = (pad_rows(t) for t in _split(b))
        return _dot(lhs, jnp.concatenate([bh, bl, bh], axis=0))
    cm = chunk - 1
    tril3 = jnp.where(_iota((chunk, 3 * chunk), 0) >= (_iota((chunk, 3 * chunk), 1) & cm), 1.0, 0.0).astype(BF)
    t_r = _iota((3 * chunk, wp), 0) & cm
    t_c = _iota((3 * chunk, wp), 1)
    triu3 = jnp.where((t_r <= t_c) & (t_c < chunk), 1.0, 0.0).astype(BF)
    onorm = on_ref[...]

    cums = {}

    def cumulative_decay(q, s):
        if (q, s) not in cums:
            r0 = q * cg + s * chunk
            bg = bg_ref[r0:r0 + chunk, :]
            gt = bgt_ref[0, vheads:2 * vheads, r0:r0 + chunk]
            cums[(q, s)] = (bg, _dot(tril3, _split3(bg[:, vheads:2 * vheads], axis=0)),
                            _dot(_split3(gt, axis=1), triu3))
        return cums[(q, s)]

    def run_group(items):
        probs = []
        for (q, s, kh) in items:
            r0 = q * cg + s * chunk
            x0 = s * chunk
            bg, gcol_all, grow_all = cumulative_decay(q, s)
            for kh in (kh,):
                qh = xs[q, x0:x0 + chunk, kh * dk:(kh + 1) * dk]
                kk_ = xs[q, x0:x0 + chunk, kdim + kh * dk:kdim + (kh + 1) * dk]
                qn = qh * lax.rsqrt(jnp.sum(qh * qh, axis=-1, keepdims=True) + EPS) * (dk ** -0.5)
                kn = kk_ * lax.rsqrt(jnp.sum(kk_ * kk_, axis=-1, keepdims=True) + EPS)
                kn_bf = kn.astype(BF)
                kn_pad = pad_rows(kn_bf)
                kk = _dot_nt(kn_bf, kn_pad)
                qk = _dot_nt(qn.astype(BF), kn_pad)
                for r in range(rep):
                    h = kh * rep + r
                    vh_ = xs[q, x0:x0 + chunk, 2 * kdim + h * dv:2 * kdim + (h + 1) * dv]
                    gcol = gcol_all[:, h:h + 1]
                    grow = grow_all[h:h + 1, :]
                    beta = bg[:, h:h + 1]
                    decay = jnp.where(causal, jnp.exp(jnp.minimum(gcol - grow, 0.0)), 0.0)
                    eg = jnp.exp(gcol)
                    gend = gcol[chunk - 1:chunk, :]
                    probs.append(dict(
                        q=q, s=s, h=h, r0=r0,
                        x=jnp.where(strict, -(beta * kk * decay), 0.0),
                        y=jnp.concatenate([vh_ * beta, kn * (beta * eg)], axis=1),
                        a_bf=(qk * decay).astype(BF),
                        qg=qn * eg,
                        kdec_bf=(kn * jnp.exp(gend - gcol)).astype(BF),
                        send=jnp.exp(gend)))

        if chunk <= 16:
            for j in range(chunk - 1):
                for p in probs:
                    p["y"] = p["y"] + p["x"][:, j:j + 1] * p["y"][j:j + 1, :]
        else:
            n_fac = int(math.log2(chunk))
            for k in range(n_fac):
                for p in probs:
                    p["y"] = p["y"] + dot3_padded(p["x"], p["y"])
                if k + 1 < n_fac:
                    for p in probs:
                        p["x"] = dot3_padded(p["x"], p["x"])

        for s in sorted({p["s"] for p in probs}):
            cur = [p for p in probs if p["s"] == s]
            for p in cur:
                p["s_old"] = s_scr[p["q"], p["h"]]
                lhs = jnp.concatenate([p["y"][:, dv:], p["qg"]], axis=0).astype(BF)
                p["ws"] = _dot(lhs, p["s_old"].astype(BF))
            for p in cur:
                p["u_bf"] = (p["y"][:, :dv] - p["ws"][:chunk]).astype(BF)
            for p in cur:
                s_scr[p["q"], p["h"]] = p["send"] * p["s_old"] + _dot_tn(p["kdec_bf"], p["u_bf"])
            for p in cur:
                o = p["ws"][chunk:] + _dot(p["a_bf"], pad_rows(p["u_bf"]))
                ms = jnp.mean(o * o, axis=-1, keepdims=True)
                zc = slice(p["h"] * dv, (p["h"] + 1) * dv)
                rows = slice(p["r0"], p["r0"] + chunk)
                o_ref[rows, zc] = (o * lax.rsqrt(ms + EPS) * onorm) * _silu(z_ref[rows, zc])

    for g0 in range(0, kheads, group_kh):
        run_group([(q, s, kh) for q in range(nseq) for s in range(nsub)
                   for kh in range(g0, min(g0 + group_kh, kheads))])

    @pl.when(c == nchunks - 1)
    def _():
        so_ref[...] = s_scr[...]


def _gdn(qkv, z, bg, bgt3, conv_w, onorm, s0, conv0, *, row0, seq_len, rows_per_step, chunk, seqs_per_step):
    nseq_total, vheads, dk, dv = s0.shape
    (qkv, qkv_cb), (z, z_cb) = qkv, z
    conv_dim = conv_w.shape[1]
    kheads = (conv_dim - vheads * dv) // (2 * dk)
    cg = rows_per_step
    nseq = seqs_per_step
    nchunks = seq_len // cg
    assert nseq == 1 or nchunks == 1, "several sequences per step only when a step covers whole sequences"
    rows = nseq * cg
    b0 = row0 // rows
    conv0p = jnp.pad(conv0, ((0, 0), (8 - (CONV_K - 1), 0), (0, 0)))
    rmap = lambda b, c: (b0 + b * nchunks + c, 0)
    st_spec = pl.BlockSpec((nseq, vheads, dk, dv), lambda b, c: (b, 0, 0, 0))
    kern = functools.partial(_gdn_kernel, kheads=kheads, vheads=vheads, dk=dk, dv=dv, chunk=chunk,
                             nsub=cg // chunk, nseq=nseq, nchunks=nchunks, group_kh=kheads)
    return pl.pallas_call(
        kern,
        grid=(nseq_total // nseq, nchunks),
        in_specs=[pl.BlockSpec((rows, conv_dim), lambda b, c: (b0 + b * nchunks + c, qkv_cb)),
                  pl.BlockSpec((rows, vheads * dv), lambda b, c: (b0 + b * nchunks + c, z_cb)),
                  pl.BlockSpec((rows, 2 * vheads), rmap),
                  pl.BlockSpec((1, 2 * vheads, rows), lambda b, c: (b * nchunks + c, 0, 0)),
                  pl.BlockSpec((CONV_K, conv_dim), lambda b, c: (0, 0)),
                  pl.BlockSpec((1, dv), lambda b, c: (0, 0)),
                  st_spec,
                  pl.BlockSpec((nseq, 8, conv_dim), lambda b, c: (b, 0, 0))],
        out_specs=[pl.BlockSpec((rows, vheads * dv), lambda b, c: (b * nchunks + c, 0)),
                   st_spec,
                   pl.BlockSpec((nseq, CONV_K - 1, conv_dim), lambda b, c: (b, 0, 0))],
        out_shape=[jax.ShapeDtypeStruct((nseq_total * seq_len, vheads * dv), F32),
                   jax.ShapeDtypeStruct(s0.shape, F32),
                   jax.ShapeDtypeStruct((nseq_total, CONV_K - 1, conv_dim), F32)],
        scratch_shapes=[pltpu.VMEM((nseq, vheads, dk, dv), F32), pltpu.VMEM((nseq, 8 + cg, conv_dim), F32),
                        pltpu.VMEM((nseq, cg, conv_dim), F32)],
        compiler_params=_cp(("arbitrary", "arbitrary")),
        name="gdn_recurrence",
    )(qkv, z, bg, bgt3, conv_w, onorm.reshape(1, dv), s0, conv0p)


def _merge_kernel(xn_ref, oa1_ref, oa2_ref, ob1_ref, ob2_ref, wga_ref, wgb_ref, wa_ref, wb_ref, o_ref,
                  oa_bf, ob_bf, *, na):
    i = pl.program_id(0)
    j = pl.program_id(1)

    @pl.when((j == 0) & (i < na))
    def _():
        oa_bf[...] = oa1_ref[...].astype(BF)
        ob_bf[...] = ob1_ref[...].astype(BF)

    @pl.when((j == 0) & (i >= na))
    def _():
        oa_bf[...] = oa2_ref[...].astype(BF)
        ob_bf[...] = ob2_ref[...].astype(BF)

    xn = xn_ref[...]
    ya = _dot(oa_bf[...], wa_ref[...])
    yb = _dot(ob_bf[...], wb_ref[...])
    ga = _dot(xn, wga_ref[...])
    gb = _dot(xn, wgb_ref[...])
    o_ref[...] = (_sigmoid(ga) * ya + _sigmoid(gb) * yb).astype(o_ref.dtype)


def _merge(xn, oa_p, oa_s, ob_p, ob_s, w_gates_bf, w_pa, w_pb):
    rp, rs = oa_p.shape[0], oa_s.shape[0]
    ka, kb = oa_p.shape[1], ob_p.shape[1]
    d = w_pa.shape[1]
    k = xn.shape[1]
    tm = _pick(math.gcd(rp, rs), 512)
    tn = _pick(d, 512, LANES)
    na, nb = rp // tm, rs // tm
    nj = d // tn
    amap = lambda i, j: (jnp.minimum(i, na - 1), 0)
    bmap = lambda i, j: (jnp.maximum(i - na, 0), 0)
    return pl.pallas_call(
        functools.partial(_merge_kernel, na=na),
        grid=(na + nb, nj),
        in_specs=[pl.BlockSpec((tm, k), lambda i, j: (i, 0)),
                  pl.BlockSpec((tm, ka), amap), pl.BlockSpec((tm, ka), bmap),
                  pl.BlockSpec((tm, kb), amap), pl.BlockSpec((tm, kb), bmap),
                  pl.BlockSpec((k, tn), lambda i, j: (0, j)),
                  pl.BlockSpec((k, tn), lambda i, j: (0, nj + j)),
                  pl.BlockSpec((ka, tn), lambda i, j: (0, j)),
                  pl.BlockSpec((kb, tn), lambda i, j: (0, j))],
        out_specs=pl.BlockSpec((tm, tn), lambda i, j: (i, j)),
        out_shape=jax.ShapeDtypeStruct((rp + rs, d), BF),
        scratch_shapes=[pltpu.VMEM((tm, ka), BF), pltpu.VMEM((tm, kb), BF)],
        compiler_params=_cp(("arbitrary", "arbitrary")),
        name="gates_branch_proj_merge",
    )(xn, oa_p, oa_s, ob_p, ob_s, w_gates_bf, w_gates_bf, w_pa.astype(BF), w_pb.astype(BF))


def _outproj_kernel(m_ref, xa_ref, xb_ref, wo_ref, nw_ref, wr_ref, h_ref, xf_ref, lg_ref, *, na):
    i = pl.program_id(0)

    def body(x_ref):
        h = x_ref[...] + _dot(m_ref[...], wo_ref[...])
        h_ref[...] = h
        ms = jnp.mean(h * h, axis=-1, keepdims=True)
        xf = h * lax.rsqrt(ms + EPS) * nw_ref[...]
        _store_token_major(xf_ref, xf)
        lg_ref[...] = _dot(xf.astype(BF), wr_ref[...])

    @pl.when(i < na)
    def _():
        body(xa_ref)

    @pl.when(i >= na)
    def _():
        body(xb_ref)


def _outproj(merged, xa, xb, w_out, norm_ffn, w_router):
    t, d = merged.shape
    tm = _pick(math.gcd(xa.shape[0], xb.shape[0]), 512)
    na, nb, sa, sb = _two_source_specs(xa.shape[0], xb.shape[0], tm, d)
    nr = w_router.shape[1]
    const = lambda i: (0, 0)
    return pl.pallas_call(
        functools.partial(_outproj_kernel, na=na),
        grid=(na + nb,),
        in_specs=[pl.BlockSpec((tm, d), lambda i: (i, 0)), sa, sb,
                  pl.BlockSpec((d, d), const, pipeline_mode=pl.Buffered(1)),
                  pl.BlockSpec((1, d), const),
                  pl.BlockSpec((d, nr), const, pipeline_mode=pl.Buffered(1))],
        out_specs=[pl.BlockSpec((tm, d), lambda i: (i, 0)),
                   pl.BlockSpec((tm * (d // LANES), LANES), lambda i: (i, 0)),
                   pl.BlockSpec((tm, nr), lambda i: (i, 0))],
        out_shape=[jax.ShapeDtypeStruct((t, d), F32), jax.ShapeDtypeStruct((t * (d // LANES), LANES), F32),
                   jax.ShapeDtypeStruct((t, nr), F32)],
        compiler_params=_cp(("arbitrary",), 56),
        name="out_proj_ffn_norm",
    )(merged, xa, xb, w_out.astype(BF), norm_ffn.reshape(1, d), w_router)


def _route_kernel(lg_ref, b_ref, eid_ref, gate_ref, rank_ref, cnt_ref, carry,
                  *, ngroups, nexp, tm):
    i = pl.program_id(0)

    @pl.when(i == 0)
    def _():
        carry[...] = jnp.zeros_like(carry)

    per_group = nexp // ngroups
    lg = lg_ref[...] + b_ref[...]
    lane = _iota(lg.shape, 1)
    big = jnp.int32(1 << 20)
    is_g = lane < ngroups
    gl = jnp.where(is_g, lg, NEG)
    gmax = jnp.max(gl, axis=-1, keepdims=True)
    gidx = jnp.min(jnp.where(gl == gmax, lane, big), axis=-1, keepdims=True)
    gsum = jnp.sum(jnp.where(is_g, jnp.exp(gl - gmax), 0.0), axis=-1, keepdims=True)
    gw = 1.0 / gsum
    elane = lane - ngroups
    in_grp = (elane >= gidx * per_group) & (elane < (gidx + 1) * per_group)
    el = jnp.where(in_grp, lg, NEG)
    v1 = jnp.max(el, axis=-1, keepdims=True)
    i1 = jnp.min(jnp.where(in_grp & (el == v1), elane, big), axis=-1, keepdims=True)
    in2 = in_grp & (elane != i1)
    el2 = jnp.where(in2, lg, NEG)
    v2 = jnp.max(el2, axis=-1, keepdims=True)
    i2 = jnp.min(jnp.where(in2 & (el2 == v2), elane, big), axis=-1, keepdims=True)
    p2 = jnp.exp(v2 - v1)
    den = 1.0 + p2
    lane2 = _iota((tm, TOP_K), 1)
    eid_ref[...] = jnp.where(lane2 == 0, i1, i2)
    gate_ref[...] = jnp.where(lane2 == 0, gw / den, gw * p2 / den)

    oh1 = (elane == i1).astype(F32)
    oh2 = (elane == i2).astype(F32)
    lower = (_iota((tm, tm), 0) > _iota((tm, tm), 1)).astype(BF)
    cs1 = _dot(lower, oh1.astype(BF))
    cs2 = _dot(lower, oh2.astype(BF))
    tot1 = jnp.sum(oh1, axis=0, keepdims=True)
    tot2 = jnp.sum(oh2, axis=0, keepdims=True)
    base = carry[0:1, :]
    r1 = jnp.sum(oh1 * (base + cs1), axis=-1, keepdims=True)
    r2 = jnp.sum(oh2 * (base + tot1 + cs2), axis=-1, keepdims=True)
    rank_ref[...] = jnp.where(lane2 == 0, r1, r2).astype(I32)
    new = base + tot1 + tot2
    carry[...] = jnp.broadcast_to(new, carry.shape)
    cnt_ref[...] = jnp.broadcast_to(new, cnt_ref.shape)


def _route(logits, bias_row, ngroups, nexp):
    t, nr = logits.shape
    tm = _pick(t, 512)
    kern = functools.partial(_route_kernel, ngroups=ngroups, nexp=nexp, tm=tm)
    return pl.pallas_call(
        kern,
        grid=(t // tm,),
        in_specs=[pl.BlockSpec((tm, nr), lambda i: (i, 0)), pl.BlockSpec((1, nr), lambda i: (0, 0))],
        out_specs=[pl.BlockSpec((tm, TOP_K), lambda i: (i, 0)),
                   pl.BlockSpec((tm, TOP_K), lambda i: (i, 0)),
                   pl.BlockSpec((tm, TOP_K), lambda i: (i, 0)),
                   pl.BlockSpec((8, nr), lambda i: (0, 0))],
        out_shape=[jax.ShapeDtypeStruct((t, TOP_K), I32), jax.ShapeDtypeStruct((t, TOP_K), F32),
                   jax.ShapeDtypeStruct((t, TOP_K), I32), jax.ShapeDtypeStruct((8, nr), F32)],
        scratch_shapes=[pltpu.VMEM((8, nr), F32)],
        compiler_params=_cp(("arbitrary",)),
        name="route_topk_rank",
    )(logits, bias_row)


def _plan_kernel(cnt_ref, eid_ref, rank_ref, pos_ref, be_ref, *, ngroups, nexp, nblocks_pad, tm):
    nr = cnt_ref.shape[1]
    cnt = cnt_ref[0:1, :]
    padded = jnp.floor((cnt + (MOE_BLOCK - 1)) * (1.0 / MOE_BLOCK)) * MOE_BLOCK
    r_i = _iota((nr, nr), 0)
    c_i = _iota((nr, nr), 1)
    padded_col = jnp.sum(jnp.where(r_i == c_i, jnp.broadcast_to(padded, (nr, nr)), 0.0), axis=1, keepdims=True)
    start = jnp.sum(jnp.where(r_i < c_i, jnp.broadcast_to(padded_col, (nr, nr)), 0.0), axis=0, keepdims=True)
    end = start + padded
    lane = _iota((tm, nr), 1)
    eid = eid_ref[...]
    rank = rank_ref[...]
    lane2 = _iota((tm, TOP_K), 1)
    pos = jnp.zeros((tm, TOP_K), F32)
    for k in range(TOP_K):
        oh = (lane - ngroups) == eid[:, k:k + 1]
        st = jnp.sum(jnp.where(oh, start, 0.0), axis=-1, keepdims=True)
        pos = jnp.where(lane2 == k, st, pos)
    pos_ref[...] = pos.astype(I32) + rank

    @pl.when(pl.program_id(0) == 0)
    def _():
        blk_row = _iota((nblocks_pad, nr), 0).astype(F32) * MOE_BLOCK
        lane_b = _iota((nblocks_pad, nr), 1)
        is_e = (lane_b >= ngroups) & (lane_b < ngroups + nexp)
        n_le = jnp.sum(jnp.where(is_e & (end <= blk_row), 1.0, 0.0), axis=-1, keepdims=True)
        e_lane = (lane_b - ngroups).astype(F32)
        nonempty = is_e & (padded > 0.0)
        e_last = jnp.max(jnp.where(nonempty, e_lane, -1.0), axis=-1, keepdims=True)
        be = jnp.minimum(n_le, e_last)
        nxt = jnp.min(jnp.where(nonempty & (e_lane > be), e_lane, 1e9), axis=-1, keepdims=True)
        nxt = jnp.where(nxt > 1e8, -1.0, nxt)
        total = jnp.sum(padded, axis=-1, keepdims=True)
        valid = jnp.where(blk_row[:, 0:1] < total, 1.0, 0.0)
        col = _iota((nblocks_pad, 4), 1)
        meta = jnp.where(col == 0, be, jnp.where(col == 1, nxt, jnp.where(col == 2, valid, 0.0)))
        be_ref[...] = meta.astype(I32)


def _plan(cnt, eid, rank, ngroups, nexp, nblocks):
    t = eid.shape[0]
    nr = cnt.shape[1]
    tm = _pick(t, 512)
    nblocks_pad = -(-nblocks // 8) * 8
    kern = functools.partial(_plan_kernel, ngroups=ngroups, nexp=nexp, nblocks_pad=nblocks_pad, tm=tm)
    return pl.pallas_call(
        kern,
        grid=(t // tm,),
        in_specs=[pl.BlockSpec((8, nr), lambda i: (0, 0)),
                  pl.BlockSpec((tm, TOP_K), lambda i: (i, 0)),
                  pl.BlockSpec((tm, TOP_K), lambda i: (i, 0))],
        out_specs=[pl.BlockSpec((tm, TOP_K), lambda i: (i, 0)),
                   pl.BlockSpec((nblocks_pad, 4), lambda i: (0, 0))],
        out_shape=[jax.ShapeDtypeStruct((t, TOP_K), I32), jax.ShapeDtypeStruct((nblocks_pad, 4), I32)],
        compiler_params=_cp(("arbitrary",)),
        name="route_plan",
    )(cnt, eid, rank)


def _invert_kernel(pos_ref, tok_ref, *, n_assign, n_rows):
    def clear(r, c):
        tok_ref[r] = 0
        return c

    lax.fori_loop(0, n_rows, clear, 0, unroll=16)

    def put(a, c):
        tok_ref[pos_ref[a]] = lax.shift_right_logical(a, TOP_K.bit_length() - 1)
        return c

    lax.fori_loop(0, n_assign, put, 0, unroll=16)


def _invert(pos_flat, n_rows):
    n_assign = pos_flat.shape[0]
    return pl.pallas_call(
        functools.partial(_invert_kernel, n_assign=n_assign, n_rows=n_rows),
        in_specs=[pl.BlockSpec(memory_space=pltpu.SMEM)],
        out_specs=pl.BlockSpec(memory_space=pltpu.SMEM),
        out_shape=jax.ShapeDtypeStruct((n_rows,), I32),
        name="route_invert",
    )(pos_flat)


def _expert_kernel(be_ref, nxt_ref, valid_ref, tok_ref, x_hbm, wg_hbm, wu_hbm, wd_hbm, y_ref,
                   xbuf0, xbuf1, xsem, wg_st, wu_st, wd_st, wsem, wg_bf, wu_bf, wd_bf, *, nch):
    i = pl.program_id(0)
    n = pl.num_programs(0)
    xbufs = (xbuf0, xbuf1)

    def row_copy(blk, s, r):
        t = tok_ref[blk * MOE_BLOCK + r]
        src = x_hbm.at[pl.ds(pl.multiple_of(t * nch, nch), nch), :]
        return pltpu.make_async_copy(src, xbufs[s].at[pl.ds(r * nch, nch), :], xsem.at[s])

    def block_wait(s):
        pltpu.make_async_copy(x_hbm.at[pl.ds(0, MOE_BLOCK * nch), :], xbufs[s], xsem.at[s]).wait()

    def weight_copies(e):
        return (pltpu.make_async_copy(wg_hbm.at[e], wg_st, wsem.at[0]),
                pltpu.make_async_copy(wu_hbm.at[e], wu_st, wsem.at[1]),
                pltpu.make_async_copy(wd_hbm.at[e], wd_st, wsem.at[2]))

    @pl.when(i == 0)
    def _():
        for r in range(MOE_BLOCK):
            row_copy(0, 0, r).start()
        for cp in weight_copies(be_ref[0]):
            cp.start(priority=1)

    prev = be_ref[jnp.maximum(i - 1, 0)]

    @pl.when((i == 0) | (be_ref[i] != prev))
    def _():
        for cp in weight_copies(be_ref[i]):
            cp.wait()
        wg_bf[...] = wg_st[...].astype(BF)
        wu_bf[...] = wu_st[...].astype(BF)
        wd_bf[...] = wd_st[...].astype(BF)

        @pl.when(nxt_ref[i] >= 0)
        def _():
            for cp in weight_copies(nxt_ref[i]):
                cp.start(priority=1)

    ahead = jnp.minimum(i + 1, n - 1)
    used = valid_ref[i] != 0
    used_ahead = (i + 1 < n) & (valid_ref[ahead] != 0)

    def step(cur, oth):
        @pl.when(used_ahead)
        def _():
            for r in range(MOE_BLOCK):
                row_copy(ahead, oth, r).start()

        block_wait(cur)
        x = _load_token_major(xbufs[cur], MOE_BLOCK, nch).astype(BF)
        hid = (_silu(_dot(x, wg_bf[...])) * _dot(x, wu_bf[...])).astype(BF)
        _store_token_major(y_ref, _dot(hid, wd_bf[...]))

    @pl.when(used & (i % 2 == 0))
    def _():
        step(0, 1)

    @pl.when(used & (i % 2 == 1))
    def _():
        step(1, 0)

    @pl.when(jnp.logical_not(used))
    def _():
        y_ref[...] = jnp.zeros_like(y_ref)


def _experts(block_e, block_next, block_valid, row_tok, xf, w_eg, w_eu, w_ed):
    nrows = row_tok.shape[0]
    d = w_eg.shape[1]
    nch = d // LANES
    nblocks = nrows // MOE_BLOCK
    de = w_eg.shape[2]
    any_spec = pl.BlockSpec(memory_space=pl.ANY)
    grid_spec = pltpu.PrefetchScalarGridSpec(
        num_scalar_prefetch=4,
        grid=(nblocks,),
        in_specs=[any_spec, any_spec, any_spec, any_spec],
        out_specs=pl.BlockSpec((MOE_BLOCK * nch, LANES), lambda i, *_: (i, 0)),
        scratch_shapes=[pltpu.VMEM((MOE_BLOCK * nch, LANES), F32), pltpu.VMEM((MOE_BLOCK * nch, LANES), F32),
                        pltpu.SemaphoreType.DMA((2,)),
                        pltpu.VMEM((d, de), F32), pltpu.VMEM((d, de), F32), pltpu.VMEM((de, d), F32),
                        pltpu.SemaphoreType.DMA((3,)),
                        pltpu.VMEM((d, de), BF), pltpu.VMEM((d, de), BF), pltpu.VMEM((de, d), BF)],
    )
    return pl.pallas_call(
        functools.partial(_expert_kernel, nch=nch),
        grid_spec=grid_spec,
        out_shape=jax.ShapeDtypeStruct((nrows * nch, LANES), F32),
        compiler_params=_cp(("arbitrary",), 48),
        name="expert_blocks",
    )(block_e, block_next, block_valid, row_tok, xf, w_eg, w_eu, w_ed)


def _final_kernel(pos_ref, h_ref, gate_ref, w_ref, yb_hbm, ya_ref, ys_ref, gbuf, sem, *, na, tm, nch):
    i = pl.program_id(0)
    n = pl.num_programs(0)
    slot = i % 2

    def row_copy(tile, slot_, r, k):
        p = pos_ref[(tile * tm + r) * TOP_K + k]
        src = yb_hbm.at[pl.ds(pl.multiple_of(p * nch, nch), nch), :]
        return pltpu.make_async_copy(src, gbuf.at[slot_, k, pl.ds(r * nch, nch), :], sem.at[slot_])

    def start_tile(tile, slot_):
        for r in range(tm):
            for k in range(TOP_K):
                row_copy(tile, slot_, r, k).start()

    @pl.when(i == 0)
    def _():
        start_tile(0, 0)

    @pl.when(i + 1 < n)
    def _():
        start_tile(i + 1, 1 - slot)

    for r in range(tm):
        for k in range(TOP_K):
            row_copy(i, slot, r, k).wait()

    gate = gate_ref[...]
    y = h_ref[...]
    ffn = _load_token_major(gbuf.at[slot, 0], tm, nch) * gate[:, 0:1]
    for k in range(1, TOP_K):
        ffn = ffn + _load_token_major(gbuf.at[slot, k], tm, nch) * gate[:, k:k + 1]
    y = y + ffn
    ms = jnp.mean(y * y, axis=-1, keepdims=True)
    out = y * lax.rsqrt(ms + EPS) * w_ref[...]

    @pl.when(i < na)
    def _():
        ya_ref[...] = out

    @pl.when(i >= na)
    def _():
        ys_ref[...] = out


def _final(pos_flat, h, gate, yb, w, rows_a, rows_b):
    t, d = h.shape
    tm = _pick(math.gcd(rows_a, rows_b), 128)
    na, nb = rows_a // tm, rows_b // tm
    grid_spec = pltpu.PrefetchScalarGridSpec(
        num_scalar_prefetch=1,
        grid=(na + nb,),
        in_specs=[pl.BlockSpec((tm, d), lambda i, pos: (i, 0)),
                  pl.BlockSpec((tm, TOP_K), lambda i, pos: (i, 0)),
                  pl.BlockSpec((1, d), lambda i, pos: (0, 0)),
                  pl.BlockSpec(memory_space=pl.ANY)],
        out_specs=[pl.BlockSpec((tm, d), lambda i, pos: (jnp.minimum(i, na - 1), 0)),
                   pl.BlockSpec((tm, d), lambda i, pos: (jnp.maximum(i - na, 0), 0))],
        scratch_shapes=[pltpu.VMEM((2, TOP_K, tm * (d // LANES), LANES), F32), pltpu.SemaphoreType.DMA((2,))],
    )
    return pl.pallas_call(
        functools.partial(_final_kernel, na=na, tm=tm, nch=d // LANES),
        grid_spec=grid_spec,
        out_shape=[jax.ShapeDtypeStruct((rows_a, d), F32), jax.ShapeDtypeStruct((rows_b, d), F32)],
        compiler_params=_cp(("arbitrary",)),
        name="combine_final_norm",
    )(pos_flat, h, gate, w.reshape(1, d), yb)


def kernel(x_prompt, x_sample, state_hgrn, state_gdn, state_conv, norm_mix, w_in, conv_w, ha_lb_logits, ha_onorm, w_pa, gd_A_log, gd_dt_bias, gd_onorm, w_pb, w_out, norm_ffn, w_router_group, b_router_group, w_router_expert, b_router_expert, w_exp_gate, w_exp_up, w_exp_down, norm_final):
    depth = state_hgrn.shape[0]
    assert depth == 1, "one decoder layer"
    bp, lp, d = x_prompt.shape
    bs, ls, _ = x_sample.shape
    _, _, ha_heads, ha_dk, ha_dv = state_hgrn.shape
    _, _, vheads, gd_dk, gd_dv = state_gdn.shape
    conv_dim = state_conv.shape[3]
    ha_kdim, ha_vdim = ha_heads * ha_dk, ha_heads * ha_dv
    gd_vdim = vheads * gd_dv
    ngroups = w_router_group.shape[2]
    nexp = w_router_expert.shape[2]
    rp, rs = bp * lp, bs * ls
    t = rp + rs

    off_f = ha_kdim
    off_i = off_f + ha_kdim
    off_g = off_i + ha_vdim
    off_qkv = off_g + ha_vdim
    off_z = off_qkv + conv_dim
    off_b = off_z + gd_vdim
    off_gate = off_b + 2 * vheads
    w_in0 = w_in[0]

    xn = _rmsnorm_bf16(x_prompt.reshape(rp, d), x_sample.reshape(rs, d), norm_mix[0])

    assert ha_kdim == ha_vdim and off_qkv % conv_dim == 0 and off_z % gd_vdim == 0
    proj, k_a = _inproj(xn, w_in0, ha_lb_logits, off_b, ha_kdim, ha_dk ** -0.5)
    q_a, logf, v_a, g_a = ((proj, cb) for cb in (0, off_f // ha_kdim, off_i // ha_kdim, off_g // ha_kdim))
    k_a = (k_a, 0)
    qkv = (proj, off_qkv // conv_dim)
    z_b = (proj, off_z // gd_vdim)
    w_gates_bf = w_in0[:, off_gate:].astype(BF)
    bg, bgt = _beta_decay(xn, w_in0[:, off_b:off_gate], gd_A_log[0], gd_dt_bias[0])

    cg_p = _pick(lp, 128, GDN_CHUNK) if lp >= GDN_CHUNK else lp
    sub_p = min(HGRN_SUB, lp)
    chunk_p = min(GDN_CHUNK, lp)
    cg_s = ls
    zeros_h = jnp.zeros((bp, ha_heads, ha_dk, ha_dv), F32)
    zeros_g = jnp.zeros((bp, vheads, gd_dk, gd_dv), F32)
    zeros_c = jnp.zeros((bp, CONV_K - 1, conv_dim), F32)

    oa_p, sh_p = _hgrn2(q_a, logf, k_a, v_a, g_a, ha_onorm[0], zeros_h,
                        row0=0, seq_len=lp, rows_per_step=cg_p, sub=sub_p, seqs_per_step=1)
    oa_s, sh_s = _hgrn2(q_a, logf, k_a, v_a, g_a, ha_onorm[0], state_hgrn[0],
                        row0=rp, seq_len=ls, rows_per_step=cg_s, sub=min(HGRN_SUB, ls),
                        seqs_per_step=math.gcd(_pick(bs, HGRN_SAMPLE_SEQS, 1), max(rp // cg_s, 1)))

    def time_on_lanes(rows0, nrows, cg):
        part = bgt[:, rows0:rows0 + nrows].reshape(2 * vheads, nrows // cg, cg)
        return jnp.transpose(part, (1, 0, 2))

    seqs_s = _pick(bs, GDN_SAMPLE_SEQS, 1)
    cg_g = _pick(lp, GDN_ROWS, GDN_CHUNK) if lp >= GDN_CHUNK else lp
    bgt_p = time_on_lanes(0, rp, cg_g)
    bgt_s = time_on_lanes(rp, rs, seqs_s * cg_s)
    ob_p, sg_p, sc_p = _gdn(qkv, z_b, bg, bgt_p, conv_w[0], gd_onorm[0], zeros_g, zeros_c,
                            row0=0, seq_len=lp, rows_per_step=cg_g, chunk=chunk_p, seqs_per_step=1)
    ob_s, sg_s, sc_s = _gdn(qkv, z_b, bg, bgt_s, conv_w[0], gd_onorm[0], state_gdn[0],
                            state_conv[0], row0=rp, seq_len=ls, rows_per_step=cg_s, chunk=min(GDN_CHUNK, ls),
                            seqs_per_step=seqs_s)

    merged = _merge(xn, oa_p, oa_s, ob_p, ob_s, w_gates_bf, w_pa[0], w_pb[0])

    nr = LANES
    w_router = jnp.concatenate([w_router_group[0], w_router_expert[0]], axis=1)
    w_router = jnp.pad(w_router, ((0, 0), (0, nr - ngroups - nexp))).astype(BF)
    b_router = jnp.pad(jnp.concatenate([b_router_group[0], b_router_expert[0]]), (0, nr - ngroups - nexp))
    h, xf, logits = _outproj(merged, x_prompt.reshape(rp, d), x_sample.reshape(rs, d), w_out[0],
                             norm_ffn[0], w_router)

    eid, gate, rank, cnt = _route(logits, b_router.reshape(1, nr), ngroups, nexp)
    nblocks = -(-(t * TOP_K) // MOE_BLOCK) + nexp
    pos, block_meta = _plan(cnt, eid, rank, ngroups, nexp, nblocks)
    block_e, block_next, block_valid = (block_meta[:nblocks, j] for j in range(3))

    pos_flat = pos.reshape(-1)
    row_tok = _invert(pos_flat, nblocks * MOE_BLOCK)
    yb = _experts(block_e, block_next, block_valid, row_tok, xf, w_exp_gate[0], w_exp_up[0], w_exp_down[0])
    y_p, y_s = _final(pos_flat, h, gate, yb, norm_final, rp, rs)
    return (y_p.reshape(bp, lp, d), y_s.reshape(bs, ls, d),
            sh_p[None], sg_p[None], sc_p[None], sh_s[None], sg_s[None], sc_s[None])
```

```python
import functools
import math

import jax
import jax.numpy as jnp
from jax import lax
from jax.experimental import pallas as pl
from jax.experimental.pallas import tpu as pltpu

F32 = jnp.float32
BF = jnp.bfloat16
I32 = jnp.int32
EPS = 1e-6
CONV_K = 4
TOP_K = 2
MOE_BLOCK = 128
GDN_CHUNK = 64
GDN_ROWS = 128
GDN_SAMPLE_SEQS = 4
HGRN_SUB = 16
HGRN_SAMPLE_SEQS = 4
LOG2E = 1.4426950408889634
LANES = 128
NEG = -3.0e38
HI = lax.Precision.HIGHEST
MIB = 1024 * 1024


def _cp(sem, vmem_mib=48):
    return pltpu.CompilerParams(dimension_semantics=sem, vmem_limit_bytes=vmem_mib * MIB)


def _pick(n, pref, mult=8):
    best = None
    for t in range(mult, min(n, pref) + 1, mult):
        if n % t == 0:
            best = t
    return best if best is not None else n


def _sigmoid(x):
    return 1.0 / (1.0 + jnp.exp(-x))


def _silu(x):
    return x * _sigmoid(x)


def _iota(shape, dim):
    return lax.broadcasted_iota(I32, shape, dim)


def _dot(a, b):
    return jnp.dot(a, b, preferred_element_type=F32)


def _dot_nt(a, b):
    return lax.dot_general(a, b, (((1,), (1,)), ((), ())), preferred_element_type=F32)


def _dot_tn(a, b):
    return lax.dot_general(a, b, (((0,), (0,)), ((), ())), preferred_element_type=F32)


def _split(a):
    hi = a.astype(BF)
    lo = (a - hi.astype(F32)).astype(BF)
    return hi, lo


def _split3(a, axis=0):
    hi = a.astype(BF)
    r1 = a - hi.astype(F32)
    mid = r1.astype(BF)
    lo = (r1 - mid.astype(F32)).astype(BF)
    return jnp.concatenate([hi, mid, lo], axis=axis)


def _dot3(a, b):
    ah, al = _split(a)
    bh, bl = _split(b)
    return _dot(ah, bh) + (_dot(ah, bl) + _dot(al, bh))


def _store_token_major(ref, x):
    rows, d = x.shape
    nch = d // LANES
    for c in range(nch):
        ref[pl.ds(c, rows, stride=nch), :] = x[:, c * LANES:(c + 1) * LANES]


def _load_token_major(ref, rows, nch):
    return jnp.concatenate([ref[pl.ds(c, rows, stride=nch), :] for c in range(nch)], axis=1)


def _two_source_specs(rows_a, rows_b, tm, width):
    na, nb = rows_a // tm, rows_b // tm
    spec_a = pl.BlockSpec((tm, width), lambda i: (jnp.minimum(i, na - 1), 0))
    spec_b = pl.BlockSpec((tm, width), lambda i: (jnp.maximum(i - na, 0), 0))
    return na, nb, spec_a, spec_b


def _rmsnorm_kernel(xa_ref, xb_ref, w_ref, o_ref, *, na):
    i = pl.program_id(0)

    def body(x_ref):
        x = x_ref[...]
        ms = jnp.mean(x * x, axis=-1, keepdims=True)
        o_ref[...] = (x * lax.rsqrt(ms + EPS) * w_ref[...]).astype(o_ref.dtype)

    @pl.when(i < na)
    def _():
        body(xa_ref)

    @pl.when(i >= na)
    def _():
        body(xb_ref)


def _rmsnorm_bf16(xa, xb, w):
    d = xa.shape[1]
    tm = _pick(math.gcd(xa.shape[0], xb.shape[0]), 512)
    na, nb, sa, sb = _two_source_specs(xa.shape[0], xb.shape[0], tm, d)
    return pl.pallas_call(
        functools.partial(_rmsnorm_kernel, na=na),
        grid=(na + nb,),
        in_specs=[sa, sb, pl.BlockSpec((1, d), lambda i: (0, 0))],
        out_specs=pl.BlockSpec((tm, d), lambda i: (i, 0)),
        out_shape=jax.ShapeDtypeStruct((xa.shape[0] + xb.shape[0], d), BF),
        compiler_params=_cp(("arbitrary",)),
        name="rmsnorm_mix",
    )(xa, xb, w.reshape(1, d))


def _inproj_kernel(x_ref, w_ref, lb_ref, p_ref, kf_ref, wbf, *, jq, jf0, jf1, q_scale):
    j = pl.program_id(0)

    @pl.when(pl.program_id(1) == 0)
    def _():
        wbf[...] = jnp.transpose(w_ref[...]).astype(BF)

    acc = _dot(x_ref[...], wbf[...])
    is_forget = (j >= jf0) & (j < jf1)

    @pl.when(is_forget)
    def _():
        logf, kf = _forget_epilogue(acc, lb_ref[...])
        p_ref[...] = logf
        kf_ref[...] = kf

    @pl.when(jnp.logical_not(is_forget))
    def _():
        p_ref[...] = acc * jnp.where(j < jq, q_scale, 1.0)


def _inproj(x_bf, w, lb_logits, ncols, kdim, q_scale):
    t, k = x_bf.shape
    tn = _pick(math.gcd(ncols, kdim), 512, LANES)
    tm = _pick(t, 1024)
    ni = t // tm
    jq, jf0, jf1 = kdim // tn, kdim // tn, 2 * kdim // tn
    nf = jf1 - jf0
    fcol = lambda j: jnp.clip(j - jf0, 0, nf - 1)
    frow = lambda j, i: jnp.where(j < jf0, 0, jnp.where(j >= jf1, ni - 1, i))
    kern = functools.partial(_inproj_kernel, jq=jq, jf0=jf0, jf1=jf1, q_scale=q_scale)
    return pl.pallas_call(
        kern,
        grid=(ncols // tn, ni),
        in_specs=[pl.BlockSpec((tm, k), lambda j, i: (i, 0)),
                  pl.BlockSpec((None, tn, k), lambda j, i: (0, j, 0)),
                  pl.BlockSpec((lb_logits.shape[0], tn), lambda j, i: (0, fcol(j)))],
        out_specs=[pl.BlockSpec((tm, tn), lambda j, i: (i, j)),
                   pl.BlockSpec((tm, tn), lambda j, i: (frow(j, i), fcol(j)))],
        out_shape=[jax.ShapeDtypeStruct((t, ncols), F32), jax.ShapeDtypeStruct((t, kdim), F32)],
        scratch_shapes=[pltpu.VMEM((k, tn), BF)],
        compiler_params=_cp(("arbitrary", "arbitrary")),
        name="in_proj",
    )(x_bf, w, lb_logits)


def _forget_epilogue(acc, lb_logits):
    m = jnp.max(lb_logits, axis=0, keepdims=True)
    e = jnp.exp(lb_logits - m)
    lb = e[0:1, :] / jnp.sum(e, axis=0, keepdims=True)
    f = lb + (1.0 - lb) * _sigmoid(acc)
    return jnp.log(f), 1.0 - f


def _beta_decay_kernel(x_ref, w_ref, wt_ref, prow_ref, pcol_ref, o_ref, ot_ref, *, vh):
    x = x_ref[...]
    acc = _dot(x, w_ref[...])
    acct = _dot_nt(wt_ref[...], x)

    def act(a, is_beta, a_neg_exp, dt_bias):
        z = a + dt_bias
        softplus = jnp.maximum(z, 0.0) + jnp.log(1.0 + jnp.exp(-jnp.abs(z)))
        return jnp.where(is_beta, _sigmoid(a), a_neg_exp * softplus)

    prow = prow_ref[...]
    pcol = pcol_ref[...]
    o_ref[...] = act(acc, _iota(acc.shape, 1) < vh, prow[0:1, :], prow[1:2, :])
    ot_ref[...] = act(acct, _iota(acct.shape, 0) < vh, pcol[:, 0:1], pcol[:, 1:2])


def _beta_decay(x_bf, w_ba, a_log, dt_bias):
    t, k = x_bf.shape
    vh = a_log.shape[0]
    tm = t if t <= 2048 else _pick(t, 1024, LANES)
    zeros = jnp.zeros((vh,), F32)
    prow = jnp.stack([jnp.concatenate([zeros, -jnp.exp(a_log)]), jnp.concatenate([zeros, dt_bias])])
    return pl.pallas_call(
        functools.partial(_beta_decay_kernel, vh=vh),
        grid=(t // tm,),
        in_specs=[pl.BlockSpec((tm, k), lambda i: (i, 0)),
                  pl.BlockSpec((k, 2 * vh), lambda i: (0, 0)),
                  pl.BlockSpec((2 * vh, k), lambda i: (0, 0)),
                  pl.BlockSpec((2, 2 * vh), lambda i: (0, 0)),
                  pl.BlockSpec((2 * vh, 2), lambda i: (0, 0))],
        out_specs=[pl.BlockSpec((tm, 2 * vh), lambda i: (i, 0)),
                   pl.BlockSpec((2 * vh, tm), lambda i: (0, i))],
        out_shape=[jax.ShapeDtypeStruct((t, 2 * vh), F32), jax.ShapeDtypeStruct((2 * vh, t), F32)],
        compiler_params=_cp(("arbitrary",)),
        name="proj_beta_decay",
    )(x_bf, w_ba.astype(BF), w_ba.T.astype(BF), prow, prow.T)


def _hgrn2_kernel(q_ref, lf_ref, k_ref, v_ref, g_ref, on_ref, s0_ref, o_ref, so_ref, s_scr, gall_scr,
                  g_scr, k_scr, v_scr, *, heads, dk, dv, sub, nsub, nseq, nchunks, group, unroll):
    c = pl.program_id(1)
    cg = sub * nsub

    @pl.when(c == 0)
    def _():
        s_scr[...] = s0_ref[...]

    rows_all = nseq * cg
    shift = sub.bit_length() - 1
    r_i = _iota((rows_all, 3 * rows_all), 0)
    c_i = _iota((rows_all, 3 * rows_all), 1) & (rows_all - 1)
    same = lax.shift_right_logical(r_i, shift) == lax.shift_right_logical(c_i, shift)
    btril3 = jnp.where((r_i >= c_i) & same, 1.0, 0.0).astype(BF)
    gall_scr[...] = _dot(btril3, _split3(lf_ref[...])) * LOG2E
    onorm = on_ref[...]
    n_aug = 16 - sub % 16
    aug_rhs = jnp.concatenate([jnp.zeros((n_aug, dv), BF), jnp.ones((n_aug, dv), BF)], axis=1)
    zeros_v = jnp.zeros((sub, dv), BF)
    zeros_aug = jnp.zeros((n_aug - 3, dk), F32)

    nh = sub // 8
    row8 = _iota((8, 1), 0)

    def block(it, carry):
        for u in range(unroll):
            sub_chunk(it * unroll + u, g_scr.at[u], k_scr.at[u], v_scr.at[u])
        return carry

    def sub_chunk(sb, g_scr, k_scr, v_scr):
        for q in range(nseq):
            rows = pl.ds(pl.multiple_of(q * cg + sb * sub, sub), sub)
            g_scr[q] = gall_scr[rows, :]
            k_scr[q] = k_ref[rows, :]
            v_scr[q] = v_ref[rows, :]
        problems = [(q, h) for q in range(nseq) for h in range(heads)]
        for g0 in range(0, len(problems), group):
            ps = []
            for (q, h) in problems[g0:g0 + group]:
                r0 = pl.multiple_of(q * cg + sb * sub, sub)
                rows = pl.ds(r0, sub)
                kc = pl.ds(h * dk, dk)
                vc = pl.ds(h * dv, dv)
                ps.append(dict(q=q, h=h, r0=r0, rows=rows, kc=kc, vc=vc,
                               gc=g_scr[q, :, kc],
                               qv=q_ref[rows, kc], k=k_ref[rows, kc], v=v_ref[rows, vc]))
            for p in ps:
                p["s_old"] = s_scr[p["q"], p["h"]]
                p["o_state"] = _dot((p["qv"] * jnp.exp2(p["gc"])).astype(BF), p["s_old"].astype(BF))
                p["o8"] = [jnp.zeros((8, dv), F32) for _ in range(nh)]
                p["q8"] = [p["qv"][8 * i:8 * i + 8] for i in range(nh)]
                p["g8"] = [p["gc"][8 * i:8 * i + 8] for i in range(nh)]
            for j in range(sub):
                for p in ps:
                    kj = k_scr[p["q"], pl.ds(j, 1), p["kc"]]
                    vj = v_scr[p["q"], pl.ds(j, 1), p["vc"]]
                    gj = g_scr[p["q"], pl.ds(j, 1), p["kc"]]
                    for i in range(j // 8, nh):
                        pr = (p["q8"][i] * kj) * jnp.exp2(p["g8"][i] - gj)
                        a_col = jnp.sum(pr, axis=-1, keepdims=True)
                        if i == j // 8:
                            a_col = jnp.where(row8 >= j % 8, a_col, 0.0)
                        p["o8"][i] = p["o8"][i] + a_col * vj
            for p in ps:
                p["o"] = p["o_state"] + (jnp.concatenate(p["o8"], axis=0) if nh > 1 else p["o8"][0])
            for p in ps:
                gend = p["gc"][sub - 1:sub, :]
                dend = jnp.exp2(gend)
                d_hi = dend.astype(BF).astype(F32)
                d_mid = (dend - d_hi).astype(BF).astype(F32)
                d_lo = (dend - d_hi - d_mid).astype(BF).astype(F32)
                kdec = p["k"] * jnp.exp2(gend - p["gc"])
                lhs = jnp.concatenate([kdec, d_hi, d_mid, d_lo, zeros_aug], axis=0).astype(BF)
                rhs = jnp.concatenate([jnp.concatenate([p["v"].astype(BF), zeros_v], axis=1), aug_rhs], axis=0)
                p["kv"] = _dot_tn(lhs, rhs)
            for p in ps:
                s_scr[p["q"], p["h"]] = p["kv"][:, dv:] * p["s_old"] + p["kv"][:, :dv]
            for p in ps:
                o = p["o"]
                ms = jnp.mean(o * o, axis=-1, keepdims=True)
                o_ref[p["rows"], p["vc"]] = (o * lax.rsqrt(ms + EPS) * onorm) * _silu(g_ref[p["rows"], p["vc"]])

    lax.fori_loop(0, nsub // unroll, block, 0)

    @pl.when(c == nchunks - 1)
    def _():
        so_ref[...] = s_scr[...]


def _hgrn2(q, lf, kf, v, gate, onorm, s0, *, row0, seq_len, rows_per_step, sub, seqs_per_step):
    nseq_total, heads, dk, dv = s0.shape
    assert dk == dv, "state decay is applied with a (dk, dv) tile"
    width = heads * dk
    cg = rows_per_step
    nseq = seqs_per_step
    nchunks = seq_len // cg
    assert nseq == 1 or nchunks == 1, "several sequences per step only when a step covers whole sequences"
    rows = nseq * cg
    b0 = row0 // rows
    assert sub & (sub - 1) == 0 and sub % 8 == 0 and rows & (rows - 1) == 0
    unroll = next(u for u in (8, 4, 2, 1) if (cg // sub) % u == 0)
    def row_spec(col_block):
        return pl.BlockSpec((rows, width), lambda b, c: (b0 + b * nchunks + c, col_block))

    st_spec = pl.BlockSpec((nseq, heads, dk, dv), lambda b, c: (b, 0, 0, 0))
    kern = functools.partial(_hgrn2_kernel, heads=heads, dk=dk, dv=dv, sub=sub, nsub=cg // sub, nseq=nseq,
                             nchunks=nchunks, group=8, unroll=unroll)
    operands = (q, lf, kf, v, gate)
    return pl.pallas_call(
        kern,
        grid=(nseq_total // nseq, nchunks),
        in_specs=[row_spec(cb) for _, cb in operands]
                 + [pl.BlockSpec((1, dv), lambda b, c: (0, 0)), st_spec],
        out_specs=[pl.BlockSpec((rows, heads * dv), lambda b, c: (b * nchunks + c, 0)), st_spec],
        out_shape=[jax.ShapeDtypeStruct((nseq_total * seq_len, heads * dv), F32),
                   jax.ShapeDtypeStruct(s0.shape, F32)],
        scratch_shapes=[pltpu.VMEM((nseq, heads, dk, dv), F32), pltpu.VMEM((rows, width), F32),
                        pltpu.VMEM((unroll, nseq, sub, width), F32), pltpu.VMEM((unroll, nseq, sub, width), F32),
                        pltpu.VMEM((unroll, nseq, sub, heads * dv), F32)],
        compiler_params=_cp(("arbitrary", "arbitrary")),
        name="hgrn2_recurrence",
    )(*[a for a, _ in operands], onorm.reshape(1, dv), s0)


def _gdn_kernel(qkv_ref, z_ref, bg_ref, bgt_ref, cw_ref, on_ref, s0_ref, c0_ref,
                o_ref, so_ref, co_ref, s_scr, xx, xs,
                *, kheads, vheads, dk, dv, chunk, nsub, nseq, nchunks, group_kh):
    c = pl.program_id(1)
    cg = chunk * nsub
    kdim = kheads * dk
    rep = vheads // kheads
    tail0 = 8 - (CONV_K - 1)

    @pl.when(c == 0)
    def _():
        s_scr[...] = s0_ref[...]
        xx[:, 0:8, :] = c0_ref[...]

    for q in range(nseq):
        xx[q, 8:8 + cg, :] = qkv_ref[q * cg:(q + 1) * cg, :]
        conv = xx[q, pl.ds(tail0, cg), :] * cw_ref[0:1, :]
        for j in range(1, CONV_K):
            conv = conv + xx[q, pl.ds(tail0 + j, cg), :] * cw_ref[j:j + 1, :]
        xs[q] = _silu(conv)

    @pl.when(c == nchunks - 1)
    def _():
        for q in range(nseq):
            co_ref[q] = xx[q, pl.ds(8 + cg - (CONV_K - 1), CONV_K - 1), :]

    for q in range(nseq):
        xx[q, 0:8, :] = xx[q, cg:cg + 8, :]

    wp = chunk
    r_i = _iota((chunk, wp), 0)
    c_i = _iota((chunk, wp), 1)
    causal = r_i >= c_i
    strict = r_i > c_i

    def pad_rows(a):
        if a.shape[0] == wp:
            return a
        if a.dtype == BF and a.shape[0] % 16:
            return pad_rows(a.astype(F32)).astype(BF)
        return jnp.concatenate([a, jnp.zeros((wp - a.shape[0],) + a.shape[1:], a.dtype)], axis=0)

    def dot3_padded(a, b):
        a_hi = a.astype(BF).astype(F32)
        lhs = jnp.concatenate([a_hi, a_hi, a - a_hi], axis=1).astype(BF)
        bh, bl = (pad_rows(t) for t in _split(b))
        return _dot(lhs, jnp.concatenate([bh, bl, bh], axis=0))
    cm = chunk - 1
    tril3 = jnp.where(_iota((chunk, 3 * chunk), 0) >= (_iota((chunk, 3 * chunk), 1) & cm), 1.0, 0.0).astype(BF)
    t_r = _iota((3 * chunk, wp), 0) & cm
    t_c = _iota((3 * chunk, wp), 1)
    triu3 = jnp.where((t_r <= t_c) & (t_c < chunk), 1.0, 0.0).astype(BF)
    onorm = on_ref[...]

    cums = {}

    def cumulative_decay(q, s):
        if (q, s) not in cums:
            r0 = q * cg + s * chunk
            bg = bg_ref[r0:r0 + chunk, :]
            gt = bgt_ref[0, vheads:2 * vheads, r0:r0 + chunk]
            cums[(q, s)] = (bg, _dot(tril3, _split3(bg[:, vheads:2 * vheads], axis=0)),
                            _dot(_split3(gt, axis=1), triu3))
        return cums[(q, s)]

    def run_group(items):
        probs = []
        for (q, s, kh) in items:
            r0 = q * cg + s * chunk
            x0 = s * chunk
            bg, gcol_all, grow_all = cumulative_decay(q, s)
            for kh in (kh,):
                qh = xs[q, x0:x0 + chunk, kh * dk:(kh + 1) * dk]
                kk_ = xs[q, x0:x0 + chunk, kdim + kh * dk:kdim + (kh + 1) * dk]
                qn = qh * lax.rsqrt(jnp.sum(qh * qh, axis=-1, keepdims=True) + EPS) * (dk ** -0.5)
                kn = kk_ * lax.rsqrt(jnp.sum(kk_ * kk_, axis=-1, keepdims=True) + EPS)
                kn_bf = kn.astype(BF)
                kn_pad = pad_rows(kn_bf)
                kk = _dot_nt(kn_bf, kn_pad)
                qk = _dot_nt(qn.astype(BF), kn_pad)
                for r in range(rep):
                    h = kh * rep + r
                    vh_ = xs[q, x0:x0 + chunk, 2 * kdim + h * dv:2 * kdim + (h + 1) * dv]
                    gcol = gcol_all[:, h:h + 1]
                    grow = grow_all[h:h + 1, :]
                    beta = bg[:, h:h + 1]
                    decay = jnp.where(causal, jnp.exp(jnp.minimum(gcol - grow, 0.0)), 0.0)
                    eg = jnp.exp(gcol)
                    gend = gcol[chunk - 1:chunk, :]
                    probs.append(dict(
                        q=q, s=s, h=h, r0=r0,
                        x=jnp.where(strict, -(beta * kk * decay), 0.0),
                        y=jnp.concatenate([vh_ * beta, kn * (beta * eg)], axis=1),
                        a_bf=(qk * decay).astype(BF),
                        qg=qn * eg,
                        kdec_bf=(kn * jnp.exp(gend - gcol)).astype(BF),
                        send=jnp.exp(gend)))

        if chunk <= 16:
            for j in range(chunk - 1):
                for p in probs:
                    p["y"] = p["y"] + p["x"][:, j:j + 1] * p["y"][j:j + 1, :]
        else:
            n_fac = int(math.log2(chunk))
            for k in range(n_fac):
                for p in probs:
                    p["y"] = p["y"] + dot3_padded(p["x"], p["y"])
                if k + 1 < n_fac:
                    for p in probs:
                        p["x"] = dot3_padded(p["x"], p["x"])

        for s in sorted({p["s"] for p in probs}):
            cur = [p for p in probs if p["s"] == s]
            for p in cur:
                p["s_old"] = s_scr[p["q"], p["h"]]
                lhs = jnp.concatenate([p["y"][:, dv:], p["qg"]], axis=0).astype(BF)
                p["ws"] = _dot(lhs, p["s_old"].astype(BF))
            for p in cur:
                p["u_bf"] = (p["y"][:, :dv] - p["ws"][:chunk]).astype(BF)
            for p in cur:
                s_scr[p["q"], p["h"]] = p["send"] * p["s_old"] + _dot_tn(p["kdec_bf"], p["u_bf"])
            for p in cur:
                o = p["ws"][chunk:] + _dot(p["a_bf"], pad_rows(p["u_bf"]))
                ms = jnp.mean(o * o, axis=-1, keepdims=True)
                zc = slice(p["h"] * dv, (p["h"] + 1) * dv)
                rows = slice(p["r0"], p["r0"] + chunk)
                o_ref[rows, zc] = (o * lax.rsqrt(ms + EPS) * onorm) * _silu(z_ref[rows, zc])

    for g0 in range(0, kheads, group_kh):
        run_group([(q, s, kh) for q in range(nseq) for s in range(nsub)
                   for kh in range(g0, min(g0 + group_kh, kheads))])

    @pl.when(c == nchunks - 1)
    def _():
        so_ref[...] = s_scr[...]


def _gdn(qkv, z, bg, bgt3, conv_w, onorm, s0, conv0, *, row0, seq_len, rows_per_step, chunk, seqs_per_step):
    nseq_total, vheads, dk, dv = s0.shape
    (qkv, qkv_cb), (z, z_cb) = qkv, z
    conv_dim = conv_w.shape[1]
    kheads = (conv_dim - vheads * dv) // (2 * dk)
    cg = rows_per_step
    nseq = seqs_per_step
    nchunks = seq_len // cg
    assert nseq == 1 or nchunks == 1, "several sequences per step only when a step covers whole sequences"
    rows = nseq * cg
    b0 = row0 // rows
    conv0p = jnp.pad(conv0, ((0, 0), (8 - (CONV_K - 1), 0), (0, 0)))
    rmap = lambda b, c: (b0 + b * nchunks + c, 0)
    st_spec = pl.BlockSpec((nseq, vheads, dk, dv), lambda b, c: (b, 0, 0, 0))
    kern = functools.partial(_gdn_kernel, kheads=kheads, vheads=vheads, dk=dk, dv=dv, chunk=chunk,
                             nsub=cg // chunk, nseq=nseq, nchunks=nchunks, group_kh=kheads)
    return pl.pallas_call(
        kern,
        grid=(nseq_total // nseq, nchunks),
        in_specs=[pl.BlockSpec((rows, conv_dim), lambda b, c: (b0 + b * nchunks + c, qkv_cb)),
                  pl.BlockSpec((rows, vheads * dv), lambda b, c: (b0 + b * nchunks + c, z_cb)),
                  pl.BlockSpec((rows, 2 * vheads), rmap),
                  pl.BlockSpec((1, 2 * vheads, rows), lambda b, c: (b * nchunks + c, 0, 0)),
                  pl.BlockSpec((CONV_K, conv_dim), lambda b, c: (0, 0)),
                  pl.BlockSpec((1, dv), lambda b, c: (0, 0)),
                  st_spec,
                  pl.BlockSpec((nseq, 8, conv_dim), lambda b, c: (b, 0, 0))],
        out_specs=[pl.BlockSpec((rows, vheads * dv), lambda b, c: (b * nchunks + c, 0)),
                   st_spec,
                   pl.BlockSpec((nseq, CONV_K - 1, conv_dim), lambda b, c: (b, 0, 0))],
        out_shape=[jax.ShapeDtypeStruct((nseq_total * seq_len, vheads * dv), F32),
                   jax.ShapeDtypeStruct(s0.shape, F32),
                   jax.ShapeDtypeStruct((nseq_total, CONV_K - 1, conv_dim), F32)],
        scratch_shapes=[pltpu.VMEM((nseq, vheads, dk, dv), F32), pltpu.VMEM((nseq, 8 + cg, conv_dim), F32),
                        pltpu.VMEM((nseq, cg, conv_dim), F32)],
        compiler_params=_cp(("arbitrary", "arbitrary")),
        name="gdn_recurrence",
    )(qkv, z, bg, bgt3, conv_w, onorm.reshape(1, dv), s0, conv0p)


def _merge_kernel(xn_ref, oa1_ref, oa2_ref, ob1_ref, ob2_ref, wga_ref, wgb_ref, wa_ref, wb_ref, o_ref,
                  oa_bf, ob_bf, *, na):
    i = pl.program_id(0)
    j = pl.program_id(1)

    @pl.when((j == 0) & (i < na))
    def _():
        oa_bf[...] = oa1_ref[...].astype(BF)
        ob_bf[...] = ob1_ref[...].astype(BF)

    @pl.when((j == 0) & (i >= na))
    def _():
        oa_bf[...] = oa2_ref[...].astype(BF)
        ob_bf[...] = ob2_ref[...].astype(BF)

    xn = xn_ref[...]
    ya = _dot(oa_bf[...], wa_ref[...])
    yb = _dot(ob_bf[...], wb_ref[...])
    ga = _dot(xn, wga_ref[...])
    gb = _dot(xn, wgb_ref[...])
    o_ref[...] = (_sigmoid(ga) * ya + _sigmoid(gb) * yb).astype(o_ref.dtype)


def _merge(xn, oa_p, oa_s, ob_p, ob_s, w_gates_bf, w_pa, w_pb):
    rp, rs = oa_p.shape[0], oa_s.shape[0]
    ka, kb = oa_p.shape[1], ob_p.shape[1]
    d = w_pa.shape[1]
    k = xn.shape[1]
    tm = _pick(math.gcd(rp, rs), 512)
    tn = _pick(d, 512, LANES)
    na, nb = rp // tm, rs // tm
    nj = d // tn
    amap = lambda i, j: (jnp.minimum(i, na - 1), 0)
    bmap = lambda i, j: (jnp.maximum(i - na, 0), 0)
    return pl.pallas_call(
        functools.partial(_merge_kernel, na=na),
        grid=(na + nb, nj),
        in_specs=[pl.BlockSpec((tm, k), lambda i, j: (i, 0)),
                  pl.BlockSpec((tm, ka), amap), pl.BlockSpec((tm, ka), bmap),
                  pl.BlockSpec((tm, kb), amap), pl.BlockSpec((tm, kb), bmap),
                  pl.BlockSpec((k, tn), lambda i, j: (0, j)),
                  pl.BlockSpec((k, tn), lambda i, j: (0, nj + j)),
                  pl.BlockSpec((ka, tn), lambda i, j: (0, j)),
                  pl.BlockSpec((kb, tn), lambda i, j: (0, j))],
        out_specs=pl.BlockSpec((tm, tn), lambda i, j: (i, j)),
        out_shape=jax.ShapeDtypeStruct((rp + rs, d), BF),
        scratch_shapes=[pltpu.VMEM((tm, ka), BF), pltpu.VMEM((tm, kb), BF)],
        compiler_params=_cp(("arbitrary", "arbitrary")),
        name="gates_branch_proj_merge",
    )(xn, oa_p, oa_s, ob_p, ob_s, w_gates_bf, w_gates_bf, w_pa.astype(BF), w_pb.astype(BF))


def _outproj_kernel(m_ref, xa_ref, xb_ref, wo_ref, nw_ref, wr_ref, h_ref, xf_ref, lg_ref, *, na):
    i = pl.program_id(0)

    def body(x_ref):
        h = x_ref[...] + _dot(m_ref[...], wo_ref[...])
        h_ref[...] = h
        ms = jnp.mean(h * h, axis=-1, keepdims=True)
        xf = h * lax.rsqrt(ms + EPS) * nw_ref[...]
        _store_token_major(xf_ref, xf)
        lg_ref[...] = _dot(xf.astype(BF), wr_ref[...])

    @pl.when(i < na)
    def _():
        body(xa_ref)

    @pl.when(i >= na)
    def _():
        body(xb_ref)


def _outproj(merged, xa, xb, w_out, norm_ffn, w_router):
    t, d = merged.shape
    tm = _pick(math.gcd(xa.shape[0], xb.shape[0]), 512)
    na, nb, sa, sb = _two_source_specs(xa.shape[0], xb.shape[0], tm, d)
    nr = w_router.shape[1]
    const = lambda i: (0, 0)
    return pl.pallas_call(
        functools.partial(_outproj_kernel, na=na),
        grid=(na + nb,),
        in_specs=[pl.BlockSpec((tm, d), lambda i: (i, 0)), sa, sb,
                  pl.BlockSpec((d, d), const, pipeline_mode=pl.Buffered(1)),
                  pl.BlockSpec((1, d), const),
                  pl.BlockSpec((d, nr), const, pipeline_mode=pl.Buffered(1))],
        out_specs=[pl.BlockSpec((tm, d), lambda i: (i, 0)),
                   pl.BlockSpec((tm * (d // LANES), LANES), lambda i: (i, 0)),
                   pl.BlockSpec((tm, nr), lambda i: (i, 0))],
        out_shape=[jax.ShapeDtypeStruct((t, d), F32), jax.ShapeDtypeStruct((t * (d // LANES), LANES), F32),
                   jax.ShapeDtypeStruct((t, nr), F32)],
        compiler_params=_cp(("arbitrary",), 56),
        name="out_proj_ffn_norm",
    )(merged, xa, xb, w_out.astype(BF), norm_ffn.reshape(1, d), w_router)


def _route_kernel(lg_ref, b_ref, eid_ref, gate_ref, rank_ref, cnt_ref, carry,
                  *, ngroups, nexp, tm):
    i = pl.program_id(0)

    @pl.when(i == 0)
    def _():
        carry[...] = jnp.zeros_like(carry)

    per_group = nexp // ngroups
    lg = lg_ref[...] + b_ref[...]
    lane = _iota(lg.shape, 1)
    big = jnp.int32(1 << 20)
    is_g = lane < ngroups
    gl = jnp.where(is_g, lg, NEG)
    gmax = jnp.max(gl, axis=-1, keepdims=True)
    gidx = jnp.min(jnp.where(gl == gmax, lane, big), axis=-1, keepdims=True)
    gsum = jnp.sum(jnp.where(is_g, jnp.exp(gl - gmax), 0.0), axis=-1, keepdims=True)
    gw = 1.0 / gsum
    elane = lane - ngroups
    in_grp = (elane >= gidx * per_group) & (elane < (gidx + 1) * per_group)
    el = jnp.where(in_grp, lg, NEG)
    v1 = jnp.max(el, axis=-1, keepdims=True)
    i1 = jnp.min(jnp.where(in_grp & (el == v1), elane, big), axis=-1, keepdims=True)
    in2 = in_grp & (elane != i1)
    el2 = jnp.where(in2, lg, NEG)
    v2 = jnp.max(el2, axis=-1, keepdims=True)
    i2 = jnp.min(jnp.where(in2 & (el2 == v2), elane, big), axis=-1, keepdims=True)
    p2 = jnp.exp(v2 - v1)
    den = 1.0 + p2
    lane2 = _iota((tm, TOP_K), 1)
    eid_ref[...] = jnp.where(lane2 == 0, i1, i2)
    gate_ref[...] = jnp.where(lane2 == 0, gw / den, gw * p2 / den)

    oh1 = (elane == i1).astype(F32)
    oh2 = (elane == i2).astype(F32)
    lower = (_iota((tm, tm), 0) > _iota((tm, tm), 1)).astype(BF)
    cs1 = _dot(lower, oh1.astype(BF))
    cs2 = _dot(lower, oh2.astype(BF))
    tot1 = jnp.sum(oh1, axis=0, keepdims=True)
    tot2 = jnp.sum(oh2, axis=0, keepdims=True)
    base = carry[0:1, :]
    r1 = jnp.sum(oh1 * (base + cs1), axis=-1, keepdims=True)
    r2 = jnp.sum(oh2 * (base + tot1 + cs2), axis=-1, keepdims=True)
    rank_ref[...] = jnp.where(lane2 == 0, r1, r2).astype(I32)
    new = base + tot1 + tot2
    carry[...] = jnp.broadcast_to(new, carry.shape)
    cnt_ref[...] = jnp.broadcast_to(new, cnt_ref.shape)


def _route(logits, bias_row, ngroups, nexp):
    t, nr = logits.shape
    tm = _pick(t, 512)
    kern = functools.partial(_route_kernel, ngroups=ngroups, nexp=nexp, tm=tm)
    return pl.pallas_call(
        kern,
        grid=(t // tm,),
        in_specs=[pl.BlockSpec((tm, nr), lambda i: (i, 0)), pl.BlockSpec((1, nr), lambda i: (0, 0))],
        out_specs=[pl.BlockSpec((tm, TOP_K), lambda i: (i, 0)),
                   pl.BlockSpec((tm, TOP_K), lambda i: (i, 0)),
                   pl.BlockSpec((tm, TOP_K), lambda i: (i, 0)),
                   pl.BlockSpec((8, nr), lambda i: (0, 0))],
        out_shape=[jax.ShapeDtypeStruct((t, TOP_K), I32), jax.ShapeDtypeStruct((t, TOP_K), F32),
                   jax.ShapeDtypeStruct((t, TOP_K), I32), jax.ShapeDtypeStruct((8, nr), F32)],
        scratch_shapes=[pltpu.VMEM((8, nr), F32)],
        compiler_params=_cp(("arbitrary",)),
        name="route_topk_rank",
    )(logits, bias_row)


def _plan_kernel(cnt_ref, eid_ref, rank_ref, pos_ref, be_ref, *, ngroups, nexp, nblocks_pad, tm):
    nr = cnt_ref.shape[1]
    cnt = cnt_ref[0:1, :]
    padded = jnp.floor((cnt + (MOE_BLOCK - 1)) * (1.0 / MOE_BLOCK)) * MOE_BLOCK
    r_i = _iota((nr, nr), 0)
    c_i = _iota((nr, nr), 1)
    padded_col = jnp.sum(jnp.where(r_i == c_i, jnp.broadcast_to(padded, (nr, nr)), 0.0), axis=1, keepdims=True)
    start = jnp.sum(jnp.where(r_i < c_i, jnp.broadcast_to(padded_col, (nr, nr)), 0.0), axis=0, keepdims=True)
    end = start + padded
    lane = _iota((tm, nr), 1)
    eid = eid_ref[...]
    rank = rank_ref[...]
    lane2 = _iota((tm, TOP_K), 1)
    pos = jnp.zeros((tm, TOP_K), F32)
    for k in range(TOP_K):
        oh = (lane - ngroups) == eid[:, k:k + 1]
        st = jnp.sum(jnp.where(oh, start, 0.0), axis=-1, keepdims=True)
        pos = jnp.where(lane2 == k, st, pos)
    pos_ref[...] = pos.astype(I32) + rank

    @pl.when(pl.program_id(0) == 0)
    def _():
        blk_row = _iota((nblocks_pad, nr), 0).astype(F32) * MOE_BLOCK
        lane_b = _iota((nblocks_pad, nr), 1)
        is_e = (lane_b >= ngroups) & (lane_b < ngroups + nexp)
        n_le = jnp.sum(jnp.where(is_e & (end <= blk_row), 1.0, 0.0), axis=-1, keepdims=True)
        e_lane = (lane_b - ngroups).astype(F32)
        nonempty = is_e & (padded > 0.0)
        e_last = jnp.max(jnp.where(nonempty, e_lane, -1.0), axis=-1, keepdims=True)
        be = jnp.minimum(n_le, e_last)
        nxt = jnp.min(jnp.where(nonempty & (e_lane > be), e_lane, 1e9), axis=-1, keepdims=True)
        nxt = jnp.where(nxt > 1e8, -1.0, nxt)
        total = jnp.sum(padded, axis=-1, keepdims=True)
        valid = jnp.where(blk_row[:, 0:1] < total, 1.0, 0.0)
        col = _iota((nblocks_pad, 4), 1)
        meta = jnp.where(col == 0, be, jnp.where(col == 1, nxt, jnp.where(col == 2, valid, 0.0)))
        be_ref[...] = meta.astype(I32)


def _plan(cnt, eid, rank, ngroups, nexp, nblocks):
    t = eid.shape[0]
    nr = cnt.shape[1]
    tm = _pick(t, 512)
    nblocks_pad = -(-nblocks // 8) * 8
    kern = functools.partial(_plan_kernel, ngroups=ngroups, nexp=nexp, nblocks_pad=nblocks_pad, tm=tm)
    return pl.pallas_call(
        kern,
        grid=(t // tm,),
        in_specs=[pl.BlockSpec((8, nr), lambda i: (0, 0)),
                  pl.BlockSpec((tm, TOP_K), lambda i: (i, 0)),
                  pl.BlockSpec((tm, TOP_K), lambda i: (i, 0))],
        out_specs=[pl.BlockSpec((tm, TOP_K), lambda i: (i, 0)),
                   pl.BlockSpec((nblocks_pad, 4), lambda i: (0, 0))],
        out_shape=[jax.ShapeDtypeStruct((t, TOP_K), I32), jax.ShapeDtypeStruct((nblocks_pad, 4), I32)],
        compiler_params=_cp(("arbitrary",)),
        name="route_plan",
    )(cnt, eid, rank)


def _invert_kernel(pos_ref, tok_ref, *, n_assign, n_rows):
    def clear(r, c):
        tok_ref[r] = 0
        return c

    lax.fori_loop(0, n_rows, clear, 0, unroll=16)

    def put(a, c):
        tok_ref[pos_ref[a]] = lax.shift_right_logical(a, TOP_K.bit_length() - 1)
        return c

    lax.fori_loop(0, n_assign, put, 0, unroll=16)


def _invert(pos_flat, n_rows):
    n_assign = pos_flat.shape[0]
    return pl.pallas_call(
        functools.partial(_invert_kernel, n_assign=n_assign, n_rows=n_rows),
        in_specs=[pl.BlockSpec(memory_space=pltpu.SMEM)],
        out_specs=pl.BlockSpec(memory_space=pltpu.SMEM),
        out_shape=jax.ShapeDtypeStruct((n_rows,), I32),
        name="route_invert",
    )(pos_flat)


def _expert_kernel(be_ref, nxt_ref, valid_ref, tok_ref, x_hbm, wg_hbm, wu_hbm, wd_hbm, y_ref,
                   xbuf0, xbuf1, xsem, wg_st, wu_st, wd_st, wsem, wg_bf, wu_bf, wd_bf, *, nch):
    i = pl.program_id(0)
    n = pl.num_programs(0)
    xbufs = (xbuf0, xbuf1)

    def row_copy(blk, s, r):
        t = tok_ref[blk * MOE_BLOCK + r]
        src = x_hbm.at[pl.ds(pl.multiple_of(t * nch, nch), nch), :]
        return pltpu.make_async_copy(src, xbufs[s].at[pl.ds(r * nch, nch), :], xsem.at[s])

    def block_wait(s):
        pltpu.make_async_copy(x_hbm.at[pl.ds(0, MOE_BLOCK * nch), :], xbufs[s], xsem.at[s]).wait()

    def weight_copies(e):
        return (pltpu.make_async_copy(wg_hbm.at[e], wg_st, wsem.at[0]),
                pltpu.make_async_copy(wu_hbm.at[e], wu_st, wsem.at[1]),
                pltpu.make_async_copy(wd_hbm.at[e], wd_st, wsem.at[2]))

    @pl.when(i == 0)
    def _():
        for r in range(MOE_BLOCK):
            row_copy(0, 0, r).start()
        for cp in weight_copies(be_ref[0]):
            cp.start(priority=1)

    prev = be_ref[jnp.maximum(i - 1, 0)]

    @pl.when((i == 0) | (be_ref[i] != prev))
    def _():
        for cp in weight_copies(be_ref[i]):
            cp.wait()
        wg_bf[...] = wg_st[...].astype(BF)
        wu_bf[...] = wu_st[...].astype(BF)
        wd_bf[...] = wd_st[...].astype(BF)

        @pl.when(nxt_ref[i] >= 0)
        def _():
            for cp in weight_copies(nxt_ref[i]):
                cp.start(priority=1)

    ahead = jnp.minimum(i + 1, n - 1)
    used = valid_ref[i] != 0
    used_ahead = (i + 1 < n) & (valid_ref[ahead] != 0)

    def step(cur, oth):
        @pl.when(used_ahead)
        def _():
            for r in range(MOE_BLOCK):
                row_copy(ahead, oth, r).start()

        block_wait(cur)
        x = _load_token_major(xbufs[cur], MOE_BLOCK, nch).astype(BF)
        hid = (_silu(_dot(x, wg_bf[...])) * _dot(x, wu_bf[...])).astype(BF)
        _store_token_major(y_ref, _dot(hid, wd_bf[...]))

    @pl.when(used & (i % 2 == 0))
    def _():
        step(0, 1)

    @pl.when(used & (i % 2 == 1))
    def _():
        step(1, 0)

    @pl.when(jnp.logical_not(used))
    def _():
        y_ref[...] = jnp.zeros_like(y_ref)


def _experts(block_e, block_next, block_valid, row_tok, xf, w_eg, w_eu, w_ed):
    nrows = row_tok.shape[0]
    d = w_eg.shape[1]
    nch = d // LANES
    nblocks = nrows // MOE_BLOCK
    de = w_eg.shape[2]
    any_spec = pl.BlockSpec(memory_space=pl.ANY)
    grid_spec = pltpu.PrefetchScalarGridSpec(
        num_scalar_prefetch=4,
        grid=(nblocks,),
        in_specs=[any_spec, any_spec, any_spec, any_spec],
        out_specs=pl.BlockSpec((MOE_BLOCK * nch, LANES), lambda i, *_: (i, 0)),
        scratch_shapes=[pltpu.VMEM((MOE_BLOCK * nch, LANES), F32), pltpu.VMEM((MOE_BLOCK * nch, LANES), F32),
                        pltpu.SemaphoreType.DMA((2,)),
                        pltpu.VMEM((d, de), F32), pltpu.VMEM((d, de), F32), pltpu.VMEM((de, d), F32),
                        pltpu.SemaphoreType.DMA((3,)),
                        pltpu.VMEM((d, de), BF), pltpu.VMEM((d, de), BF), pltpu.VMEM((de, d), BF)],
    )
    return pl.pallas_call(
        functools.partial(_expert_kernel, nch=nch),
        grid_spec=grid_spec,
        out_shape=jax.ShapeDtypeStruct((nrows * nch, LANES), F32),
        compiler_params=_cp(("arbitrary",), 48),
        name="expert_blocks",
    )(block_e, block_next, block_valid, row_tok, xf, w_eg, w_eu, w_ed)


def _final_kernel(pos_ref, h_ref, gate_ref, w_ref, yb_hbm, ya_ref, ys_ref, gbuf, sem, *, na, tm, nch):
    i = pl.program_id(0)
    n = pl.num_programs(0)
    slot = i % 2

    def row_copy(tile, slot_, r, k):
        p = pos_ref[(tile * tm + r) * TOP_K + k]
        src = yb_hbm.at[pl.ds(pl.multiple_of(p * nch, nch), nch), :]
        return pltpu.make_async_copy(src, gbuf.at[slot_, k, pl.ds(r * nch, nch), :], sem.at[slot_])

    def start_tile(tile, slot_):
        for r in range(tm):
            for k in range(TOP_K):
                row_copy(tile, slot_, r, k).start()

    @pl.when(i == 0)
    def _():
        start_tile(0, 0)

    @pl.when(i + 1 < n)
    def _():
        start_tile(i + 1, 1 - slot)

    for r in range(tm):
        for k in range(TOP_K):
            row_copy(i, slot, r, k).wait()

    gate = gate_ref[...]
    y = h_ref[...]
    ffn = _load_token_major(gbuf.at[slot, 0], tm, nch) * gate[:, 0:1]
    for k in range(1, TOP_K):
        ffn = ffn + _load_token_major(gbuf.at[slot, k], tm, nch) * gate[:, k:k + 1]
    y = y + ffn
    ms = jnp.mean(y * y, axis=-1, keepdims=True)
    out = y * lax.rsqrt(ms + EPS) * w_ref[...]

    @pl.when(i < na)
    def _():
        ya_ref[...] = out

    @pl.when(i >= na)
    def _():
        ys_ref[...] = out


def _final(pos_flat, h, gate, yb, w, rows_a, rows_b):
    t, d = h.shape
    tm = _pick(math.gcd(rows_a, rows_b), 128)
    na, nb = rows_a // tm, rows_b // tm
    grid_spec = pltpu.PrefetchScalarGridSpec(
        num_scalar_prefetch=1,
        grid=(na + nb,),
        in_specs=[pl.BlockSpec((tm, d), lambda i, pos: (i, 0)),
                  pl.BlockSpec((tm, TOP_K), lambda i, pos: (i, 0)),
                  pl.BlockSpec((1, d), lambda i, pos: (0, 0)),
                  pl.BlockSpec(memory_space=pl.ANY)],
        out_specs=[pl.BlockSpec((tm, d), lambda i, pos: (jnp.minimum(i, na - 1), 0)),
                   pl.BlockSpec((tm, d), lambda i, pos: (jnp.maximum(i - na, 0), 0))],
        scratch_shapes=[pltpu.VMEM((2, TOP_K, tm * (d // LANES), LANES), F32), pltpu.SemaphoreType.DMA((2,))],
    )
    return pl.pallas_call(
        functools.partial(_final_kernel, na=na, tm=tm, nch=d // LANES),
        grid_spec=grid_spec,
        out_shape=[jax.ShapeDtypeStruct((rows_a, d), F32), jax.ShapeDtypeStruct((rows_b, d), F32)],
        compiler_params=_cp(("arbitrary",)),
        name="combine_final_norm",
    )(pos_flat, h, gate, w.reshape(1, d), yb)


def kernel(x_prompt, x_sample, state_hgrn, state_gdn, state_conv, norm_mix, w_in, conv_w, ha_lb_logits, ha_onorm, w_pa, gd_A_log, gd_dt_bias, gd_onorm, w_pb, w_out, norm_ffn, w_router_group, b_router_group, w_router_expert, b_router_expert, w_exp_gate, w_exp_up, w_exp_down, norm_final):
    depth = state_hgrn.shape[0]
    assert depth == 1, "one decoder layer"
    bp, lp, d = x_prompt.shape
    bs, ls, _ = x_sample.shape
    _, _, ha_heads, ha_dk, ha_dv = state_hgrn.shape
    _, _, vheads, gd_dk, gd_dv = state_gdn.shape
    conv_dim = state_conv.shape[3]
    ha_kdim, ha_vdim = ha_heads * ha_dk, ha_heads * ha_dv
    gd_vdim = vheads * gd_dv
    ngroups = w_router_group.shape[2]
    nexp = w_router_expert.shape[2]
    rp, rs = bp * lp, bs * ls
    t = rp + rs

    off_f = ha_kdim
    off_i = off_f + ha_kdim
    off_g = off_i + ha_vdim
    off_qkv = off_g + ha_vdim
    off_z = off_qkv + conv_dim
    off_b = off_z + gd_vdim
    off_gate = off_b + 2 * vheads

    xn = _rmsnorm_bf16(x_prompt.reshape(rp, d), x_sample.reshape(rs, d), norm_mix[0])

    assert ha_kdim == ha_vdim and off_qkv % conv_dim == 0 and off_z % gd_vdim == 0
    w_t = jnp.swapaxes(w_in, 1, 2)
    proj, k_a = _inproj(xn, w_t, ha_lb_logits, off_b, ha_kdim, ha_dk ** -0.5)
    q_a, logf, v_a, g_a = ((proj, cb) for cb in (0, off_f // ha_kdim, off_i // ha_kdim, off_g // ha_kdim))
    k_a = (k_a, 0)
    qkv = (proj, off_qkv // conv_dim)
    z_b = (proj, off_z // gd_vdim)
    w_gates_bf = jnp.swapaxes(w_t[0, off_gate:, :], 0, 1).astype(BF)
    bg, bgt = _beta_decay(xn, jnp.swapaxes(w_t[0, off_b:off_gate, :], 0, 1), gd_A_log[0], gd_dt_bias[0])

    cg_p = _pick(lp, 128, GDN_CHUNK) if lp >= GDN_CHUNK else lp
    sub_p = min(HGRN_SUB, lp)
    chunk_p = min(GDN_CHUNK, lp)
    cg_s = ls
    zeros_h = jnp.zeros((bp, ha_heads, ha_dk, ha_dv), F32)
    zeros_g = jnp.zeros((bp, vheads, gd_dk, gd_dv), F32)
    zeros_c = jnp.zeros((bp, CONV_K - 1, conv_dim), F32)

    oa_p, sh_p = _hgrn2(q_a, logf, k_a, v_a, g_a, ha_onorm[0], zeros_h,
                        row0=0, seq_len=lp, rows_per_step=cg_p, sub=sub_p, seqs_per_step=1)
    oa_s, sh_s = _hgrn2(q_a, logf, k_a, v_a, g_a, ha_onorm[0], state_hgrn[0],
                        row0=rp, seq_len=ls, rows_per_step=cg_s, sub=min(HGRN_SUB, ls),
                        seqs_per_step=math.gcd(_pick(bs, HGRN_SAMPLE_SEQS, 1), max(rp // cg_s, 1)))

    def time_on_lanes(rows0, nrows, cg):
        part = bgt[:, rows0:rows0 + nrows].reshape(2 * vheads, nrows // cg, cg)
        return jnp.transpose(part, (1, 0, 2))

    seqs_s = _pick(bs, GDN_SAMPLE_SEQS, 1)
    cg_g = _pick(lp, GDN_ROWS, GDN_CHUNK) if lp >= GDN_CHUNK else lp
    bgt_p = time_on_lanes(0, rp, cg_g)
    bgt_s = time_on_lanes(rp, rs, seqs_s * cg_s)
    ob_p, sg_p, sc_p = _gdn(qkv, z_b, bg, bgt_p, conv_w[0], gd_onorm[0], zeros_g, zeros_c,
                            row0=0, seq_len=lp, rows_per_step=cg_g, chunk=chunk_p, seqs_per_step=1)
    ob_s, sg_s, sc_s = _gdn(qkv, z_b, bg, bgt_s, conv_w[0], gd_onorm[0], state_gdn[0],
                            state_conv[0], row0=rp, seq_len=ls, rows_per_step=cg_s, chunk=min(GDN_CHUNK, ls),
                            seqs_per_step=seqs_s)

    merged = _merge(xn, oa_p, oa_s, ob_p, ob_s, w_gates_bf, w_pa[0], w_pb[0])

    nr = LANES
    w_router = jnp.concatenate([w_router_group[0], w_router_expert[0]], axis=1)
    w_router = jnp.pad(w_router, ((0, 0), (0, nr - ngroups - nexp))).astype(BF)
    b_router = jnp.pad(jnp.concatenate([b_router_group[0], b_router_expert[0]]), (0, nr - ngroups - nexp))
    h, xf, logits = _outproj(merged, x_prompt.reshape(rp, d), x_sample.reshape(rs, d), w_out[0],
                             norm_ffn[0], w_router)

    eid, gate, rank, cnt = _route(logits, b_router.reshape(1, nr), ngroups, nexp)
    nblocks = -(-(t * TOP_K) // MOE_BLOCK) + nexp
    pos, block_meta = _plan(cnt, eid, rank, ngroups, nexp, nblocks)
    block_e, block_next, block_valid = (block_meta[:nblocks, j] for j in range(3))

    pos_flat = pos.reshape(-1)
    row_tok = _invert(pos_flat, nblocks * MOE_BLOCK)
    yb = _experts(block_e, block_next, block_valid, row_tok, xf, w_exp_gate[0], w_exp_up[0], w_exp_down[0])
    y_p, y_s = _final(pos_flat, h, gate, yb, norm_final, rp, rs)
    return (y_p.reshape(bp, lp, d), y_s.reshape(bs, ls, d),
            sh_p[None], sg_p[None], sc_p[None], sh_s[None], sg_s[None], sc_s[None])
```

```python
import functools
import math

import jax
import jax.numpy as jnp
from jax import lax
from jax.experimental import pallas as pl
from jax.experimental.pallas import tpu as pltpu

F32 = jnp.float32
BF = jnp.bfloat16
I32 = jnp.int32
EPS = 1e-6
CONV_K = 4
TOP_K = 2
MOE_BLOCK = 128
GDN_CHUNK = 64
GDN_ROWS = 128
GDN_SAMPLE_SEQS = 4
HGRN_SUB = 16
HGRN_SAMPLE_SEQS = 4
LOG2E = 1.4426950408889634
LANES = 128
NEG = -3.0e38
HI = lax.Precision.HIGHEST
MIB = 1024 * 1024


def _cp(sem, vmem_mib=48):
    return pltpu.CompilerParams(dimension_semantics=sem, vmem_limit_bytes=vmem_mib * MIB)


def _pick(n, pref, mult=8):
    best = None
    for t in range(mult, min(n, pref) + 1, mult):
        if n % t == 0:
            best = t
    return best if best is not None else n


def _sigmoid(x):
    return 1.0 / (1.0 + jnp.exp(-x))


def _silu(x):
    return x * _sigmoid(x)


def _iota(shape, dim):
    return lax.broadcasted_iota(I32, shape, dim)


def _dot(a, b):
    return jnp.dot(a, b, preferred_element_type=F32)


def _dot_nt(a, b):
    return lax.dot_general(a, b, (((1,), (1,)), ((), ())), preferred_element_type=F32)


def _dot_tn(a, b):
    return lax.dot_general(a, b, (((0,), (0,)), ((), ())), preferred_element_type=F32)


def _split(a):
    hi = a.astype(BF)
    lo = (a - hi.astype(F32)).astype(BF)
    return hi, lo


def _split3(a, axis=0):
    hi = a.astype(BF)
    r1 = a - hi.astype(F32)
    mid = r1.astype(BF)
    lo = (r1 - mid.astype(F32)).astype(BF)
    return jnp.concatenate([hi, mid, lo], axis=axis)


def _dot3(a, b):
    ah, al = _split(a)
    bh, bl = _split(b)
    return _dot(ah, bh) + (_dot(ah, bl) + _dot(al, bh))


def _store_token_major(ref, x):
    rows, d = x.shape
    nch = d // LANES
    for c in range(nch):
        ref[pl.ds(c, rows, stride=nch), :] = x[:, c * LANES:(c + 1) * LANES]


def _load_token_major(ref, rows, nch):
    return jnp.concatenate([ref[pl.ds(c, rows, stride=nch), :] for c in range(nch)], axis=1)


def _two_source_specs(rows_a, rows_b, tm, width):
    na, nb = rows_a // tm, rows_b // tm
    spec_a = pl.BlockSpec((tm, width), lambda i: (jnp.minimum(i, na - 1), 0))
    spec_b = pl.BlockSpec((tm, width), lambda i: (jnp.maximum(i - na, 0), 0))
    return na, nb, spec_a, spec_b


def _rmsnorm_kernel(xa_ref, xb_ref, w_ref, o_ref, *, na):
    i = pl.program_id(0)

    def body(x_ref):
        x = x_ref[...]
        ms = jnp.mean(x * x, axis=-1, keepdims=True)
        o_ref[...] = (x * lax.rsqrt(ms + EPS) * w_ref[...]).astype(o_ref.dtype)

    @pl.when(i < na)
    def _():
        body(xa_ref)

    @pl.when(i >= na)
    def _():
        body(xb_ref)


def _rmsnorm_bf16(xa, xb, w):
    d = xa.shape[1]
    tm = _pick(math.gcd(xa.shape[0], xb.shape[0]), 512)
    na, nb, sa, sb = _two_source_specs(xa.shape[0], xb.shape[0], tm, d)
    return pl.pallas_call(
        functools.partial(_rmsnorm_kernel, na=na),
        grid=(na + nb,),
        in_specs=[sa, sb, pl.BlockSpec((1, d), lambda i: (0, 0))],
        out_specs=pl.BlockSpec((tm, d), lambda i: (i, 0)),
        out_shape=jax.ShapeDtypeStruct((xa.shape[0] + xb.shape[0], d), BF),
        compiler_params=_cp(("arbitrary",)),
        name="rmsnorm_mix",
    )(xa, xb, w.reshape(1, d))


def _inproj_kernel(x_ref, w_ref, lb_ref, p_ref, kf_ref, *, jq, jf0, jf1, q_scale):
    j = pl.program_id(0)
    acc = _dot(x_ref[...], w_ref[...])
    is_forget = (j >= jf0) & (j < jf1)

    @pl.when(is_forget)
    def _():
        logf, kf = _forget_epilogue(acc, lb_ref[...])
        p_ref[...] = logf
        kf_ref[...] = kf

    @pl.when(jnp.logical_not(is_forget))
    def _():
        p_ref[...] = acc * jnp.where(j < jq, q_scale, 1.0)


def _inproj(x_bf, w, lb_logits, ncols, kdim, q_scale):
    t, k = x_bf.shape
    tn = _pick(math.gcd(ncols, kdim), 1024, LANES)
    tm = _pick(t, 1024)
    ni = t // tm
    jq, jf0, jf1 = kdim // tn, kdim // tn, 2 * kdim // tn
    nf = jf1 - jf0
    fcol = lambda j: jnp.clip(j - jf0, 0, nf - 1)
    frow = lambda j, i: jnp.where(j < jf0, 0, jnp.where(j >= jf1, ni - 1, i))
    kern = functools.partial(_inproj_kernel, jq=jq, jf0=jf0, jf1=jf1, q_scale=q_scale)
    return pl.pallas_call(
        kern,
        grid=(ncols // tn, ni),
        in_specs=[pl.BlockSpec((tm, k), lambda j, i: (i, 0)),
                  pl.BlockSpec((k, tn), lambda j, i: (0, j)),
                  pl.BlockSpec((lb_logits.shape[0], tn), lambda j, i: (0, fcol(j)))],
        out_specs=[pl.BlockSpec((tm, tn), lambda j, i: (i, j)),
                   pl.BlockSpec((tm, tn), lambda j, i: (frow(j, i), fcol(j)))],
        out_shape=[jax.ShapeDtypeStruct((t, ncols), F32), jax.ShapeDtypeStruct((t, kdim), F32)],
        compiler_params=_cp(("arbitrary", "arbitrary")),
        name="in_proj",
    )(x_bf, w, lb_logits)


def _forget_epilogue(acc, lb_logits):
    m = jnp.max(lb_logits, axis=0, keepdims=True)
    e = jnp.exp(lb_logits - m)
    lb = e[0:1, :] / jnp.sum(e, axis=0, keepdims=True)
    f = lb + (1.0 - lb) * _sigmoid(acc)
    return jnp.log(f), 1.0 - f


def _beta_decay_kernel(x_ref, w_ref, wt_ref, prow_ref, pcol_ref, o_ref, ot_ref, *, vh):
    x = x_ref[...]
    acc = _dot(x, w_ref[...])
    acct = _dot_nt(wt_ref[...], x)

    def act(a, is_beta, a_neg_exp, dt_bias):
        z = a + dt_bias
        softplus = jnp.maximum(z, 0.0) + jnp.log(1.0 + jnp.exp(-jnp.abs(z)))
        return jnp.where(is_beta, _sigmoid(a), a_neg_exp * softplus)

    prow = prow_ref[...]
    pcol = pcol_ref[...]
    o_ref[...] = act(acc, _iota(acc.shape, 1) < vh, prow[0:1, :], prow[1:2, :])
    ot_ref[...] = act(acct, _iota(acct.shape, 0) < vh, pcol[:, 0:1], pcol[:, 1:2])


def _beta_decay(x_bf, w_ba, a_log, dt_bias):
    t, k = x_bf.shape
    vh = a_log.shape[0]
    tm = t if t <= 2048 else _pick(t, 1024, LANES)
    zeros = jnp.zeros((vh,), F32)
    prow = jnp.stack([jnp.concatenate([zeros, -jnp.exp(a_log)]), jnp.concatenate([zeros, dt_bias])])
    return pl.pallas_call(
        functools.partial(_beta_decay_kernel, vh=vh),
        grid=(t // tm,),
        in_specs=[pl.BlockSpec((tm, k), lambda i: (i, 0)),
                  pl.BlockSpec((k, 2 * vh), lambda i: (0, 0)),
                  pl.BlockSpec((2 * vh, k), lambda i: (0, 0)),
                  pl.BlockSpec((2, 2 * vh), lambda i: (0, 0)),
                  pl.BlockSpec((2 * vh, 2), lambda i: (0, 0))],
        out_specs=[pl.BlockSpec((tm, 2 * vh), lambda i: (i, 0)),
                   pl.BlockSpec((2 * vh, tm), lambda i: (0, i))],
        out_shape=[jax.ShapeDtypeStruct((t, 2 * vh), F32), jax.ShapeDtypeStruct((2 * vh, t), F32)],
        compiler_params=_cp(("arbitrary",)),
        name="proj_beta_decay",
    )(x_bf, w_ba.astype(BF), w_ba.T.astype(BF), prow, prow.T)


def _hgrn2_kernel(q_ref, lf_ref, k_ref, v_ref, g_ref, on_ref, s0_ref, o_ref, so_ref, s_scr, gall_scr,
                  g_scr, k_scr, v_scr, *, heads, dk, dv, sub, nsub, nseq, nchunks, group, unroll):
    c = pl.program_id(1)
    cg = sub * nsub

    @pl.when(c == 0)
    def _():
        s_scr[...] = s0_ref[...]

    rows_all = nseq * cg
    shift = sub.bit_length() - 1
    r_i = _iota((rows_all, 3 * rows_all), 0)
    c_i = _iota((rows_all, 3 * rows_all), 1) & (rows_all - 1)
    same = lax.shift_right_logical(r_i, shift) == lax.shift_right_logical(c_i, shift)
    btril3 = jnp.where((r_i >= c_i) & same, 1.0, 0.0).astype(BF)
    gall_scr[...] = _dot(btril3, _split3(lf_ref[...])) * LOG2E
    onorm = on_ref[...]
    n_aug = 16 - sub % 16
    aug_rhs = jnp.concatenate([jnp.zeros((n_aug, dv), BF), jnp.ones((n_aug, dv), BF)], axis=1)
    zeros_v = jnp.zeros((sub, dv), BF)
    zeros_aug = jnp.zeros((n_aug - 3, dk), F32)

    nh = sub // 8
    row8 = _iota((8, 1), 0)

    def block(it, carry):
        for u in range(unroll):
            sub_chunk(it * unroll + u, g_scr.at[u], k_scr.at[u], v_scr.at[u])
        return carry

    def sub_chunk(sb, g_scr, k_scr, v_scr):
        for q in range(nseq):
            rows = pl.ds(pl.multiple_of(q * cg + sb * sub, sub), sub)
            g_scr[q] = gall_scr[rows, :]
            k_scr[q] = k_ref[rows, :]
            v_scr[q] = v_ref[rows, :]
        problems = [(q, h) for q in range(nseq) for h in range(heads)]
        for g0 in range(0, len(problems), group):
            ps = []
            for (q, h) in problems[g0:g0 + group]:
                r0 = pl.multiple_of(q * cg + sb * sub, sub)
                rows = pl.ds(r0, sub)
                kc = pl.ds(h * dk, dk)
                vc = pl.ds(h * dv, dv)
                ps.append(dict(q=q, h=h, r0=r0, rows=rows, kc=kc, vc=vc,
                               gc=g_scr[q, :, kc],
                               qv=q_ref[rows, kc], k=k_ref[rows, kc], v=v_ref[rows, vc]))
            for p in ps:
                p["s_old"] = s_scr[p["q"], p["h"]]
                p["o_state"] = _dot((p["qv"] * jnp.exp2(p["gc"])).astype(BF), p["s_old"].astype(BF))
                p["o8"] = [jnp.zeros((8, dv), F32) for _ in range(nh)]
                p["q8"] = [p["qv"][8 * i:8 * i + 8] for i in range(nh)]
                p["g8"] = [p["gc"][8 * i:8 * i + 8] for i in range(nh)]
            for j in range(sub):
                for p in ps:
                    kj = k_scr[p["q"], pl.ds(j, 1), p["kc"]]
                    vj = v_scr[p["q"], pl.ds(j, 1), p["vc"]]
                    gj = g_scr[p["q"], pl.ds(j, 1), p["kc"]]
                    for i in range(j // 8, nh):
                        pr = (p["q8"][i] * kj) * jnp.exp2(p["g8"][i] - gj)
                        a_col = jnp.sum(pr, axis=-1, keepdims=True)
                        if i == j // 8:
                            a_col = jnp.where(row8 >= j % 8, a_col, 0.0)
                        p["o8"][i] = p["o8"][i] + a_col * vj
            for p in ps:
                p["o"] = p["o_state"] + (jnp.concatenate(p["o8"], axis=0) if nh > 1 else p["o8"][0])
            for p in ps:
                gend = p["gc"][sub - 1:sub, :]
                dend = jnp.exp2(gend)
                d_hi = dend.astype(BF).astype(F32)
                d_mid = (dend - d_hi).astype(BF).astype(F32)
                d_lo = (dend - d_hi - d_mid).astype(BF).astype(F32)
                kdec = p["k"] * jnp.exp2(gend - p["gc"])
                lhs = jnp.concatenate([kdec, d_hi, d_mid, d_lo, zeros_aug], axis=0).astype(BF)
                rhs = jnp.concatenate([jnp.concatenate([p["v"].astype(BF), zeros_v], axis=1), aug_rhs], axis=0)
                p["kv"] = _dot_tn(lhs, rhs)
            for p in ps:
                s_scr[p["q"], p["h"]] = p["kv"][:, dv:] * p["s_old"] + p["kv"][:, :dv]
            for p in ps:
                o = p["o"]
                ms = jnp.mean(o * o, axis=-1, keepdims=True)
                o_ref[p["rows"], p["vc"]] = (o * lax.rsqrt(ms + EPS) * onorm) * _silu(g_ref[p["rows"], p["vc"]])

    lax.fori_loop(0, nsub // unroll, block, 0)

    @pl.when(c == nchunks - 1)
    def _():
        so_ref[...] = s_scr[...]


def _hgrn2(q, lf, kf, v, gate, onorm, s0, *, row0, seq_len, rows_per_step, sub, seqs_per_step):
    nseq_total, heads, dk, dv = s0.shape
    assert dk == dv, "state decay is applied with a (dk, dv) tile"
    width = heads * dk
    cg = rows_per_step
    nseq = seqs_per_step
    nchunks = seq_len // cg
    assert nseq == 1 or nchunks == 1, "several sequences per step only when a step covers whole sequences"
    rows = nseq * cg
    b0 = row0 // rows
    assert sub & (sub - 1) == 0 and sub % 8 == 0 and rows & (rows - 1) == 0
    unroll = next(u for u in (8, 4, 2, 1) if (cg // sub) % u == 0)
    def row_spec(col_block):
        return pl.BlockSpec((rows, width), lambda b, c: (b0 + b * nchunks + c, col_block))

    st_spec = pl.BlockSpec((nseq, heads, dk, dv), lambda b, c: (b, 0, 0, 0))
    kern = functools.partial(_hgrn2_kernel, heads=heads, dk=dk, dv=dv, sub=sub, nsub=cg // sub, nseq=nseq,
                             nchunks=nchunks, group=8, unroll=unroll)
    operands = (q, lf, kf, v, gate)
    return pl.pallas_call(
        kern,
        grid=(nseq_total // nseq, nchunks),
        in_specs=[row_spec(cb) for _, cb in operands]
                 + [pl.BlockSpec((1, dv), lambda b, c: (0, 0)), st_spec],
        out_specs=[pl.BlockSpec((rows, heads * dv), lambda b, c: (b * nchunks + c, 0)), st_spec],
        out_shape=[jax.ShapeDtypeStruct((nseq_total * seq_len, heads * dv), F32),
                   jax.ShapeDtypeStruct(s0.shape, F32)],
        scratch_shapes=[pltpu.VMEM((nseq, heads, dk, dv), F32), pltpu.VMEM((rows, width), F32),
                        pltpu.VMEM((unroll, nseq, sub, width), F32), pltpu.VMEM((unroll, nseq, sub, width), F32),
                        pltpu.VMEM((unroll, nseq, sub, heads * dv), F32)],
        compiler_params=_cp(("arbitrary", "arbitrary")),
        name="hgrn2_recurrence",
    )(*[a for a, _ in operands], onorm.reshape(1, dv), s0)


def _gdn_kernel(qkv_ref, z_ref, bg_ref, bgt_ref, cw_ref, on_ref, s0_ref, c0_ref,
                o_ref, so_ref, co_ref, s_scr, xx, xs,
                *, kheads, vheads, dk, dv, chunk, nsub, nseq, nchunks, group_kh):
    c = pl.program_id(1)
    cg = chunk * nsub
    kdim = kheads * dk
    rep = vheads // kheads
    tail0 = 8 - (CONV_K - 1)

    @pl.when(c == 0)
    def _():
        s_scr[...] = s0_ref[...]
        xx[:, 0:8, :] = c0_ref[...]

    for q in range(nseq):
        xx[q, 8:8 + cg, :] = qkv_ref[q * cg:(q + 1) * cg, :]
        conv = xx[q, pl.ds(tail0, cg), :] * cw_ref[0:1, :]
        for j in range(1, CONV_K):
            conv = conv + xx[q, pl.ds(tail0 + j, cg), :] * cw_ref[j:j + 1, :]
        xs[q] = _silu(conv)

    @pl.when(c == nchunks - 1)
    def _():
        for q in range(nseq):
            co_ref[q] = xx[q, pl.ds(8 + cg - (CONV_K - 1), CONV_K - 1), :]

    for q in range(nseq):
        xx[q, 0:8, :] = xx[q, cg:cg + 8, :]

    wp = chunk
    r_i = _iota((chunk, wp), 0)
    c_i = _iota((chunk, wp), 1)
    causal = r_i >= c_i
    strict = r_i > c_i

    def pad_rows(a):
        if a.shape[0] == wp:
            return a
        if a.dtype == BF and a.shape[0] % 16:
            return pad_rows(a.astype(F32)).astype(BF)
        return jnp.concatenate([a, jnp.zeros((wp - a.shape[0],) + a.shape[1:], a.dtype)], axis=0)

    def dot3_padded(a, b):
        a_hi = a.astype(BF).astype(F32)
        lhs = jnp.concatenate([a_hi, a_hi, a - a_hi], axis=1).astype(BF)
        bh, bl = (pad_rows(t) for t in _split(b))
        return _dot(lhs, jnp.concatenate([bh, bl, bh], axis=0))
    cm = chunk - 1
    tril3 = jnp.where(_iota((chunk, 3 * chunk), 0) >= (_iota((chunk, 3 * chunk), 1) & cm), 1.0, 0.0).astype(BF)
    t_r = _iota((3 * chunk, wp), 0) & cm
    t_c = _iota((3 * chunk, wp), 1)
    triu3 = jnp.where((t_r <= t_c) & (t_c < chunk), 1.0, 0.0).astype(BF)
    onorm = on_ref[...]

    cums = {}

    def cumulative_decay(q, s):
        if (q, s) not in cums:
            r0 = q * cg + s * chunk
            bg = bg_ref[r0:r0 + chunk, :]
            gt = bgt_ref[0, vheads:2 * vheads, r0:r0 + chunk]
            cums[(q, s)] = (bg, _dot(tril3, _split3(bg[:, vheads:2 * vheads], axis=0)),
                            _dot(_split3(gt, axis=1), triu3))
        return cums[(q, s)]

    def run_group(items):
        probs = []
        for (q, s, kh) in items:
            r0 = q * cg + s * chunk
            x0 = s * chunk
            bg, gcol_all, grow_all = cumulative_decay(q, s)
            for kh in (kh,):
                qh = xs[q, x0:x0 + chunk, kh * dk:(kh + 1) * dk]
                kk_ = xs[q, x0:x0 + chunk, kdim + kh * dk:kdim + (kh + 1) * dk]
                qn = qh * lax.rsqrt(jnp.sum(qh * qh, axis=-1, keepdims=True) + EPS) * (dk ** -0.5)
                kn = kk_ * lax.rsqrt(jnp.sum(kk_ * kk_, axis=-1, keepdims=True) + EPS)
                kn_bf = kn.astype(BF)
                kn_pad = pad_rows(kn_bf)
                kk = _dot_nt(kn_bf, kn_pad)
                qk = _dot_nt(qn.astype(BF), kn_pad)
                for r in range(rep):
                    h = kh * rep + r
                    vh_ = xs[q, x0:x0 + chunk, 2 * kdim + h * dv:2 * kdim + (h + 1) * dv]
                    gcol = gcol_all[:, h:h + 1]
                    grow = grow_all[h:h + 1, :]
                    beta = bg[:, h:h + 1]
                    decay = jnp.where(causal, jnp.exp(jnp.minimum(gcol - grow, 0.0)), 0.0)
                    eg = jnp.exp(gcol)
                    gend = gcol[chunk - 1:chunk, :]
                    probs.append(dict(
                        q=q, s=s, h=h, r0=r0,
                        x=jnp.where(strict, -(beta * kk * decay), 0.0),
                        y=jnp.concatenate([vh_ * beta, kn * (beta * eg)], axis=1),
                        a_bf=(qk * decay).astype(BF),
                        qg=qn * eg,
                        kdec_bf=(kn * jnp.exp(gend - gcol)).astype(BF),
                        send=jnp.exp(gend)))

        if chunk <= 16:
            for j in range(chunk - 1):
                for p in probs:
                    p["y"] = p["y"] + p["x"][:, j:j + 1] * p["y"][j:j + 1, :]
        else:
            n_fac = int(math.log2(chunk))
            for k in range(n_fac):
                for p in probs:
                    p["y"] = p["y"] + dot3_padded(p["x"], p["y"])
                if k + 1 < n_fac:
                    for p in probs:
                        p["x"] = dot3_padded(p["x"], p["x"])

        for s in sorted({p["s"] for p in probs}):
            cur = [p for p in probs if p["s"] == s]
            for p in cur:
                p["s_old"] = s_scr[p["q"], p["h"]]
                lhs = jnp.concatenate([p["y"][:, dv:], p["qg"]], axis=0).astype(BF)
                p["ws"] = _dot(lhs, p["s_old"].astype(BF))
            for p in cur:
                p["u_bf"] = (p["y"][:, :dv] - p["ws"][:chunk]).astype(BF)
            for p in cur:
                s_scr[p["q"], p["h"]] = p["send"] * p["s_old"] + _dot_tn(p["kdec_bf"], p["u_bf"])
            for p in cur:
                o = p["ws"][chunk:] + _dot(p["a_bf"], pad_rows(p["u_bf"]))
                ms = jnp.mean(o * o, axis=-1, keepdims=True)
                zc = slice(p["h"] * dv, (p["h"] + 1) * dv)
                rows = slice(p["r0"], p["r0"] + chunk)
                o_ref[rows, zc] = (o * lax.rsqrt(ms + EPS) * onorm) * _silu(z_ref[rows, zc])

    for g0 in range(0, kheads, group_kh):
        run_group([(q, s, kh) for q in range(nseq) for s in range(nsub)
                   for kh in range(g0, min(g0 + group_kh, kheads))])

    @pl.when(c == nchunks - 1)
    def _():
        so_ref[...] = s_scr[...]


def _gdn(qkv, z, bg, bgt3, conv_w, onorm, s0, conv0, *, row0, seq_len, rows_per_step, chunk, seqs_per_step):
    nseq_total, vheads, dk, dv = s0.shape
    (qkv, qkv_cb), (z, z_cb) = qkv, z
    conv_dim = conv_w.shape[1]
    kheads = (conv_dim - vheads * dv) // (2 * dk)
    cg = rows_per_step
    nseq = seqs_per_step
    nchunks = seq_len // cg
    assert nseq == 1 or nchunks == 1, "several sequences per step only when a step covers whole sequences"
    rows = nseq * cg
    b0 = row0 // rows
    conv0p = jnp.pad(conv0, ((0, 0), (8 - (CONV_K - 1), 0), (0, 0)))
    rmap = lambda b, c: (b0 + b * nchunks + c, 0)
    st_spec = pl.BlockSpec((nseq, vheads, dk, dv), lambda b, c: (b, 0, 0, 0))
    kern = functools.partial(_gdn_kernel, kheads=kheads, vheads=vheads, dk=dk, dv=dv, chunk=chunk,
                             nsub=cg // chunk, nseq=nseq, nchunks=nchunks, group_kh=kheads)
    return pl.pallas_call(
        kern,
        grid=(nseq_total // nseq, nchunks),
        in_specs=[pl.BlockSpec((rows, conv_dim), lambda b, c: (b0 + b * nchunks + c, qkv_cb)),
                  pl.BlockSpec((rows, vheads * dv), lambda b, c: (b0 + b * nchunks + c, z_cb)),
                  pl.BlockSpec((rows, 2 * vheads), rmap),
                  pl.BlockSpec((1, 2 * vheads, rows), lambda b, c: (b * nchunks + c, 0, 0)),
                  pl.BlockSpec((CONV_K, conv_dim), lambda b, c: (0, 0)),
                  pl.BlockSpec((1, dv), lambda b, c: (0, 0)),
                  st_spec,
                  pl.BlockSpec((nseq, 8, conv_dim), lambda b, c: (b, 0, 0))],
        out_specs=[pl.BlockSpec((rows, vheads * dv), lambda b, c: (b * nchunks + c, 0)),
                   st_spec,
                   pl.BlockSpec((nseq, CONV_K - 1, conv_dim), lambda b, c: (b, 0, 0))],
        out_shape=[jax.ShapeDtypeStruct((nseq_total * seq_len, vheads * dv), F32),
                   jax.ShapeDtypeStruct(s0.shape, F32),
                   jax.ShapeDtypeStruct((nseq_total, CONV_K - 1, conv_dim), F32)],
        scratch_shapes=[pltpu.VMEM((nseq, vheads, dk, dv), F32), pltpu.VMEM((nseq, 8 + cg, conv_dim), F32),
                        pltpu.VMEM((nseq, cg, conv_dim), F32)],
        compiler_params=_cp(("arbitrary", "arbitrary")),
        name="gdn_recurrence",
    )(qkv, z, bg, bgt3, conv_w, onorm.reshape(1, dv), s0, conv0p)


def _merge_kernel(xn_ref, oa1_ref, oa2_ref, ob1_ref, ob2_ref, wga_ref, wgb_ref, wa_ref, wb_ref, o_ref,
                  oa_bf, ob_bf, *, na):
    i = pl.program_id(0)
    j = pl.program_id(1)

    @pl.when((j == 0) & (i < na))
    def _():
        oa_bf[...] = oa1_ref[...].astype(BF)
        ob_bf[...] = ob1_ref[...].astype(BF)

    @pl.when((j == 0) & (i >= na))
    def _():
        oa_bf[...] = oa2_ref[...].astype(BF)
        ob_bf[...] = ob2_ref[...].astype(BF)

    xn = xn_ref[...]
    ya = _dot(oa_bf[...], wa_ref[...])
    yb = _dot(ob_bf[...], wb_ref[...])
    ga = _dot(xn, wga_ref[...])
    gb = _dot(xn, wgb_ref[...])
    o_ref[...] = (_sigmoid(ga) * ya + _sigmoid(gb) * yb).astype(o_ref.dtype)


def _merge(xn, oa_p, oa_s, ob_p, ob_s, w_gates_bf, w_pa, w_pb):
    rp, rs = oa_p.shape[0], oa_s.shape[0]
    ka, kb = oa_p.shape[1], ob_p.shape[1]
    d = w_pa.shape[1]
    k = xn.shape[1]
    tm = _pick(math.gcd(rp, rs), 512)
    tn = _pick(d, 512, LANES)
    na, nb = rp // tm, rs // tm
    nj = d // tn
    amap = lambda i, j: (jnp.minimum(i, na - 1), 0)
    bmap = lambda i, j: (jnp.maximum(i - na, 0), 0)
    return pl.pallas_call(
        functools.partial(_merge_kernel, na=na),
        grid=(na + nb, nj),
        in_specs=[pl.BlockSpec((tm, k), lambda i, j: (i, 0)),
                  pl.BlockSpec((tm, ka), amap), pl.BlockSpec((tm, ka), bmap),
                  pl.BlockSpec((tm, kb), amap), pl.BlockSpec((tm, kb), bmap),
                  pl.BlockSpec((k, tn), lambda i, j: (0, j)),
                  pl.BlockSpec((k, tn), lambda i, j: (0, nj + j)),
                  pl.BlockSpec((ka, tn), lambda i, j: (0, j)),
                  pl.BlockSpec((kb, tn), lambda i, j: (0, j))],
        out_specs=pl.BlockSpec((tm, tn), lambda i, j: (i, j)),
        out_shape=jax.ShapeDtypeStruct((rp + rs, d), BF),
        scratch_shapes=[pltpu.VMEM((tm, ka), BF), pltpu.VMEM((tm, kb), BF)],
        compiler_params=_cp(("arbitrary", "arbitrary")),
        name="gates_branch_proj_merge",
    )(xn, oa_p, oa_s, ob_p, ob_s, w_gates_bf, w_gates_bf, w_pa.astype(BF), w_pb.astype(BF))


def _outproj_kernel(m_ref, xa_ref, xb_ref, wo_ref, nw_ref, wr_ref, h_ref, xf_ref, lg_ref, *, na):
    i = pl.program_id(0)

    def body(x_ref):
        h = x_ref[...] + _dot(m_ref[...], wo_ref[...])
        h_ref[...] = h
        ms = jnp.mean(h * h, axis=-1, keepdims=True)
        xf = h * lax.rsqrt(ms + EPS) * nw_ref[...]
        _store_token_major(xf_ref, xf)
        lg_ref[...] = _dot(xf.astype(BF), wr_ref[...])

    @pl.when(i < na)
    def _():
        body(xa_ref)

    @pl.when(i >= na)
    def _():
        body(xb_ref)


def _outproj(merged, xa, xb, w_out, norm_ffn, w_router):
    t, d = merged.shape
    tm = _pick(math.gcd(xa.shape[0], xb.shape[0]), 512)
    na, nb, sa, sb = _two_source_specs(xa.shape[0], xb.shape[0], tm, d)
    nr = w_router.shape[1]
    const = lambda i: (0, 0)
    return pl.pallas_call(
        functools.partial(_outproj_kernel, na=na),
        grid=(na + nb,),
        in_specs=[pl.BlockSpec((tm, d), lambda i: (i, 0)), sa, sb,
                  pl.BlockSpec((d, d), const, pipeline_mode=pl.Buffered(1)),
                  pl.BlockSpec((1, d), const),
                  pl.BlockSpec((d, nr), const, pipeline_mode=pl.Buffered(1))],
        out_specs=[pl.BlockSpec((tm, d), lambda i: (i, 0)),
                   pl.BlockSpec((tm * (d // LANES), LANES), lambda i: (i, 0)),
                   pl.BlockSpec((tm, nr), lambda i: (i, 0))],
        out_shape=[jax.ShapeDtypeStruct((t, d), F32), jax.ShapeDtypeStruct((t * (d // LANES), LANES), F32),
                   jax.ShapeDtypeStruct((t, nr), F32)],
        compiler_params=_cp(("arbitrary",), 56),
        name="out_proj_ffn_norm",
    )(merged, xa, xb, w_out.astype(BF), norm_ffn.reshape(1, d), w_router)


def _route_kernel(lg_ref, b_ref, eid_ref, gate_ref, rank_ref, cnt_ref, carry,
                  *, ngroups, nexp, tm):
    i = pl.program_id(0)

    @pl.when(i == 0)
    def _():
        carry[...] = jnp.zeros_like(carry)

    per_group = nexp // ngroups
    lg = lg_ref[...] + b_ref[...]
    lane = _iota(lg.shape, 1)
    big = jnp.int32(1 << 20)
    is_g = lane < ngroups
    gl = jnp.where(is_g, lg, NEG)
    gmax = jnp.max(gl, axis=-1, keepdims=True)
    gidx = jnp.min(jnp.where(gl == gmax, lane, big), axis=-1, keepdims=True)
    gsum = jnp.sum(jnp.where(is_g, jnp.exp(gl - gmax), 0.0), axis=-1, keepdims=True)
    gw = 1.0 / gsum
    elane = lane - ngroups
    in_grp = (elane >= gidx * per_group) & (elane < (gidx + 1) * per_group)
    el = jnp.where(in_grp, lg, NEG)
    v1 = jnp.max(el, axis=-1, keepdims=True)
    i1 = jnp.min(jnp.where(in_grp & (el == v1), elane, big), axis=-1, keepdims=True)
    in2 = in_grp & (elane != i1)
    el2 = jnp.where(in2, lg, NEG)
    v2 = jnp.max(el2, axis=-1, keepdims=True)
    i2 = jnp.min(jnp.where(in2 & (el2 == v2), elane, big), axis=-1, keepdims=True)
    p2 = jnp.exp(v2 - v1)
    den = 1.0 + p2
    lane2 = _iota((tm, TOP_K), 1)
    eid_ref[...] = jnp.where(lane2 == 0, i1, i2)
    gate_ref[...] = jnp.where(lane2 == 0, gw / den, gw * p2 / den)

    oh1 = (elane == i1).astype(F32)
    oh2 = (elane == i2).astype(F32)
    lower = (_iota((tm, tm), 0) > _iota((tm, tm), 1)).astype(BF)
    cs1 = _dot(lower, oh1.astype(BF))
    cs2 = _dot(lower, oh2.astype(BF))
    tot1 = jnp.sum(oh1, axis=0, keepdims=True)
    tot2 = jnp.sum(oh2, axis=0, keepdims=True)
    base = carry[0:1, :]
    r1 = jnp.sum(oh1 * (base + cs1), axis=-1, keepdims=True)
    r2 = jnp.sum(oh2 * (base + tot1 + cs2), axis=-1, keepdims=True)
    rank_ref[...] = jnp.where(lane2 == 0, r1, r2).astype(I32)
    new = base + tot1 + tot2
    carry[...] = jnp.broadcast_to(new, carry.shape)
    cnt_ref[...] = jnp.broadcast_to(new, cnt_ref.shape)


def _route(logits, bias_row, ngroups, nexp):
    t, nr = logits.shape
    tm = _pick(t, 512)
    kern = functools.partial(_route_kernel, ngroups=ngroups, nexp=nexp, tm=tm)
    return pl.pallas_call(
        kern,
        grid=(t // tm,),
        in_specs=[pl.BlockSpec((tm, nr), lambda i: (i, 0)), pl.BlockSpec((1, nr), lambda i: (0, 0))],
        out_specs=[pl.BlockSpec((tm, TOP_K), lambda i: (i, 0)),
                   pl.BlockSpec((tm, TOP_K), lambda i: (i, 0)),
                   pl.BlockSpec((tm, TOP_K), lambda i: (i, 0)),
                   pl.BlockSpec((8, nr), lambda i: (0, 0))],
        out_shape=[jax.ShapeDtypeStruct((t, TOP_K), I32), jax.ShapeDtypeStruct((t, TOP_K), F32),
                   jax.ShapeDtypeStruct((t, TOP_K), I32), jax.ShapeDtypeStruct((8, nr), F32)],
        scratch_shapes=[pltpu.VMEM((8, nr), F32)],
        compiler_params=_cp(("arbitrary",)),
        name="route_topk_rank",
    )(logits, bias_row)


def _plan_kernel(cnt_ref, eid_ref, rank_ref, pos_ref, be_ref, *, ngroups, nexp, nblocks_pad, tm):
    nr = cnt_ref.shape[1]
    cnt = cnt_ref[0:1, :]
    padded = jnp.floor((cnt + (MOE_BLOCK - 1)) * (1.0 / MOE_BLOCK)) * MOE_BLOCK
    r_i = _iota((nr, nr), 0)
    c_i = _iota((nr, nr), 1)
    padded_col = jnp.sum(jnp.where(r_i == c_i, jnp.broadcast_to(padded, (nr, nr)), 0.0), axis=1, keepdims=True)
    start = jnp.sum(jnp.where(r_i < c_i, jnp.broadcast_to(padded_col, (nr, nr)), 0.0), axis=0, keepdims=True)
    end = start + padded
    lane = _iota((tm, nr), 1)
    eid = eid_ref[...]
    rank = rank_ref[...]
    lane2 = _iota((tm, TOP_K), 1)
    pos = jnp.zeros((tm, TOP_K), F32)
    for k in range(TOP_K):
        oh = (lane - ngroups) == eid[:, k:k + 1]
        st = jnp.sum(jnp.where(oh, start, 0.0), axis=-1, keepdims=True)
        pos = jnp.where(lane2 == k, st, pos)
    pos_ref[...] = pos.astype(I32) + rank

    @pl.when(pl.program_id(0) == 0)
    def _():
        blk_row = _iota((nblocks_pad, nr), 0).astype(F32) * MOE_BLOCK
        lane_b = _iota((nblocks_pad, nr), 1)
        is_e = (lane_b >= ngroups) & (lane_b < ngroups + nexp)
        n_le = jnp.sum(jnp.where(is_e & (end <= blk_row), 1.0, 0.0), axis=-1, keepdims=True)
        e_lane = (lane_b - ngroups).astype(F32)
        nonempty = is_e & (padded > 0.0)
        e_last = jnp.max(jnp.where(nonempty, e_lane, -1.0), axis=-1, keepdims=True)
        be = jnp.minimum(n_le, e_last)
        nxt = jnp.min(jnp.where(nonempty & (e_lane > be), e_lane, 1e9), axis=-1, keepdims=True)
        nxt = jnp.where(nxt > 1e8, -1.0, nxt)
        total = jnp.sum(padded, axis=-1, keepdims=True)
        valid = jnp.where(blk_row[:, 0:1] < total, 1.0, 0.0)
        col = _iota((nblocks_pad, 4), 1)
        meta = jnp.where(col == 0, be, jnp.where(col == 1, nxt, jnp.where(col == 2, valid, 0.0)))
        be_ref[...] = meta.astype(I32)


def _plan(cnt, eid, rank, ngroups, nexp, nblocks):
    t = eid.shape[0]
    nr = cnt.shape[1]
    tm = _pick(t, 512)
    nblocks_pad = -(-nblocks // 8) * 8
    kern = functools.partial(_plan_kernel, ngroups=ngroups, nexp=nexp, nblocks_pad=nblocks_pad, tm=tm)
    return pl.pallas_call(
        kern,
        grid=(t // tm,),
        in_specs=[pl.BlockSpec((8, nr), lambda i: (0, 0)),
                  pl.BlockSpec((tm, TOP_K), lambda i: (i, 0)),
                  pl.BlockSpec((tm, TOP_K), lambda i: (i, 0))],
        out_specs=[pl.BlockSpec((tm, TOP_K), lambda i: (i, 0)),
                   pl.BlockSpec((nblocks_pad, 4), lambda i: (0, 0))],
        out_shape=[jax.ShapeDtypeStruct((t, TOP_K), I32), jax.ShapeDtypeStruct((nblocks_pad, 4), I32)],
        compiler_params=_cp(("arbitrary",)),
        name="route_plan",
    )(cnt, eid, rank)


def _invert_kernel(pos_ref, tok_ref, *, n_assign, n_rows):
    def clear(r, c):
        tok_ref[r] = 0
        return c

    lax.fori_loop(0, n_rows, clear, 0, unroll=16)

    def put(a, c):
        tok_ref[pos_ref[a]] = lax.shift_right_logical(a, TOP_K.bit_length() - 1)
        return c

    lax.fori_loop(0, n_assign, put, 0, unroll=16)


def _invert(pos_flat, n_rows):
    n_assign = pos_flat.shape[0]
    return pl.pallas_call(
        functools.partial(_invert_kernel, n_assign=n_assign, n_rows=n_rows),
        in_specs=[pl.BlockSpec(memory_space=pltpu.SMEM)],
        out_specs=pl.BlockSpec(memory_space=pltpu.SMEM),
        out_shape=jax.ShapeDtypeStruct((n_rows,), I32),
        name="route_invert",
    )(pos_flat)


def _expert_kernel(be_ref, nxt_ref, valid_ref, tok_ref, x_hbm, wg_hbm, wu_hbm, wd_hbm, y_ref,
                   xbuf0, xbuf1, xsem, wg_st, wu_st, wd_st, wsem, wg_bf, wu_bf, wd_bf, *, nch):
    i = pl.program_id(0)
    n = pl.num_programs(0)
    xbufs = (xbuf0, xbuf1)

    def row_copy(blk, s, r):
        t = tok_ref[blk * MOE_BLOCK + r]
        src = x_hbm.at[pl.ds(pl.multiple_of(t * nch, nch), nch), :]
        return pltpu.make_async_copy(src, xbufs[s].at[pl.ds(r * nch, nch), :], xsem.at[s])

    def block_wait(s):
        pltpu.make_async_copy(x_hbm.at[pl.ds(0, MOE_BLOCK * nch), :], xbufs[s], xsem.at[s]).wait()

    def weight_copies(e):
        return (pltpu.make_async_copy(wg_hbm.at[e], wg_st, wsem.at[0]),
                pltpu.make_async_copy(wu_hbm.at[e], wu_st, wsem.at[1]),
                pltpu.make_async_copy(wd_hbm.at[e], wd_st, wsem.at[2]))

    @pl.when(i == 0)
    def _():
        for r in range(MOE_BLOCK):
            row_copy(0, 0, r).start()
        for cp in weight_copies(be_ref[0]):
            cp.start(priority=1)

    prev = be_ref[jnp.maximum(i - 1, 0)]

    @pl.when((i == 0) | (be_ref[i] != prev))
    def _():
        for cp in weight_copies(be_ref[i]):
            cp.wait()
        wg_bf[...] = wg_st[...].astype(BF)
        wu_bf[...] = wu_st[...].astype(BF)
        wd_bf[...] = wd_st[...].astype(BF)

        @pl.when(nxt_ref[i] >= 0)
        def _():
            for cp in weight_copies(nxt_ref[i]):
                cp.start(priority=1)

    ahead = jnp.minimum(i + 1, n - 1)
    used = valid_ref[i] != 0
    used_ahead = (i + 1 < n) & (valid_ref[ahead] != 0)

    def step(cur, oth):
        @pl.when(used_ahead)
        def _():
            for r in range(MOE_BLOCK):
                row_copy(ahead, oth, r).start()

        block_wait(cur)
        x = _load_token_major(xbufs[cur], MOE_BLOCK, nch).astype(BF)
        hid = (_silu(_dot(x, wg_bf[...])) * _dot(x, wu_bf[...])).astype(BF)
        _store_token_major(y_ref, _dot(hid, wd_bf[...]))

    @pl.when(used & (i % 2 == 0))
    def _():
        step(0, 1)

    @pl.when(used & (i % 2 == 1))
    def _():
        step(1, 0)

    @pl.when(jnp.logical_not(used))
    def _():
        y_ref[...] = jnp.zeros_like(y_ref)


def _experts(block_e, block_next, block_valid, row_tok, xf, w_eg, w_eu, w_ed):
    nrows = row_tok.shape[0]
    d = w_eg.shape[1]
    nch = d // LANES
    nblocks = nrows // MOE_BLOCK
    de = w_eg.shape[2]
    any_spec = pl.BlockSpec(memory_space=pl.ANY)
    grid_spec = pltpu.PrefetchScalarGridSpec(
        num_scalar_prefetch=4,
        grid=(nblocks,),
        in_specs=[any_spec, any_spec, any_spec, any_spec],
        out_specs=pl.BlockSpec((MOE_BLOCK * nch, LANES), lambda i, *_: (i, 0)),
        scratch_shapes=[pltpu.VMEM((MOE_BLOCK * nch, LANES), F32), pltpu.VMEM((MOE_BLOCK * nch, LANES), F32),
                        pltpu.SemaphoreType.DMA((2,)),
                        pltpu.VMEM((d, de), F32), pltpu.VMEM((d, de), F32), pltpu.VMEM((de, d), F32),
                        pltpu.SemaphoreType.DMA((3,)),
                        pltpu.VMEM((d, de), BF), pltpu.VMEM((d, de), BF), pltpu.VMEM((de, d), BF)],
    )
    return pl.pallas_call(
        functools.partial(_expert_kernel, nch=nch),
        grid_spec=grid_spec,
        out_shape=jax.ShapeDtypeStruct((nrows * nch, LANES), F32),
        compiler_params=_cp(("arbitrary",), 48),
        name="expert_blocks",
    )(block_e, block_next, block_valid, row_tok, xf, w_eg, w_eu, w_ed)


def _final_kernel(pos_ref, h_ref, gate_ref, w_ref, yb_hbm, ya_ref, ys_ref, gbuf, sem, *, na, tm, nch):
    i = pl.program_id(0)
    n = pl.num_programs(0)
    slot = i % 2

    def row_copy(tile, slot_, r, k):
        p = pos_ref[(tile * tm + r) * TOP_K + k]
        src = yb_hbm.at[pl.ds(pl.multiple_of(p * nch, nch), nch), :]
        return pltpu.make_async_copy(src, gbuf.at[slot_, k, pl.ds(r * nch, nch), :], sem.at[slot_])

    def start_tile(tile, slot_):
        for r in range(tm):
            for k in range(TOP_K):
                row_copy(tile, slot_, r, k).start()

    @pl.when(i == 0)
    def _():
        start_tile(0, 0)

    @pl.when(i + 1 < n)
    def _():
        start_tile(i + 1, 1 - slot)

    for r in range(tm):
        for k in range(TOP_K):
            row_copy(i, slot, r, k).wait()

    gate = gate_ref[...]
    y = h_ref[...]
    ffn = _load_token_major(gbuf.at[slot, 0], tm, nch) * gate[:, 0:1]
    for k in range(1, TOP_K):
        ffn = ffn + _load_token_major(gbuf.at[slot, k], tm, nch) * gate[:, k:k + 1]
    y = y + ffn
    ms = jnp.mean(y * y, axis=-1, keepdims=True)
    out = y * lax.rsqrt(ms + EPS) * w_ref[...]

    @pl.when(i < na)
    def _():
        ya_ref[...] = out

    @pl.when(i >= na)
    def _():
        ys_ref[...] = out


def _final(pos_flat, h, gate, yb, w, rows_a, rows_b):
    t, d = h.shape
    tm = _pick(math.gcd(rows_a, rows_b), 128)
    na, nb = rows_a // tm, rows_b // tm
    grid_spec = pltpu.PrefetchScalarGridSpec(
        num_scalar_prefetch=1,
        grid=(na + nb,),
        in_specs=[pl.BlockSpec((tm, d), lambda i, pos: (i, 0)),
                  pl.BlockSpec((tm, TOP_K), lambda i, pos: (i, 0)),
                  pl.BlockSpec((1, d), lambda i, pos: (0, 0)),
                  pl.BlockSpec(memory_space=pl.ANY)],
        out_specs=[pl.BlockSpec((tm, d), lambda i, pos: (jnp.minimum(i, na - 1), 0)),
                   pl.BlockSpec((tm, d), lambda i, pos: (jnp.maximum(i - na, 0), 0))],
        scratch_shapes=[pltpu.VMEM((2, TOP_K, tm * (d // LANES), LANES), F32), pltpu.SemaphoreType.DMA((2,))],
    )
    return pl.pallas_call(
        functools.partial(_final_kernel, na=na, tm=tm, nch=d // LANES),
        grid_spec=grid_spec,
        out_shape=[jax.ShapeDtypeStruct((rows_a, d), F32), jax.ShapeDtypeStruct((rows_b, d), F32)],
        compiler_params=_cp(("arbitrary",)),
        name="combine_final_norm",
    )(pos_flat, h, gate, w.reshape(1, d), yb)


def kernel(x_prompt, x_sample, state_hgrn, state_gdn, state_conv, norm_mix, w_in, conv_w, ha_lb_logits, ha_onorm, w_pa, gd_A_log, gd_dt_bias, gd_onorm, w_pb, w_out, norm_ffn, w_router_group, b_router_group, w_router_expert, b_router_expert, w_exp_gate, w_exp_up, w_exp_down, norm_final):
    depth = state_hgrn.shape[0]
    assert depth == 1, "one decoder layer"
    bp, lp, d = x_prompt.shape
    bs, ls, _ = x_sample.shape
    _, _, ha_heads, ha_dk, ha_dv = state_hgrn.shape
    _, _, vheads, gd_dk, gd_dv = state_gdn.shape
    conv_dim = state_conv.shape[3]
    ha_kdim, ha_vdim = ha_heads * ha_dk, ha_heads * ha_dv
    gd_vdim = vheads * gd_dv
    ngroups = w_router_group.shape[2]
    nexp = w_router_expert.shape[2]
    rp, rs = bp * lp, bs * ls
    t = rp + rs

    off_f = ha_kdim
    off_i = off_f + ha_kdim
    off_g = off_i + ha_vdim
    off_qkv = off_g + ha_vdim
    off_z = off_qkv + conv_dim
    off_b = off_z + gd_vdim
    off_gate = off_b + 2 * vheads

    xn = _rmsnorm_bf16(x_prompt.reshape(rp, d), x_sample.reshape(rs, d), norm_mix[0])

    assert ha_kdim == ha_vdim and off_qkv % conv_dim == 0 and off_z % gd_vdim == 0
    w_bf = w_in[0].astype(BF)
    proj, k_a = _inproj(xn, w_bf, ha_lb_logits, off_b, ha_kdim, ha_dk ** -0.5)
    q_a, logf, v_a, g_a = ((proj, cb) for cb in (0, off_f // ha_kdim, off_i // ha_kdim, off_g // ha_kdim))
    k_a = (k_a, 0)
    qkv = (proj, off_qkv // conv_dim)
    z_b = (proj, off_z // gd_vdim)
    w_gates_bf = w_bf[:, off_gate:]
    bg, bgt = _beta_decay(xn, w_bf[:, off_b:off_gate], gd_A_log[0], gd_dt_bias[0])

    cg_p = _pick(lp, 128, GDN_CHUNK) if lp >= GDN_CHUNK else lp
    sub_p = min(HGRN_SUB, lp)
    chunk_p = min(GDN_CHUNK, lp)
    cg_s = ls
    zeros_h = jnp.zeros((bp, ha_heads, ha_dk, ha_dv), F32)
    zeros_g = jnp.zeros((bp, vheads, gd_dk, gd_dv), F32)
    zeros_c = jnp.zeros((bp, CONV_K - 1, conv_dim), F32)

    oa_p, sh_p = _hgrn2(q_a, logf, k_a, v_a, g_a, ha_onorm[0], zeros_h,
                        row0=0, seq_len=lp, rows_per_step=cg_p, sub=sub_p, seqs_per_step=1)
    oa_s, sh_s = _hgrn2(q_a, logf, k_a, v_a, g_a, ha_onorm[0], state_hgrn[0],
                        row0=rp, seq_len=ls, rows_per_step=cg_s, sub=min(HGRN_SUB, ls),
                        seqs_per_step=math.gcd(_pick(bs, HGRN_SAMPLE_SEQS, 1), max(rp // cg_s, 1)))

    def time_on_lanes(rows0, nrows, cg):
        part = bgt[:, rows0:rows0 + nrows].reshape(2 * vheads, nrows // cg, cg)
        return jnp.transpose(part, (1, 0, 2))

    seqs_s = _pick(bs, GDN_SAMPLE_SEQS, 1)
    cg_g = _pick(lp, GDN_ROWS, GDN_CHUNK) if lp >= GDN_CHUNK else lp
    bgt_p = time_on_lanes(0, rp, cg_g)
    bgt_s = time_on_lanes(rp, rs, seqs_s * cg_s)
    ob_p, sg_p, sc_p = _gdn(qkv, z_b, bg, bgt_p, conv_w[0], gd_onorm[0], zeros_g, zeros_c,
                            row0=0, seq_len=lp, rows_per_step=cg_g, chunk=chunk_p, seqs_per_step=1)
    ob_s, sg_s, sc_s = _gdn(qkv, z_b, bg, bgt_s, conv_w[0], gd_onorm[0], state_gdn[0],
                            state_conv[0], row0=rp, seq_len=ls, rows_per_step=cg_s, chunk=min(GDN_CHUNK, ls),
                            seqs_per_step=seqs_s)

    merged = _merge(xn, oa_p, oa_s, ob_p, ob_s, w_gates_bf, w_pa[0], w_pb[0])

    nr = LANES
    w_router = jnp.concatenate([w_router_group[0], w_router_expert[0]], axis=1)
    w_router = jnp.pad(w_router, ((0, 0), (0, nr - ngroups - nexp))).astype(BF)
    b_router = jnp.pad(jnp.concatenate([b_router_group[0], b_router_expert[0]]), (0, nr - ngroups - nexp))
    h, xf, logits = _outproj(merged, x_prompt.reshape(rp, d), x_sample.reshape(rs, d), w_out[0],
                             norm_ffn[0], w_router)

    eid, gate, rank, cnt = _route(logits, b_router.reshape(1, nr), ngroups, nexp)
    nblocks = -(-(t * TOP_K) // MOE_BLOCK) + nexp
    pos, block_meta = _plan(cnt, eid, rank, ngroups, nexp, nblocks)
    block_e, block_next, block_valid = (block_meta[:nblocks, j] for j in range(3))

    pos_flat = pos.reshape(-1)
    row_tok = _invert(pos_flat, nblocks * MOE_BLOCK)
    yb = _experts(block_e, block_next, block_valid, row_tok, xf, w_exp_gate[0], w_exp_up[0], w_exp_down[0])
    y_p, y_s = _final(pos_flat, h, gate, yb, norm_final, rp, rs)
    return (y_p.reshape(bp, lp, d), y_s.reshape(bs, ls, d),
            sh_p[None], sg_p[None], sc_p[None], sh_s[None], sg_s[None], sc_s[None])
```

```python
import functools
import math

import jax
import jax.numpy as jnp
from jax import lax
from jax.experimental import pallas as pl
from jax.experimental.pallas import tpu as pltpu

F32 = jnp.float32
BF = jnp.bfloat16
I32 = jnp.int32
EPS = 1e-6
CONV_K = 4
TOP_K = 2
MOE_BLOCK = 128
GDN_CHUNK = 64
GDN_ROWS = 128
GDN_SAMPLE_SEQS = 8
HGRN_SUB = 16
HGRN_SAMPLE_SEQS = 8
LOG2E = 1.4426950408889634
LANES = 128
NEG = -3.0e38
HI = lax.Precision.HIGHEST
MIB = 1024 * 1024


def _cp(sem, vmem_mib=48):
    return pltpu.CompilerParams(dimension_semantics=sem, vmem_limit_bytes=vmem_mib * MIB)


def _pick(n, pref, mult=8):
    best = None
    for t in range(mult, min(n, pref) + 1, mult):
        if n % t == 0:
            best = t
    return best if best is not None else n


def _sigmoid(x):
    return 1.0 / (1.0 + jnp.exp(-x))


def _silu(x):
    return x * _sigmoid(x)


def _iota(shape, dim):
    return lax.broadcasted_iota(I32, shape, dim)


def _dot(a, b):
    return jnp.dot(a, b, preferred_element_type=F32)


def _dot_nt(a, b):
    return lax.dot_general(a, b, (((1,), (1,)), ((), ())), preferred_element_type=F32)


def _dot_tn(a, b):
    return lax.dot_general(a, b, (((0,), (0,)), ((), ())), preferred_element_type=F32)


def _split(a):
    hi = a.astype(BF)
    lo = (a - hi.astype(F32)).astype(BF)
    return hi, lo


def _split3(a, axis=0):
    hi = a.astype(BF)
    r1 = a - hi.astype(F32)
    mid = r1.astype(BF)
    lo = (r1 - mid.astype(F32)).astype(BF)
    return jnp.concatenate([hi, mid, lo], axis=axis)


def _dot3(a, b):
    ah, al = _split(a)
    bh, bl = _split(b)
    return _dot(ah, bh) + (_dot(ah, bl) + _dot(al, bh))


def _store_token_major(ref, x):
    rows, d = x.shape
    nch = d // LANES
    for c in range(nch):
        ref[pl.ds(c, rows, stride=nch), :] = x[:, c * LANES:(c + 1) * LANES]


def _load_token_major(ref, rows, nch):
    return jnp.concatenate([ref[pl.ds(c, rows, stride=nch), :] for c in range(nch)], axis=1)


def _two_source_specs(rows_a, rows_b, tm, width):
    na, nb = rows_a // tm, rows_b // tm
    spec_a = pl.BlockSpec((tm, width), lambda i: (jnp.minimum(i, na - 1), 0))
    spec_b = pl.BlockSpec((tm, width), lambda i: (jnp.maximum(i - na, 0), 0))
    return na, nb, spec_a, spec_b


def _rmsnorm_kernel(xa_ref, xb_ref, w_ref, o_ref, *, na):
    i = pl.program_id(0)

    def body(x_ref):
        x = x_ref[...]
        ms = jnp.mean(x * x, axis=-1, keepdims=True)
        o_ref[...] = (x * lax.rsqrt(ms + EPS) * w_ref[...]).astype(o_ref.dtype)

    @pl.when(i < na)
    def _():
        body(xa_ref)

    @pl.when(i >= na)
    def _():
        body(xb_ref)


def _rmsnorm_bf16(xa, xb, w):
    d = xa.shape[1]
    tm = _pick(math.gcd(xa.shape[0], xb.shape[0]), 512)
    na, nb, sa, sb = _two_source_specs(xa.shape[0], xb.shape[0], tm, d)
    return pl.pallas_call(
        functools.partial(_rmsnorm_kernel, na=na),
        grid=(na + nb,),
        in_specs=[sa, sb, pl.BlockSpec((1, d), lambda i: (0, 0))],
        out_specs=pl.BlockSpec((tm, d), lambda i: (i, 0)),
        out_shape=jax.ShapeDtypeStruct((xa.shape[0] + xb.shape[0], d), BF),
        compiler_params=_cp(("arbitrary",)),
        name="rmsnorm_mix",
    )(xa, xb, w.reshape(1, d))


def _inproj_kernel(x_ref, w_ref, lb_ref, p_ref, kf_ref, *, jq, jf0, jf1, q_scale):
    j = pl.program_id(0)
    acc = _dot(x_ref[...], w_ref[...])
    is_forget = (j >= jf0) & (j < jf1)

    @pl.when(is_forget)
    def _():
        logf, kf = _forget_epilogue(acc, lb_ref[...])
        p_ref[...] = logf
        kf_ref[...] = kf

    @pl.when(jnp.logical_not(is_forget))
    def _():
        p_ref[...] = acc * jnp.where(j < jq, q_scale, 1.0)


def _inproj(x_bf, w, lb_logits, ncols, kdim, q_scale):
    t, k = x_bf.shape
    tn = _pick(math.gcd(ncols, kdim), 1024, LANES)
    tm = _pick(t, 1024)
    ni = t // tm
    jq, jf0, jf1 = kdim // tn, kdim // tn, 2 * kdim // tn
    nf = jf1 - jf0
    fcol = lambda j: jnp.clip(j - jf0, 0, nf - 1)
    frow = lambda j, i: jnp.where(j < jf0, 0, jnp.where(j >= jf1, ni - 1, i))
    kern = functools.partial(_inproj_kernel, jq=jq, jf0=jf0, jf1=jf1, q_scale=q_scale)
    return pl.pallas_call(
        kern,
        grid=(ncols // tn, ni),
        in_specs=[pl.BlockSpec((tm, k), lambda j, i: (i, 0)),
                  pl.BlockSpec((k, tn), lambda j, i: (0, j)),
                  pl.BlockSpec((lb_logits.shape[0], tn), lambda j, i: (0, fcol(j)))],
        out_specs=[pl.BlockSpec((tm, tn), lambda j, i: (i, j)),
                   pl.BlockSpec((tm, tn), lambda j, i: (frow(j, i), fcol(j)))],
        out_shape=[jax.ShapeDtypeStruct((t, ncols), F32), jax.ShapeDtypeStruct((t, kdim), F32)],
        compiler_params=_cp(("arbitrary", "arbitrary")),
        name="in_proj",
    )(x_bf, w, lb_logits)


def _forget_epilogue(acc, lb_logits):
    m = jnp.max(lb_logits, axis=0, keepdims=True)
    e = jnp.exp(lb_logits - m)
    lb = e[0:1, :] / jnp.sum(e, axis=0, keepdims=True)
    f = lb + (1.0 - lb) * _sigmoid(acc)
    return jnp.log(f), 1.0 - f


def _beta_decay_kernel(x_ref, w_ref, wt_ref, prow_ref, pcol_ref, o_ref, ot_ref, *, vh):
    x = x_ref[...]
    acc = _dot(x, w_ref[...])
    acct = _dot_nt(wt_ref[...], x)

    def act(a, is_beta, a_neg_exp, dt_bias):
        z = a + dt_bias
        softplus = jnp.maximum(z, 0.0) + jnp.log(1.0 + jnp.exp(-jnp.abs(z)))
        return jnp.where(is_beta, _sigmoid(a), a_neg_exp * softplus)

    prow = prow_ref[...]
    pcol = pcol_ref[...]
    o_ref[...] = act(acc, _iota(acc.shape, 1) < vh, prow[0:1, :], prow[1:2, :])
    ot_ref[...] = act(acct, _iota(acct.shape, 0) < vh, pcol[:, 0:1], pcol[:, 1:2])


def _beta_decay(x_bf, w_ba, a_log, dt_bias):
    t, k = x_bf.shape
    vh = a_log.shape[0]
    tm = t if t <= 2048 else _pick(t, 1024, LANES)
    zeros = jnp.zeros((vh,), F32)
    prow = jnp.stack([jnp.concatenate([zeros, -jnp.exp(a_log)]), jnp.concatenate([zeros, dt_bias])])
    return pl.pallas_call(
        functools.partial(_beta_decay_kernel, vh=vh),
        grid=(t // tm,),
        in_specs=[pl.BlockSpec((tm, k), lambda i: (i, 0)),
                  pl.BlockSpec((k, 2 * vh), lambda i: (0, 0)),
                  pl.BlockSpec((2 * vh, k), lambda i: (0, 0)),
                  pl.BlockSpec((2, 2 * vh), lambda i: (0, 0)),
                  pl.BlockSpec((2 * vh, 2), lambda i: (0, 0))],
        out_specs=[pl.BlockSpec((tm, 2 * vh), lambda i: (i, 0)),
                   pl.BlockSpec((2 * vh, tm), lambda i: (0, i))],
        out_shape=[jax.ShapeDtypeStruct((t, 2 * vh), F32), jax.ShapeDtypeStruct((2 * vh, t), F32)],
        compiler_params=_cp(("arbitrary",)),
        name="proj_beta_decay",
    )(x_bf, w_ba.astype(BF), w_ba.T.astype(BF), prow, prow.T)


def _hgrn2_kernel(q_ref, lf_ref, k_ref, v_ref, g_ref, on_ref, s0_ref, o_ref, so_ref, s_scr, gall_scr,
                  g_scr, k_scr, v_scr, *, heads, dk, dv, sub, nsub, nseq, nchunks, group, unroll):
    c = pl.program_id(1)
    cg = sub * nsub

    @pl.when(c == 0)
    def _():
        s_scr[...] = s0_ref[...]

    rows_all = nseq * cg
    shift = sub.bit_length() - 1
    r_i = _iota((rows_all, 3 * rows_all), 0)
    c_i = _iota((rows_all, 3 * rows_all), 1) & (rows_all - 1)
    same = lax.shift_right_logical(r_i, shift) == lax.shift_right_logical(c_i, shift)
    btril3 = jnp.where((r_i >= c_i) & same, 1.0, 0.0).astype(BF)
    gall_scr[...] = _dot(btril3, _split3(lf_ref[...])) * LOG2E
    onorm = on_ref[...]
    n_aug = 16 - sub % 16
    aug_rhs = jnp.concatenate([jnp.zeros((n_aug, dv), BF), jnp.ones((n_aug, dv), BF)], axis=1)
    zeros_v = jnp.zeros((sub, dv), BF)
    zeros_aug = jnp.zeros((n_aug - 3, dk), F32)

    nh = sub // 8
    row8 = _iota((8, 1), 0)

    def block(it, carry):
        for u in range(unroll):
            sub_chunk(it * unroll + u, g_scr.at[u], k_scr.at[u], v_scr.at[u])
        return carry

    def sub_chunk(sb, g_scr, k_scr, v_scr):
        for q in range(nseq):
            rows = pl.ds(pl.multiple_of(q * cg + sb * sub, sub), sub)
            g_scr[q] = gall_scr[rows, :]
            k_scr[q] = k_ref[rows, :]
            v_scr[q] = v_ref[rows, :]
        problems = [(q, h) for q in range(nseq) for h in range(heads)]
        for g0 in range(0, len(problems), group):
            ps = []
            for (q, h) in problems[g0:g0 + group]:
                r0 = pl.multiple_of(q * cg + sb * sub, sub)
                rows = pl.ds(r0, sub)
                kc = pl.ds(h * dk, dk)
                vc = pl.ds(h * dv, dv)
                ps.append(dict(q=q, h=h, r0=r0, rows=rows, kc=kc, vc=vc,
                               gc=g_scr[q, :, kc],
                               qv=q_ref[rows, kc], k=k_ref[rows, kc], v=v_ref[rows, vc]))
            for p in ps:
                p["s_old"] = s_scr[p["q"], p["h"]]
                p["o_state"] = _dot((p["qv"] * jnp.exp2(p["gc"])).astype(BF), p["s_old"].astype(BF))
                p["o8"] = [jnp.zeros((8, dv), F32) for _ in range(nh)]
                p["q8"] = [p["qv"][8 * i:8 * i + 8] for i in range(nh)]
                p["g8"] = [p["gc"][8 * i:8 * i + 8] for i in range(nh)]
            for j in range(sub):
                for p in ps:
                    kj = k_scr[p["q"], pl.ds(j, 1), p["kc"]]
                    vj = v_scr[p["q"], pl.ds(j, 1), p["vc"]]
                    gj = g_scr[p["q"], pl.ds(j, 1), p["kc"]]
                    for i in range(j // 8, nh):
                        pr = (p["q8"][i] * kj) * jnp.exp2(p["g8"][i] - gj)
                        a_col = jnp.sum(pr, axis=-1, keepdims=True)
                        if i == j // 8:
                            a_col = jnp.where(row8 >= j % 8, a_col, 0.0)
                        p["o8"][i] = p["o8"][i] + a_col * vj
            for p in ps:
                p["o"] = p["o_state"] + (jnp.concatenate(p["o8"], axis=0) if nh > 1 else p["o8"][0])
            for p in ps:
                gend = p["gc"][sub - 1:sub, :]
                dend = jnp.exp2(gend)
                d_hi = dend.astype(BF).astype(F32)
                d_mid = (dend - d_hi).astype(BF).astype(F32)
                d_lo = (dend - d_hi - d_mid).astype(BF).astype(F32)
                kdec = p["k"] * jnp.exp2(gend - p["gc"])
                lhs = jnp.concatenate([kdec, d_hi, d_mid, d_lo, zeros_aug], axis=0).astype(BF)
                rhs = jnp.concatenate([jnp.concatenate([p["v"].astype(BF), zeros_v], axis=1), aug_rhs], axis=0)
                p["kv"] = _dot_tn(lhs, rhs)
            for p in ps:
                s_scr[p["q"], p["h"]] = p["kv"][:, dv:] * p["s_old"] + p["kv"][:, :dv]
            for p in ps:
                o = p["o"]
                ms = jnp.mean(o * o, axis=-1, keepdims=True)
                o_ref[p["rows"], p["vc"]] = (o * lax.rsqrt(ms + EPS) * onorm) * _silu(g_ref[p["rows"], p["vc"]])

    lax.fori_loop(0, nsub // unroll, block, 0)

    @pl.when(c == nchunks - 1)
    def _():
        so_ref[...] = s_scr[...]


def _hgrn2(q, lf, kf, v, gate, onorm, s0, *, row0, seq_len, rows_per_step, sub, seqs_per_step):
    nseq_total, heads, dk, dv = s0.shape
    assert dk == dv, "state decay is applied with a (dk, dv) tile"
    width = heads * dk
    cg = rows_per_step
    nseq = seqs_per_step
    nchunks = seq_len // cg
    assert nseq == 1 or nchunks == 1, "several sequences per step only when a step covers whole sequences"
    rows = nseq * cg
    b0 = row0 // rows
    assert sub & (sub - 1) == 0 and sub % 8 == 0 and rows & (rows - 1) == 0
    unroll = next(u for u in (8, 4, 2, 1) if (cg // sub) % u == 0)
    def row_spec(col_block):
        return pl.BlockSpec((rows, width), lambda b, c: (b0 + b * nchunks + c, col_block))

    st_spec = pl.BlockSpec((nseq, heads, dk, dv), lambda b, c: (b, 0, 0, 0))
    kern = functools.partial(_hgrn2_kernel, heads=heads, dk=dk, dv=dv, sub=sub, nsub=cg // sub, nseq=nseq,
                             nchunks=nchunks, group=8, unroll=unroll)
    operands = (q, lf, kf, v, gate)
    return pl.pallas_call(
        kern,
        grid=(nseq_total // nseq, nchunks),
        in_specs=[row_spec(cb) for _, cb in operands]
                 + [pl.BlockSpec((1, dv), lambda b, c: (0, 0)), st_spec],
        out_specs=[pl.BlockSpec((rows, heads * dv), lambda b, c: (b * nchunks + c, 0)), st_spec],
        out_shape=[jax.ShapeDtypeStruct((nseq_total * seq_len, heads * dv), F32),
                   jax.ShapeDtypeStruct(s0.shape, F32)],
        scratch_shapes=[pltpu.VMEM((nseq, heads, dk, dv), F32), pltpu.VMEM((rows, width), F32),
                        pltpu.VMEM((unroll, nseq, sub, width), F32), pltpu.VMEM((unroll, nseq, sub, width), F32),
                        pltpu.VMEM((unroll, nseq, sub, heads * dv), F32)],
        compiler_params=_cp(("arbitrary", "arbitrary")),
        name="hgrn2_recurrence",
    )(*[a for a, _ in operands], onorm.reshape(1, dv), s0)


def _gdn_kernel(qkv_ref, z_ref, bg_ref, bgt_ref, cw_ref, on_ref, s0_ref, c0_ref,
                o_ref, so_ref, co_ref, s_scr, xx, xs,
                *, kheads, vheads, dk, dv, chunk, nsub, nseq, nchunks, group_kh):
    c = pl.program_id(1)
    cg = chunk * nsub
    kdim = kheads * dk
    rep = vheads // kheads
    tail0 = 8 - (CONV_K - 1)

    @pl.when(c == 0)
    def _():
        s_scr[...] = s0_ref[...]
        xx[:, 0:8, :] = c0_ref[...]

    for q in range(nseq):
        xx[q, 8:8 + cg, :] = qkv_ref[q * cg:(q + 1) * cg, :]
        conv = xx[q, pl.ds(tail0, cg), :] * cw_ref[0:1, :]
        for j in range(1, CONV_K):
            conv = conv + xx[q, pl.ds(tail0 + j, cg), :] * cw_ref[j:j + 1, :]
        xs[q] = _silu(conv)

    @pl.when(c == nchunks - 1)
    def _():
        for q in range(nseq):
            co_ref[q] = xx[q, pl.ds(8 + cg - (CONV_K - 1), CONV_K - 1), :]

    for q in range(nseq):
        xx[q, 0:8, :] = xx[q, cg:cg + 8, :]

    wp = chunk
    r_i = _iota((chunk, wp), 0)
    c_i = _iota((chunk, wp), 1)
    causal = r_i >= c_i
    strict = r_i > c_i

    def pad_rows(a):
        if a.shape[0] == wp:
            return a
        if a.dtype == BF and a.shape[0] % 16:
            return pad_rows(a.astype(F32)).astype(BF)
        return jnp.concatenate([a, jnp.zeros((wp - a.shape[0],) + a.shape[1:], a.dtype)], axis=0)

    def split_lhs(a):
        a_hi = a.astype(BF).astype(F32)
        return jnp.concatenate([a_hi, a_hi, a - a_hi], axis=1).astype(BF)

    def split_rhs(b):
        bh, bl = (pad_rows(t) for t in _split(b))
        return jnp.concatenate([bh, bl, bh], axis=0)
    cm = chunk - 1
    tril3 = jnp.where(_iota((chunk, 3 * chunk), 0) >= (_iota((chunk, 3 * chunk), 1) & cm), 1.0, 0.0).astype(BF)
    t_r = _iota((3 * chunk, wp), 0) & cm
    t_c = _iota((3 * chunk, wp), 1)
    triu3 = jnp.where((t_r <= t_c) & (t_c < chunk), 1.0, 0.0).astype(BF)
    onorm = on_ref[...]

    cums = {}

    def cumulative_decay(q, s):
        if (q, s) not in cums:
            r0 = q * cg + s * chunk
            bg = bg_ref[r0:r0 + chunk, :]
            gt = bgt_ref[0, vheads:2 * vheads, r0:r0 + chunk]
            cums[(q, s)] = (bg, _dot(tril3, _split3(bg[:, vheads:2 * vheads], axis=0)),
                            _dot(_split3(gt, axis=1), triu3))
        return cums[(q, s)]

    def run_group(items):
        probs = []
        for (q, s, kh) in items:
            r0 = q * cg + s * chunk
            x0 = s * chunk
            bg, gcol_all, grow_all = cumulative_decay(q, s)
            for kh in (kh,):
                qh = xs[q, x0:x0 + chunk, kh * dk:(kh + 1) * dk]
                kk_ = xs[q, x0:x0 + chunk, kdim + kh * dk:kdim + (kh + 1) * dk]
                qn = qh * lax.rsqrt(jnp.sum(qh * qh, axis=-1, keepdims=True) + EPS) * (dk ** -0.5)
                kn = kk_ * lax.rsqrt(jnp.sum(kk_ * kk_, axis=-1, keepdims=True) + EPS)
                kn_bf = kn.astype(BF)
                kn_pad = pad_rows(kn_bf)
                kk = _dot_nt(kn_bf, kn_pad)
                qk = _dot_nt(qn.astype(BF), kn_pad)
                for r in range(rep):
                    h = kh * rep + r
                    vh_ = xs[q, x0:x0 + chunk, 2 * kdim + h * dv:2 * kdim + (h + 1) * dv]
                    gcol = gcol_all[:, h:h + 1]
                    grow = grow_all[h:h + 1, :]
                    beta = bg[:, h:h + 1]
                    decay = jnp.where(causal, jnp.exp(jnp.minimum(gcol - grow, 0.0)), 0.0)
                    eg = jnp.exp(gcol)
                    gend = gcol[chunk - 1:chunk, :]
                    probs.append(dict(
                        q=q, s=s, h=h, r0=r0,
                        x=jnp.where(strict, -(beta * kk * decay), 0.0),
                        y=jnp.concatenate([vh_ * beta, kn * (beta * eg)], axis=1),
                        a_bf=(qk * decay).astype(BF),
                        qg=qn * eg,
                        kdec_bf=(kn * jnp.exp(gend - gcol)).astype(BF),
                        send=jnp.exp(gend)))

        if chunk <= 16:
            for j in range(chunk - 1):
                for p in probs:
                    p["y"] = p["y"] + p["x"][:, j:j + 1] * p["y"][j:j + 1, :]
        else:
            n_fac = int(math.log2(chunk))
            for k in range(n_fac):
                for p in probs:
                    p["x_lhs"] = split_lhs(p["x"])
                    p["y"] = p["y"] + _dot(p["x_lhs"], split_rhs(p["y"]))
                if k + 1 < n_fac:
                    for p in probs:
                        p["x"] = _dot(p["x_lhs"], split_rhs(p["x"]))

        for s in sorted({p["s"] for p in probs}):
            cur = [p for p in probs if p["s"] == s]
            for p in cur:
                p["s_old"] = s_scr[p["q"], p["h"]]
                lhs = jnp.concatenate([p["y"][:, dv:], p["qg"]], axis=0).astype(BF)
                p["ws"] = _dot(lhs, p["s_old"].astype(BF))
            for p in cur:
                p["u_bf"] = (p["y"][:, :dv] - p["ws"][:chunk]).astype(BF)
            for p in cur:
                s_scr[p["q"], p["h"]] = p["send"] * p["s_old"] + _dot_tn(p["kdec_bf"], p["u_bf"])
            for p in cur:
                o = p["ws"][chunk:] + _dot(p["a_bf"], pad_rows(p["u_bf"]))
                ms = jnp.mean(o * o, axis=-1, keepdims=True)
                zc = slice(p["h"] * dv, (p["h"] + 1) * dv)
                rows = slice(p["r0"], p["r0"] + chunk)
                o_ref[rows, zc] = (o * lax.rsqrt(ms + EPS) * onorm) * _silu(z_ref[rows, zc])

    for g0 in range(0, kheads, group_kh):
        run_group([(q, s, kh) for q in range(nseq) for s in range(nsub)
                   for kh in range(g0, min(g0 + group_kh, kheads))])

    @pl.when(c == nchunks - 1)
    def _():
        so_ref[...] = s_scr[...]


def _gdn(qkv, z, bg, bgt3, conv_w, onorm, s0, conv0, *, row0, seq_len, rows_per_step, chunk, seqs_per_step):
    nseq_total, vheads, dk, dv = s0.shape
    (qkv, qkv_cb), (z, z_cb) = qkv, z
    conv_dim = conv_w.shape[1]
    kheads = (conv_dim - vheads * dv) // (2 * dk)
    cg = rows_per_step
    nseq = seqs_per_step
    nchunks = seq_len // cg
    assert nseq == 1 or nchunks == 1, "several sequences per step only when a step covers whole sequences"
    rows = nseq * cg
    b0 = row0 // rows
    conv0p = jnp.pad(conv0, ((0, 0), (8 - (CONV_K - 1), 0), (0, 0)))
    rmap = lambda b, c: (b0 + b * nchunks + c, 0)
    st_spec = pl.BlockSpec((nseq, vheads, dk, dv), lambda b, c: (b, 0, 0, 0))
    kern = functools.partial(_gdn_kernel, kheads=kheads, vheads=vheads, dk=dk, dv=dv, chunk=chunk,
                             nsub=cg // chunk, nseq=nseq, nchunks=nchunks, group_kh=kheads)
    return pl.pallas_call(
        kern,
        grid=(nseq_total // nseq, nchunks),
        in_specs=[pl.BlockSpec((rows, conv_dim), lambda b, c: (b0 + b * nchunks + c, qkv_cb)),
                  pl.BlockSpec((rows, vheads * dv), lambda b, c: (b0 + b * nchunks + c, z_cb)),
                  pl.BlockSpec((rows, 2 * vheads), rmap),
                  pl.BlockSpec((1, 2 * vheads, rows), lambda b, c: (b * nchunks + c, 0, 0)),
                  pl.BlockSpec((CONV_K, conv_dim), lambda b, c: (0, 0)),
                  pl.BlockSpec((1, dv), lambda b, c: (0, 0)),
                  st_spec,
                  pl.BlockSpec((nseq, 8, conv_dim), lambda b, c: (b, 0, 0))],
        out_specs=[pl.BlockSpec((rows, vheads * dv), lambda b, c: (b * nchunks + c, 0)),
                   st_spec,
                   pl.BlockSpec((nseq, CONV_K - 1, conv_dim), lambda b, c: (b, 0, 0))],
        out_shape=[jax.ShapeDtypeStruct((nseq_total * seq_len, vheads * dv), F32),
                   jax.ShapeDtypeStruct(s0.shape, F32),
                   jax.ShapeDtypeStruct((nseq_total, CONV_K - 1, conv_dim), F32)],
        scratch_shapes=[pltpu.VMEM((nseq, vheads, dk, dv), F32), pltpu.VMEM((nseq, 8 + cg, conv_dim), F32),
                        pltpu.VMEM((nseq, cg, conv_dim), F32)],
        compiler_params=_cp(("arbitrary", "arbitrary")),
        name="gdn_recurrence",
    )(qkv, z, bg, bgt3, conv_w, onorm.reshape(1, dv), s0, conv0p)


def _merge_kernel(xn_ref, oa1_ref, oa2_ref, ob1_ref, ob2_ref, wga_ref, wgb_ref, wa_ref, wb_ref, o_ref,
                  oa_bf, ob_bf, *, na):
    i = pl.program_id(0)
    j = pl.program_id(1)

    @pl.when((j == 0) & (i < na))
    def _():
        oa_bf[...] = oa1_ref[...].astype(BF)
        ob_bf[...] = ob1_ref[...].astype(BF)

    @pl.when((j == 0) & (i >= na))
    def _():
        oa_bf[...] = oa2_ref[...].astype(BF)
        ob_bf[...] = ob2_ref[...].astype(BF)

    xn = xn_ref[...]
    ya = _dot(oa_bf[...], wa_ref[...])
    yb = _dot(ob_bf[...], wb_ref[...])
    ga = _dot(xn, wga_ref[...])
    gb = _dot(xn, wgb_ref[...])
    o_ref[...] = (_sigmoid(ga) * ya + _sigmoid(gb) * yb).astype(o_ref.dtype)


def _merge(xn, oa_p, oa_s, ob_p, ob_s, w_gates_bf, w_pa, w_pb):
    rp, rs = oa_p.shape[0], oa_s.shape[0]
    ka, kb = oa_p.shape[1], ob_p.shape[1]
    d = w_pa.shape[1]
    k = xn.shape[1]
    tm = _pick(math.gcd(rp, rs), 512)
    tn = _pick(d, 512, LANES)
    na, nb = rp // tm, rs // tm
    nj = d // tn
    amap = lambda i, j: (jnp.minimum(i, na - 1), 0)
    bmap = lambda i, j: (jnp.maximum(i - na, 0), 0)
    return pl.pallas_call(
        functools.partial(_merge_kernel, na=na),
        grid=(na + nb, nj),
        in_specs=[pl.BlockSpec((tm, k), lambda i, j: (i, 0)),
                  pl.BlockSpec((tm, ka), amap), pl.BlockSpec((tm, ka), bmap),
                  pl.BlockSpec((tm, kb), amap), pl.BlockSpec((tm, kb), bmap),
                  pl.BlockSpec((k, tn), lambda i, j: (0, j)),
                  pl.BlockSpec((k, tn), lambda i, j: (0, nj + j)),
                  pl.BlockSpec((ka, tn), lambda i, j: (0, j)),
                  pl.BlockSpec((kb, tn), lambda i, j: (0, j))],
        out_specs=pl.BlockSpec((tm, tn), lambda i, j: (i, j)),
        out_shape=jax.ShapeDtypeStruct((rp + rs, d), BF),
        scratch_shapes=[pltpu.VMEM((tm, ka), BF), pltpu.VMEM((tm, kb), BF)],
        compiler_params=_cp(("arbitrary", "arbitrary")),
        name="gates_branch_proj_merge",
    )(xn, oa_p, oa_s, ob_p, ob_s, w_gates_bf, w_gates_bf, w_pa.astype(BF), w_pb.astype(BF))


def _outproj_kernel(m_ref, xa_ref, xb_ref, wo_ref, nw_ref, wr_ref, h_ref, xf_ref, lg_ref, *, na):
    i = pl.program_id(0)

    def body(x_ref):
        h = x_ref[...] + _dot(m_ref[...], wo_ref[...])
        h_ref[...] = h
        ms = jnp.mean(h * h, axis=-1, keepdims=True)
        xf = h * lax.rsqrt(ms + EPS) * nw_ref[...]
        _store_token_major(xf_ref, xf)
        lg_ref[...] = _dot(xf.astype(BF), wr_ref[...])

    @pl.when(i < na)
    def _():
        body(xa_ref)

    @pl.when(i >= na)
    def _():
        body(xb_ref)


def _outproj(merged, xa, xb, w_out, norm_ffn, w_router):
    t, d = merged.shape
    tm = _pick(math.gcd(xa.shape[0], xb.shape[0]), 512)
    na, nb, sa, sb = _two_source_specs(xa.shape[0], xb.shape[0], tm, d)
    nr = w_router.shape[1]
    const = lambda i: (0, 0)
    return pl.pallas_call(
        functools.partial(_outproj_kernel, na=na),
        grid=(na + nb,),
        in_specs=[pl.BlockSpec((tm, d), lambda i: (i, 0)), sa, sb,
                  pl.BlockSpec((d, d), const, pipeline_mode=pl.Buffered(1)),
                  pl.BlockSpec((1, d), const),
                  pl.BlockSpec((d, nr), const, pipeline_mode=pl.Buffered(1))],
        out_specs=[pl.BlockSpec((tm, d), lambda i: (i, 0)),
                   pl.BlockSpec((tm * (d // LANES), LANES), lambda i: (i, 0)),
                   pl.BlockSpec((tm, nr), lambda i: (i, 0))],
        out_shape=[jax.ShapeDtypeStruct((t, d), F32), jax.ShapeDtypeStruct((t * (d // LANES), LANES), F32),
                   jax.ShapeDtypeStruct((t, nr), F32)],
        compiler_params=_cp(("arbitrary",), 56),
        name="out_proj_ffn_norm",
    )(merged, xa, xb, w_out.astype(BF), norm_ffn.reshape(1, d), w_router)


def _route_kernel(lg_ref, b_ref, eid_ref, gate_ref, rank_ref, cnt_ref, carry,
                  *, ngroups, nexp, tm):
    i = pl.program_id(0)

    @pl.when(i == 0)
    def _():
        carry[...] = jnp.zeros_like(carry)

    per_group = nexp // ngroups
    lg = lg_ref[...] + b_ref[...]
    lane = _iota(lg.shape, 1)
    big = jnp.int32(1 << 20)
    is_g = lane < ngroups
    gl = jnp.where(is_g, lg, NEG)
    gmax = jnp.max(gl, axis=-1, keepdims=True)
    gidx = jnp.min(jnp.where(gl == gmax, lane, big), axis=-1, keepdims=True)
    gsum = jnp.sum(jnp.where(is_g, jnp.exp(gl - gmax), 0.0), axis=-1, keepdims=True)
    gw = 1.0 / gsum
    elane = lane - ngroups
    in_grp = (elane >= gidx * per_group) & (elane < (gidx + 1) * per_group)
    el = jnp.where(in_grp, lg, NEG)
    v1 = jnp.max(el, axis=-1, keepdims=True)
    i1 = jnp.min(jnp.where(in_grp & (el == v1), elane, big), axis=-1, keepdims=True)
    in2 = in_grp & (elane != i1)
    el2 = jnp.where(in2, lg, NEG)
    v2 = jnp.max(el2, axis=-1, keepdims=True)
    i2 = jnp.min(jnp.where(in2 & (el2 == v2), elane, big), axis=-1, keepdims=True)
    p2 = jnp.exp(v2 - v1)
    den = 1.0 + p2
    lane2 = _iota((tm, TOP_K), 1)
    eid_ref[...] = jnp.where(lane2 == 0, i1, i2)
    gate_ref[...] = jnp.where(lane2 == 0, gw / den, gw * p2 / den)

    oh1 = (elane == i1).astype(F32)
    oh2 = (elane == i2).astype(F32)
    lower = (_iota((tm, tm), 0) > _iota((tm, tm), 1)).astype(BF)
    cs1 = _dot(lower, oh1.astype(BF))
    cs2 = _dot(lower, oh2.astype(BF))
    tot1 = jnp.sum(oh1, axis=0, keepdims=True)
    tot2 = jnp.sum(oh2, axis=0, keepdims=True)
    base = carry[0:1, :]
    r1 = jnp.sum(oh1 * (base + cs1), axis=-1, keepdims=True)
    r2 = jnp.sum(oh2 * (base + tot1 + cs2), axis=-1, keepdims=True)
    rank_ref[...] = jnp.where(lane2 == 0, r1, r2).astype(I32)
    new = base + tot1 + tot2
    carry[...] = jnp.broadcast_to(new, carry.shape)
    cnt_ref[...] = jnp.broadcast_to(new, cnt_ref.shape)


def _route(logits, bias_row, ngroups, nexp):
    t, nr = logits.shape
    tm = _pick(t, 512)
    kern = functools.partial(_route_kernel, ngroups=ngroups, nexp=nexp, tm=tm)
    return pl.pallas_call(
        kern,
        grid=(t // tm,),
        in_specs=[pl.BlockSpec((tm, nr), lambda i: (i, 0)), pl.BlockSpec((1, nr), lambda i: (0, 0))],
        out_specs=[pl.BlockSpec((tm, TOP_K), lambda i: (i, 0)),
                   pl.BlockSpec((tm, TOP_K), lambda i: (i, 0)),
                   pl.BlockSpec((tm, TOP_K), lambda i: (i, 0)),
                   pl.BlockSpec((8, nr), lambda i: (0, 0))],
        out_shape=[jax.ShapeDtypeStruct((t, TOP_K), I32), jax.ShapeDtypeStruct((t, TOP_K), F32),
                   jax.ShapeDtypeStruct((t, TOP_K), I32), jax.ShapeDtypeStruct((8, nr), F32)],
        scratch_shapes=[pltpu.VMEM((8, nr), F32)],
        compiler_params=_cp(("arbitrary",)),
        name="route_topk_rank",
    )(logits, bias_row)


def _plan_kernel(cnt_ref, eid_ref, rank_ref, pos_ref, be_ref, *, ngroups, nexp, nblocks_pad, tm):
    nr = cnt_ref.shape[1]
    cnt = cnt_ref[0:1, :]
    padded = jnp.floor((cnt + (MOE_BLOCK - 1)) * (1.0 / MOE_BLOCK)) * MOE_BLOCK
    r_i = _iota((nr, nr), 0)
    c_i = _iota((nr, nr), 1)
    padded_col = jnp.sum(jnp.where(r_i == c_i, jnp.broadcast_to(padded, (nr, nr)), 0.0), axis=1, keepdims=True)
    start = jnp.sum(jnp.where(r_i < c_i, jnp.broadcast_to(padded_col, (nr, nr)), 0.0), axis=0, keepdims=True)
    end = start + padded
    lane = _iota((tm, nr), 1)
    eid = eid_ref[...]
    rank = rank_ref[...]
    lane2 = _iota((tm, TOP_K), 1)
    pos = jnp.zeros((tm, TOP_K), F32)
    for k in range(TOP_K):
        oh = (lane - ngroups) == eid[:, k:k + 1]
        st = jnp.sum(jnp.where(oh, start, 0.0), axis=-1, keepdims=True)
        pos = jnp.where(lane2 == k, st, pos)
    pos_ref[...] = pos.astype(I32) + rank

    @pl.when(pl.program_id(0) == 0)
    def _():
        blk_row = _iota((nblocks_pad, nr), 0).astype(F32) * MOE_BLOCK
        lane_b = _iota((nblocks_pad, nr), 1)
        is_e = (lane_b >= ngroups) & (lane_b < ngroups + nexp)
        n_le = jnp.sum(jnp.where(is_e & (end <= blk_row), 1.0, 0.0), axis=-1, keepdims=True)
        e_lane = (lane_b - ngroups).astype(F32)
        nonempty = is_e & (padded > 0.0)
        e_last = jnp.max(jnp.where(nonempty, e_lane, -1.0), axis=-1, keepdims=True)
        be = jnp.minimum(n_le, e_last)
        nxt = jnp.min(jnp.where(nonempty & (e_lane > be), e_lane, 1e9), axis=-1, keepdims=True)
        nxt = jnp.where(nxt > 1e8, -1.0, nxt)
        total = jnp.sum(padded, axis=-1, keepdims=True)
        valid = jnp.where(blk_row[:, 0:1] < total, 1.0, 0.0)
        col = _iota((nblocks_pad, 4), 1)
        meta = jnp.where(col == 0, be, jnp.where(col == 1, nxt, jnp.where(col == 2, valid, 0.0)))
        be_ref[...] = meta.astype(I32)


def _plan(cnt, eid, rank, ngroups, nexp, nblocks):
    t = eid.shape[0]
    nr = cnt.shape[1]
    tm = _pick(t, 512)
    nblocks_pad = -(-nblocks // 8) * 8
    kern = functools.partial(_plan_kernel, ngroups=ngroups, nexp=nexp, nblocks_pad=nblocks_pad, tm=tm)
    return pl.pallas_call(
        kern,
        grid=(t // tm,),
        in_specs=[pl.BlockSpec((8, nr), lambda i: (0, 0)),
                  pl.BlockSpec((tm, TOP_K), lambda i: (i, 0)),
                  pl.BlockSpec((tm, TOP_K), lambda i: (i, 0))],
        out_specs=[pl.BlockSpec((tm, TOP_K), lambda i: (i, 0)),
                   pl.BlockSpec((nblocks_pad, 4), lambda i: (0, 0))],
        out_shape=[jax.ShapeDtypeStruct((t, TOP_K), I32), jax.ShapeDtypeStruct((nblocks_pad, 4), I32)],
        compiler_params=_cp(("arbitrary",)),
        name="route_plan",
    )(cnt, eid, rank)


def _invert_kernel(pos_ref, tok_ref, *, n_assign, n_rows):
    def clear(r, c):
        tok_ref[r] = 0
        return c

    lax.fori_loop(0, n_rows, clear, 0, unroll=16)

    def put(a, c):
        tok_ref[pos_ref[a]] = lax.shift_right_logical(a, TOP_K.bit_length() - 1)
        return c

    lax.fori_loop(0, n_assign, put, 0, unroll=16)


def _invert(pos_flat, n_rows):
    n_assign = pos_flat.shape[0]
    return pl.pallas_call(
        functools.partial(_invert_kernel, n_assign=n_assign, n_rows=n_rows),
        in_specs=[pl.BlockSpec(memory_space=pltpu.SMEM)],
        out_specs=pl.BlockSpec(memory_space=pltpu.SMEM),
        out_shape=jax.ShapeDtypeStruct((n_rows,), I32),
        name="route_invert",
    )(pos_flat)


def _expert_kernel(be_ref, nxt_ref, valid_ref, tok_ref, x_hbm, wg_hbm, wu_hbm, wd_hbm, y_ref,
                   xbuf0, xbuf1, xsem, wg_st, wu_st, wd_st, wsem, wg_bf, wu_bf, wd_bf, *, nch):
    i = pl.program_id(0)
    n = pl.num_programs(0)
    xbufs = (xbuf0, xbuf1)

    def row_copy(blk, s, r):
        t = tok_ref[blk * MOE_BLOCK + r]
        src = x_hbm.at[pl.ds(pl.multiple_of(t * nch, nch), nch), :]
        return pltpu.make_async_copy(src, xbufs[s].at[pl.ds(r * nch, nch), :], xsem.at[s])

    def block_wait(s):
        pltpu.make_async_copy(x_hbm.at[pl.ds(0, MOE_BLOCK * nch), :], xbufs[s], xsem.at[s]).wait()

    def weight_copies(e):
        return (pltpu.make_async_copy(wg_hbm.at[e], wg_st, wsem.at[0]),
                pltpu.make_async_copy(wu_hbm.at[e], wu_st, wsem.at[1]),
                pltpu.make_async_copy(wd_hbm.at[e], wd_st, wsem.at[2]))

    @pl.when(i == 0)
    def _():
        for r in range(MOE_BLOCK):
            row_copy(0, 0, r).start()
        for cp in weight_copies(be_ref[0]):
            cp.start(priority=1)

    prev = be_ref[jnp.maximum(i - 1, 0)]

    @pl.when((i == 0) | (be_ref[i] != prev))
    def _():
        for cp in weight_copies(be_ref[i]):
            cp.wait()
        wg_bf[...] = wg_st[...].astype(BF)
        wu_bf[...] = wu_st[...].astype(BF)
        wd_bf[...] = wd_st[...].astype(BF)

        @pl.when(nxt_ref[i] >= 0)
        def _():
            for cp in weight_copies(nxt_ref[i]):
                cp.start(priority=1)

    ahead = jnp.minimum(i + 1, n - 1)
    used = valid_ref[i] != 0
    used_ahead = (i + 1 < n) & (valid_ref[ahead] != 0)

    def step(cur, oth):
        @pl.when(used_ahead)
        def _():
            for r in range(MOE_BLOCK):
                row_copy(ahead, oth, r).start()

        block_wait(cur)
        x = _load_token_major(xbufs[cur], MOE_BLOCK, nch).astype(BF)
        hid = (_silu(_dot(x, wg_bf[...])) * _dot(x, wu_bf[...])).astype(BF)
        _store_token_major(y_ref, _dot(hid, wd_bf[...]))

    @pl.when(used & (i % 2 == 0))
    def _():
        step(0, 1)

    @pl.when(used & (i % 2 == 1))
    def _():
        step(1, 0)

    @pl.when(jnp.logical_not(used))
    def _():
        y_ref[...] = jnp.zeros_like(y_ref)


def _experts(block_e, block_next, block_valid, row_tok, xf, w_eg, w_eu, w_ed):
    nrows = row_tok.shape[0]
    d = w_eg.shape[1]
    nch = d // LANES
    nblocks = nrows // MOE_BLOCK
    de = w_eg.shape[2]
    any_spec = pl.BlockSpec(memory_space=pl.ANY)
    grid_spec = pltpu.PrefetchScalarGridSpec(
        num_scalar_prefetch=4,
        grid=(nblocks,),
        in_specs=[any_spec, any_spec, any_spec, any_spec],
        out_specs=pl.BlockSpec((MOE_BLOCK * nch, LANES), lambda i, *_: (i, 0)),
        scratch_shapes=[pltpu.VMEM((MOE_BLOCK * nch, LANES), F32), pltpu.VMEM((MOE_BLOCK * nch, LANES), F32),
                        pltpu.SemaphoreType.DMA((2,)),
                        pltpu.VMEM((d, de), F32), pltpu.VMEM((d, de), F32), pltpu.VMEM((de, d), F32),
                        pltpu.SemaphoreType.DMA((3,)),
                        pltpu.VMEM((d, de), BF), pltpu.VMEM((d, de), BF), pltpu.VMEM((de, d), BF)],
    )
    return pl.pallas_call(
        functools.partial(_expert_kernel, nch=nch),
        grid_spec=grid_spec,
        out_shape=jax.ShapeDtypeStruct((nrows * nch, LANES), F32),
        compiler_params=_cp(("arbitrary",), 48),
        name="expert_blocks",
    )(block_e, block_next, block_valid, row_tok, xf, w_eg, w_eu, w_ed)


def _final_kernel(pos_ref, h_ref, gate_ref, w_ref, yb_hbm, ya_ref, ys_ref, gbuf, sem, *, na, tm, nch):
    i = pl.program_id(0)
    n = pl.num_programs(0)
    slot = i % 2

    def row_copy(tile, slot_, r, k):
        p = pos_ref[(tile * tm + r) * TOP_K + k]
        src = yb_hbm.at[pl.ds(pl.multiple_of(p * nch, nch), nch), :]
        return pltpu.make_async_copy(src, gbuf.at[slot_, k, pl.ds(r * nch, nch), :], sem.at[slot_])

    def start_tile(tile, slot_):
        for r in range(tm):
            for k in range(TOP_K):
                row_copy(tile, slot_, r, k).start()

    @pl.when(i == 0)
    def _():
        start_tile(0, 0)

    @pl.when(i + 1 < n)
    def _():
        start_tile(i + 1, 1 - slot)

    for r in range(tm):
        for k in range(TOP_K):
            row_copy(i, slot, r, k).wait()

    gate = gate_ref[...]
    y = h_ref[...]
    ffn = _load_token_major(gbuf.at[slot, 0], tm, nch) * gate[:, 0:1]
    for k in range(1, TOP_K):
        ffn = ffn + _load_token_major(gbuf.at[slot, k], tm, nch) * gate[:, k:k + 1]
    y = y + ffn
    ms = jnp.mean(y * y, axis=-1, keepdims=True)
    out = y * lax.rsqrt(ms + EPS) * w_ref[...]

    @pl.when(i < na)
    def _():
        ya_ref[...] = out

    @pl.when(i >= na)
    def _():
        ys_ref[...] = out


def _final(pos_flat, h, gate, yb, w, rows_a, rows_b):
    t, d = h.shape
    tm = _pick(math.gcd(rows_a, rows_b), 128)
    na, nb = rows_a // tm, rows_b // tm
    grid_spec = pltpu.PrefetchScalarGridSpec(
        num_scalar_prefetch=1,
        grid=(na + nb,),
        in_specs=[pl.BlockSpec((tm, d), lambda i, pos: (i, 0)),
                  pl.BlockSpec((tm, TOP_K), lambda i, pos: (i, 0)),
                  pl.BlockSpec((1, d), lambda i, pos: (0, 0)),
                  pl.BlockSpec(memory_space=pl.ANY)],
        out_specs=[pl.BlockSpec((tm, d), lambda i, pos: (jnp.minimum(i, na - 1), 0)),
                   pl.BlockSpec((tm, d), lambda i, pos: (jnp.maximum(i - na, 0), 0))],
        scratch_shapes=[pltpu.VMEM((2, TOP_K, tm * (d // LANES), LANES), F32), pltpu.SemaphoreType.DMA((2,))],
    )
    return pl.pallas_call(
        functools.partial(_final_kernel, na=na, tm=tm, nch=d // LANES),
        grid_spec=grid_spec,
        out_shape=[jax.ShapeDtypeStruct((rows_a, d), F32), jax.ShapeDtypeStruct((rows_b, d), F32)],
        compiler_params=_cp(("arbitrary",)),
        name="combine_final_norm",
    )(pos_flat, h, gate, w.reshape(1, d), yb)


def kernel(x_prompt, x_sample, state_hgrn, state_gdn, state_conv, norm_mix, w_in, conv_w, ha_lb_logits, ha_onorm, w_pa, gd_A_log, gd_dt_bias, gd_onorm, w_pb, w_out, norm_ffn, w_router_group, b_router_group, w_router_expert, b_router_expert, w_exp_gate, w_exp_up, w_exp_down, norm_final):
    depth = state_hgrn.shape[0]
    assert depth == 1, "one decoder layer"
    bp, lp, d = x_prompt.shape
    bs, ls, _ = x_sample.shape
    _, _, ha_heads, ha_dk, ha_dv = state_hgrn.shape
    _, _, vheads, gd_dk, gd_dv = state_gdn.shape
    conv_dim = state_conv.shape[3]
    ha_kdim, ha_vdim = ha_heads * ha_dk, ha_heads * ha_dv
    gd_vdim = vheads * gd_dv
    ngroups = w_router_group.shape[2]
    nexp = w_router_expert.shape[2]
    rp, rs = bp * lp, bs * ls
    t = rp + rs

    off_f = ha_kdim
    off_i = off_f + ha_kdim
    off_g = off_i + ha_vdim
    off_qkv = off_g + ha_vdim
    off_z = off_qkv + conv_dim
    off_b = off_z + gd_vdim
    off_gate = off_b + 2 * vheads

    xn = _rmsnorm_bf16(x_prompt.reshape(rp, d), x_sample.reshape(rs, d), norm_mix[0])

    assert ha_kdim == ha_vdim and off_qkv % conv_dim == 0 and off_z % gd_vdim == 0
    w_bf = w_in[0].astype(BF)
    proj, k_a = _inproj(xn, w_bf, ha_lb_logits, off_b, ha_kdim, ha_dk ** -0.5)
    q_a, logf, v_a, g_a = ((proj, cb) for cb in (0, off_f // ha_kdim, off_i // ha_kdim, off_g // ha_kdim))
    k_a = (k_a, 0)
    qkv = (proj, off_qkv // conv_dim)
    z_b = (proj, off_z // gd_vdim)
    w_gates_bf = w_bf[:, off_gate:]
    bg, bgt = _beta_decay(xn, w_bf[:, off_b:off_gate], gd_A_log[0], gd_dt_bias[0])

    cg_p = _pick(lp, 128, GDN_CHUNK) if lp >= GDN_CHUNK else lp
    sub_p = min(HGRN_SUB, lp)
    chunk_p = min(GDN_CHUNK, lp)
    cg_s = ls
    zeros_h = jnp.zeros((bp, ha_heads, ha_dk, ha_dv), F32)
    zeros_g = jnp.zeros((bp, vheads, gd_dk, gd_dv), F32)
    zeros_c = jnp.zeros((bp, CONV_K - 1, conv_dim), F32)

    oa_p, sh_p = _hgrn2(q_a, logf, k_a, v_a, g_a, ha_onorm[0], zeros_h,
                        row0=0, seq_len=lp, rows_per_step=cg_p, sub=sub_p, seqs_per_step=1)
    oa_s, sh_s = _hgrn2(q_a, logf, k_a, v_a, g_a, ha_onorm[0], state_hgrn[0],
                        row0=rp, seq_len=ls, rows_per_step=cg_s, sub=min(HGRN_SUB, ls),
                        seqs_per_step=math.gcd(_pick(bs, HGRN_SAMPLE_SEQS, 1), max(rp // cg_s, 1)))

    def time_on_lanes(rows0, nrows, cg):
        part = bgt[:, rows0:rows0 + nrows].reshape(2 * vheads, nrows // cg, cg)
        return jnp.transpose(part, (1, 0, 2))

    seqs_s = _pick(bs, GDN_SAMPLE_SEQS, 1)
    cg_g = _pick(lp, GDN_ROWS, GDN_CHUNK) if lp >= GDN_CHUNK else lp
    bgt_p = time_on_lanes(0, rp, cg_g)
    bgt_s = time_on_lanes(rp, rs, seqs_s * cg_s)
    ob_p, sg_p, sc_p = _gdn(qkv, z_b, bg, bgt_p, conv_w[0], gd_onorm[0], zeros_g, zeros_c,
                            row0=0, seq_len=lp, rows_per_step=cg_g, chunk=chunk_p, seqs_per_step=1)
    ob_s, sg_s, sc_s = _gdn(qkv, z_b, bg, bgt_s, conv_w[0], gd_onorm[0], state_gdn[0],
                            state_conv[0], row0=rp, seq_len=ls, rows_per_step=cg_s, chunk=min(GDN_CHUNK, ls),
                            seqs_per_step=seqs_s)

    merged = _merge(xn, oa_p, oa_s, ob_p, ob_s, w_gates_bf, w_pa[0], w_pb[0])

    nr = LANES
    w_router = jnp.concatenate([w_router_group[0], w_router_expert[0]], axis=1)
    w_router = jnp.pad(w_router, ((0, 0), (0, nr - ngroups - nexp))).astype(BF)
    b_router = jnp.pad(jnp.concatenate([b_router_group[0], b_router_expert[0]]), (0, nr - ngroups - nexp))
    h, xf, logits = _outproj(merged, x_prompt.reshape(rp, d), x_sample.reshape(rs, d), w_out[0],
                             norm_ffn[0], w_router)

    eid, gate, rank, cnt = _route(logits, b_router.reshape(1, nr), ngroups, nexp)
    nblocks = -(-(t * TOP_K) // MOE_BLOCK) + nexp
    pos, block_meta = _plan(cnt, eid, rank, ngroups, nexp, nblocks)
    block_e, block_next, block_valid = (block_meta[:nblocks, j] for j in range(3))

    pos_flat = pos.reshape(-1)
    row_tok = _invert(pos_flat, nblocks * MOE_BLOCK)
    yb = _experts(block_e, block_next, block_valid, row_tok, xf, w_exp_gate[0], w_exp_up[0], w_exp_down[0])
    y_p, y_s = _final(pos_flat, h, gate, yb, norm_final, rp, rs)
    return (y_p.reshape(bp, lp, d), y_s.reshape(bs, ls, d),
            sh_p[None], sg_p[None], sc_p[None], sh_s[None], sg_s[None], sc_s[None])
```

```python
import functools
import math

import jax
import jax.numpy as jnp
from jax import lax
from jax.experimental import pallas as pl
from jax.experimental.pallas import tpu as pltpu

F32 = jnp.float32
BF = jnp.bfloat16
I32 = jnp.int32
EPS = 1e-6
CONV_K = 4
TOP_K = 2
MOE_BLOCK = 128
GDN_CHUNK = 64
GDN_ROWS = 128
GDN_SAMPLE_SEQS = 8
HGRN_SUB = 16
HGRN_SAMPLE_SEQS = 8
LOG2E = 1.4426950408889634
LANES = 128
NEG = -3.0e38
MIB = 1024 * 1024


def _cp(sem, vmem_mib=48):
    return pltpu.CompilerParams(dimension_semantics=sem, vmem_limit_bytes=vmem_mib * MIB)


def _pick(n, pref, mult=8):
    best = None
    for t in range(mult, min(n, pref) + 1, mult):
        if n % t == 0:
            best = t
    return best if best is not None else n


def _sigmoid(x):
    return 1.0 / (1.0 + jnp.exp(-x))


def _silu(x):
    return x * _sigmoid(x)


def _iota(shape, dim):
    return lax.broadcasted_iota(I32, shape, dim)


def _dot(a, b):
    return jnp.dot(a, b, preferred_element_type=F32)


def _dot_nt(a, b):
    return lax.dot_general(a, b, (((1,), (1,)), ((), ())), preferred_element_type=F32)


def _dot_tn(a, b):
    return lax.dot_general(a, b, (((0,), (0,)), ((), ())), preferred_element_type=F32)


def _split(a):
    hi = a.astype(BF)
    lo = (a - hi.astype(F32)).astype(BF)
    return hi, lo


def _split3(a, axis=0):
    hi = a.astype(BF)
    r1 = a - hi.astype(F32)
    mid = r1.astype(BF)
    lo = (r1 - mid.astype(F32)).astype(BF)
    return jnp.concatenate([hi, mid, lo], axis=axis)


def _two_source_specs(rows_a, rows_b, tm, width):
    na, nb = rows_a // tm, rows_b // tm
    spec_a = pl.BlockSpec((tm, width), lambda i: (jnp.minimum(i, na - 1), 0))
    spec_b = pl.BlockSpec((tm, width), lambda i: (jnp.maximum(i - na, 0), 0))
    return na, nb, spec_a, spec_b


def _rmsnorm_kernel(xa_ref, xb_ref, w_ref, o_ref, *, na):
    i = pl.program_id(0)

    def body(x_ref):
        x = x_ref[...]
        ms = jnp.mean(x * x, axis=-1, keepdims=True)
        o_ref[...] = (x * lax.rsqrt(ms + EPS) * w_ref[...]).astype(o_ref.dtype)

    @pl.when(i < na)
    def _():
        body(xa_ref)

    @pl.when(i >= na)
    def _():
        body(xb_ref)


def _rmsnorm_bf16(xa, xb, w):
    d = xa.shape[1]
    tm = _pick(math.gcd(xa.shape[0], xb.shape[0]), 512)
    na, nb, sa, sb = _two_source_specs(xa.shape[0], xb.shape[0], tm, d)
    return pl.pallas_call(
        functools.partial(_rmsnorm_kernel, na=na),
        grid=(na + nb,),
        in_specs=[sa, sb, pl.BlockSpec((1, d), lambda i: (0, 0))],
        out_specs=pl.BlockSpec((tm, d), lambda i: (i, 0)),
        out_shape=jax.ShapeDtypeStruct((xa.shape[0] + xb.shape[0], d), BF),
        compiler_params=_cp(("arbitrary",)),
        name="rmsnorm_mix",
    )(xa, xb, w.reshape(1, d))


def _inproj_kernel(x_ref, w_ref, lb_ref, p_ref, kf_ref, *, jq, jf0, jf1, q_scale):
    j = pl.program_id(0)
    acc = _dot(x_ref[...], w_ref[...])
    is_forget = (j >= jf0) & (j < jf1)

    @pl.when(is_forget)
    def _():
        logf, kf = _forget_epilogue(acc, lb_ref[...])
        p_ref[...] = logf
        kf_ref[...] = kf

    @pl.when(jnp.logical_not(is_forget))
    def _():
        p_ref[...] = acc * jnp.where(j < jq, q_scale, 1.0)


def _inproj(x_bf, w, lb_logits, ncols, kdim, q_scale):
    t, k = x_bf.shape
    tn = _pick(math.gcd(ncols, kdim), 1024, LANES)
    tm = _pick(t, 1024)
    ni = t // tm
    jq, jf0, jf1 = kdim // tn, kdim // tn, 2 * kdim // tn
    nf = jf1 - jf0
    fcol = lambda j: jnp.clip(j - jf0, 0, nf - 1)
    frow = lambda j, i: jnp.where(j < jf0, 0, jnp.where(j >= jf1, ni - 1, i))
    kern = functools.partial(_inproj_kernel, jq=jq, jf0=jf0, jf1=jf1, q_scale=q_scale)
    return pl.pallas_call(
        kern,
        grid=(ncols // tn, ni),
        in_specs=[pl.BlockSpec((tm, k), lambda j, i: (i, 0)),
                  pl.BlockSpec((k, tn), lambda j, i: (0, j)),
                  pl.BlockSpec((lb_logits.shape[0], tn), lambda j, i: (0, fcol(j)))],
        out_specs=[pl.BlockSpec((tm, tn), lambda j, i: (i, j)),
                   pl.BlockSpec((tm, tn), lambda j, i: (frow(j, i), fcol(j)))],
        out_shape=[jax.ShapeDtypeStruct((t, ncols), F32), jax.ShapeDtypeStruct((t, kdim), F32)],
        compiler_params=_cp(("arbitrary", "arbitrary")),
        name="in_proj",
    )(x_bf, w, lb_logits)


def _forget_epilogue(acc, lb_logits):
    m = jnp.max(lb_logits, axis=0, keepdims=True)
    e = jnp.exp(lb_logits - m)
    lb = e[0:1, :] / jnp.sum(e, axis=0, keepdims=True)
    f = lb + (1.0 - lb) * _sigmoid(acc)
    return jnp.log(f), 1.0 - f


def _beta_decay_kernel(x_ref, w_ref, wt_ref, prow_ref, pcol_ref, o_ref, ot_ref, *, vh):
    x = x_ref[...]
    acc = _dot(x, w_ref[...])
    acct = _dot_nt(wt_ref[...], x)

    def act(a, is_beta, a_neg_exp, dt_bias):
        z = a + dt_bias
        softplus = jnp.maximum(z, 0.0) + jnp.log(1.0 + jnp.exp(-jnp.abs(z)))
        return jnp.where(is_beta, _sigmoid(a), a_neg_exp * softplus)

    prow = prow_ref[...]
    pcol = pcol_ref[...]
    o_ref[...] = act(acc, _iota(acc.shape, 1) < vh, prow[0:1, :], prow[1:2, :])
    ot_ref[...] = act(acct, _iota(acct.shape, 0) < vh, pcol[:, 0:1], pcol[:, 1:2])


def _beta_decay(x_bf, w_ba, a_log, dt_bias):
    t, k = x_bf.shape
    vh = a_log.shape[0]
    tm = t if t <= 2048 else _pick(t, 1024, LANES)
    zeros = jnp.zeros((vh,), F32)
    prow = jnp.stack([jnp.concatenate([zeros, -jnp.exp(a_log)]), jnp.concatenate([zeros, dt_bias])])
    return pl.pallas_call(
        functools.partial(_beta_decay_kernel, vh=vh),
        grid=(t // tm,),
        in_specs=[pl.BlockSpec((tm, k), lambda i: (i, 0)),
                  pl.BlockSpec((k, 2 * vh), lambda i: (0, 0)),
                  pl.BlockSpec((2 * vh, k), lambda i: (0, 0)),
                  pl.BlockSpec((2, 2 * vh), lambda i: (0, 0)),
                  pl.BlockSpec((2 * vh, 2), lambda i: (0, 0))],
        out_specs=[pl.BlockSpec((tm, 2 * vh), lambda i: (i, 0)),
                   pl.BlockSpec((2 * vh, tm), lambda i: (0, i))],
        out_shape=[jax.ShapeDtypeStruct((t, 2 * vh), F32), jax.ShapeDtypeStruct((2 * vh, t), F32)],
        compiler_params=_cp(("arbitrary",)),
        name="proj_beta_decay",
    )(x_bf, w_ba.astype(BF), w_ba.T.astype(BF), prow, prow.T)


def _hgrn2_kernel(q_ref, lf_ref, k_ref, v_ref, g_ref, on_ref, s0_ref, o_ref, so_ref, s_scr, gall_scr,
                  g_scr, k_scr, v_scr, *, heads, dk, dv, sub, nsub, nseq, nchunks, group, unroll):
    c = pl.program_id(1)
    cg = sub * nsub

    @pl.when(c == 0)
    def _():
        s_scr[...] = s0_ref[...]

    rows_all = nseq * cg
    shift = sub.bit_length() - 1
    r_i = _iota((rows_all, 3 * rows_all), 0)
    c_i = _iota((rows_all, 3 * rows_all), 1) & (rows_all - 1)
    same = lax.shift_right_logical(r_i, shift) == lax.shift_right_logical(c_i, shift)
    btril3 = jnp.where((r_i >= c_i) & same, 1.0, 0.0).astype(BF)
    gall_scr[...] = _dot(btril3, _split3(lf_ref[...])) * LOG2E
    onorm = on_ref[...]
    n_aug = 16 - sub % 16
    aug_rhs = jnp.concatenate([jnp.zeros((n_aug, dv), BF), jnp.ones((n_aug, dv), BF)], axis=1)
    zeros_v = jnp.zeros((sub, dv), BF)
    zeros_aug = jnp.zeros((n_aug - 3, dk), F32)

    nh = sub // 8
    row8 = _iota((8, 1), 0)

    def block(it, carry):
        for u in range(unroll):
            sub_chunk(it * unroll + u, g_scr.at[u], k_scr.at[u], v_scr.at[u])
        return carry

    def sub_chunk(sb, g_scr, k_scr, v_scr):
        for q in range(nseq):
            rows = pl.ds(pl.multiple_of(q * cg + sb * sub, sub), sub)
            g_scr[q] = gall_scr[rows, :]
            k_scr[q] = k_ref[rows, :]
            v_scr[q] = v_ref[rows, :]
        problems = [(q, h) for q in range(nseq) for h in range(heads)]
        for g0 in range(0, len(problems), group):
            ps = []
            for (q, h) in problems[g0:g0 + group]:
                r0 = pl.multiple_of(q * cg + sb * sub, sub)
                rows = pl.ds(r0, sub)
                kc = pl.ds(h * dk, dk)
                vc = pl.ds(h * dv, dv)
                ps.append(dict(q=q, h=h, r0=r0, rows=rows, kc=kc, vc=vc,
                               gc=g_scr[q, :, kc],
                               qv=q_ref[rows, kc], k=k_ref[rows, kc], v=v_ref[rows, vc]))
            for p in ps:
                p["s_old"] = s_scr[p["q"], p["h"]]
                p["o_state"] = _dot((p["qv"] * jnp.exp2(p["gc"])).astype(BF), p["s_old"].astype(BF))
                p["o8"] = [jnp.zeros((8, dv), F32) for _ in range(nh)]
                p["q8"] = [p["qv"][8 * i:8 * i + 8] for i in range(nh)]
                p["g8"] = [p["gc"][8 * i:8 * i + 8] for i in range(nh)]
            for j in range(sub):
                for p in ps:
                    kj = k_scr[p["q"], pl.ds(j, 1), p["kc"]]
                    vj = v_scr[p["q"], pl.ds(j, 1), p["vc"]]
                    gj = g_scr[p["q"], pl.ds(j, 1), p["kc"]]
                    for i in range(j // 8, nh):
                        pr = (p["q8"][i] * kj) * jnp.exp2(p["g8"][i] - gj)
                        a_col = jnp.sum(pr, axis=-1, keepdims=True)
                        if i == j // 8:
                            a_col = jnp.where(row8 >= j % 8, a_col, 0.0)
                        p["o8"][i] = p["o8"][i] + a_col * vj
            for p in ps:
                p["o"] = p["o_state"] + (jnp.concatenate(p["o8"], axis=0) if nh > 1 else p["o8"][0])
            for p in ps:
                gend = p["gc"][sub - 1:sub, :]
                dend = jnp.exp2(gend)
                d_hi = dend.astype(BF).astype(F32)
                d_mid = (dend - d_hi).astype(BF).astype(F32)
                d_lo = (dend - d_hi - d_mid).astype(BF).astype(F32)
                kdec = p["k"] * jnp.exp2(gend - p["gc"])
                lhs = jnp.concatenate([kdec, d_hi, d_mid, d_lo, zeros_aug], axis=0).astype(BF)
                rhs = jnp.concatenate([jnp.concatenate([p["v"].astype(BF), zeros_v], axis=1), aug_rhs], axis=0)
                p["kv"] = _dot_tn(lhs, rhs)
            for p in ps:
                s_scr[p["q"], p["h"]] = p["kv"][:, dv:] * p["s_old"] + p["kv"][:, :dv]
            for p in ps:
                o = p["o"]
                ms = jnp.mean(o * o, axis=-1, keepdims=True)
                o_ref[p["rows"], p["vc"]] = (o * lax.rsqrt(ms + EPS) * onorm) * _silu(g_ref[p["rows"], p["vc"]])

    lax.fori_loop(0, nsub // unroll, block, 0)

    @pl.when(c == nchunks - 1)
    def _():
        so_ref[...] = s_scr[...]


def _hgrn2(q, lf, kf, v, gate, onorm, s0, *, row0, seq_len, rows_per_step, sub, seqs_per_step):
    nseq_total, heads, dk, dv = s0.shape
    assert dk == dv, "state decay is applied with a (dk, dv) tile"
    width = heads * dk
    cg = rows_per_step
    nseq = seqs_per_step
    nchunks = seq_len // cg
    assert nseq == 1 or nchunks == 1, "several sequences per step only when a step covers whole sequences"
    rows = nseq * cg
    b0 = row0 // rows
    assert sub & (sub - 1) == 0 and sub % 8 == 0 and rows & (rows - 1) == 0
    unroll = next(u for u in (8, 4, 2, 1) if (cg // sub) % u == 0)
    def row_spec(col_block):
        return pl.BlockSpec((rows, width), lambda b, c: (b0 + b * nchunks + c, col_block))

    st_spec = pl.BlockSpec((nseq, heads, dk, dv), lambda b, c: (b, 0, 0, 0))
    kern = functools.partial(_hgrn2_kernel, heads=heads, dk=dk, dv=dv, sub=sub, nsub=cg // sub, nseq=nseq,
                             nchunks=nchunks, group=8, unroll=unroll)
    operands = (q, lf, kf, v, gate)
    return pl.pallas_call(
        kern,
        grid=(nseq_total // nseq, nchunks),
        in_specs=[row_spec(cb) for _, cb in operands]
                 + [pl.BlockSpec((1, dv), lambda b, c: (0, 0)), st_spec],
        out_specs=[pl.BlockSpec((rows, heads * dv), lambda b, c: (b * nchunks + c, 0)), st_spec],
        out_shape=[jax.ShapeDtypeStruct((nseq_total * seq_len, heads * dv), F32),
                   jax.ShapeDtypeStruct(s0.shape, F32)],
        scratch_shapes=[pltpu.VMEM((nseq, heads, dk, dv), F32), pltpu.VMEM((rows, width), F32),
                        pltpu.VMEM((unroll, nseq, sub, width), F32), pltpu.VMEM((unroll, nseq, sub, width), F32),
                        pltpu.VMEM((unroll, nseq, sub, heads * dv), F32)],
        compiler_params=_cp(("arbitrary", "arbitrary")),
        name="hgrn2_recurrence",
    )(*[a for a, _ in operands], onorm.reshape(1, dv), s0)


def _gdn_kernel(qkv_ref, z_ref, bg_ref, bgt_ref, cw_ref, on_ref, s0_ref, c0_ref,
                o_ref, so_ref, co_ref, s_scr, xx, xs,
                *, kheads, vheads, dk, dv, chunk, nsub, nseq, nchunks, group_kh):
    c = pl.program_id(1)
    cg = chunk * nsub
    kdim = kheads * dk
    rep = vheads // kheads
    tail0 = 8 - (CONV_K - 1)

    @pl.when(c == 0)
    def _():
        s_scr[...] = s0_ref[...]
        xx[:, 0:8, :] = c0_ref[...]

    for q in range(nseq):
        xx[q, 8:8 + cg, :] = qkv_ref[q * cg:(q + 1) * cg, :]
        conv = xx[q, pl.ds(tail0, cg), :] * cw_ref[0:1, :]
        for j in range(1, CONV_K):
            conv = conv + xx[q, pl.ds(tail0 + j, cg), :] * cw_ref[j:j + 1, :]
        xs[q] = _silu(conv)

    @pl.when(c == nchunks - 1)
    def _():
        for q in range(nseq):
            co_ref[q] = xx[q, pl.ds(8 + cg - (CONV_K - 1), CONV_K - 1), :]

    for q in range(nseq):
        xx[q, 0:8, :] = xx[q, cg:cg + 8, :]

    wp = chunk
    r_i = _iota((chunk, wp), 0)
    c_i = _iota((chunk, wp), 1)
    causal = r_i >= c_i
    strict = r_i > c_i

    def pad_rows(a):
        if a.shape[0] == wp:
            return a
        if a.dtype == BF and a.shape[0] % 16:
            return pad_rows(a.astype(F32)).astype(BF)
        return jnp.concatenate([a, jnp.zeros((wp - a.shape[0],) + a.shape[1:], a.dtype)], axis=0)

    def split_lhs(a):
        a_hi = a.astype(BF).astype(F32)
        return jnp.concatenate([a_hi, a_hi, a - a_hi], axis=1).astype(BF)

    def split_rhs(b):
        bh, bl = (pad_rows(t) for t in _split(b))
        return jnp.concatenate([bh, bl, bh], axis=0)
    cm = chunk - 1
    tril3 = jnp.where(_iota((chunk, 3 * chunk), 0) >= (_iota((chunk, 3 * chunk), 1) & cm), 1.0, 0.0).astype(BF)
    t_r = _iota((3 * chunk, wp), 0) & cm
    t_c = _iota((3 * chunk, wp), 1)
    triu3 = jnp.where((t_r <= t_c) & (t_c < chunk), 1.0, 0.0).astype(BF)
    onorm = on_ref[...]

    cums = {}

    def cumulative_decay(q, s):
        if (q, s) not in cums:
            r0 = q * cg + s * chunk
            bg = bg_ref[r0:r0 + chunk, :]
            gt = bgt_ref[0, vheads:2 * vheads, r0:r0 + chunk]
            cums[(q, s)] = (bg, _dot(tril3, _split3(bg[:, vheads:2 * vheads], axis=0)),
                            _dot(_split3(gt, axis=1), triu3))
        return cums[(q, s)]

    def run_group(items):
        probs = []
        for (q, s, kh) in items:
            r0 = q * cg + s * chunk
            x0 = s * chunk
            bg, gcol_all, grow_all = cumulative_decay(q, s)
            for kh in (kh,):
                qh = xs[q, x0:x0 + chunk, kh * dk:(kh + 1) * dk]
                kk_ = xs[q, x0:x0 + chunk, kdim + kh * dk:kdim + (kh + 1) * dk]
                qn = qh * lax.rsqrt(jnp.sum(qh * qh, axis=-1, keepdims=True) + EPS) * (dk ** -0.5)
                kn = kk_ * lax.rsqrt(jnp.sum(kk_ * kk_, axis=-1, keepdims=True) + EPS)
                kn_bf = kn.astype(BF)
                kn_pad = pad_rows(kn_bf)
                kk = _dot_nt(kn_bf, kn_pad)
                qk = _dot_nt(qn.astype(BF), kn_pad)
                for r in range(rep):
                    h = kh * rep + r
                    vh_ = xs[q, x0:x0 + chunk, 2 * kdim + h * dv:2 * kdim + (h + 1) * dv]
                    gcol = gcol_all[:, h:h + 1]
                    grow = grow_all[h:h + 1, :]
                    beta = bg[:, h:h + 1]
                    decay = jnp.where(causal, jnp.exp(jnp.minimum(gcol - grow, 0.0)), 0.0)
                    eg = jnp.exp(gcol)
                    gend = gcol[chunk - 1:chunk, :]
                    probs.append(dict(
                        q=q, s=s, h=h, r0=r0,
                        x=jnp.where(strict, -(beta * kk * decay), 0.0),
                        y=jnp.concatenate([vh_ * beta, kn * (beta * eg)], axis=1),
                        a_bf=(qk * decay).astype(BF),
                        qg=qn * eg,
                        kdec_bf=(kn * jnp.exp(gend - gcol)).astype(BF),
                        send=jnp.exp(gend)))

        if chunk <= 16:
            for j in range(chunk - 1):
                for p in probs:
                    p["y"] = p["y"] + p["x"][:, j:j + 1] * p["y"][j:j + 1, :]
        else:
            n_fac = int(math.log2(chunk))
            for k in range(n_fac):
                for p in probs:
                    p["x_lhs"] = split_lhs(p["x"])
                    p["y"] = p["y"] + _dot(p["x_lhs"], split_rhs(p["y"]))
                if k + 1 < n_fac:
                    for p in probs:
                        p["x"] = _dot(p["x_lhs"], split_rhs(p["x"]))

        for s in sorted({p["s"] for p in probs}):
            cur = [p for p in probs if p["s"] == s]
            for p in cur:
                p["s_old"] = s_scr[p["q"], p["h"]]
                lhs = jnp.concatenate([p["y"][:, dv:], p["qg"]], axis=0).astype(BF)
                p["ws"] = _dot(lhs, p["s_old"].astype(BF))
            for p in cur:
                p["u_bf"] = (p["y"][:, :dv] - p["ws"][:chunk]).astype(BF)
            for p in cur:
                s_scr[p["q"], p["h"]] = p["send"] * p["s_old"] + _dot_tn(p["kdec_bf"], p["u_bf"])
            for p in cur:
                o = p["ws"][chunk:] + _dot(p["a_bf"], pad_rows(p["u_bf"]))
                ms = jnp.mean(o * o, axis=-1, keepdims=True)
                zc = slice(p["h"] * dv, (p["h"] + 1) * dv)
                rows = slice(p["r0"], p["r0"] + chunk)
                o_ref[rows, zc] = (o * lax.rsqrt(ms + EPS) * onorm) * _silu(z_ref[rows, zc])

    for g0 in range(0, kheads, group_kh):
        run_group([(q, s, kh) for q in range(nseq) for s in range(nsub)
                   for kh in range(g0, min(g0 + group_kh, kheads))])

    @pl.when(c == nchunks - 1)
    def _():
        so_ref[...] = s_scr[...]


def _gdn(qkv, z, bg, bgt3, conv_w, onorm, s0, conv0, *, row0, seq_len, rows_per_step, chunk, seqs_per_step):
    nseq_total, vheads, dk, dv = s0.shape
    (qkv, qkv_cb), (z, z_cb) = qkv, z
    conv_dim = conv_w.shape[1]
    kheads = (conv_dim - vheads * dv) // (2 * dk)
    cg = rows_per_step
    nseq = seqs_per_step
    nchunks = seq_len // cg
    assert nseq == 1 or nchunks == 1, "several sequences per step only when a step covers whole sequences"
    rows = nseq * cg
    b0 = row0 // rows
    conv0p = jnp.pad(conv0, ((0, 0), (8 - (CONV_K - 1), 0), (0, 0)))
    rmap = lambda b, c: (b0 + b * nchunks + c, 0)
    st_spec = pl.BlockSpec((nseq, vheads, dk, dv), lambda b, c: (b, 0, 0, 0))
    kern = functools.partial(_gdn_kernel, kheads=kheads, vheads=vheads, dk=dk, dv=dv, chunk=chunk,
                             nsub=cg // chunk, nseq=nseq, nchunks=nchunks, group_kh=kheads)
    return pl.pallas_call(
        kern,
        grid=(nseq_total // nseq, nchunks),
        in_specs=[pl.BlockSpec((rows, conv_dim), lambda b, c: (b0 + b * nchunks + c, qkv_cb)),
                  pl.BlockSpec((rows, vheads * dv), lambda b, c: (b0 + b * nchunks + c, z_cb)),
                  pl.BlockSpec((rows, 2 * vheads), rmap),
                  pl.BlockSpec((1, 2 * vheads, rows), lambda b, c: (b * nchunks + c, 0, 0)),
                  pl.BlockSpec((CONV_K, conv_dim), lambda b, c: (0, 0)),
                  pl.BlockSpec((1, dv), lambda b, c: (0, 0)),
                  st_spec,
                  pl.BlockSpec((nseq, 8, conv_dim), lambda b, c: (b, 0, 0))],
        out_specs=[pl.BlockSpec((rows, vheads * dv), lambda b, c: (b * nchunks + c, 0)),
                   st_spec,
                   pl.BlockSpec((nseq, CONV_K - 1, conv_dim), lambda b, c: (b, 0, 0))],
        out_shape=[jax.ShapeDtypeStruct((nseq_total * seq_len, vheads * dv), F32),
                   jax.ShapeDtypeStruct(s0.shape, F32),
                   jax.ShapeDtypeStruct((nseq_total, CONV_K - 1, conv_dim), F32)],
        scratch_shapes=[pltpu.VMEM((nseq, vheads, dk, dv), F32), pltpu.VMEM((nseq, 8 + cg, conv_dim), F32),
                        pltpu.VMEM((nseq, cg, conv_dim), F32)],
        compiler_params=_cp(("arbitrary", "arbitrary")),
        name="gdn_recurrence",
    )(qkv, z, bg, bgt3, conv_w, onorm.reshape(1, dv), s0, conv0p)


def _merge_kernel(xn_ref, oa1_ref, oa2_ref, ob1_ref, ob2_ref, wga_ref, wgb_ref, wa_ref, wb_ref, o_ref,
                  oa_bf, ob_bf, *, na):
    i = pl.program_id(0)
    j = pl.program_id(1)

    @pl.when((j == 0) & (i < na))
    def _():
        oa_bf[...] = oa1_ref[...].astype(BF)
        ob_bf[...] = ob1_ref[...].astype(BF)

    @pl.when((j == 0) & (i >= na))
    def _():
        oa_bf[...] = oa2_ref[...].astype(BF)
        ob_bf[...] = ob2_ref[...].astype(BF)

    xn = xn_ref[...]
    ya = _dot(oa_bf[...], wa_ref[...])
    yb = _dot(ob_bf[...], wb_ref[...])
    ga = _dot(xn, wga_ref[...])
    gb = _dot(xn, wgb_ref[...])
    o_ref[...] = (_sigmoid(ga) * ya + _sigmoid(gb) * yb).astype(o_ref.dtype)


def _merge(xn, oa_p, oa_s, ob_p, ob_s, w_gates_bf, w_pa, w_pb):
    rp, rs = oa_p.shape[0], oa_s.shape[0]
    ka, kb = oa_p.shape[1], ob_p.shape[1]
    d = w_pa.shape[1]
    k = xn.shape[1]
    tm = _pick(math.gcd(rp, rs), 512)
    tn = _pick(d, 512, LANES)
    na, nb = rp // tm, rs // tm
    nj = d // tn
    amap = lambda i, j: (jnp.minimum(i, na - 1), 0)
    bmap = lambda i, j: (jnp.maximum(i - na, 0), 0)
    return pl.pallas_call(
        functools.partial(_merge_kernel, na=na),
        grid=(na + nb, nj),
        in_specs=[pl.BlockSpec((tm, k), lambda i, j: (i, 0)),
                  pl.BlockSpec((tm, ka), amap), pl.BlockSpec((tm, ka), bmap),
                  pl.BlockSpec((tm, kb), amap), pl.BlockSpec((tm, kb), bmap),
                  pl.BlockSpec((k, tn), lambda i, j: (0, j)),
                  pl.BlockSpec((k, tn), lambda i, j: (0, nj + j)),
                  pl.BlockSpec((ka, tn), lambda i, j: (0, j)),
                  pl.BlockSpec((kb, tn), lambda i, j: (0, j))],
        out_specs=pl.BlockSpec((tm, tn), lambda i, j: (i, j)),
        out_shape=jax.ShapeDtypeStruct((rp + rs, d), BF),
        scratch_shapes=[pltpu.VMEM((tm, ka), BF), pltpu.VMEM((tm, kb), BF)],
        compiler_params=_cp(("arbitrary", "arbitrary")),
        name="gates_branch_proj_merge",
    )(xn, oa_p, oa_s, ob_p, ob_s, w_gates_bf, w_gates_bf, w_pa.astype(BF), w_pb.astype(BF))


def _outproj_kernel(m_ref, xa_ref, xb_ref, wo_ref, nw_ref, wr_ref, h_ref, xf_ref, lg_ref, *, na):
    i = pl.program_id(0)

    def body(x_ref):
        h = x_ref[...] + _dot(m_ref[...], wo_ref[...])
        h_ref[...] = h
        ms = jnp.mean(h * h, axis=-1, keepdims=True)
        xf = h * lax.rsqrt(ms + EPS) * nw_ref[...]
        xf_ref[...] = xf
        lg_ref[...] = _dot(xf.astype(BF), wr_ref[...])

    @pl.when(i < na)
    def _():
        body(xa_ref)

    @pl.when(i >= na)
    def _():
        body(xb_ref)


def _outproj(merged, xa, xb, w_out, norm_ffn, w_router):
    t, d = merged.shape
    tm = _pick(math.gcd(xa.shape[0], xb.shape[0]), 512)
    na, nb, sa, sb = _two_source_specs(xa.shape[0], xb.shape[0], tm, d)
    nr = w_router.shape[1]
    const = lambda i: (0, 0)
    return pl.pallas_call(
        functools.partial(_outproj_kernel, na=na),
        grid=(na + nb,),
        in_specs=[pl.BlockSpec((tm, d), lambda i: (i, 0)), sa, sb,
                  pl.BlockSpec((d, d), const, pipeline_mode=pl.Buffered(1)),
                  pl.BlockSpec((1, d), const),
                  pl.BlockSpec((d, nr), const, pipeline_mode=pl.Buffered(1))],
        out_specs=[pl.BlockSpec((tm, d), lambda i: (i, 0)),
                   pl.BlockSpec((tm, d), lambda i: (i, 0)),
                   pl.BlockSpec((tm, nr), lambda i: (i, 0))],
        out_shape=[jax.ShapeDtypeStruct((t, d), F32), jax.ShapeDtypeStruct((t, d), F32),
                   jax.ShapeDtypeStruct((t, nr), F32)],
        compiler_params=_cp(("arbitrary",), 56),
        name="out_proj_ffn_norm",
    )(merged, xa, xb, w_out.astype(BF), norm_ffn.reshape(1, d), w_router)


def _route_kernel(lg_ref, b_ref, eid_ref, gate_ref, rank_ref, cnt_ref, carry,
                  *, ngroups, nexp, tm):
    i = pl.program_id(0)

    @pl.when(i == 0)
    def _():
        carry[...] = jnp.zeros_like(carry)

    per_group = nexp // ngroups
    lg = lg_ref[...] + b_ref[...]
    lane = _iota(lg.shape, 1)
    big = jnp.int32(1 << 20)
    is_g = lane < ngroups
    gl = jnp.where(is_g, lg, NEG)
    gmax = jnp.max(gl, axis=-1, keepdims=True)
    gidx = jnp.min(jnp.where(gl == gmax, lane, big), axis=-1, keepdims=True)
    gsum = jnp.sum(jnp.where(is_g, jnp.exp(gl - gmax), 0.0), axis=-1, keepdims=True)
    gw = 1.0 / gsum
    elane = lane - ngroups
    in_grp = (elane >= gidx * per_group) & (elane < (gidx + 1) * per_group)
    el = jnp.where(in_grp, lg, NEG)
    v1 = jnp.max(el, axis=-1, keepdims=True)
    i1 = jnp.min(jnp.where(in_grp & (el == v1), elane, big), axis=-1, keepdims=True)
    in2 = in_grp & (elane != i1)
    el2 = jnp.where(in2, lg, NEG)
    v2 = jnp.max(el2, axis=-1, keepdims=True)
    i2 = jnp.min(jnp.where(in2 & (el2 == v2), elane, big), axis=-1, keepdims=True)
    p2 = jnp.exp(v2 - v1)
    den = 1.0 + p2
    lane2 = _iota((tm, TOP_K), 1)
    eid_ref[...] = jnp.where(lane2 == 0, i1, i2)
    gate_ref[...] = jnp.where(lane2 == 0, gw / den, gw * p2 / den)

    oh1 = (elane == i1).astype(F32)
    oh2 = (elane == i2).astype(F32)
    lower = (_iota((tm, tm), 0) > _iota((tm, tm), 1)).astype(BF)
    cs1 = _dot(lower, oh1.astype(BF))
    cs2 = _dot(lower, oh2.astype(BF))
    tot1 = jnp.sum(oh1, axis=0, keepdims=True)
    tot2 = jnp.sum(oh2, axis=0, keepdims=True)
    base = carry[0:1, :]
    r1 = jnp.sum(oh1 * (base + cs1), axis=-1, keepdims=True)
    r2 = jnp.sum(oh2 * (base + tot1 + cs2), axis=-1, keepdims=True)
    rank_ref[...] = jnp.where(lane2 == 0, r1, r2).astype(I32)
    new = base + tot1 + tot2
    carry[...] = jnp.broadcast_to(new, carry.shape)
    cnt_ref[...] = jnp.broadcast_to(new, cnt_ref.shape)


def _route(logits, bias_row, ngroups, nexp):
    t, nr = logits.shape
    tm = _pick(t, 512)
    kern = functools.partial(_route_kernel, ngroups=ngroups, nexp=nexp, tm=tm)
    return pl.pallas_call(
        kern,
        grid=(t // tm,),
        in_specs=[pl.BlockSpec((tm, nr), lambda i: (i, 0)), pl.BlockSpec((1, nr), lambda i: (0, 0))],
        out_specs=[pl.BlockSpec((tm, TOP_K), lambda i: (i, 0)),
                   pl.BlockSpec((tm, TOP_K), lambda i: (i, 0)),
                   pl.BlockSpec((tm, TOP_K), lambda i: (i, 0)),
                   pl.BlockSpec((8, nr), lambda i: (0, 0))],
        out_shape=[jax.ShapeDtypeStruct((t, TOP_K), I32), jax.ShapeDtypeStruct((t, TOP_K), F32),
                   jax.ShapeDtypeStruct((t, TOP_K), I32), jax.ShapeDtypeStruct((8, nr), F32)],
        scratch_shapes=[pltpu.VMEM((8, nr), F32)],
        compiler_params=_cp(("arbitrary",)),
        name="route_topk_rank",
    )(logits, bias_row)


def _plan_kernel(cnt_ref, eid_ref, rank_ref, pos_ref, be_ref, *, ngroups, nexp, nblocks_pad, tm):
    nr = cnt_ref.shape[1]
    cnt = cnt_ref[0:1, :]
    padded = jnp.floor((cnt + (MOE_BLOCK - 1)) * (1.0 / MOE_BLOCK)) * MOE_BLOCK
    r_i = _iota((nr, nr), 0)
    c_i = _iota((nr, nr), 1)
    padded_col = jnp.sum(jnp.where(r_i == c_i, jnp.broadcast_to(padded, (nr, nr)), 0.0), axis=1, keepdims=True)
    start = jnp.sum(jnp.where(r_i < c_i, jnp.broadcast_to(padded_col, (nr, nr)), 0.0), axis=0, keepdims=True)
    end = start + padded
    lane = _iota((tm, nr), 1)
    eid = eid_ref[...]
    rank = rank_ref[...]
    lane2 = _iota((tm, TOP_K), 1)
    pos = jnp.zeros((tm, TOP_K), F32)
    for k in range(TOP_K):
        oh = (lane - ngroups) == eid[:, k:k + 1]
        st = jnp.sum(jnp.where(oh, start, 0.0), axis=-1, keepdims=True)
        pos = jnp.where(lane2 == k, st, pos)
    pos_ref[...] = pos.astype(I32) + rank

    @pl.when(pl.program_id(0) == 0)
    def _():
        blk_row = _iota((nblocks_pad, nr), 0).astype(F32) * MOE_BLOCK
        lane_b = _iota((nblocks_pad, nr), 1)
        is_e = (lane_b >= ngroups) & (lane_b < ngroups + nexp)
        n_le = jnp.sum(jnp.where(is_e & (end <= blk_row), 1.0, 0.0), axis=-1, keepdims=True)
        e_lane = (lane_b - ngroups).astype(F32)
        nonempty = is_e & (padded > 0.0)
        e_last = jnp.max(jnp.where(nonempty, e_lane, -1.0), axis=-1, keepdims=True)
        be = jnp.minimum(n_le, e_last)
        nxt = jnp.min(jnp.where(nonempty & (e_lane > be), e_lane, 1e9), axis=-1, keepdims=True)
        nxt = jnp.where(nxt > 1e8, -1.0, nxt)
        total = jnp.sum(padded, axis=-1, keepdims=True)
        valid = jnp.where(blk_row[:, 0:1] < total, 1.0, 0.0)
        col = _iota((nblocks_pad, 4), 1)
        meta = jnp.where(col == 0, be, jnp.where(col == 1, nxt, jnp.where(col == 2, valid, 0.0)))
        be_ref[...] = meta.astype(I32)


def _plan(cnt, eid, rank, ngroups, nexp, nblocks):
    t = eid.shape[0]
    nr = cnt.shape[1]
    tm = _pick(t, 512)
    nblocks_pad = -(-nblocks // 8) * 8
    kern = functools.partial(_plan_kernel, ngroups=ngroups, nexp=nexp, nblocks_pad=nblocks_pad, tm=tm)
    return pl.pallas_call(
        kern,
        grid=(t // tm,),
        in_specs=[pl.BlockSpec((8, nr), lambda i: (0, 0)),
                  pl.BlockSpec((tm, TOP_K), lambda i: (i, 0)),
                  pl.BlockSpec((tm, TOP_K), lambda i: (i, 0))],
        out_specs=[pl.BlockSpec((tm, TOP_K), lambda i: (i, 0)),
                   pl.BlockSpec((nblocks_pad, 4), lambda i: (0, 0))],
        out_shape=[jax.ShapeDtypeStruct((t, TOP_K), I32), jax.ShapeDtypeStruct((nblocks_pad, 4), I32)],
        compiler_params=_cp(("arbitrary",)),
        name="route_plan",
    )(cnt, eid, rank)


def _invert_kernel(pos_ref, tok_ref, *, n_assign, n_rows):
    def clear(r, c):
        tok_ref[r] = 0
        return c

    lax.fori_loop(0, n_rows, clear, 0, unroll=16)

    def put(a, c):
        tok_ref[pos_ref[a]] = lax.shift_right_logical(a, TOP_K.bit_length() - 1)
        return c

    lax.fori_loop(0, n_assign, put, 0, unroll=16)


def _invert(pos_flat, n_rows):
    n_assign = pos_flat.shape[0]
    return pl.pallas_call(
        functools.partial(_invert_kernel, n_assign=n_assign, n_rows=n_rows),
        in_specs=[pl.BlockSpec(memory_space=pltpu.SMEM)],
        out_specs=pl.BlockSpec(memory_space=pltpu.SMEM),
        out_shape=jax.ShapeDtypeStruct((n_rows,), I32),
        name="route_invert",
    )(pos_flat)


def _expert_kernel(be_ref, nxt_ref, valid_ref, tok_ref, x_hbm, wg_hbm, wu_hbm, wd_hbm, y_ref,
                   xbuf0, xbuf1, xsem, wg_st, wu_st, wd_st, wsem, wg_bf, wu_bf, wd_bf):
    i = pl.program_id(0)
    n = pl.num_programs(0)
    xbufs = (xbuf0, xbuf1)

    def row_copy(blk, s, r):
        t = tok_ref[blk * MOE_BLOCK + r]
        return pltpu.make_async_copy(x_hbm.at[pl.ds(t, 1), :], xbufs[s].at[pl.ds(r, 1), :], xsem.at[s])

    def block_wait(s):
        pltpu.make_async_copy(x_hbm.at[pl.ds(0, MOE_BLOCK), :], xbufs[s], xsem.at[s]).wait()

    def weight_copies(e):
        return (pltpu.make_async_copy(wg_hbm.at[e], wg_st, wsem.at[0]),
                pltpu.make_async_copy(wu_hbm.at[e], wu_st, wsem.at[1]),
                pltpu.make_async_copy(wd_hbm.at[e], wd_st, wsem.at[2]))

    @pl.when(i == 0)
    def _():
        for r in range(MOE_BLOCK):
            row_copy(0, 0, r).start()
        for cp in weight_copies(be_ref[0]):
            cp.start(priority=1)

    prev = be_ref[jnp.maximum(i - 1, 0)]

    @pl.when((i == 0) | (be_ref[i] != prev))
    def _():
        for cp in weight_copies(be_ref[i]):
            cp.wait()
        wg_bf[...] = wg_st[...].astype(BF)
        wu_bf[...] = wu_st[...].astype(BF)
        wd_bf[...] = wd_st[...].astype(BF)

        @pl.when(nxt_ref[i] >= 0)
        def _():
            for cp in weight_copies(nxt_ref[i]):
                cp.start(priority=1)

    ahead = jnp.minimum(i + 1, n - 1)
    used = valid_ref[i] != 0
    used_ahead = (i + 1 < n) & (valid_ref[ahead] != 0)

    def step(cur, oth):
        @pl.when(used_ahead)
        def _():
            for r in range(MOE_BLOCK):
                row_copy(ahead, oth, r).start()

        block_wait(cur)
        x = xbufs[cur][...].astype(BF)
        hid = (_silu(_dot(x, wg_bf[...])) * _dot(x, wu_bf[...])).astype(BF)
        y_ref[...] = _dot(hid, wd_bf[...])

    @pl.when(used & (i % 2 == 0))
    def _():
        step(0, 1)

    @pl.when(used & (i % 2 == 1))
    def _():
        step(1, 0)

    @pl.when(jnp.logical_not(used))
    def _():
        y_ref[...] = jnp.zeros_like(y_ref)


def _experts(block_e, block_next, block_valid, row_tok, xf, w_eg, w_eu, w_ed):
    nrows = row_tok.shape[0]
    d = w_eg.shape[1]
    nblocks = nrows // MOE_BLOCK
    de = w_eg.shape[2]
    any_spec = pl.BlockSpec(memory_space=pl.ANY)
    grid_spec = pltpu.PrefetchScalarGridSpec(
        num_scalar_prefetch=4,
        grid=(nblocks,),
        in_specs=[any_spec, any_spec, any_spec, any_spec],
        out_specs=pl.BlockSpec((MOE_BLOCK, d), lambda i, *_: (i, 0)),
        scratch_shapes=[pltpu.VMEM((MOE_BLOCK, d), F32), pltpu.VMEM((MOE_BLOCK, d), F32),
                        pltpu.SemaphoreType.DMA((2,)),
                        pltpu.VMEM((d, de), F32), pltpu.VMEM((d, de), F32), pltpu.VMEM((de, d), F32),
                        pltpu.SemaphoreType.DMA((3,)),
                        pltpu.VMEM((d, de), BF), pltpu.VMEM((d, de), BF), pltpu.VMEM((de, d), BF)],
    )
    return pl.pallas_call(
        _expert_kernel,
        grid_spec=grid_spec,
        out_shape=jax.ShapeDtypeStruct((nrows, d), F32),
        compiler_params=_cp(("arbitrary",), 48),
        name="expert_blocks",
    )(block_e, block_next, block_valid, row_tok, xf, w_eg, w_eu, w_ed)


def _final_kernel(pos_ref, h_ref, gate_ref, w_ref, yb_hbm, ya_ref, ys_ref, gbuf, sem, *, na, tm):
    i = pl.program_id(0)
    n = pl.num_programs(0)
    slot = i % 2

    def row_copy(tile, slot_, r, k):
        p = pos_ref[(tile * tm + r) * TOP_K + k]
        return pltpu.make_async_copy(yb_hbm.at[pl.ds(p, 1), :], gbuf.at[slot_, k, pl.ds(r, 1), :], sem.at[slot_])

    def start_tile(tile, slot_):
        for r in range(tm):
            for k in range(TOP_K):
                row_copy(tile, slot_, r, k).start()

    @pl.when(i == 0)
    def _():
        start_tile(0, 0)

    @pl.when(i + 1 < n)
    def _():
        start_tile(i + 1, 1 - slot)

    for r in range(tm):
        for k in range(TOP_K):
            row_copy(i, slot, r, k).wait()

    gate = gate_ref[...]
    y = h_ref[...]
    ffn = gbuf[slot, 0] * gate[:, 0:1]
    for k in range(1, TOP_K):
        ffn = ffn + gbuf[slot, k] * gate[:, k:k + 1]
    y = y + ffn
    ms = jnp.mean(y * y, axis=-1, keepdims=True)
    out = y * lax.rsqrt(ms + EPS) * w_ref[...]

    @pl.when(i < na)
    def _():
        ya_ref[...] = out

    @pl.when(i >= na)
    def _():
        ys_ref[...] = out


def _final(pos_flat, h, gate, yb, w, rows_a, rows_b):
    t, d = h.shape
    tm = _pick(math.gcd(rows_a, rows_b), 128)
    na, nb = rows_a // tm, rows_b // tm
    grid_spec = pltpu.PrefetchScalarGridSpec(
        num_scalar_prefetch=1,
        grid=(na + nb,),
        in_specs=[pl.BlockSpec((tm, d), lambda i, pos: (i, 0)),
                  pl.BlockSpec((tm, TOP_K), lambda i, pos: (i, 0)),
                  pl.BlockSpec((1, d), lambda i, pos: (0, 0)),
                  pl.BlockSpec(memory_space=pl.ANY)],
        out_specs=[pl.BlockSpec((tm, d), lambda i, pos: (jnp.minimum(i, na - 1), 0)),
                   pl.BlockSpec((tm, d), lambda i, pos: (jnp.maximum(i - na, 0), 0))],
        scratch_shapes=[pltpu.VMEM((2, TOP_K, tm, d), F32), pltpu.SemaphoreType.DMA((2,))],
    )
    return pl.pallas_call(
        functools.partial(_final_kernel, na=na, tm=tm),
        grid_spec=grid_spec,
        out_shape=[jax.ShapeDtypeStruct((rows_a, d), F32), jax.ShapeDtypeStruct((rows_b, d), F32)],
        compiler_params=_cp(("arbitrary",)),
        name="combine_final_norm",
    )(pos_flat, h, gate, w.reshape(1, d), yb)


def kernel(x_prompt, x_sample, state_hgrn, state_gdn, state_conv, norm_mix, w_in, conv_w, ha_lb_logits, ha_onorm, w_pa, gd_A_log, gd_dt_bias, gd_onorm, w_pb, w_out, norm_ffn, w_router_group, b_router_group, w_router_expert, b_router_expert, w_exp_gate, w_exp_up, w_exp_down, norm_final):
    depth = state_hgrn.shape[0]
    assert depth == 1, "one decoder layer"
    bp, lp, d = x_prompt.shape
    bs, ls, _ = x_sample.shape
    _, _, ha_heads, ha_dk, ha_dv = state_hgrn.shape
    _, _, vheads, gd_dk, gd_dv = state_gdn.shape
    conv_dim = state_conv.shape[3]
    ha_kdim, ha_vdim = ha_heads * ha_dk, ha_heads * ha_dv
    gd_vdim = vheads * gd_dv
    ngroups = w_router_group.shape[2]
    nexp = w_router_expert.shape[2]
    rp, rs = bp * lp, bs * ls
    t = rp + rs

    off_f = ha_kdim
    off_i = off_f + ha_kdim
    off_g = off_i + ha_vdim
    off_qkv = off_g + ha_vdim
    off_z = off_qkv + conv_dim
    off_b = off_z + gd_vdim
    off_gate = off_b + 2 * vheads

    xn = _rmsnorm_bf16(x_prompt.reshape(rp, d), x_sample.reshape(rs, d), norm_mix[0])

    assert ha_kdim == ha_vdim and off_qkv % conv_dim == 0 and off_z % gd_vdim == 0
    w_bf = w_in[0].astype(BF)
    proj, k_a = _inproj(xn, w_bf, ha_lb_logits, off_b, ha_kdim, ha_dk ** -0.5)
    q_a, logf, v_a, g_a = ((proj, cb) for cb in (0, off_f // ha_kdim, off_i // ha_kdim, off_g // ha_kdim))
    k_a = (k_a, 0)
    qkv = (proj, off_qkv // conv_dim)
    z_b = (proj, off_z // gd_vdim)
    w_gates_bf = w_bf[:, off_gate:]
    bg, bgt = _beta_decay(xn, w_bf[:, off_b:off_gate], gd_A_log[0], gd_dt_bias[0])

    cg_p = _pick(lp, 128, GDN_CHUNK) if lp >= GDN_CHUNK else lp
    sub_p = min(HGRN_SUB, lp)
    chunk_p = min(GDN_CHUNK, lp)
    cg_s = ls
    zeros_h = jnp.zeros((bp, ha_heads, ha_dk, ha_dv), F32)
    zeros_g = jnp.zeros((bp, vheads, gd_dk, gd_dv), F32)
    zeros_c = jnp.zeros((bp, CONV_K - 1, conv_dim), F32)

    oa_p, sh_p = _hgrn2(q_a, logf, k_a, v_a, g_a, ha_onorm[0], zeros_h,
                        row0=0, seq_len=lp, rows_per_step=cg_p, sub=sub_p, seqs_per_step=1)
    oa_s, sh_s = _hgrn2(q_a, logf, k_a, v_a, g_a, ha_onorm[0], state_hgrn[0],
                        row0=rp, seq_len=ls, rows_per_step=cg_s, sub=min(HGRN_SUB, ls),
                        seqs_per_step=math.gcd(_pick(bs, HGRN_SAMPLE_SEQS, 1), max(rp // cg_s, 1)))

    def time_on_lanes(rows0, nrows, cg):
        part = bgt[:, rows0:rows0 + nrows].reshape(2 * vheads, nrows // cg, cg)
        return jnp.transpose(part, (1, 0, 2))

    seqs_s = _pick(bs, GDN_SAMPLE_SEQS, 1)
    cg_g = _pick(lp, GDN_ROWS, GDN_CHUNK) if lp >= GDN_CHUNK else lp
    bgt_p = time_on_lanes(0, rp, cg_g)
    bgt_s = time_on_lanes(rp, rs, seqs_s * cg_s)
    ob_p, sg_p, sc_p = _gdn(qkv, z_b, bg, bgt_p, conv_w[0], gd_onorm[0], zeros_g, zeros_c,
                            row0=0, seq_len=lp, rows_per_step=cg_g, chunk=chunk_p, seqs_per_step=1)
    ob_s, sg_s, sc_s = _gdn(qkv, z_b, bg, bgt_s, conv_w[0], gd_onorm[0], state_gdn[0],
                            state_conv[0], row0=rp, seq_len=ls, rows_per_step=cg_s, chunk=min(GDN_CHUNK, ls),
                            seqs_per_step=seqs_s)

    merged = _merge(xn, oa_p, oa_s, ob_p, ob_s, w_gates_bf, w_pa[0], w_pb[0])

    nr = LANES
    w_router = jnp.concatenate([w_router_group[0], w_router_expert[0]], axis=1)
    w_router = jnp.pad(w_router, ((0, 0), (0, nr - ngroups - nexp))).astype(BF)
    b_router = jnp.pad(jnp.concatenate([b_router_group[0], b_router_expert[0]]), (0, nr - ngroups - nexp))
    h, xf, logits = _outproj(merged, x_prompt.reshape(rp, d), x_sample.reshape(rs, d), w_out[0],
                             norm_ffn[0], w_router)

    eid, gate, rank, cnt = _route(logits, b_router.reshape(1, nr), ngroups, nexp)
    nblocks = -(-(t * TOP_K) // MOE_BLOCK) + nexp
    pos, block_meta = _plan(cnt, eid, rank, ngroups, nexp, nblocks)
    block_e, block_next, block_valid = (block_meta[:nblocks, j] for j in range(3))

    pos_flat = pos.reshape(-1)
    row_tok = _invert(pos_flat, nblocks * MOE_BLOCK)
    yb = _experts(block_e, block_next, block_valid, row_tok, xf, w_exp_gate[0], w_exp_up[0], w_exp_down[0])
    y_p, y_s = _final(pos_flat, h, gate, yb, norm_final, rp, rs)
    return (y_p.reshape(bp, lp, d), y_s.reshape(bs, ls, d),
            sh_p[None], sg_p[None], sc_p[None], sh_s[None], sg_s[None], sc_s[None])
```

```python
import functools
import math

import jax
import jax.numpy as jnp
from jax import lax
from jax.experimental import pallas as pl
from jax.experimental.pallas import tpu as pltpu

F32 = jnp.float32
BF = jnp.bfloat16
I32 = jnp.int32
EPS = 1e-6
CONV_K = 4
TOP_K = 2
MOE_BLOCK = 128
GDN_CHUNK = 64
GDN_ROWS = 128
GDN_SAMPLE_SEQS = 8
HGRN_SUB = 16
HGRN_SAMPLE_SEQS = 8
LOG2E = 1.4426950408889634
LANES = 128
NEG = -3.0e38
MIB = 1024 * 1024


def _cp(sem, vmem_mib=48):
    return pltpu.CompilerParams(dimension_semantics=sem, vmem_limit_bytes=vmem_mib * MIB)


def _pick(n, pref, mult=8):
    best = None
    for t in range(mult, min(n, pref) + 1, mult):
        if n % t == 0:
            best = t
    return best if best is not None else n


def _sigmoid(x):
    return 1.0 / (1.0 + jnp.exp(-x))


def _silu(x):
    return x * _sigmoid(x)


def _iota(shape, dim):
    return lax.broadcasted_iota(I32, shape, dim)


def _dot(a, b):
    return jnp.dot(a, b, preferred_element_type=F32)


def _dot_nt(a, b):
    return lax.dot_general(a, b, (((1,), (1,)), ((), ())), preferred_element_type=F32)


def _dot_tn(a, b):
    return lax.dot_general(a, b, (((0,), (0,)), ((), ())), preferred_element_type=F32)


def _split(a):
    hi = a.astype(BF)
    lo = (a - hi.astype(F32)).astype(BF)
    return hi, lo


def _split3(a, axis=0):
    hi = a.astype(BF)
    r1 = a - hi.astype(F32)
    mid = r1.astype(BF)
    lo = (r1 - mid.astype(F32)).astype(BF)
    return jnp.concatenate([hi, mid, lo], axis=axis)


def _two_source_specs(rows_a, rows_b, tm, width):
    na, nb = rows_a // tm, rows_b // tm
    spec_a = pl.BlockSpec((tm, width), lambda i: (jnp.minimum(i, na - 1), 0))
    spec_b = pl.BlockSpec((tm, width), lambda i: (jnp.maximum(i - na, 0), 0))
    return na, nb, spec_a, spec_b


def _rmsnorm_kernel(xa_ref, xb_ref, w_ref, o_ref, *, na):
    i = pl.program_id(0)

    def body(x_ref):
        x = x_ref[...]
        ms = jnp.mean(x * x, axis=-1, keepdims=True)
        o_ref[...] = (x * lax.rsqrt(ms + EPS) * w_ref[...]).astype(o_ref.dtype)

    @pl.when(i < na)
    def _():
        body(xa_ref)

    @pl.when(i >= na)
    def _():
        body(xb_ref)


def _rmsnorm_bf16(xa, xb, w):
    d = xa.shape[1]
    tm = _pick(math.gcd(xa.shape[0], xb.shape[0]), 512)
    na, nb, sa, sb = _two_source_specs(xa.shape[0], xb.shape[0], tm, d)
    return pl.pallas_call(
        functools.partial(_rmsnorm_kernel, na=na),
        grid=(na + nb,),
        in_specs=[sa, sb, pl.BlockSpec((1, d), lambda i: (0, 0))],
        out_specs=pl.BlockSpec((tm, d), lambda i: (i, 0)),
        out_shape=jax.ShapeDtypeStruct((xa.shape[0] + xb.shape[0], d), BF),
        compiler_params=_cp(("arbitrary",), 24),
        name="rmsnorm_mix",
    )(xa, xb, w.reshape(1, d))


def _inproj_kernel(x_ref, w_ref, lb_ref, p_ref, kf_ref, *, jq, jf0, jf1, q_scale):
    j = pl.program_id(0)
    acc = _dot(x_ref[...], w_ref[...])
    is_forget = (j >= jf0) & (j < jf1)

    @pl.when(is_forget)
    def _():
        logf, kf = _forget_epilogue(acc, lb_ref[...])
        p_ref[...] = logf
        kf_ref[...] = kf

    @pl.when(jnp.logical_not(is_forget))
    def _():
        p_ref[...] = acc * jnp.where(j < jq, q_scale, 1.0)


def _inproj(x_bf, w, lb_logits, ncols, kdim, q_scale):
    t, k = x_bf.shape
    tn = _pick(math.gcd(ncols, kdim), 1024, LANES)
    tm = _pick(t, 1024)
    ni = t // tm
    jq, jf0, jf1 = kdim // tn, kdim // tn, 2 * kdim // tn
    nf = jf1 - jf0
    fcol = lambda j: jnp.clip(j - jf0, 0, nf - 1)
    frow = lambda j, i: jnp.where(j < jf0, 0, jnp.where(j >= jf1, ni - 1, i))
    kern = functools.partial(_inproj_kernel, jq=jq, jf0=jf0, jf1=jf1, q_scale=q_scale)
    return pl.pallas_call(
        kern,
        grid=(ncols // tn, ni),
        in_specs=[pl.BlockSpec((tm, k), lambda j, i: (i, 0)),
                  pl.BlockSpec((k, tn), lambda j, i: (0, j)),
                  pl.BlockSpec((lb_logits.shape[0], tn), lambda j, i: (0, fcol(j)))],
        out_specs=[pl.BlockSpec((tm, tn), lambda j, i: (i, j)),
                   pl.BlockSpec((tm, tn), lambda j, i: (frow(j, i), fcol(j)))],
        out_shape=[jax.ShapeDtypeStruct((t, ncols), F32), jax.ShapeDtypeStruct((t, kdim), F32)],
        compiler_params=_cp(("arbitrary", "arbitrary"), 44),
        name="in_proj",
    )(x_bf, w, lb_logits)


def _forget_epilogue(acc, lb_logits):
    m = jnp.max(lb_logits, axis=0, keepdims=True)
    e = jnp.exp(lb_logits - m)
    lb = e[0:1, :] / jnp.sum(e, axis=0, keepdims=True)
    f = lb + (1.0 - lb) * _sigmoid(acc)
    return jnp.log(f), 1.0 - f


def _beta_decay_kernel(x_ref, w_ref, wt_ref, prow_ref, pcol_ref, o_ref, ot_ref, *, vh):
    x = x_ref[...]
    acc = _dot(x, w_ref[...])
    acct = _dot_nt(wt_ref[...], x)

    def act(a, is_beta, a_neg_exp, dt_bias):
        z = a + dt_bias
        softplus = jnp.maximum(z, 0.0) + jnp.log(1.0 + jnp.exp(-jnp.abs(z)))
        return jnp.where(is_beta, _sigmoid(a), a_neg_exp * softplus)

    prow = prow_ref[...]
    pcol = pcol_ref[...]
    o_ref[...] = act(acc, _iota(acc.shape, 1) < vh, prow[0:1, :], prow[1:2, :])
    ot_ref[...] = act(acct, _iota(acct.shape, 0) < vh, pcol[:, 0:1], pcol[:, 1:2])


def _beta_decay(x_bf, w_ba, a_log, dt_bias):
    t, k = x_bf.shape
    vh = a_log.shape[0]
    tm = t if t <= 2048 else _pick(t, 1024, LANES)
    zeros = jnp.zeros((vh,), F32)
    prow = jnp.stack([jnp.concatenate([zeros, -jnp.exp(a_log)]), jnp.concatenate([zeros, dt_bias])])
    return pl.pallas_call(
        functools.partial(_beta_decay_kernel, vh=vh),
        grid=(t // tm,),
        in_specs=[pl.BlockSpec((tm, k), lambda i: (i, 0)),
                  pl.BlockSpec((k, 2 * vh), lambda i: (0, 0)),
                  pl.BlockSpec((2 * vh, k), lambda i: (0, 0)),
                  pl.BlockSpec((2, 2 * vh), lambda i: (0, 0)),
                  pl.BlockSpec((2 * vh, 2), lambda i: (0, 0))],
        out_specs=[pl.BlockSpec((tm, 2 * vh), lambda i: (i, 0)),
                   pl.BlockSpec((2 * vh, tm), lambda i: (0, i))],
        out_shape=[jax.ShapeDtypeStruct((t, 2 * vh), F32), jax.ShapeDtypeStruct((2 * vh, t), F32)],
        compiler_params=_cp(("arbitrary",), 16),
        name="proj_beta_decay",
    )(x_bf, w_ba.astype(BF), w_ba.T.astype(BF), prow, prow.T)


def _hgrn2_kernel(q_ref, lf_ref, k_ref, v_ref, g_ref, on_ref, s0_ref, o_ref, so_ref, s_scr, gall_scr,
                  g_scr, k_scr, v_scr, *, heads, dk, dv, sub, nsub, nseq, nchunks, group, unroll):
    c = pl.program_id(1)
    cg = sub * nsub

    @pl.when(c == 0)
    def _():
        s_scr[...] = s0_ref[...]

    rows_all = nseq * cg
    shift = sub.bit_length() - 1
    r_i = _iota((rows_all, 3 * rows_all), 0)
    c_i = _iota((rows_all, 3 * rows_all), 1) & (rows_all - 1)
    same = lax.shift_right_logical(r_i, shift) == lax.shift_right_logical(c_i, shift)
    btril3 = jnp.where((r_i >= c_i) & same, 1.0, 0.0).astype(BF)
    gall_scr[...] = _dot(btril3, _split3(lf_ref[...])) * LOG2E
    onorm = on_ref[...]
    n_aug = 16 - sub % 16
    aug_rhs = jnp.concatenate([jnp.zeros((n_aug, dv), BF), jnp.ones((n_aug, dv), BF)], axis=1)
    zeros_v = jnp.zeros((sub, dv), BF)
    zeros_aug = jnp.zeros((n_aug - 3, dk), F32)

    nh = sub // 8
    row8 = _iota((8, 1), 0)

    def block(it, carry):
        for u in range(unroll):
            sub_chunk(it * unroll + u, g_scr.at[u], k_scr.at[u], v_scr.at[u])
        return carry

    def sub_chunk(sb, g_scr, k_scr, v_scr):
        for q in range(nseq):
            rows = pl.ds(pl.multiple_of(q * cg + sb * sub, sub), sub)
            g_scr[q] = gall_scr[rows, :]
            k_scr[q] = k_ref[rows, :]
            v_scr[q] = v_ref[rows, :]
        problems = [(q, h) for q in range(nseq) for h in range(heads)]
        for g0 in range(0, len(problems), group):
            ps = []
            for (q, h) in problems[g0:g0 + group]:
                r0 = pl.multiple_of(q * cg + sb * sub, sub)
                rows = pl.ds(r0, sub)
                kc = pl.ds(h * dk, dk)
                vc = pl.ds(h * dv, dv)
                ps.append(dict(q=q, h=h, r0=r0, rows=rows, kc=kc, vc=vc,
                               gc=g_scr[q, :, kc],
                               qv=q_ref[rows, kc], k=k_ref[rows, kc], v=v_ref[rows, vc]))
            for p in ps:
                p["s_old"] = s_scr[p["q"], p["h"]]
                p["o_state"] = _dot((p["qv"] * jnp.exp2(p["gc"])).astype(BF), p["s_old"].astype(BF))
                p["o8"] = [jnp.zeros((8, dv), F32) for _ in range(nh)]
                p["q8"] = [p["qv"][8 * i:8 * i + 8] for i in range(nh)]
                p["g8"] = [p["gc"][8 * i:8 * i + 8] for i in range(nh)]
            for j in range(sub):
                for p in ps:
                    kj = k_scr[p["q"], pl.ds(j, 1), p["kc"]]
                    vj = v_scr[p["q"], pl.ds(j, 1), p["vc"]]
                    gj = g_scr[p["q"], pl.ds(j, 1), p["kc"]]
                    for i in range(j // 8, nh):
                        pr = (p["q8"][i] * kj) * jnp.exp2(p["g8"][i] - gj)
                        a_col = jnp.sum(pr, axis=-1, keepdims=True)
                        if i == j // 8:
                            a_col = jnp.where(row8 >= j % 8, a_col, 0.0)
                        p["o8"][i] = p["o8"][i] + a_col * vj
            for p in ps:
                p["o"] = p["o_state"] + (jnp.concatenate(p["o8"], axis=0) if nh > 1 else p["o8"][0])
            for p in ps:
                gend = p["gc"][sub - 1:sub, :]
                dend = jnp.exp2(gend)
                d_hi = dend.astype(BF).astype(F32)
                d_mid = (dend - d_hi).astype(BF).astype(F32)
                d_lo = (dend - d_hi - d_mid).astype(BF).astype(F32)
                kdec = p["k"] * jnp.exp2(gend - p["gc"])
                lhs = jnp.concatenate([kdec, d_hi, d_mid, d_lo, zeros_aug], axis=0).astype(BF)
                rhs = jnp.concatenate([jnp.concatenate([p["v"].astype(BF), zeros_v], axis=1), aug_rhs], axis=0)
                p["kv"] = _dot_tn(lhs, rhs)
            for p in ps:
                s_scr[p["q"], p["h"]] = p["kv"][:, dv:] * p["s_old"] + p["kv"][:, :dv]
            for p in ps:
                o = p["o"]
                ms = jnp.mean(o * o, axis=-1, keepdims=True)
                o_ref[p["rows"], p["vc"]] = (o * lax.rsqrt(ms + EPS) * onorm) * _silu(g_ref[p["rows"], p["vc"]])

    lax.fori_loop(0, nsub // unroll, block, 0)

    @pl.when(c == nchunks - 1)
    def _():
        so_ref[...] = s_scr[...]


def _hgrn2(q, lf, kf, v, gate, onorm, s0, *, row0, seq_len, rows_per_step, sub, seqs_per_step):
    nseq_total, heads, dk, dv = s0.shape
    assert dk == dv, "state decay is applied with a (dk, dv) tile"
    width = heads * dk
    cg = rows_per_step
    nseq = seqs_per_step
    nchunks = seq_len // cg
    assert nseq == 1 or nchunks == 1, "several sequences per step only when a step covers whole sequences"
    rows = nseq * cg
    b0 = row0 // rows
    assert sub & (sub - 1) == 0 and sub % 8 == 0 and rows & (rows - 1) == 0
    unroll = next(u for u in (8, 4, 2, 1) if (cg // sub) % u == 0)
    def row_spec(col_block):
        return pl.BlockSpec((rows, width), lambda b, c: (b0 + b * nchunks + c, col_block))

    st_spec = pl.BlockSpec((nseq, heads, dk, dv), lambda b, c: (b, 0, 0, 0))
    kern = functools.partial(_hgrn2_kernel, heads=heads, dk=dk, dv=dv, sub=sub, nsub=cg // sub, nseq=nseq,
                             nchunks=nchunks, group=8, unroll=unroll)
    operands = (q, lf, kf, v, gate)
    return pl.pallas_call(
        kern,
        grid=(nseq_total // nseq, nchunks),
        in_specs=[row_spec(cb) for _, cb in operands]
                 + [pl.BlockSpec((1, dv), lambda b, c: (0, 0)), st_spec],
        out_specs=[pl.BlockSpec((rows, heads * dv), lambda b, c: (b * nchunks + c, 0)), st_spec],
        out_shape=[jax.ShapeDtypeStruct((nseq_total * seq_len, heads * dv), F32),
                   jax.ShapeDtypeStruct(s0.shape, F32)],
        scratch_shapes=[pltpu.VMEM((nseq, heads, dk, dv), F32), pltpu.VMEM((rows, width), F32),
                        pltpu.VMEM((unroll, nseq, sub, width), F32), pltpu.VMEM((unroll, nseq, sub, width), F32),
                        pltpu.VMEM((unroll, nseq, sub, heads * dv), F32)],
        compiler_params=_cp(("arbitrary", "arbitrary"), 24),
        name="hgrn2_recurrence",
    )(*[a for a, _ in operands], onorm.reshape(1, dv), s0)


def _gdn_kernel(qkv_ref, z_ref, bg_ref, bgt_ref, cw_ref, on_ref, s0_ref, c0_ref,
                o_ref, so_ref, co_ref, s_scr, xx, xs,
                *, kheads, vheads, dk, dv, chunk, nsub, nseq, nchunks, group_kh):
    c = pl.program_id(1)
    cg = chunk * nsub
    kdim = kheads * dk
    rep = vheads // kheads
    tail0 = 8 - (CONV_K - 1)

    @pl.when(c == 0)
    def _():
        s_scr[...] = s0_ref[...]
        xx[:, 0:8, :] = c0_ref[...]

    for q in range(nseq):
        xx[q, 8:8 + cg, :] = qkv_ref[q * cg:(q + 1) * cg, :]
        conv = xx[q, pl.ds(tail0, cg), :] * cw_ref[0:1, :]
        for j in range(1, CONV_K):
            conv = conv + xx[q, pl.ds(tail0 + j, cg), :] * cw_ref[j:j + 1, :]
        xs[q] = _silu(conv)

    @pl.when(c == nchunks - 1)
    def _():
        for q in range(nseq):
            co_ref[q] = xx[q, pl.ds(8 + cg - (CONV_K - 1), CONV_K - 1), :]

    for q in range(nseq):
        xx[q, 0:8, :] = xx[q, cg:cg + 8, :]

    wp = chunk
    r_i = _iota((chunk, wp), 0)
    c_i = _iota((chunk, wp), 1)
    causal = r_i >= c_i
    strict = r_i > c_i

    def pad_rows(a):
        if a.shape[0] == wp:
            return a
        if a.dtype == BF and a.shape[0] % 16:
            return pad_rows(a.astype(F32)).astype(BF)
        return jnp.concatenate([a, jnp.zeros((wp - a.shape[0],) + a.shape[1:], a.dtype)], axis=0)

    def split_lhs(a):
        a_hi = a.astype(BF).astype(F32)
        return jnp.concatenate([a_hi, a_hi, a - a_hi], axis=1).astype(BF)

    def split_rhs(b):
        bh, bl = (pad_rows(t) for t in _split(b))
        return jnp.concatenate([bh, bl, bh], axis=0)
    cm = chunk - 1
    tril3 = jnp.where(_iota((chunk, 3 * chunk), 0) >= (_iota((chunk, 3 * chunk), 1) & cm), 1.0, 0.0).astype(BF)
    t_r = _iota((3 * chunk, wp), 0) & cm
    t_c = _iota((3 * chunk, wp), 1)
    triu3 = jnp.where((t_r <= t_c) & (t_c < chunk), 1.0, 0.0).astype(BF)
    onorm = on_ref[...]

    cums = {}

    def cumulative_decay(q, s):
        if (q, s) not in cums:
            r0 = q * cg + s * chunk
            bg = bg_ref[r0:r0 + chunk, :]
            gt = bgt_ref[0, vheads:2 * vheads, r0:r0 + chunk]
            cums[(q, s)] = (bg, _dot(tril3, _split3(bg[:, vheads:2 * vheads], axis=0)),
                            _dot(_split3(gt, axis=1), triu3))
        return cums[(q, s)]

    def run_group(items):
        probs = []
        for (q, s, kh) in items:
            r0 = q * cg + s * chunk
            x0 = s * chunk
            bg, gcol_all, grow_all = cumulative_decay(q, s)
            for kh in (kh,):
                qh = xs[q, x0:x0 + chunk, kh * dk:(kh + 1) * dk]
                kk_ = xs[q, x0:x0 + chunk, kdim + kh * dk:kdim + (kh + 1) * dk]
                qn = qh * lax.rsqrt(jnp.sum(qh * qh, axis=-1, keepdims=True) + EPS) * (dk ** -0.5)
                kn = kk_ * lax.rsqrt(jnp.sum(kk_ * kk_, axis=-1, keepdims=True) + EPS)
                kn_bf = kn.astype(BF)
                kn_pad = pad_rows(kn_bf)
                kk = _dot_nt(kn_bf, kn_pad)
                qk = _dot_nt(qn.astype(BF), kn_pad)
                for r in range(rep):
                    h = kh * rep + r
                    vh_ = xs[q, x0:x0 + chunk, 2 * kdim + h * dv:2 * kdim + (h + 1) * dv]
                    gcol = gcol_all[:, h:h + 1]
                    grow = grow_all[h:h + 1, :]
                    beta = bg[:, h:h + 1]
                    decay = jnp.where(causal, jnp.exp(jnp.minimum(gcol - grow, 0.0)), 0.0)
                    eg = jnp.exp(gcol)
                    gend = gcol[chunk - 1:chunk, :]
                    probs.append(dict(
                        q=q, s=s, h=h, r0=r0,
                        x=jnp.where(strict, -(beta * kk * decay), 0.0),
                        y=jnp.concatenate([vh_ * beta, kn * (beta * eg)], axis=1),
                        a_bf=(qk * decay).astype(BF),
                        qg=qn * eg,
                        kdec_bf=(kn * jnp.exp(gend - gcol)).astype(BF),
                        send=jnp.exp(gend)))

        if chunk <= 16:
            for j in range(chunk - 1):
                for p in probs:
                    p["y"] = p["y"] + p["x"][:, j:j + 1] * p["y"][j:j + 1, :]
        else:
            n_fac = int(math.log2(chunk))
            for k in range(n_fac):
                for p in probs:
                    p["x_lhs"] = split_lhs(p["x"])
                    p["y"] = p["y"] + _dot(p["x_lhs"], split_rhs(p["y"]))
                if k + 1 < n_fac:
                    for p in probs:
                        p["x"] = _dot(p["x_lhs"], split_rhs(p["x"]))

        for s in sorted({p["s"] for p in probs}):
            cur = [p for p in probs if p["s"] == s]
            for p in cur:
                p["s_old"] = s_scr[p["q"], p["h"]]
                lhs = jnp.concatenate([p["y"][:, dv:], p["qg"]], axis=0).astype(BF)
                p["ws"] = _dot(lhs, p["s_old"].astype(BF))
            for p in cur:
                p["u_bf"] = (p["y"][:, :dv] - p["ws"][:chunk]).astype(BF)
            for p in cur:
                s_scr[p["q"], p["h"]] = p["send"] * p["s_old"] + _dot_tn(p["kdec_bf"], p["u_bf"])
            for p in cur:
                o = p["ws"][chunk:] + _dot(p["a_bf"], pad_rows(p["u_bf"]))
                ms = jnp.mean(o * o, axis=-1, keepdims=True)
                zc = slice(p["h"] * dv, (p["h"] + 1) * dv)
                rows = slice(p["r0"], p["r0"] + chunk)
                o_ref[rows, zc] = (o * lax.rsqrt(ms + EPS) * onorm) * _silu(z_ref[rows, zc])

    for g0 in range(0, kheads, group_kh):
        run_group([(q, s, kh) for q in range(nseq) for s in range(nsub)
                   for kh in range(g0, min(g0 + group_kh, kheads))])

    @pl.when(c == nchunks - 1)
    def _():
        so_ref[...] = s_scr[...]


def _gdn(qkv, z, bg, bgt3, conv_w, onorm, s0, conv0, *, row0, seq_len, rows_per_step, chunk, seqs_per_step):
    nseq_total, vheads, dk, dv = s0.shape
    (qkv, qkv_cb), (z, z_cb) = qkv, z
    conv_dim = conv_w.shape[1]
    kheads = (conv_dim - vheads * dv) // (2 * dk)
    cg = rows_per_step
    nseq = seqs_per_step
    nchunks = seq_len // cg
    assert nseq == 1 or nchunks == 1, "several sequences per step only when a step covers whole sequences"
    rows = nseq * cg
    b0 = row0 // rows
    conv0p = jnp.pad(conv0, ((0, 0), (8 - (CONV_K - 1), 0), (0, 0)))
    rmap = lambda b, c: (b0 + b * nchunks + c, 0)
    st_spec = pl.BlockSpec((nseq, vheads, dk, dv), lambda b, c: (b, 0, 0, 0))
    kern = functools.partial(_gdn_kernel, kheads=kheads, vheads=vheads, dk=dk, dv=dv, chunk=chunk,
                             nsub=cg // chunk, nseq=nseq, nchunks=nchunks, group_kh=kheads)
    return pl.pallas_call(
        kern,
        grid=(nseq_total // nseq, nchunks),
        in_specs=[pl.BlockSpec((rows, conv_dim), lambda b, c: (b0 + b * nchunks + c, qkv_cb)),
                  pl.BlockSpec((rows, vheads * dv), lambda b, c: (b0 + b * nchunks + c, z_cb)),
                  pl.BlockSpec((rows, 2 * vheads), rmap),
                  pl.BlockSpec((1, 2 * vheads, rows), lambda b, c: (b * nchunks + c, 0, 0)),
                  pl.BlockSpec((CONV_K, conv_dim), lambda b, c: (0, 0)),
                  pl.BlockSpec((1, dv), lambda b, c: (0, 0)),
                  st_spec,
                  pl.BlockSpec((nseq, 8, conv_dim), lambda b, c: (b, 0, 0))],
        out_specs=[pl.BlockSpec((rows, vheads * dv), lambda b, c: (b * nchunks + c, 0)),
                   st_spec,
                   pl.BlockSpec((nseq, CONV_K - 1, conv_dim), lambda b, c: (b, 0, 0))],
        out_shape=[jax.ShapeDtypeStruct((nseq_total * seq_len, vheads * dv), F32),
                   jax.ShapeDtypeStruct(s0.shape, F32),
                   jax.ShapeDtypeStruct((nseq_total, CONV_K - 1, conv_dim), F32)],
        scratch_shapes=[pltpu.VMEM((nseq, vheads, dk, dv), F32), pltpu.VMEM((nseq, 8 + cg, conv_dim), F32),
                        pltpu.VMEM((nseq, cg, conv_dim), F32)],
        compiler_params=_cp(("arbitrary", "arbitrary"), 28),
        name="gdn_recurrence",
    )(qkv, z, bg, bgt3, conv_w, onorm.reshape(1, dv), s0, conv0p)


def _merge_kernel(xn_ref, oa1_ref, oa2_ref, ob1_ref, ob2_ref, wga_ref, wgb_ref, wa_ref, wb_ref, o_ref,
                  oa_bf, ob_bf, *, na):
    i = pl.program_id(0)
    j = pl.program_id(1)

    @pl.when((j == 0) & (i < na))
    def _():
        oa_bf[...] = oa1_ref[...].astype(BF)
        ob_bf[...] = ob1_ref[...].astype(BF)

    @pl.when((j == 0) & (i >= na))
    def _():
        oa_bf[...] = oa2_ref[...].astype(BF)
        ob_bf[...] = ob2_ref[...].astype(BF)

    xn = xn_ref[...]
    ya = _dot(oa_bf[...], wa_ref[...])
    yb = _dot(ob_bf[...], wb_ref[...])
    ga = _dot(xn, wga_ref[...])
    gb = _dot(xn, wgb_ref[...])
    o_ref[...] = (_sigmoid(ga) * ya + _sigmoid(gb) * yb).astype(o_ref.dtype)


def _merge(xn, oa_p, oa_s, ob_p, ob_s, w_gates_bf, w_pa, w_pb):
    rp, rs = oa_p.shape[0], oa_s.shape[0]
    ka, kb = oa_p.shape[1], ob_p.shape[1]
    d = w_pa.shape[1]
    k = xn.shape[1]
    tm = _pick(math.gcd(rp, rs), 512)
    tn = _pick(d, 512, LANES)
    na, nb = rp // tm, rs // tm
    nj = d // tn
    amap = lambda i, j: (jnp.minimum(i, na - 1), 0)
    bmap = lambda i, j: (jnp.maximum(i - na, 0), 0)
    return pl.pallas_call(
        functools.partial(_merge_kernel, na=na),
        grid=(na + nb, nj),
        in_specs=[pl.BlockSpec((tm, k), lambda i, j: (i, 0)),
                  pl.BlockSpec((tm, ka), amap), pl.BlockSpec((tm, ka), bmap),
                  pl.BlockSpec((tm, kb), amap), pl.BlockSpec((tm, kb), bmap),
                  pl.BlockSpec((k, tn), lambda i, j: (0, j)),
                  pl.BlockSpec((k, tn), lambda i, j: (0, nj + j)),
                  pl.BlockSpec((ka, tn), lambda i, j: (0, j)),
                  pl.BlockSpec((kb, tn), lambda i, j: (0, j))],
        out_specs=pl.BlockSpec((tm, tn), lambda i, j: (i, j)),
        out_shape=jax.ShapeDtypeStruct((rp + rs, d), BF),
        scratch_shapes=[pltpu.VMEM((tm, ka), BF), pltpu.VMEM((tm, kb), BF)],
        compiler_params=_cp(("arbitrary", "arbitrary"), 40),
        name="gates_branch_proj_merge",
    )(xn, oa_p, oa_s, ob_p, ob_s, w_gates_bf, w_gates_bf, w_pa.astype(BF), w_pb.astype(BF))


def _outproj_kernel(m_ref, xa_ref, xb_ref, wo_ref, nw_ref, wr_ref, h_ref, xf_ref, lg_ref, *, na):
    i = pl.program_id(0)

    def body(x_ref):
        h = x_ref[...] + _dot(m_ref[...], wo_ref[...])
        h_ref[...] = h
        ms = jnp.mean(h * h, axis=-1, keepdims=True)
        xf = h * lax.rsqrt(ms + EPS) * nw_ref[...]
        xf_ref[...] = xf
        lg_ref[...] = _dot(xf.astype(BF), wr_ref[...])

    @pl.when(i < na)
    def _():
        body(xa_ref)

    @pl.when(i >= na)
    def _():
        body(xb_ref)


def _outproj(merged, xa, xb, w_out, norm_ffn, w_router):
    t, d = merged.shape
    tm = _pick(math.gcd(xa.shape[0], xb.shape[0]), 512)
    na, nb, sa, sb = _two_source_specs(xa.shape[0], xb.shape[0], tm, d)
    nr = w_router.shape[1]
    const = lambda i: (0, 0)
    return pl.pallas_call(
        functools.partial(_outproj_kernel, na=na),
        grid=(na + nb,),
        in_specs=[pl.BlockSpec((tm, d), lambda i: (i, 0)), sa, sb,
                  pl.BlockSpec((d, d), const, pipeline_mode=pl.Buffered(1)),
                  pl.BlockSpec((1, d), const),
                  pl.BlockSpec((d, nr), const, pipeline_mode=pl.Buffered(1))],
        out_specs=[pl.BlockSpec((tm, d), lambda i: (i, 0)),
                   pl.BlockSpec((tm, d), lambda i: (i, 0)),
                   pl.BlockSpec((tm, nr), lambda i: (i, 0))],
        out_shape=[jax.ShapeDtypeStruct((t, d), F32), jax.ShapeDtypeStruct((t, d), F32),
                   jax.ShapeDtypeStruct((t, nr), F32)],
        compiler_params=_cp(("arbitrary",), 48),
        name="out_proj_ffn_norm",
    )(merged, xa, xb, w_out.astype(BF), norm_ffn.reshape(1, d), w_router)


def _route_kernel(lg_ref, b_ref, eid_ref, gate_ref, rank_ref, cnt_ref, carry,
                  *, ngroups, nexp, tm):
    i = pl.program_id(0)

    @pl.when(i == 0)
    def _():
        carry[...] = jnp.zeros_like(carry)

    per_group = nexp // ngroups
    lg = lg_ref[...] + b_ref[...]
    lane = _iota(lg.shape, 1)
    big = jnp.int32(1 << 20)
    is_g = lane < ngroups
    gl = jnp.where(is_g, lg, NEG)
    gmax = jnp.max(gl, axis=-1, keepdims=True)
    gidx = jnp.min(jnp.where(gl == gmax, lane, big), axis=-1, keepdims=True)
    gsum = jnp.sum(jnp.where(is_g, jnp.exp(gl - gmax), 0.0), axis=-1, keepdims=True)
    gw = 1.0 / gsum
    elane = lane - ngroups
    in_grp = (elane >= gidx * per_group) & (elane < (gidx + 1) * per_group)
    el = jnp.where(in_grp, lg, NEG)
    v1 = jnp.max(el, axis=-1, keepdims=True)
    i1 = jnp.min(jnp.where(in_grp & (el == v1), elane, big), axis=-1, keepdims=True)
    in2 = in_grp & (elane != i1)
    el2 = jnp.where(in2, lg, NEG)
    v2 = jnp.max(el2, axis=-1, keepdims=True)
    i2 = jnp.min(jnp.where(in2 & (el2 == v2), elane, big), axis=-1, keepdims=True)
    p2 = jnp.exp(v2 - v1)
    den = 1.0 + p2
    lane2 = _iota((tm, TOP_K), 1)
    eid_ref[...] = jnp.where(lane2 == 0, i1, i2)
    gate_ref[...] = jnp.where(lane2 == 0, gw / den, gw * p2 / den)

    oh1 = (elane == i1).astype(F32)
    oh2 = (elane == i2).astype(F32)
    lower = (_iota((tm, tm), 0) > _iota((tm, tm), 1)).astype(BF)
    cs1 = _dot(lower, oh1.astype(BF))
    cs2 = _dot(lower, oh2.astype(BF))
    tot1 = jnp.sum(oh1, axis=0, keepdims=True)
    tot2 = jnp.sum(oh2, axis=0, keepdims=True)
    base = carry[0:1, :]
    r1 = jnp.sum(oh1 * (base + cs1), axis=-1, keepdims=True)
    r2 = jnp.sum(oh2 * (base + tot1 + cs2), axis=-1, keepdims=True)
    rank_ref[...] = jnp.where(lane2 == 0, r1, r2).astype(I32)
    new = base + tot1 + tot2
    carry[...] = jnp.broadcast_to(new, carry.shape)
    cnt_ref[...] = jnp.broadcast_to(new, cnt_ref.shape)


def _route(logits, bias_row, ngroups, nexp):
    t, nr = logits.shape
    tm = _pick(t, 512)
    kern = functools.partial(_route_kernel, ngroups=ngroups, nexp=nexp, tm=tm)
    return pl.pallas_call(
        kern,
        grid=(t // tm,),
        in_specs=[pl.BlockSpec((tm, nr), lambda i: (i, 0)), pl.BlockSpec((1, nr), lambda i: (0, 0))],
        out_specs=[pl.BlockSpec((tm, TOP_K), lambda i: (i, 0)),
                   pl.BlockSpec((tm, TOP_K), lambda i: (i, 0)),
                   pl.BlockSpec((tm, TOP_K), lambda i: (i, 0)),
                   pl.BlockSpec((8, nr), lambda i: (0, 0))],
        out_shape=[jax.ShapeDtypeStruct((t, TOP_K), I32), jax.ShapeDtypeStruct((t, TOP_K), F32),
                   jax.ShapeDtypeStruct((t, TOP_K), I32), jax.ShapeDtypeStruct((8, nr), F32)],
        scratch_shapes=[pltpu.VMEM((8, nr), F32)],
        compiler_params=_cp(("arbitrary",), 16),
        name="route_topk_rank",
    )(logits, bias_row)


def _plan_kernel(cnt_ref, eid_ref, rank_ref, pos_ref, be_ref, *, ngroups, nexp, nblocks_pad, tm):
    nr = cnt_ref.shape[1]
    cnt = cnt_ref[0:1, :]
    padded = jnp.floor((cnt + (MOE_BLOCK - 1)) * (1.0 / MOE_BLOCK)) * MOE_BLOCK
    r_i = _iota((nr, nr), 0)
    c_i = _iota((nr, nr), 1)
    padded_col = jnp.sum(jnp.where(r_i == c_i, jnp.broadcast_to(padded, (nr, nr)), 0.0), axis=1, keepdims=True)
    start = jnp.sum(jnp.where(r_i < c_i, jnp.broadcast_to(padded_col, (nr, nr)), 0.0), axis=0, keepdims=True)
    end = start + padded
    lane = _iota((tm, nr), 1)
    eid = eid_ref[...]
    rank = rank_ref[...]
    lane2 = _iota((tm, TOP_K), 1)
    pos = jnp.zeros((tm, TOP_K), F32)
    for k in range(TOP_K):
        oh = (lane - ngroups) == eid[:, k:k + 1]
        st = jnp.sum(jnp.where(oh, start, 0.0), axis=-1, keepdims=True)
        pos = jnp.where(lane2 == k, st, pos)
    pos_ref[...] = pos.astype(I32) + rank

    @pl.when(pl.program_id(0) == 0)
    def _():
        blk_row = _iota((nblocks_pad, nr), 0).astype(F32) * MOE_BLOCK
        lane_b = _iota((nblocks_pad, nr), 1)
        is_e = (lane_b >= ngroups) & (lane_b < ngroups + nexp)
        n_le = jnp.sum(jnp.where(is_e & (end <= blk_row), 1.0, 0.0), axis=-1, keepdims=True)
        e_lane = (lane_b - ngroups).astype(F32)
        nonempty = is_e & (padded > 0.0)
        e_last = jnp.max(jnp.where(nonempty, e_lane, -1.0), axis=-1, keepdims=True)
        be = jnp.minimum(n_le, e_last)
        nxt = jnp.min(jnp.where(nonempty & (e_lane > be), e_lane, 1e9), axis=-1, keepdims=True)
        nxt = jnp.where(nxt > 1e8, -1.0, nxt)
        total = jnp.sum(padded, axis=-1, keepdims=True)
        valid = jnp.where(blk_row[:, 0:1] < total, 1.0, 0.0)
        col = _iota((nblocks_pad, 4), 1)
        meta = jnp.where(col == 0, be, jnp.where(col == 1, nxt, jnp.where(col == 2, valid, 0.0)))
        be_ref[...] = meta.astype(I32)


def _plan(cnt, eid, rank, ngroups, nexp, nblocks):
    t = eid.shape[0]
    nr = cnt.shape[1]
    tm = _pick(t, 512)
    nblocks_pad = -(-nblocks // 8) * 8
    kern = functools.partial(_plan_kernel, ngroups=ngroups, nexp=nexp, nblocks_pad=nblocks_pad, tm=tm)
    return pl.pallas_call(
        kern,
        grid=(t // tm,),
        in_specs=[pl.BlockSpec((8, nr), lambda i: (0, 0)),
                  pl.BlockSpec((tm, TOP_K), lambda i: (i, 0)),
                  pl.BlockSpec((tm, TOP_K), lambda i: (i, 0))],
        out_specs=[pl.BlockSpec((tm, TOP_K), lambda i: (i, 0)),
                   pl.BlockSpec((nblocks_pad, 4), lambda i: (0, 0))],
        out_shape=[jax.ShapeDtypeStruct((t, TOP_K), I32), jax.ShapeDtypeStruct((nblocks_pad, 4), I32)],
        compiler_params=_cp(("arbitrary",), 16),
        name="route_plan",
    )(cnt, eid, rank)


def _invert_kernel(pos_ref, tok_ref, *, n_assign, n_rows):
    def clear(r, c):
        tok_ref[r] = 0
        return c

    lax.fori_loop(0, n_rows, clear, 0, unroll=16)

    def put(a, c):
        tok_ref[pos_ref[a]] = lax.shift_right_logical(a, TOP_K.bit_length() - 1)
        return c

    lax.fori_loop(0, n_assign, put, 0, unroll=16)


def _invert(pos_flat, n_rows):
    n_assign = pos_flat.shape[0]
    return pl.pallas_call(
        functools.partial(_invert_kernel, n_assign=n_assign, n_rows=n_rows),
        in_specs=[pl.BlockSpec(memory_space=pltpu.SMEM)],
        out_specs=pl.BlockSpec(memory_space=pltpu.SMEM),
        out_shape=jax.ShapeDtypeStruct((n_rows,), I32),
        name="route_invert",
    )(pos_flat)


def _expert_kernel(be_ref, nxt_ref, valid_ref, tok_ref, x_hbm, wg_hbm, wu_hbm, wd_hbm, y_ref,
                   xbuf0, xbuf1, xsem, wg_st, wu_st, wd_st, wsem, wg_bf, wu_bf, wd_bf):
    i = pl.program_id(0)
    n = pl.num_programs(0)
    xbufs = (xbuf0, xbuf1)

    def row_copy(blk, s, r):
        t = tok_ref[blk * MOE_BLOCK + r]
        return pltpu.make_async_copy(x_hbm.at[pl.ds(t, 1), :], xbufs[s].at[pl.ds(r, 1), :], xsem.at[s])

    def block_wait(s):
        pltpu.make_async_copy(x_hbm.at[pl.ds(0, MOE_BLOCK), :], xbufs[s], xsem.at[s]).wait()

    def weight_copies(e):
        return (pltpu.make_async_copy(wg_hbm.at[e], wg_st, wsem.at[0]),
                pltpu.make_async_copy(wu_hbm.at[e], wu_st, wsem.at[1]),
                pltpu.make_async_copy(wd_hbm.at[e], wd_st, wsem.at[2]))

    @pl.when(i == 0)
    def _():
        for r in range(MOE_BLOCK):
            row_copy(0, 0, r).start()
        for cp in weight_copies(be_ref[0]):
            cp.start(priority=1)

    prev = be_ref[jnp.maximum(i - 1, 0)]

    @pl.when((i == 0) | (be_ref[i] != prev))
    def _():
        for cp in weight_copies(be_ref[i]):
            cp.wait()
        wg_bf[...] = wg_st[...].astype(BF)
        wu_bf[...] = wu_st[...].astype(BF)
        wd_bf[...] = wd_st[...].astype(BF)

        @pl.when(nxt_ref[i] >= 0)
        def _():
            for cp in weight_copies(nxt_ref[i]):
                cp.start(priority=1)

    ahead = jnp.minimum(i + 1, n - 1)
    used = valid_ref[i] != 0
    used_ahead = (i + 1 < n) & (valid_ref[ahead] != 0)

    def step(cur, oth):
        @pl.when(used_ahead)
        def _():
            for r in range(MOE_BLOCK):
                row_copy(ahead, oth, r).start()

        block_wait(cur)
        x = xbufs[cur][...].astype(BF)
        hid = (_silu(_dot(x, wg_bf[...])) * _dot(x, wu_bf[...])).astype(BF)
        y_ref[...] = _dot(hid, wd_bf[...])

    @pl.when(used & (i % 2 == 0))
    def _():
        step(0, 1)

    @pl.when(used & (i % 2 == 1))
    def _():
        step(1, 0)

    @pl.when(jnp.logical_not(used))
    def _():
        y_ref[...] = jnp.zeros_like(y_ref)


def _experts(block_e, block_next, block_valid, row_tok, xf, w_eg, w_eu, w_ed):
    nrows = row_tok.shape[0]
    d = w_eg.shape[1]
    nblocks = nrows // MOE_BLOCK
    de = w_eg.shape[2]
    any_spec = pl.BlockSpec(memory_space=pl.ANY)
    grid_spec = pltpu.PrefetchScalarGridSpec(
        num_scalar_prefetch=4,
        grid=(nblocks,),
        in_specs=[any_spec, any_spec, any_spec, any_spec],
        out_specs=pl.BlockSpec((MOE_BLOCK, d), lambda i, *_: (i, 0)),
        scratch_shapes=[pltpu.VMEM((MOE_BLOCK, d), F32), pltpu.VMEM((MOE_BLOCK, d), F32),
                        pltpu.SemaphoreType.DMA((2,)),
                        pltpu.VMEM((d, de), F32), pltpu.VMEM((d, de), F32), pltpu.VMEM((de, d), F32),
                        pltpu.SemaphoreType.DMA((3,)),
                        pltpu.VMEM((d, de), BF), pltpu.VMEM((d, de), BF), pltpu.VMEM((de, d), BF)],
    )
    return pl.pallas_call(
        _expert_kernel,
        grid_spec=grid_spec,
        out_shape=jax.ShapeDtypeStruct((nrows, d), F32),
        compiler_params=_cp(("arbitrary",), 28),
        name="expert_blocks",
    )(block_e, block_next, block_valid, row_tok, xf, w_eg, w_eu, w_ed)


def _final_kernel(pos_ref, h_ref, gate_ref, w_ref, yb_hbm, ya_ref, ys_ref, gbuf, sem, *, na, tm):
    i = pl.program_id(0)
    n = pl.num_programs(0)
    slot = i % 2

    def row_copy(tile, slot_, r, k):
        p = pos_ref[(tile * tm + r) * TOP_K + k]
        return pltpu.make_async_copy(yb_hbm.at[pl.ds(p, 1), :], gbuf.at[slot_, k, pl.ds(r, 1), :], sem.at[slot_])

    def start_tile(tile, slot_):
        for r in range(tm):
            for k in range(TOP_K):
                row_copy(tile, slot_, r, k).start()

    @pl.when(i == 0)
    def _():
        start_tile(0, 0)

    @pl.when(i + 1 < n)
    def _():
        start_tile(i + 1, 1 - slot)

    for r in range(tm):
        for k in range(TOP_K):
            row_copy(i, slot, r, k).wait()

    gate = gate_ref[...]
    y = h_ref[...]
    ffn = gbuf[slot, 0] * gate[:, 0:1]
    for k in range(1, TOP_K):
        ffn = ffn + gbuf[slot, k] * gate[:, k:k + 1]
    y = y + ffn
    ms = jnp.mean(y * y, axis=-1, keepdims=True)
    out = y * lax.rsqrt(ms + EPS) * w_ref[...]

    @pl.when(i < na)
    def _():
        ya_ref[...] = out

    @pl.when(i >= na)
    def _():
        ys_ref[...] = out


def _final(pos_flat, h, gate, yb, w, rows_a, rows_b):
    t, d = h.shape
    tm = _pick(math.gcd(rows_a, rows_b), 128)
    na, nb = rows_a // tm, rows_b // tm
    grid_spec = pltpu.PrefetchScalarGridSpec(
        num_scalar_prefetch=1,
        grid=(na + nb,),
        in_specs=[pl.BlockSpec((tm, d), lambda i, pos: (i, 0)),
                  pl.BlockSpec((tm, TOP_K), lambda i, pos: (i, 0)),
                  pl.BlockSpec((1, d), lambda i, pos: (0, 0)),
                  pl.BlockSpec(memory_space=pl.ANY)],
        out_specs=[pl.BlockSpec((tm, d), lambda i, pos: (jnp.minimum(i, na - 1), 0)),
                   pl.BlockSpec((tm, d), lambda i, pos: (jnp.maximum(i - na, 0), 0))],
        scratch_shapes=[pltpu.VMEM((2, TOP_K, tm, d), F32), pltpu.SemaphoreType.DMA((2,))],
    )
    return pl.pallas_call(
        functools.partial(_final_kernel, na=na, tm=tm),
        grid_spec=grid_spec,
        out_shape=[jax.ShapeDtypeStruct((rows_a, d), F32), jax.ShapeDtypeStruct((rows_b, d), F32)],
        compiler_params=_cp(("arbitrary",), 16),
        name="combine_final_norm",
    )(pos_flat, h, gate, w.reshape(1, d), yb)


def kernel(x_prompt, x_sample, state_hgrn, state_gdn, state_conv, norm_mix, w_in, conv_w, ha_lb_logits, ha_onorm, w_pa, gd_A_log, gd_dt_bias, gd_onorm, w_pb, w_out, norm_ffn, w_router_group, b_router_group, w_router_expert, b_router_expert, w_exp_gate, w_exp_up, w_exp_down, norm_final):
    depth = state_hgrn.shape[0]
    assert depth == 1, "one decoder layer"
    bp, lp, d = x_prompt.shape
    bs, ls, _ = x_sample.shape
    _, _, ha_heads, ha_dk, ha_dv = state_hgrn.shape
    _, _, vheads, gd_dk, gd_dv = state_gdn.shape
    conv_dim = state_conv.shape[3]
    ha_kdim, ha_vdim = ha_heads * ha_dk, ha_heads * ha_dv
    gd_vdim = vheads * gd_dv
    ngroups = w_router_group.shape[2]
    nexp = w_router_expert.shape[2]
    rp, rs = bp * lp, bs * ls
    t = rp + rs

    off_f = ha_kdim
    off_i = off_f + ha_kdim
    off_g = off_i + ha_vdim
    off_qkv = off_g + ha_vdim
    off_z = off_qkv + conv_dim
    off_b = off_z + gd_vdim
    off_gate = off_b + 2 * vheads

    xn = _rmsnorm_bf16(x_prompt.reshape(rp, d), x_sample.reshape(rs, d), norm_mix[0])

    assert ha_kdim == ha_vdim and off_qkv % conv_dim == 0 and off_z % gd_vdim == 0
    w_bf = w_in[0].astype(BF)
    proj, k_a = _inproj(xn, w_bf, ha_lb_logits, off_b, ha_kdim, ha_dk ** -0.5)
    q_a, logf, v_a, g_a = ((proj, cb) for cb in (0, off_f // ha_kdim, off_i // ha_kdim, off_g // ha_kdim))
    k_a = (k_a, 0)
    qkv = (proj, off_qkv // conv_dim)
    z_b = (proj, off_z // gd_vdim)
    w_gates_bf = w_bf[:, off_gate:]
    bg, bgt = _beta_decay(xn, w_bf[:, off_b:off_gate], gd_A_log[0], gd_dt_bias[0])

    cg_p = _pick(lp, 128, GDN_CHUNK) if lp >= GDN_CHUNK else lp
    sub_p = min(HGRN_SUB, lp)
    chunk_p = min(GDN_CHUNK, lp)
    cg_s = ls
    zeros_h = jnp.zeros((bp, ha_heads, ha_dk, ha_dv), F32)
    zeros_g = jnp.zeros((bp, vheads, gd_dk, gd_dv), F32)
    zeros_c = jnp.zeros((bp, CONV_K - 1, conv_dim), F32)

    oa_p, sh_p = _hgrn2(q_a, logf, k_a, v_a, g_a, ha_onorm[0], zeros_h,
                        row0=0, seq_len=lp, rows_per_step=cg_p, sub=sub_p, seqs_per_step=1)
    oa_s, sh_s = _hgrn2(q_a, logf, k_a, v_a, g_a, ha_onorm[0], state_hgrn[0],
                        row0=rp, seq_len=ls, rows_per_step=cg_s, sub=min(HGRN_SUB, ls),
                        seqs_per_step=math.gcd(_pick(bs, HGRN_SAMPLE_SEQS, 1), max(rp // cg_s, 1)))

    def time_on_lanes(rows0, nrows, cg):
        part = bgt[:, rows0:rows0 + nrows].reshape(2 * vheads, nrows // cg, cg)
        return jnp.transpose(part, (1, 0, 2))

    seqs_s = _pick(bs, GDN_SAMPLE_SEQS, 1)
    cg_g = _pick(lp, GDN_ROWS, GDN_CHUNK) if lp >= GDN_CHUNK else lp
    bgt_p = time_on_lanes(0, rp, cg_g)
    bgt_s = time_on_lanes(rp, rs, seqs_s * cg_s)
    ob_p, sg_p, sc_p = _gdn(qkv, z_b, bg, bgt_p, conv_w[0], gd_onorm[0], zeros_g, zeros_c,
                            row0=0, seq_len=lp, rows_per_step=cg_g, chunk=chunk_p, seqs_per_step=1)
    ob_s, sg_s, sc_s = _gdn(qkv, z_b, bg, bgt_s, conv_w[0], gd_onorm[0], state_gdn[0],
                            state_conv[0], row0=rp, seq_len=ls, rows_per_step=cg_s, chunk=min(GDN_CHUNK, ls),
                            seqs_per_step=seqs_s)

    merged = _merge(xn, oa_p, oa_s, ob_p, ob_s, w_gates_bf, w_pa[0], w_pb[0])

    nr = LANES
    w_router = jnp.concatenate([w_router_group[0], w_router_expert[0]], axis=1)
    w_router = jnp.pad(w_router, ((0, 0), (0, nr - ngroups - nexp))).astype(BF)
    b_router = jnp.pad(jnp.concatenate([b_router_group[0], b_router_expert[0]]), (0, nr - ngroups - nexp))
    h, xf, logits = _outproj(merged, x_prompt.reshape(rp, d), x_sample.reshape(rs, d), w_out[0],
                             norm_ffn[0], w_router)

    eid, gate, rank, cnt = _route(logits, b_router.reshape(1, nr), ngroups, nexp)
    nblocks = -(-(t * TOP_K) // MOE_BLOCK) + nexp
    pos, block_meta = _plan(cnt, eid, rank, ngroups, nexp, nblocks)
    block_e, block_next, block_valid = (block_meta[:nblocks, j] for j in range(3))

    pos_flat = pos.reshape(-1)
    row_tok = _invert(pos_flat, nblocks * MOE_BLOCK)
    yb = _experts(block_e, block_next, block_valid, row_tok, xf, w_exp_gate[0], w_exp_up[0], w_exp_down[0])
    y_p, y_s = _final(pos_flat, h, gate, yb, norm_final, rp, rs)
    return (y_p.reshape(bp, lp, d), y_s.reshape(bs, ls, d),
            sh_p[None], sg_p[None], sc_p[None], sh_s[None], sg_s[None], sc_s[None])
```

```python
import functools
import math

import jax
import jax.numpy as jnp
from jax import lax
from jax.experimental import pallas as pl
from jax.experimental.pallas import tpu as pltpu

F32 = jnp.float32
BF = jnp.bfloat16
I32 = jnp.int32
EPS = 1e-6
CONV_K = 4
TOP_K = 2
MOE_BLOCK = 128
GDN_CHUNK = 64
GDN_ROWS = 256
GDN_SAMPLE_SEQS = 8
HGRN_SUB = 16
HGRN_ROWS = 128
HGRN_SAMPLE_SEQS = 8
LOG2E = 1.4426950408889634
LANES = 128
NEG = -3.0e38
MIB = 1024 * 1024


def _cp(sem, vmem_mib=48):
    return pltpu.CompilerParams(dimension_semantics=sem, vmem_limit_bytes=vmem_mib * MIB)


def _pick(n, pref, mult=8):
    best = None
    for t in range(mult, min(n, pref) + 1, mult):
        if n % t == 0:
            best = t
    return best if best is not None else n


def _sigmoid(x):
    return 1.0 / (1.0 + jnp.exp(-x))


def _silu(x):
    return x * _sigmoid(x)


def _iota(shape, dim):
    return lax.broadcasted_iota(I32, shape, dim)


def _dot(a, b):
    return jnp.dot(a, b, preferred_element_type=F32)


def _dot_nt(a, b):
    return lax.dot_general(a, b, (((1,), (1,)), ((), ())), preferred_element_type=F32)


def _dot_tn(a, b):
    return lax.dot_general(a, b, (((0,), (0,)), ((), ())), preferred_element_type=F32)


def _split(a):
    hi = a.astype(BF)
    lo = (a - hi.astype(F32)).astype(BF)
    return hi, lo


def _split3(a, axis=0):
    hi = a.astype(BF)
    r1 = a - hi.astype(F32)
    mid = r1.astype(BF)
    lo = (r1 - mid.astype(F32)).astype(BF)
    return jnp.concatenate([hi, mid, lo], axis=axis)


def _two_source_specs(rows_a, rows_b, tm, width):
    na, nb = rows_a // tm, rows_b // tm
    spec_a = pl.BlockSpec((tm, width), lambda i: (jnp.minimum(i, na - 1), 0))
    spec_b = pl.BlockSpec((tm, width), lambda i: (jnp.maximum(i - na, 0), 0))
    return na, nb, spec_a, spec_b


def _rmsnorm_kernel(xa_ref, xb_ref, w_ref, o_ref, *, na):
    i = pl.program_id(0)

    def body(x_ref):
        x = x_ref[...]
        ms = jnp.mean(x * x, axis=-1, keepdims=True)
        o_ref[...] = (x * lax.rsqrt(ms + EPS) * w_ref[...]).astype(o_ref.dtype)

    @pl.when(i < na)
    def _():
        body(xa_ref)

    @pl.when(i >= na)
    def _():
        body(xb_ref)


def _rmsnorm_bf16(xa, xb, w):
    d = xa.shape[1]
    tm = _pick(math.gcd(xa.shape[0], xb.shape[0]), 512)
    na, nb, sa, sb = _two_source_specs(xa.shape[0], xb.shape[0], tm, d)
    return pl.pallas_call(
        functools.partial(_rmsnorm_kernel, na=na),
        grid=(na + nb,),
        in_specs=[sa, sb, pl.BlockSpec((1, d), lambda i: (0, 0))],
        out_specs=pl.BlockSpec((tm, d), lambda i: (i, 0)),
        out_shape=jax.ShapeDtypeStruct((xa.shape[0] + xb.shape[0], d), BF),
        compiler_params=_cp(("arbitrary",), 24),
        name="rmsnorm_mix",
    )(xa, xb, w.reshape(1, d))


def _inproj_kernel(x_ref, w_ref, lb_ref, p_ref, kf_ref, *, jq, jf0, jf1, q_scale):
    j = pl.program_id(0)
    acc = _dot(x_ref[...], w_ref[...])
    is_forget = (j >= jf0) & (j < jf1)

    @pl.when(is_forget)
    def _():
        logf, kf = _forget_epilogue(acc, lb_ref[...])
        p_ref[...] = logf
        kf_ref[...] = kf

    @pl.when(jnp.logical_not(is_forget))
    def _():
        p_ref[...] = acc * jnp.where(j < jq, q_scale, 1.0)


def _inproj(x_bf, w, lb_logits, ncols, kdim, q_scale):
    t, k = x_bf.shape
    tn = _pick(math.gcd(ncols, kdim), 1024, LANES)
    tm = _pick(t, 1024)
    ni = t // tm
    jq, jf0, jf1 = kdim // tn, kdim // tn, 2 * kdim // tn
    nf = jf1 - jf0
    fcol = lambda j: jnp.clip(j - jf0, 0, nf - 1)
    frow = lambda j, i: jnp.where(j < jf0, 0, jnp.where(j >= jf1, ni - 1, i))
    kern = functools.partial(_inproj_kernel, jq=jq, jf0=jf0, jf1=jf1, q_scale=q_scale)
    return pl.pallas_call(
        kern,
        grid=(ncols // tn, ni),
        in_specs=[pl.BlockSpec((tm, k), lambda j, i: (i, 0)),
                  pl.BlockSpec((k, tn), lambda j, i: (0, j)),
                  pl.BlockSpec((lb_logits.shape[0], tn), lambda j, i: (0, fcol(j)))],
        out_specs=[pl.BlockSpec((tm, tn), lambda j, i: (i, j)),
                   pl.BlockSpec((tm, tn), lambda j, i: (frow(j, i), fcol(j)))],
        out_shape=[jax.ShapeDtypeStruct((t, ncols), F32), jax.ShapeDtypeStruct((t, kdim), F32)],
        compiler_params=_cp(("arbitrary", "arbitrary"), 44),
        name="in_proj",
    )(x_bf, w, lb_logits)


def _forget_epilogue(acc, lb_logits):
    m = jnp.max(lb_logits, axis=0, keepdims=True)
    e = jnp.exp(lb_logits - m)
    lb = e[0:1, :] / jnp.sum(e, axis=0, keepdims=True)
    f = lb + (1.0 - lb) * _sigmoid(acc)
    return jnp.log(f), 1.0 - f


def _beta_decay_kernel(x_ref, w_ref, wt_ref, prow_ref, pcol_ref, o_ref, ot_ref, *, vh):
    x = x_ref[...]
    acc = _dot(x, w_ref[...])
    acct = _dot_nt(wt_ref[...], x)

    def act(a, is_beta, a_neg_exp, dt_bias):
        z = a + dt_bias
        softplus = jnp.maximum(z, 0.0) + jnp.log(1.0 + jnp.exp(-jnp.abs(z)))
        return jnp.where(is_beta, _sigmoid(a), a_neg_exp * softplus)

    prow = prow_ref[...]
    pcol = pcol_ref[...]
    o_ref[...] = act(acc, _iota(acc.shape, 1) < vh, prow[0:1, :], prow[1:2, :])
    ot_ref[...] = act(acct, _iota(acct.shape, 0) < vh, pcol[:, 0:1], pcol[:, 1:2])


def _beta_decay(x_bf, w_ba, a_log, dt_bias):
    t, k = x_bf.shape
    vh = a_log.shape[0]
    tm = t if t <= 2048 else _pick(t, 1024, LANES)
    zeros = jnp.zeros((vh,), F32)
    prow = jnp.stack([jnp.concatenate([zeros, -jnp.exp(a_log)]), jnp.concatenate([zeros, dt_bias])])
    return pl.pallas_call(
        functools.partial(_beta_decay_kernel, vh=vh),
        grid=(t // tm,),
        in_specs=[pl.BlockSpec((tm, k), lambda i: (i, 0)),
                  pl.BlockSpec((k, 2 * vh), lambda i: (0, 0)),
                  pl.BlockSpec((2 * vh, k), lambda i: (0, 0)),
                  pl.BlockSpec((2, 2 * vh), lambda i: (0, 0)),
                  pl.BlockSpec((2 * vh, 2), lambda i: (0, 0))],
        out_specs=[pl.BlockSpec((tm, 2 * vh), lambda i: (i, 0)),
                   pl.BlockSpec((2 * vh, tm), lambda i: (0, i))],
        out_shape=[jax.ShapeDtypeStruct((t, 2 * vh), F32), jax.ShapeDtypeStruct((2 * vh, t), F32)],
        compiler_params=_cp(("arbitrary",), 16),
        name="proj_beta_decay",
    )(x_bf, w_ba.astype(BF), w_ba.T.astype(BF), prow, prow.T)


def _hgrn2_kernel(q_ref, lf_ref, k_ref, v_ref, g_ref, on_ref, s0_ref, o_ref, so_ref, s_scr, gall_scr,
                  g_scr, k_scr, v_scr, *, heads, dk, dv, sub, nsub, nseq, nchunks, group, unroll):
    c = pl.program_id(1)
    cg = sub * nsub

    @pl.when(c == 0)
    def _():
        s_scr[...] = s0_ref[...]

    rows_all = nseq * cg
    shift = sub.bit_length() - 1
    r_i = _iota((rows_all, 3 * rows_all), 0)
    c_i = _iota((rows_all, 3 * rows_all), 1) & (rows_all - 1)
    same = lax.shift_right_logical(r_i, shift) == lax.shift_right_logical(c_i, shift)
    btril3 = jnp.where((r_i >= c_i) & same, 1.0, 0.0).astype(BF)
    gall_scr[...] = _dot(btril3, _split3(lf_ref[...])) * LOG2E
    onorm = on_ref[...]
    n_aug = 16 - sub % 16
    aug_rhs = jnp.concatenate([jnp.zeros((n_aug, dv), BF), jnp.ones((n_aug, dv), BF)], axis=1)
    zeros_v = jnp.zeros((sub, dv), BF)
    zeros_aug = jnp.zeros((n_aug - 3, dk), F32)

    nh = sub // 8
    row8 = _iota((8, 1), 0)

    def block(it, carry):
        for u in range(unroll):
            sub_chunk(it * unroll + u, g_scr.at[u], k_scr.at[u], v_scr.at[u])
        return carry

    def sub_chunk(sb, g_scr, k_scr, v_scr):
        for q in range(nseq):
            rows = pl.ds(pl.multiple_of(q * cg + sb * sub, sub), sub)
            g_scr[q] = gall_scr[rows, :]
            k_scr[q] = k_ref[rows, :]
            v_scr[q] = v_ref[rows, :]
        problems = [(q, h) for q in range(nseq) for h in range(heads)]
        for g0 in range(0, len(problems), group):
            ps = []
            for (q, h) in problems[g0:g0 + group]:
                r0 = pl.multiple_of(q * cg + sb * sub, sub)
                rows = pl.ds(r0, sub)
                kc = pl.ds(h * dk, dk)
                vc = pl.ds(h * dv, dv)
                ps.append(dict(q=q, h=h, r0=r0, rows=rows, kc=kc, vc=vc,
                               gc=g_scr[q, :, kc],
                               qv=q_ref[rows, kc], k=k_ref[rows, kc], v=v_ref[rows, vc]))
            for p in ps:
                p["s_old"] = s_scr[p["q"], p["h"]]
                p["o_state"] = _dot((p["qv"] * jnp.exp2(p["gc"])).astype(BF), p["s_old"].astype(BF))
                p["o8"] = [jnp.zeros((8, dv), F32) for _ in range(nh)]
                p["q8"] = [p["qv"][8 * i:8 * i + 8] for i in range(nh)]
                p["g8"] = [p["gc"][8 * i:8 * i + 8] for i in range(nh)]
            for j in range(sub):
                for p in ps:
                    kj = k_scr[p["q"], pl.ds(j, 1), p["kc"]]
                    vj = v_scr[p["q"], pl.ds(j, 1), p["vc"]]
                    gj = g_scr[p["q"], pl.ds(j, 1), p["kc"]]
                    for i in range(j // 8, nh):
                        pr = (p["q8"][i] * kj) * jnp.exp2(p["g8"][i] - gj)
                        a_col = jnp.sum(pr, axis=-1, keepdims=True)
                        if i == j // 8:
                            a_col = jnp.where(row8 >= j % 8, a_col, 0.0)
                        p["o8"][i] = p["o8"][i] + a_col * vj
            for p in ps:
                p["o"] = p["o_state"] + (jnp.concatenate(p["o8"], axis=0) if nh > 1 else p["o8"][0])
            for p in ps:
                gend = p["gc"][sub - 1:sub, :]
                dend = jnp.exp2(gend)
                d_hi = dend.astype(BF).astype(F32)
                d_mid = (dend - d_hi).astype(BF).astype(F32)
                d_lo = (dend - d_hi - d_mid).astype(BF).astype(F32)
                kdec = p["k"] * jnp.exp2(gend - p["gc"])
                lhs = jnp.concatenate([kdec, d_hi, d_mid, d_lo, zeros_aug], axis=0).astype(BF)
                rhs = jnp.concatenate([jnp.concatenate([p["v"].astype(BF), zeros_v], axis=1), aug_rhs], axis=0)
                p["kv"] = _dot_tn(lhs, rhs)
            for p in ps:
                s_scr[p["q"], p["h"]] = p["kv"][:, dv:] * p["s_old"] + p["kv"][:, :dv]
            for p in ps:
                o = p["o"]
                ms = jnp.mean(o * o, axis=-1, keepdims=True)
                o_ref[p["rows"], p["vc"]] = (o * lax.rsqrt(ms + EPS) * onorm) * _silu(g_ref[p["rows"], p["vc"]])

    lax.fori_loop(0, nsub // unroll, block, 0)

    @pl.when(c == nchunks - 1)
    def _():
        so_ref[...] = s_scr[...]


def _hgrn2(q, lf, kf, v, gate, onorm, s0, *, row0, seq_len, rows_per_step, sub, seqs_per_step):
    nseq_total, heads, dk, dv = s0.shape
    assert dk == dv, "state decay is applied with a (dk, dv) tile"
    width = heads * dk
    cg = rows_per_step
    nseq = seqs_per_step
    nchunks = seq_len // cg
    assert nseq == 1 or nchunks == 1, "several sequences per step only when a step covers whole sequences"
    rows = nseq * cg
    b0 = row0 // rows
    assert sub & (sub - 1) == 0 and sub % 8 == 0 and rows & (rows - 1) == 0
    unroll = next(u for u in (8, 4, 2, 1) if (cg // sub) % u == 0)
    def row_spec(col_block):
        return pl.BlockSpec((rows, width), lambda b, c: (b0 + b * nchunks + c, col_block))

    st_spec = pl.BlockSpec((nseq, heads, dk, dv), lambda b, c: (b, 0, 0, 0))
    kern = functools.partial(_hgrn2_kernel, heads=heads, dk=dk, dv=dv, sub=sub, nsub=cg // sub, nseq=nseq,
                             nchunks=nchunks, group=8, unroll=unroll)
    operands = (q, lf, kf, v, gate)
    return pl.pallas_call(
        kern,
        grid=(nseq_total // nseq, nchunks),
        in_specs=[row_spec(cb) for _, cb in operands]
                 + [pl.BlockSpec((1, dv), lambda b, c: (0, 0)), st_spec],
        out_specs=[pl.BlockSpec((rows, heads * dv), lambda b, c: (b * nchunks + c, 0)), st_spec],
        out_shape=[jax.ShapeDtypeStruct((nseq_total * seq_len, heads * dv), F32),
                   jax.ShapeDtypeStruct(s0.shape, F32)],
        scratch_shapes=[pltpu.VMEM((nseq, heads, dk, dv), F32), pltpu.VMEM((rows, width), F32),
                        pltpu.VMEM((unroll, nseq, sub, width), F32), pltpu.VMEM((unroll, nseq, sub, width), F32),
                        pltpu.VMEM((unroll, nseq, sub, heads * dv), F32)],
        compiler_params=_cp(("arbitrary", "arbitrary"), 24),
        name="hgrn2_recurrence",
    )(*[a for a, _ in operands], onorm.reshape(1, dv), s0)


def _gdn_kernel(qkv_ref, z_ref, bg_ref, bgt_ref, cw_ref, on_ref, s0_ref, c0_ref,
                o_ref, so_ref, co_ref, s_scr, xx, xs,
                *, kheads, vheads, dk, dv, chunk, nsub, nseq, nchunks, group_kh):
    c = pl.program_id(1)
    cg = chunk * nsub
    kdim = kheads * dk
    rep = vheads // kheads
    tail0 = 8 - (CONV_K - 1)

    @pl.when(c == 0)
    def _():
        s_scr[...] = s0_ref[...]
        xx[:, 0:8, :] = c0_ref[...]

    for q in range(nseq):
        xx[q, 8:8 + cg, :] = qkv_ref[q * cg:(q + 1) * cg, :]
        conv = xx[q, pl.ds(tail0, cg), :] * cw_ref[0:1, :]
        for j in range(1, CONV_K):
            conv = conv + xx[q, pl.ds(tail0 + j, cg), :] * cw_ref[j:j + 1, :]
        xs[q] = _silu(conv)

    @pl.when(c == nchunks - 1)
    def _():
        for q in range(nseq):
            co_ref[q] = xx[q, pl.ds(8 + cg - (CONV_K - 1), CONV_K - 1), :]

    for q in range(nseq):
        xx[q, 0:8, :] = xx[q, cg:cg + 8, :]

    wp = chunk
    r_i = _iota((chunk, wp), 0)
    c_i = _iota((chunk, wp), 1)
    causal = r_i >= c_i
    strict = r_i > c_i

    def pad_rows(a):
        if a.shape[0] == wp:
            return a
        if a.dtype == BF and a.shape[0] % 16:
            return pad_rows(a.astype(F32)).astype(BF)
        return jnp.concatenate([a, jnp.zeros((wp - a.shape[0],) + a.shape[1:], a.dtype)], axis=0)

    def split_lhs(a):
        a_hi = a.astype(BF).astype(F32)
        return jnp.concatenate([a_hi, a_hi, a - a_hi], axis=1).astype(BF)

    def split_rhs(b):
        bh, bl = (pad_rows(t) for t in _split(b))
        return jnp.concatenate([bh, bl, bh], axis=0)
    cm = chunk - 1
    tril3 = jnp.where(_iota((chunk, 3 * chunk), 0) >= (_iota((chunk, 3 * chunk), 1) & cm), 1.0, 0.0).astype(BF)
    t_r = _iota((3 * chunk, wp), 0) & cm
    t_c = _iota((3 * chunk, wp), 1)
    triu3 = jnp.where((t_r <= t_c) & (t_c < chunk), 1.0, 0.0).astype(BF)
    onorm = on_ref[...]

    cums = {}

    def cumulative_decay(q, s):
        if (q, s) not in cums:
            r0 = q * cg + s * chunk
            bg = bg_ref[r0:r0 + chunk, :]
            gt = bgt_ref[0, vheads:2 * vheads, r0:r0 + chunk]
            cums[(q, s)] = (bg, _dot(tril3, _split3(bg[:, vheads:2 * vheads], axis=0)),
                            _dot(_split3(gt, axis=1), triu3))
        return cums[(q, s)]

    def run_group(items):
        probs = []
        for (q, s, kh) in items:
            r0 = q * cg + s * chunk
            x0 = s * chunk
            bg, gcol_all, grow_all = cumulative_decay(q, s)
            for kh in (kh,):
                qh = xs[q, x0:x0 + chunk, kh * dk:(kh + 1) * dk]
                kk_ = xs[q, x0:x0 + chunk, kdim + kh * dk:kdim + (kh + 1) * dk]
                qn = qh * lax.rsqrt(jnp.sum(qh * qh, axis=-1, keepdims=True) + EPS) * (dk ** -0.5)
                kn = kk_ * lax.rsqrt(jnp.sum(kk_ * kk_, axis=-1, keepdims=True) + EPS)
                kn_bf = kn.astype(BF)
                kn_pad = pad_rows(kn_bf)
                kk = _dot_nt(kn_bf, kn_pad)
                qk = _dot_nt(qn.astype(BF), kn_pad)
                for r in range(rep):
                    h = kh * rep + r
                    vh_ = xs[q, x0:x0 + chunk, 2 * kdim + h * dv:2 * kdim + (h + 1) * dv]
                    gcol = gcol_all[:, h:h + 1]
                    grow = grow_all[h:h + 1, :]
                    beta = bg[:, h:h + 1]
                    decay = jnp.where(causal, jnp.exp(jnp.minimum(gcol - grow, 0.0)), 0.0)
                    eg = jnp.exp(gcol)
                    gend = gcol[chunk - 1:chunk, :]
                    probs.append(dict(
                        q=q, s=s, h=h, r0=r0,
                        x=jnp.where(strict, -(beta * kk * decay), 0.0),
                        y=jnp.concatenate([vh_ * beta, kn * (beta * eg)], axis=1),
                        a_bf=(qk * decay).astype(BF),
                        qg=qn * eg,
                        kdec_bf=(kn * jnp.exp(gend - gcol)).astype(BF),
                        send=jnp.exp(gend)))

        if chunk <= 16:
            for j in range(chunk - 1):
                for p in probs:
                    p["y"] = p["y"] + p["x"][:, j:j + 1] * p["y"][j:j + 1, :]
        else:
            n_fac = int(math.log2(chunk))
            for k in range(n_fac):
                for p in probs:
                    p["x_lhs"] = split_lhs(p["x"])
                    p["y"] = p["y"] + _dot(p["x_lhs"], split_rhs(p["y"]))
                if k + 1 < n_fac:
                    for p in probs:
                        p["x"] = _dot(p["x_lhs"], split_rhs(p["x"]))

        for s in sorted({p["s"] for p in probs}):
            cur = [p for p in probs if p["s"] == s]
            for p in cur:
                p["s_old"] = s_scr[p["q"], p["h"]]
                lhs = jnp.concatenate([p["y"][:, dv:], p["qg"]], axis=0).astype(BF)
                p["ws"] = _dot(lhs, p["s_old"].astype(BF))
            for p in cur:
                p["u_bf"] = (p["y"][:, :dv] - p["ws"][:chunk]).astype(BF)
            for p in cur:
                s_scr[p["q"], p["h"]] = p["send"] * p["s_old"] + _dot_tn(p["kdec_bf"], p["u_bf"])
            for p in cur:
                o = p["ws"][chunk:] + _dot(p["a_bf"], pad_rows(p["u_bf"]))
                ms = jnp.mean(o * o, axis=-1, keepdims=True)
                zc = slice(p["h"] * dv, (p["h"] + 1) * dv)
                rows = slice(p["r0"], p["r0"] + chunk)
                o_ref[rows, zc] = (o * lax.rsqrt(ms + EPS) * onorm) * _silu(z_ref[rows, zc])

    for g0 in range(0, kheads, group_kh):
        run_group([(q, s, kh) for q in range(nseq) for s in range(nsub)
                   for kh in range(g0, min(g0 + group_kh, kheads))])

    @pl.when(c == nchunks - 1)
    def _():
        so_ref[...] = s_scr[...]


def _gdn(qkv, z, bg, bgt3, conv_w, onorm, s0, conv0, *, row0, seq_len, rows_per_step, chunk, seqs_per_step):
    nseq_total, vheads, dk, dv = s0.shape
    (qkv, qkv_cb), (z, z_cb) = qkv, z
    conv_dim = conv_w.shape[1]
    kheads = (conv_dim - vheads * dv) // (2 * dk)
    cg = rows_per_step
    nseq = seqs_per_step
    nchunks = seq_len // cg
    assert nseq == 1 or nchunks == 1, "several sequences per step only when a step covers whole sequences"
    rows = nseq * cg
    b0 = row0 // rows
    conv0p = jnp.pad(conv0, ((0, 0), (8 - (CONV_K - 1), 0), (0, 0)))
    rmap = lambda b, c: (b0 + b * nchunks + c, 0)
    st_spec = pl.BlockSpec((nseq, vheads, dk, dv), lambda b, c: (b, 0, 0, 0))
    kern = functools.partial(_gdn_kernel, kheads=kheads, vheads=vheads, dk=dk, dv=dv, chunk=chunk,
                             nsub=cg // chunk, nseq=nseq, nchunks=nchunks, group_kh=kheads)
    return pl.pallas_call(
        kern,
        grid=(nseq_total // nseq, nchunks),
        in_specs=[pl.BlockSpec((rows, conv_dim), lambda b, c: (b0 + b * nchunks + c, qkv_cb)),
                  pl.BlockSpec((rows, vheads * dv), lambda b, c: (b0 + b * nchunks + c, z_cb)),
                  pl.BlockSpec((rows, 2 * vheads), rmap),
                  pl.BlockSpec((1, 2 * vheads, rows), lambda b, c: (b * nchunks + c, 0, 0)),
                  pl.BlockSpec((CONV_K, conv_dim), lambda b, c: (0, 0)),
                  pl.BlockSpec((1, dv), lambda b, c: (0, 0)),
                  st_spec,
                  pl.BlockSpec((nseq, 8, conv_dim), lambda b, c: (b, 0, 0))],
        out_specs=[pl.BlockSpec((rows, vheads * dv), lambda b, c: (b * nchunks + c, 0)),
                   st_spec,
                   pl.BlockSpec((nseq, CONV_K - 1, conv_dim), lambda b, c: (b, 0, 0))],
        out_shape=[jax.ShapeDtypeStruct((nseq_total * seq_len, vheads * dv), F32),
                   jax.ShapeDtypeStruct(s0.shape, F32),
                   jax.ShapeDtypeStruct((nseq_total, CONV_K - 1, conv_dim), F32)],
        scratch_shapes=[pltpu.VMEM((nseq, vheads, dk, dv), F32), pltpu.VMEM((nseq, 8 + cg, conv_dim), F32),
                        pltpu.VMEM((nseq, cg, conv_dim), F32)],
        compiler_params=_cp(("arbitrary", "arbitrary"), 28),
        name="gdn_recurrence",
    )(qkv, z, bg, bgt3, conv_w, onorm.reshape(1, dv), s0, conv0p)


def _merge_kernel(xn_ref, oa1_ref, oa2_ref, ob1_ref, ob2_ref, wga_ref, wgb_ref, wa_ref, wb_ref, o_ref,
                  oa_bf, ob_bf, *, na):
    i = pl.program_id(0)
    j = pl.program_id(1)

    @pl.when((j == 0) & (i < na))
    def _():
        oa_bf[...] = oa1_ref[...].astype(BF)
        ob_bf[...] = ob1_ref[...].astype(BF)

    @pl.when((j == 0) & (i >= na))
    def _():
        oa_bf[...] = oa2_ref[...].astype(BF)
        ob_bf[...] = ob2_ref[...].astype(BF)

    xn = xn_ref[...]
    ya = _dot(oa_bf[...], wa_ref[...])
    yb = _dot(ob_bf[...], wb_ref[...])
    ga = _dot(xn, wga_ref[...])
    gb = _dot(xn, wgb_ref[...])
    o_ref[...] = (_sigmoid(ga) * ya + _sigmoid(gb) * yb).astype(o_ref.dtype)


def _merge(xn, oa_p, oa_s, ob_p, ob_s, w_gates_bf, w_pa, w_pb):
    rp, rs = oa_p.shape[0], oa_s.shape[0]
    ka, kb = oa_p.shape[1], ob_p.shape[1]
    d = w_pa.shape[1]
    k = xn.shape[1]
    tm = _pick(math.gcd(rp, rs), 512)
    tn = _pick(d, 512, LANES)
    na, nb = rp // tm, rs // tm
    nj = d // tn
    amap = lambda i, j: (jnp.minimum(i, na - 1), 0)
    bmap = lambda i, j: (jnp.maximum(i - na, 0), 0)
    return pl.pallas_call(
        functools.partial(_merge_kernel, na=na),
        grid=(na + nb, nj),
        in_specs=[pl.BlockSpec((tm, k), lambda i, j: (i, 0)),
                  pl.BlockSpec((tm, ka), amap), pl.BlockSpec((tm, ka), bmap),
                  pl.BlockSpec((tm, kb), amap), pl.BlockSpec((tm, kb), bmap),
                  pl.BlockSpec((k, tn), lambda i, j: (0, j)),
                  pl.BlockSpec((k, tn), lambda i, j: (0, nj + j)),
                  pl.BlockSpec((ka, tn), lambda i, j: (0, j)),
                  pl.BlockSpec((kb, tn), lambda i, j: (0, j))],
        out_specs=pl.BlockSpec((tm, tn), lambda i, j: (i, j)),
        out_shape=jax.ShapeDtypeStruct((rp + rs, d), BF),
        scratch_shapes=[pltpu.VMEM((tm, ka), BF), pltpu.VMEM((tm, kb), BF)],
        compiler_params=_cp(("arbitrary", "arbitrary"), 40),
        name="gates_branch_proj_merge",
    )(xn, oa_p, oa_s, ob_p, ob_s, w_gates_bf, w_gates_bf, w_pa.astype(BF), w_pb.astype(BF))


def _outproj_kernel(m_ref, xa_ref, xb_ref, wo_ref, nw_ref, wr_ref, h_ref, xf_ref, lg_ref, *, na):
    i = pl.program_id(0)

    def body(x_ref):
        h = x_ref[...] + _dot(m_ref[...], wo_ref[...])
        h_ref[...] = h
        ms = jnp.mean(h * h, axis=-1, keepdims=True)
        xf = h * lax.rsqrt(ms + EPS) * nw_ref[...]
        xf_ref[...] = xf
        lg_ref[...] = _dot(xf.astype(BF), wr_ref[...])

    @pl.when(i < na)
    def _():
        body(xa_ref)

    @pl.when(i >= na)
    def _():
        body(xb_ref)


def _outproj(merged, xa, xb, w_out, norm_ffn, w_router):
    t, d = merged.shape
    tm = _pick(math.gcd(xa.shape[0], xb.shape[0]), 512)
    na, nb, sa, sb = _two_source_specs(xa.shape[0], xb.shape[0], tm, d)
    nr = w_router.shape[1]
    const = lambda i: (0, 0)
    return pl.pallas_call(
        functools.partial(_outproj_kernel, na=na),
        grid=(na + nb,),
        in_specs=[pl.BlockSpec((tm, d), lambda i: (i, 0)), sa, sb,
                  pl.BlockSpec((d, d), const, pipeline_mode=pl.Buffered(1)),
                  pl.BlockSpec((1, d), const),
                  pl.BlockSpec((d, nr), const, pipeline_mode=pl.Buffered(1))],
        out_specs=[pl.BlockSpec((tm, d), lambda i: (i, 0)),
                   pl.BlockSpec((tm, d), lambda i: (i, 0)),
                   pl.BlockSpec((tm, nr), lambda i: (i, 0))],
        out_shape=[jax.ShapeDtypeStruct((t, d), F32), jax.ShapeDtypeStruct((t, d), F32),
                   jax.ShapeDtypeStruct((t, nr), F32)],
        compiler_params=_cp(("arbitrary",), 48),
        name="out_proj_ffn_norm",
    )(merged, xa, xb, w_out.astype(BF), norm_ffn.reshape(1, d), w_router)


def _route_kernel(lg_ref, b_ref, eid_ref, gate_ref, rank_ref, cnt_ref, carry,
                  *, ngroups, nexp, tm):
    i = pl.program_id(0)

    @pl.when(i == 0)
    def _():
        carry[...] = jnp.zeros_like(carry)

    per_group = nexp // ngroups
    lg = lg_ref[...] + b_ref[...]
    lane = _iota(lg.shape, 1)
    big = jnp.int32(1 << 20)
    is_g = lane < ngroups
    gl = jnp.where(is_g, lg, NEG)
    gmax = jnp.max(gl, axis=-1, keepdims=True)
    gidx = jnp.min(jnp.where(gl == gmax, lane, big), axis=-1, keepdims=True)
    gsum = jnp.sum(jnp.where(is_g, jnp.exp(gl - gmax), 0.0), axis=-1, keepdims=True)
    gw = 1.0 / gsum
    elane = lane - ngroups
    in_grp = (elane >= gidx * per_group) & (elane < (gidx + 1) * per_group)
    el = jnp.where(in_grp, lg, NEG)
    v1 = jnp.max(el, axis=-1, keepdims=True)
    i1 = jnp.min(jnp.where(in_grp & (el == v1), elane, big), axis=-1, keepdims=True)
    in2 = in_grp & (elane != i1)
    el2 = jnp.where(in2, lg, NEG)
    v2 = jnp.max(el2, axis=-1, keepdims=True)
    i2 = jnp.min(jnp.where(in2 & (el2 == v2), elane, big), axis=-1, keepdims=True)
    p2 = jnp.exp(v2 - v1)
    den = 1.0 + p2
    lane2 = _iota((tm, TOP_K), 1)
    eid_ref[...] = jnp.where(lane2 == 0, i1, i2)
    gate_ref[...] = jnp.where(lane2 == 0, gw / den, gw * p2 / den)

    oh1 = (elane == i1).astype(F32)
    oh2 = (elane == i2).astype(F32)
    lower = (_iota((tm, tm), 0) > _iota((tm, tm), 1)).astype(BF)
    cs1 = _dot(lower, oh1.astype(BF))
    cs2 = _dot(lower, oh2.astype(BF))
    tot1 = jnp.sum(oh1, axis=0, keepdims=True)
    tot2 = jnp.sum(oh2, axis=0, keepdims=True)
    base = carry[0:1, :]
    r1 = jnp.sum(oh1 * (base + cs1), axis=-1, keepdims=True)
    r2 = jnp.sum(oh2 * (base + tot1 + cs2), axis=-1, keepdims=True)
    rank_ref[...] = jnp.where(lane2 == 0, r1, r2).astype(I32)
    new = base + tot1 + tot2
    carry[...] = jnp.broadcast_to(new, carry.shape)
    cnt_ref[...] = jnp.broadcast_to(new, cnt_ref.shape)


def _route(logits, bias_row, ngroups, nexp):
    t, nr = logits.shape
    tm = _pick(t, 512)
    kern = functools.partial(_route_kernel, ngroups=ngroups, nexp=nexp, tm=tm)
    return pl.pallas_call(
        kern,
        grid=(t // tm,),
        in_specs=[pl.BlockSpec((tm, nr), lambda i: (i, 0)), pl.BlockSpec((1, nr), lambda i: (0, 0))],
        out_specs=[pl.BlockSpec((tm, TOP_K), lambda i: (i, 0)),
                   pl.BlockSpec((tm, TOP_K), lambda i: (i, 0)),
                   pl.BlockSpec((tm, TOP_K), lambda i: (i, 0)),
                   pl.BlockSpec((8, nr), lambda i: (0, 0))],
        out_shape=[jax.ShapeDtypeStruct((t, TOP_K), I32), jax.ShapeDtypeStruct((t, TOP_K), F32),
                   jax.ShapeDtypeStruct((t, TOP_K), I32), jax.ShapeDtypeStruct((8, nr), F32)],
        scratch_shapes=[pltpu.VMEM((8, nr), F32)],
        compiler_params=_cp(("arbitrary",), 16),
        name="route_topk_rank",
    )(logits, bias_row)


def _plan_kernel(cnt_ref, eid_ref, rank_ref, pos_ref, be_ref, *, ngroups, nexp, nblocks_pad, tm):
    nr = cnt_ref.shape[1]
    cnt = cnt_ref[0:1, :]
    padded = jnp.floor((cnt + (MOE_BLOCK - 1)) * (1.0 / MOE_BLOCK)) * MOE_BLOCK
    r_i = _iota((nr, nr), 0)
    c_i = _iota((nr, nr), 1)
    padded_col = jnp.sum(jnp.where(r_i == c_i, jnp.broadcast_to(padded, (nr, nr)), 0.0), axis=1, keepdims=True)
    start = jnp.sum(jnp.where(r_i < c_i, jnp.broadcast_to(padded_col, (nr, nr)), 0.0), axis=0, keepdims=True)
    end = start + padded
    lane = _iota((tm, nr), 1)
    eid = eid_ref[...]
    rank = rank_ref[...]
    lane2 = _iota((tm, TOP_K), 1)
    pos = jnp.zeros((tm, TOP_K), F32)
    for k in range(TOP_K):
        oh = (lane - ngroups) == eid[:, k:k + 1]
        st = jnp.sum(jnp.where(oh, start, 0.0), axis=-1, keepdims=True)
        pos = jnp.where(lane2 == k, st, pos)
    pos_ref[...] = pos.astype(I32) + rank

    @pl.when(pl.program_id(0) == 0)
    def _():
        blk_row = _iota((nblocks_pad, nr), 0).astype(F32) * MOE_BLOCK
        lane_b = _iota((nblocks_pad, nr), 1)
        is_e = (lane_b >= ngroups) & (lane_b < ngroups + nexp)
        n_le = jnp.sum(jnp.where(is_e & (end <= blk_row), 1.0, 0.0), axis=-1, keepdims=True)
        e_lane = (lane_b - ngroups).astype(F32)
        nonempty = is_e & (padded > 0.0)
        e_last = jnp.max(jnp.where(nonempty, e_lane, -1.0), axis=-1, keepdims=True)
        be = jnp.minimum(n_le, e_last)
        nxt = jnp.min(jnp.where(nonempty & (e_lane > be), e_lane, 1e9), axis=-1, keepdims=True)
        nxt = jnp.where(nxt > 1e8, -1.0, nxt)
        total = jnp.sum(padded, axis=-1, keepdims=True)
        valid = jnp.where(blk_row[:, 0:1] < total, 1.0, 0.0)
        col = _iota((nblocks_pad, 4), 1)
        meta = jnp.where(col == 0, be, jnp.where(col == 1, nxt, jnp.where(col == 2, valid, 0.0)))
        be_ref[...] = meta.astype(I32)


def _plan(cnt, eid, rank, ngroups, nexp, nblocks):
    t = eid.shape[0]
    nr = cnt.shape[1]
    tm = _pick(t, 512)
    nblocks_pad = -(-nblocks // 8) * 8
    kern = functools.partial(_plan_kernel, ngroups=ngroups, nexp=nexp, nblocks_pad=nblocks_pad, tm=tm)
    return pl.pallas_call(
        kern,
        grid=(t // tm,),
        in_specs=[pl.BlockSpec((8, nr), lambda i: (0, 0)),
                  pl.BlockSpec((tm, TOP_K), lambda i: (i, 0)),
                  pl.BlockSpec((tm, TOP_K), lambda i: (i, 0))],
        out_specs=[pl.BlockSpec((tm, TOP_K), lambda i: (i, 0)),
                   pl.BlockSpec((nblocks_pad, 4), lambda i: (0, 0))],
        out_shape=[jax.ShapeDtypeStruct((t, TOP_K), I32), jax.ShapeDtypeStruct((nblocks_pad, 4), I32)],
        compiler_params=_cp(("arbitrary",), 16),
        name="route_plan",
    )(cnt, eid, rank)


def _invert_kernel(pos_ref, tok_ref, *, n_assign, n_rows):
    def clear(r, c):
        tok_ref[r] = 0
        return c

    lax.fori_loop(0, n_rows, clear, 0, unroll=16)

    def put(t, c):
        for k in range(TOP_K):
            tok_ref[pos_ref[t * TOP_K + k]] = t
        return c

    lax.fori_loop(0, n_assign // TOP_K, put, 0, unroll=8)


def _invert(pos_flat, n_rows):
    n_assign = pos_flat.shape[0]
    return pl.pallas_call(
        functools.partial(_invert_kernel, n_assign=n_assign, n_rows=n_rows),
        in_specs=[pl.BlockSpec(memory_space=pltpu.SMEM)],
        out_specs=pl.BlockSpec(memory_space=pltpu.SMEM),
        out_shape=jax.ShapeDtypeStruct((n_rows,), I32),
        name="route_invert",
    )(pos_flat)


def _expert_kernel(be_ref, nxt_ref, valid_ref, tok_ref, x_hbm, wg_hbm, wu_hbm, wd_hbm, y_ref,
                   xbuf0, xbuf1, xsem, wg_st, wu_st, wd_st, wsem, wg_bf, wu_bf, wd_bf):
    i = pl.program_id(0)
    n = pl.num_programs(0)
    xbufs = (xbuf0, xbuf1)

    def row_copy(blk, s, r):
        t = tok_ref[blk * MOE_BLOCK + r]
        return pltpu.make_async_copy(x_hbm.at[pl.ds(t, 1), :], xbufs[s].at[pl.ds(r, 1), :], xsem.at[s])

    def block_wait(s):
        pltpu.make_async_copy(x_hbm.at[pl.ds(0, MOE_BLOCK), :], xbufs[s], xsem.at[s]).wait()

    def weight_copies(e):
        return (pltpu.make_async_copy(wg_hbm.at[e], wg_st, wsem.at[0]),
                pltpu.make_async_copy(wu_hbm.at[e], wu_st, wsem.at[1]),
                pltpu.make_async_copy(wd_hbm.at[e], wd_st, wsem.at[2]))

    @pl.when(i == 0)
    def _():
        for r in range(MOE_BLOCK):
            row_copy(0, 0, r).start()
        for cp in weight_copies(be_ref[0]):
            cp.start(priority=1)

    prev = be_ref[jnp.maximum(i - 1, 0)]

    @pl.when((i == 0) | (be_ref[i] != prev))
    def _():
        for cp in weight_copies(be_ref[i]):
            cp.wait()
        wg_bf[...] = wg_st[...].astype(BF)
        wu_bf[...] = wu_st[...].astype(BF)
        wd_bf[...] = wd_st[...].astype(BF)

        @pl.when(nxt_ref[i] >= 0)
        def _():
            for cp in weight_copies(nxt_ref[i]):
                cp.start(priority=1)

    ahead = jnp.minimum(i + 1, n - 1)
    used = valid_ref[i] != 0
    used_ahead = (i + 1 < n) & (valid_ref[ahead] != 0)

    def step(cur, oth):
        @pl.when(used_ahead)
        def _():
            for r in range(MOE_BLOCK):
                row_copy(ahead, oth, r).start()

        block_wait(cur)
        x = xbufs[cur][...].astype(BF)
        hid = (_silu(_dot(x, wg_bf[...])) * _dot(x, wu_bf[...])).astype(BF)
        y_ref[...] = _dot(hid, wd_bf[...])

    @pl.when(used & (i % 2 == 0))
    def _():
        step(0, 1)

    @pl.when(used & (i % 2 == 1))
    def _():
        step(1, 0)

    @pl.when(jnp.logical_not(used))
    def _():
        y_ref[...] = jnp.zeros_like(y_ref)


def _experts(block_e, block_next, block_valid, row_tok, xf, w_eg, w_eu, w_ed):
    nrows = row_tok.shape[0]
    d = w_eg.shape[1]
    nblocks = nrows // MOE_BLOCK
    de = w_eg.shape[2]
    any_spec = pl.BlockSpec(memory_space=pl.ANY)
    grid_spec = pltpu.PrefetchScalarGridSpec(
        num_scalar_prefetch=4,
        grid=(nblocks,),
        in_specs=[any_spec, any_spec, any_spec, any_spec],
        out_specs=pl.BlockSpec((MOE_BLOCK, d), lambda i, *_: (i, 0)),
        scratch_shapes=[pltpu.VMEM((MOE_BLOCK, d), F32), pltpu.VMEM((MOE_BLOCK, d), F32),
                        pltpu.SemaphoreType.DMA((2,)),
                        pltpu.VMEM((d, de), F32), pltpu.VMEM((d, de), F32), pltpu.VMEM((de, d), F32),
                        pltpu.SemaphoreType.DMA((3,)),
                        pltpu.VMEM((d, de), BF), pltpu.VMEM((d, de), BF), pltpu.VMEM((de, d), BF)],
    )
    return pl.pallas_call(
        _expert_kernel,
        grid_spec=grid_spec,
        out_shape=jax.ShapeDtypeStruct((nrows, d), F32),
        compiler_params=_cp(("arbitrary",), 28),
        name="expert_blocks",
    )(block_e, block_next, block_valid, row_tok, xf, w_eg, w_eu, w_ed)


def _final_kernel(pos_ref, h_ref, gate_ref, w_ref, yb_hbm, ya_ref, ys_ref, gbuf, sem, *, na, tm):
    i = pl.program_id(0)
    n = pl.num_programs(0)
    slot = i % 2

    def row_copy(tile, slot_, r, k):
        p = pos_ref[(tile * tm + r) * TOP_K + k]
        return pltpu.make_async_copy(yb_hbm.at[pl.ds(p, 1), :], gbuf.at[slot_, k, pl.ds(r, 1), :], sem.at[slot_])

    def start_tile(tile, slot_):
        for r in range(tm):
            for k in range(TOP_K):
                row_copy(tile, slot_, r, k).start()

    @pl.when(i == 0)
    def _():
        start_tile(0, 0)

    @pl.when(i + 1 < n)
    def _():
        start_tile(i + 1, 1 - slot)

    for r in range(tm):
        for k in range(TOP_K):
            row_copy(i, slot, r, k).wait()

    gate = gate_ref[...]
    y = h_ref[...]
    ffn = gbuf[slot, 0] * gate[:, 0:1]
    for k in range(1, TOP_K):
        ffn = ffn + gbuf[slot, k] * gate[:, k:k + 1]
    y = y + ffn
    ms = jnp.mean(y * y, axis=-1, keepdims=True)
    out = y * lax.rsqrt(ms + EPS) * w_ref[...]

    @pl.when(i < na)
    def _():
        ya_ref[...] = out

    @pl.when(i >= na)
    def _():
        ys_ref[...] = out


def _final(pos_flat, h, gate, yb, w, rows_a, rows_b):
    t, d = h.shape
    tm = _pick(math.gcd(rows_a, rows_b), 128)
    na, nb = rows_a // tm, rows_b // tm
    grid_spec = pltpu.PrefetchScalarGridSpec(
        num_scalar_prefetch=1,
        grid=(na + nb,),
        in_specs=[pl.BlockSpec((tm, d), lambda i, pos: (i, 0)),
                  pl.BlockSpec((tm, TOP_K), lambda i, pos: (i, 0)),
                  pl.BlockSpec((1, d), lambda i, pos: (0, 0)),
                  pl.BlockSpec(memory_space=pl.ANY)],
        out_specs=[pl.BlockSpec((tm, d), lambda i, pos: (jnp.minimum(i, na - 1), 0)),
                   pl.BlockSpec((tm, d), lambda i, pos: (jnp.maximum(i - na, 0), 0))],
        scratch_shapes=[pltpu.VMEM((2, TOP_K, tm, d), F32), pltpu.SemaphoreType.DMA((2,))],
    )
    return pl.pallas_call(
        functools.partial(_final_kernel, na=na, tm=tm),
        grid_spec=grid_spec,
        out_shape=[jax.ShapeDtypeStruct((rows_a, d), F32), jax.ShapeDtypeStruct((rows_b, d), F32)],
        compiler_params=_cp(("arbitrary",), 16),
        name="combine_final_norm",
    )(pos_flat, h, gate, w.reshape(1, d), yb)


def kernel(x_prompt, x_sample, state_hgrn, state_gdn, state_conv, norm_mix, w_in, conv_w, ha_lb_logits, ha_onorm, w_pa, gd_A_log, gd_dt_bias, gd_onorm, w_pb, w_out, norm_ffn, w_router_group, b_router_group, w_router_expert, b_router_expert, w_exp_gate, w_exp_up, w_exp_down, norm_final):
    depth = state_hgrn.shape[0]
    assert depth == 1, "one decoder layer"
    bp, lp, d = x_prompt.shape
    bs, ls, _ = x_sample.shape
    _, _, ha_heads, ha_dk, ha_dv = state_hgrn.shape
    _, _, vheads, gd_dk, gd_dv = state_gdn.shape
    conv_dim = state_conv.shape[3]
    ha_kdim, ha_vdim = ha_heads * ha_dk, ha_heads * ha_dv
    gd_vdim = vheads * gd_dv
    ngroups = w_router_group.shape[2]
    nexp = w_router_expert.shape[2]
    rp, rs = bp * lp, bs * ls
    t = rp + rs

    off_f = ha_kdim
    off_i = off_f + ha_kdim
    off_g = off_i + ha_vdim
    off_qkv = off_g + ha_vdim
    off_z = off_qkv + conv_dim
    off_b = off_z + gd_vdim
    off_gate = off_b + 2 * vheads

    xn = _rmsnorm_bf16(x_prompt.reshape(rp, d), x_sample.reshape(rs, d), norm_mix[0])

    assert ha_kdim == ha_vdim and off_qkv % conv_dim == 0 and off_z % gd_vdim == 0
    w_bf = w_in[0].astype(BF)
    proj, k_a = _inproj(xn, w_bf, ha_lb_logits, off_b, ha_kdim, ha_dk ** -0.5)
    q_a, logf, v_a, g_a = ((proj, cb) for cb in (0, off_f // ha_kdim, off_i // ha_kdim, off_g // ha_kdim))
    k_a = (k_a, 0)
    qkv = (proj, off_qkv // conv_dim)
    z_b = (proj, off_z // gd_vdim)
    w_gates_bf = w_bf[:, off_gate:]
    bg, bgt = _beta_decay(xn, w_bf[:, off_b:off_gate], gd_A_log[0], gd_dt_bias[0])

    cg_p = _pick(lp, HGRN_ROWS, GDN_CHUNK) if lp >= GDN_CHUNK else lp
    sub_p = min(HGRN_SUB, lp)
    chunk_p = min(GDN_CHUNK, lp)
    cg_s = ls
    zeros_h = jnp.zeros((bp, ha_heads, ha_dk, ha_dv), F32)
    zeros_g = jnp.zeros((bp, vheads, gd_dk, gd_dv), F32)
    zeros_c = jnp.zeros((bp, CONV_K - 1, conv_dim), F32)

    oa_p, sh_p = _hgrn2(q_a, logf, k_a, v_a, g_a, ha_onorm[0], zeros_h,
                        row0=0, seq_len=lp, rows_per_step=cg_p, sub=sub_p, seqs_per_step=1)
    oa_s, sh_s = _hgrn2(q_a, logf, k_a, v_a, g_a, ha_onorm[0], state_hgrn[0],
                        row0=rp, seq_len=ls, rows_per_step=cg_s, sub=min(HGRN_SUB, ls),
                        seqs_per_step=math.gcd(_pick(bs, HGRN_SAMPLE_SEQS, 1), max(rp // cg_s, 1)))

    def time_on_lanes(rows0, nrows, cg):
        part = bgt[:, rows0:rows0 + nrows].reshape(2 * vheads, nrows // cg, cg)
        return jnp.transpose(part, (1, 0, 2))

    seqs_s = _pick(bs, GDN_SAMPLE_SEQS, 1)
    cg_g = _pick(lp, GDN_ROWS, GDN_CHUNK) if lp >= GDN_CHUNK else lp
    bgt_p = time_on_lanes(0, rp, cg_g)
    bgt_s = time_on_lanes(rp, rs, seqs_s * cg_s)
    ob_p, sg_p, sc_p = _gdn(qkv, z_b, bg, bgt_p, conv_w[0], gd_onorm[0], zeros_g, zeros_c,
                            row0=0, seq_len=lp, rows_per_step=cg_g, chunk=chunk_p, seqs_per_step=1)
    ob_s, sg_s, sc_s = _gdn(qkv, z_b, bg, bgt_s, conv_w[0], gd_onorm[0], state_gdn[0],
                            state_conv[0], row0=rp, seq_len=ls, rows_per_step=cg_s, chunk=min(GDN_CHUNK, ls),
                            seqs_per_step=seqs_s)

    merged = _merge(xn, oa_p, oa_s, ob_p, ob_s, w_gates_bf, w_pa[0], w_pb[0])

    nr = LANES
    w_router = jnp.concatenate([w_router_group[0], w_router_expert[0]], axis=1)
    w_router = jnp.pad(w_router, ((0, 0), (0, nr - ngroups - nexp))).astype(BF)
    b_router = jnp.pad(jnp.concatenate([b_router_group[0], b_router_expert[0]]), (0, nr - ngroups - nexp))
    h, xf, logits = _outproj(merged, x_prompt.reshape(rp, d), x_sample.reshape(rs, d), w_out[0],
                             norm_ffn[0], w_router)

    eid, gate, rank, cnt = _route(logits, b_router.reshape(1, nr), ngroups, nexp)
    nblocks = -(-(t * TOP_K) // MOE_BLOCK) + nexp
    pos, block_meta = _plan(cnt, eid, rank, ngroups, nexp, nblocks)
    block_e, block_next, block_valid = (block_meta[:nblocks, j] for j in range(3))

    pos_flat = pos.reshape(-1)
    row_tok = _invert(pos_flat, nblocks * MOE_BLOCK)
    yb = _experts(block_e, block_next, block_valid, row_tok, xf, w_exp_gate[0], w_exp_up[0], w_exp_down[0])
    y_p, y_s = _final(pos_flat, h, gate, yb, norm_final, rp, rs)
    return (y_p.reshape(bp, lp, d), y_s.reshape(bs, ls, d),
            sh_p[None], sg_p[None], sc_p[None], sh_s[None], sg_s[None], sc_s[None])
```

```python
import functools
import math

import jax
import jax.numpy as jnp
from jax import lax
from jax.experimental import pallas as pl
from jax.experimental.pallas import tpu as pltpu

F32 = jnp.float32
BF = jnp.bfloat16
I32 = jnp.int32
EPS = 1e-6
CONV_K = 4
TOP_K = 2
MOE_BLOCK = 128
GDN_CHUNK = 64
GDN_ROWS = 256
GDN_SAMPLE_SEQS = 8
HGRN_SUB = 16
HGRN_ROWS = 128
HGRN_SAMPLE_SEQS = 8
LOG2E = 1.4426950408889634
LANES = 128
NEG = -3.0e38
MIB = 1024 * 1024


def _cp(sem, vmem_mib=48):
    return pltpu.CompilerParams(dimension_semantics=sem, vmem_limit_bytes=vmem_mib * MIB)


def _pick(n, pref, mult=8):
    best = None
    for t in range(mult, min(n, pref) + 1, mult):
        if n % t == 0:
            best = t
    return best if best is not None else n


def _sigmoid(x):
    return 1.0 / (1.0 + jnp.exp(-x))


def _silu(x):
    return x * _sigmoid(x)


def _iota(shape, dim):
    return lax.broadcasted_iota(I32, shape, dim)


def _dot(a, b):
    return jnp.dot(a, b, preferred_element_type=F32)


def _dot_nt(a, b):
    return lax.dot_general(a, b, (((1,), (1,)), ((), ())), preferred_element_type=F32)


def _dot_tn(a, b):
    return lax.dot_general(a, b, (((0,), (0,)), ((), ())), preferred_element_type=F32)


def _split(a):
    hi = a.astype(BF)
    lo = (a - hi.astype(F32)).astype(BF)
    return hi, lo


def _split3(a, axis=0):
    hi = a.astype(BF)
    r1 = a - hi.astype(F32)
    mid = r1.astype(BF)
    lo = (r1 - mid.astype(F32)).astype(BF)
    return jnp.concatenate([hi, mid, lo], axis=axis)


def _two_source_specs(rows_a, rows_b, tm, width):
    na, nb = rows_a // tm, rows_b // tm
    spec_a = pl.BlockSpec((tm, width), lambda i: (jnp.minimum(i, na - 1), 0))
    spec_b = pl.BlockSpec((tm, width), lambda i: (jnp.maximum(i - na, 0), 0))
    return na, nb, spec_a, spec_b


def _rmsnorm_kernel(xa_ref, xb_ref, w_ref, o_ref, *, na):
    i = pl.program_id(0)

    def body(x_ref):
        x = x_ref[...]
        ms = jnp.mean(x * x, axis=-1, keepdims=True)
        o_ref[...] = (x * lax.rsqrt(ms + EPS) * w_ref[...]).astype(o_ref.dtype)

    @pl.when(i < na)
    def _():
        body(xa_ref)

    @pl.when(i >= na)
    def _():
        body(xb_ref)


def _rmsnorm_bf16(xa, xb, w):
    d = xa.shape[1]
    tm = _pick(math.gcd(xa.shape[0], xb.shape[0]), 512)
    na, nb, sa, sb = _two_source_specs(xa.shape[0], xb.shape[0], tm, d)
    return pl.pallas_call(
        functools.partial(_rmsnorm_kernel, na=na),
        grid=(na + nb,),
        in_specs=[sa, sb, pl.BlockSpec((1, d), lambda i: (0, 0))],
        out_specs=pl.BlockSpec((tm, d), lambda i: (i, 0)),
        out_shape=jax.ShapeDtypeStruct((xa.shape[0] + xb.shape[0], d), BF),
        compiler_params=_cp(("arbitrary",), 24),
        name="rmsnorm_mix",
    )(xa, xb, w.reshape(1, d))


def _inproj_kernel(x_ref, w_ref, lb_ref, wba_ref, wbat_ref, prow_ref, pcol_ref, p_ref, kf_ref, bg_ref, bgt_ref,
                   *, jq, jf0, jf1, q_scale, vh):
    j = pl.program_id(0)

    @pl.when(j == 0)
    def _():
        x = x_ref[...]
        ba = _dot(x, wba_ref[...])
        bat = _dot_nt(wbat_ref[...], x)

        def act(a, is_beta, a_neg_exp, dt_bias):
            z = a + dt_bias
            softplus = jnp.maximum(z, 0.0) + jnp.log(1.0 + jnp.exp(-jnp.abs(z)))
            return jnp.where(is_beta, _sigmoid(a), a_neg_exp * softplus)

        prow = prow_ref[...]
        pcol = pcol_ref[...]
        bg_ref[...] = act(ba, _iota(ba.shape, 1) < vh, prow[0:1, :], prow[1:2, :])
        bgt_ref[...] = act(bat, _iota(bat.shape, 0) < vh, pcol[:, 0:1], pcol[:, 1:2])

    acc = _dot(x_ref[...], w_ref[...])
    is_forget = (j >= jf0) & (j < jf1)

    @pl.when(is_forget)
    def _():
        logf, kf = _forget_epilogue(acc, lb_ref[...])
        p_ref[...] = logf
        kf_ref[...] = kf

    @pl.when(jnp.logical_not(is_forget))
    def _():
        p_ref[...] = acc * jnp.where(j < jq, q_scale, 1.0)


def _inproj(x_bf, w, lb_logits, ncols, kdim, q_scale, w_ba, a_log, dt_bias):
    t, k = x_bf.shape
    vh = a_log.shape[0]
    zeros = jnp.zeros((vh,), F32)
    prow = jnp.stack([jnp.concatenate([zeros, -jnp.exp(a_log)]), jnp.concatenate([zeros, dt_bias])])
    const = lambda j, i: (0, 0)
    tn = _pick(math.gcd(ncols, kdim), 1024, LANES)
    tm = _pick(t, 1024)
    ni = t // tm
    jq, jf0, jf1 = kdim // tn, kdim // tn, 2 * kdim // tn
    nf = jf1 - jf0
    fcol = lambda j: jnp.clip(j - jf0, 0, nf - 1)
    frow = lambda j, i: jnp.where(j < jf0, 0, jnp.where(j >= jf1, ni - 1, i))
    brow = lambda j, i: jnp.where(j == 0, i, ni - 1)
    kern = functools.partial(_inproj_kernel, jq=jq, jf0=jf0, jf1=jf1, q_scale=q_scale, vh=vh)
    return pl.pallas_call(
        kern,
        grid=(ncols // tn, ni),
        in_specs=[pl.BlockSpec((tm, k), lambda j, i: (i, 0)),
                  pl.BlockSpec((k, tn), lambda j, i: (0, j)),
                  pl.BlockSpec((lb_logits.shape[0], tn), lambda j, i: (0, fcol(j))),
                  pl.BlockSpec((k, 2 * vh), const),
                  pl.BlockSpec((2 * vh, k), const),
                  pl.BlockSpec((2, 2 * vh), const),
                  pl.BlockSpec((2 * vh, 2), const)],
        out_specs=[pl.BlockSpec((tm, tn), lambda j, i: (i, j)),
                   pl.BlockSpec((tm, tn), lambda j, i: (frow(j, i), fcol(j))),
                   pl.BlockSpec((tm, 2 * vh), lambda j, i: (brow(j, i), 0)),
                   pl.BlockSpec((2 * vh, tm), lambda j, i: (0, brow(j, i)))],
        out_shape=[jax.ShapeDtypeStruct((t, ncols), F32), jax.ShapeDtypeStruct((t, kdim), F32),
                   jax.ShapeDtypeStruct((t, 2 * vh), F32), jax.ShapeDtypeStruct((2 * vh, t), F32)],
        compiler_params=_cp(("arbitrary", "arbitrary"), 44),
        name="in_proj",
    )(x_bf, w, lb_logits, w_ba.astype(BF), w_ba.T.astype(BF), prow, prow.T)


def _forget_epilogue(acc, lb_logits):
    m = jnp.max(lb_logits, axis=0, keepdims=True)
    e = jnp.exp(lb_logits - m)
    lb = e[0:1, :] / jnp.sum(e, axis=0, keepdims=True)
    f = lb + (1.0 - lb) * _sigmoid(acc)
    return jnp.log(f), 1.0 - f


def _hgrn2_kernel(q_ref, lf_ref, k_ref, v_ref, g_ref, on_ref, s0_ref, o_ref, so_ref, s_scr, gall_scr,
                  g_scr, k_scr, v_scr, *, heads, dk, dv, sub, nsub, nseq, nchunks, group, unroll):
    c = pl.program_id(1)
    cg = sub * nsub

    @pl.when(c == 0)
    def _():
        s_scr[...] = s0_ref[...]

    rows_all = nseq * cg
    shift = sub.bit_length() - 1
    r_i = _iota((rows_all, 3 * rows_all), 0)
    c_i = _iota((rows_all, 3 * rows_all), 1) & (rows_all - 1)
    same = lax.shift_right_logical(r_i, shift) == lax.shift_right_logical(c_i, shift)
    btril3 = jnp.where((r_i >= c_i) & same, 1.0, 0.0).astype(BF)
    gall_scr[...] = _dot(btril3, _split3(lf_ref[...])) * LOG2E
    onorm = on_ref[...]
    n_aug = 16 - sub % 16
    aug_rhs = jnp.concatenate([jnp.zeros((n_aug, dv), BF), jnp.ones((n_aug, dv), BF)], axis=1)
    zeros_v = jnp.zeros((sub, dv), BF)
    zeros_aug = jnp.zeros((n_aug - 3, dk), F32)

    nh = sub // 8
    row8 = _iota((8, 1), 0)

    def block(it, carry):
        for u in range(unroll):
            sub_chunk(it * unroll + u, g_scr.at[u], k_scr.at[u], v_scr.at[u])
        return carry

    def sub_chunk(sb, g_scr, k_scr, v_scr):
        for q in range(nseq):
            rows = pl.ds(pl.multiple_of(q * cg + sb * sub, sub), sub)
            g_scr[q] = gall_scr[rows, :]
            k_scr[q] = k_ref[rows, :]
            v_scr[q] = v_ref[rows, :]
        problems = [(q, h) for q in range(nseq) for h in range(heads)]
        for g0 in range(0, len(problems), group):
            ps = []
            for (q, h) in problems[g0:g0 + group]:
                r0 = pl.multiple_of(q * cg + sb * sub, sub)
                rows = pl.ds(r0, sub)
                kc = pl.ds(h * dk, dk)
                vc = pl.ds(h * dv, dv)
                ps.append(dict(q=q, h=h, r0=r0, rows=rows, kc=kc, vc=vc,
                               gc=g_scr[q, :, kc],
                               qv=q_ref[rows, kc], k=k_ref[rows, kc], v=v_ref[rows, vc]))
            for p in ps:
                p["s_old"] = s_scr[p["q"], p["h"]]
                p["o_state"] = _dot((p["qv"] * jnp.exp2(p["gc"])).astype(BF), p["s_old"].astype(BF))
                p["o8"] = [jnp.zeros((8, dv), F32) for _ in range(nh)]
                p["q8"] = [p["qv"][8 * i:8 * i + 8] for i in range(nh)]
                p["g8"] = [p["gc"][8 * i:8 * i + 8] for i in range(nh)]
            for j in range(sub):
                for p in ps:
                    kj = k_scr[p["q"], pl.ds(j, 1), p["kc"]]
                    vj = v_scr[p["q"], pl.ds(j, 1), p["vc"]]
                    gj = g_scr[p["q"], pl.ds(j, 1), p["kc"]]
                    for i in range(j // 8, nh):
                        pr = (p["q8"][i] * kj) * jnp.exp2(p["g8"][i] - gj)
                        a_col = jnp.sum(pr, axis=-1, keepdims=True)
                        if i == j // 8:
                            a_col = jnp.where(row8 >= j % 8, a_col, 0.0)
                        p["o8"][i] = p["o8"][i] + a_col * vj
            for p in ps:
                p["o"] = p["o_state"] + (jnp.concatenate(p["o8"], axis=0) if nh > 1 else p["o8"][0])
            for p in ps:
                gend = p["gc"][sub - 1:sub, :]
                dend = jnp.exp2(gend)
                d_hi = dend.astype(BF).astype(F32)
                d_mid = (dend - d_hi).astype(BF).astype(F32)
                d_lo = (dend - d_hi - d_mid).astype(BF).astype(F32)
                kdec = p["k"] * jnp.exp2(gend - p["gc"])
                lhs = jnp.concatenate([kdec, d_hi, d_mid, d_lo, zeros_aug], axis=0).astype(BF)
                rhs = jnp.concatenate([jnp.concatenate([p["v"].astype(BF), zeros_v], axis=1), aug_rhs], axis=0)
                p["kv"] = _dot_tn(lhs, rhs)
            for p in ps:
                s_scr[p["q"], p["h"]] = p["kv"][:, dv:] * p["s_old"] + p["kv"][:, :dv]
            for p in ps:
                o = p["o"]
                ms = jnp.mean(o * o, axis=-1, keepdims=True)
                o_ref[p["rows"], p["vc"]] = (o * lax.rsqrt(ms + EPS) * onorm) * _silu(g_ref[p["rows"], p["vc"]])

    lax.fori_loop(0, nsub // unroll, block, 0)

    @pl.when(c == nchunks - 1)
    def _():
        so_ref[...] = s_scr[...]


def _hgrn2(q, lf, kf, v, gate, onorm, s0, *, row0, seq_len, rows_per_step, sub, seqs_per_step):
    nseq_total, heads, dk, dv = s0.shape
    assert dk == dv, "state decay is applied with a (dk, dv) tile"
    width = heads * dk
    cg = rows_per_step
    nseq = seqs_per_step
    nchunks = seq_len // cg
    assert nseq == 1 or nchunks == 1, "several sequences per step only when a step covers whole sequences"
    rows = nseq * cg
    b0 = row0 // rows
    assert sub & (sub - 1) == 0 and sub % 8 == 0 and rows & (rows - 1) == 0
    unroll = next(u for u in (8, 4, 2, 1) if (cg // sub) % u == 0)
    def row_spec(col_block):
        return pl.BlockSpec((rows, width), lambda b, c: (b0 + b * nchunks + c, col_block))

    st_spec = pl.BlockSpec((nseq, heads, dk, dv), lambda b, c: (b, 0, 0, 0))
    kern = functools.partial(_hgrn2_kernel, heads=heads, dk=dk, dv=dv, sub=sub, nsub=cg // sub, nseq=nseq,
                             nchunks=nchunks, group=8, unroll=unroll)
    operands = (q, lf, kf, v, gate)
    return pl.pallas_call(
        kern,
        grid=(nseq_total // nseq, nchunks),
        in_specs=[row_spec(cb) for _, cb in operands]
                 + [pl.BlockSpec((1, dv), lambda b, c: (0, 0)), st_spec],
        out_specs=[pl.BlockSpec((rows, heads * dv), lambda b, c: (b * nchunks + c, 0)), st_spec],
        out_shape=[jax.ShapeDtypeStruct((nseq_total * seq_len, heads * dv), F32),
                   jax.ShapeDtypeStruct(s0.shape, F32)],
        scratch_shapes=[pltpu.VMEM((nseq, heads, dk, dv), F32), pltpu.VMEM((rows, width), F32),
                        pltpu.VMEM((unroll, nseq, sub, width), F32), pltpu.VMEM((unroll, nseq, sub, width), F32),
                        pltpu.VMEM((unroll, nseq, sub, heads * dv), F32)],
        compiler_params=_cp(("arbitrary", "arbitrary"), 24),
        name="hgrn2_recurrence",
    )(*[a for a, _ in operands], onorm.reshape(1, dv), s0)


def _gdn_kernel(qkv_ref, z_ref, bg_ref, bgt_ref, cw_ref, on_ref, s0_ref, c0_ref,
                o_ref, so_ref, co_ref, s_scr, xx, xs,
                *, kheads, vheads, dk, dv, chunk, nsub, nseq, nchunks, group_kh):
    c = pl.program_id(1)
    cg = chunk * nsub
    kdim = kheads * dk
    rep = vheads // kheads
    tail0 = 8 - (CONV_K - 1)

    @pl.when(c == 0)
    def _():
        s_scr[...] = s0_ref[...]
        xx[:, 0:8, :] = c0_ref[...]

    for q in range(nseq):
        xx[q, 8:8 + cg, :] = qkv_ref[q * cg:(q + 1) * cg, :]
        conv = xx[q, pl.ds(tail0, cg), :] * cw_ref[0:1, :]
        for j in range(1, CONV_K):
            conv = conv + xx[q, pl.ds(tail0 + j, cg), :] * cw_ref[j:j + 1, :]
        xs[q] = _silu(conv)

    @pl.when(c == nchunks - 1)
    def _():
        for q in range(nseq):
            co_ref[q] = xx[q, pl.ds(8 + cg - (CONV_K - 1), CONV_K - 1), :]

    for q in range(nseq):
        xx[q, 0:8, :] = xx[q, cg:cg + 8, :]

    wp = chunk
    r_i = _iota((chunk, wp), 0)
    c_i = _iota((chunk, wp), 1)
    causal = r_i >= c_i
    strict = r_i > c_i

    def pad_rows(a):
        if a.shape[0] == wp:
            return a
        if a.dtype == BF and a.shape[0] % 16:
            return pad_rows(a.astype(F32)).astype(BF)
        return jnp.concatenate([a, jnp.zeros((wp - a.shape[0],) + a.shape[1:], a.dtype)], axis=0)

    def split_lhs(a):
        a_hi = a.astype(BF).astype(F32)
        return jnp.concatenate([a_hi, a_hi, a - a_hi], axis=1).astype(BF)

    def split_rhs(b):
        bh, bl = (pad_rows(t) for t in _split(b))
        return jnp.concatenate([bh, bl, bh], axis=0)
    cm = chunk - 1
    tril3 = jnp.where(_iota((chunk, 3 * chunk), 0) >= (_iota((chunk, 3 * chunk), 1) & cm), 1.0, 0.0).astype(BF)
    t_r = _iota((3 * chunk, wp), 0) & cm
    t_c = _iota((3 * chunk, wp), 1)
    triu3 = jnp.where((t_r <= t_c) & (t_c < chunk), 1.0, 0.0).astype(BF)
    onorm = on_ref[...]

    cums = {}

    def cumulative_decay(q, s):
        if (q, s) not in cums:
            r0 = q * cg + s * chunk
            bg = bg_ref[r0:r0 + chunk, :]
            gt = bgt_ref[0, vheads:2 * vheads, r0:r0 + chunk]
            cums[(q, s)] = (bg, _dot(tril3, _split3(bg[:, vheads:2 * vheads], axis=0)),
                            _dot(_split3(gt, axis=1), triu3))
        return cums[(q, s)]

    def run_group(items):
        probs = []
        for (q, s, kh) in items:
            r0 = q * cg + s * chunk
            x0 = s * chunk
            bg, gcol_all, grow_all = cumulative_decay(q, s)
            for kh in (kh,):
                qh = xs[q, x0:x0 + chunk, kh * dk:(kh + 1) * dk]
                kk_ = xs[q, x0:x0 + chunk, kdim + kh * dk:kdim + (kh + 1) * dk]
                qn = qh * lax.rsqrt(jnp.sum(qh * qh, axis=-1, keepdims=True) + EPS) * (dk ** -0.5)
                kn = kk_ * lax.rsqrt(jnp.sum(kk_ * kk_, axis=-1, keepdims=True) + EPS)
                kn_bf = kn.astype(BF)
                kn_pad = pad_rows(kn_bf)
                kk = _dot_nt(kn_bf, kn_pad)
                qk = _dot_nt(qn.astype(BF), kn_pad)
                for r in range(rep):
                    h = kh * rep + r
                    vh_ = xs[q, x0:x0 + chunk, 2 * kdim + h * dv:2 * kdim + (h + 1) * dv]
                    gcol = gcol_all[:, h:h + 1]
                    grow = grow_all[h:h + 1, :]
                    beta = bg[:, h:h + 1]
                    decay = jnp.where(causal, jnp.exp(jnp.minimum(gcol - grow, 0.0)), 0.0)
                    eg = jnp.exp(gcol)
                    gend = gcol[chunk - 1:chunk, :]
                    probs.append(dict(
                        q=q, s=s, h=h, r0=r0,
                        x=jnp.where(strict, -(beta * kk * decay), 0.0),
                        y=jnp.concatenate([vh_ * beta, kn * (beta * eg)], axis=1),
                        a_bf=(qk * decay).astype(BF),
                        qg=qn * eg,
                        kdec_bf=(kn * jnp.exp(gend - gcol)).astype(BF),
                        send=jnp.exp(gend)))

        if chunk <= 16:
            for j in range(chunk - 1):
                for p in probs:
                    p["y"] = p["y"] + p["x"][:, j:j + 1] * p["y"][j:j + 1, :]
        else:
            n_fac = int(math.log2(chunk))
            for k in range(n_fac):
                for p in probs:
                    p["x_lhs"] = split_lhs(p["x"])
                    p["y"] = p["y"] + _dot(p["x_lhs"], split_rhs(p["y"]))
                if k + 1 < n_fac:
                    for p in probs:
                        p["x"] = _dot(p["x_lhs"], split_rhs(p["x"]))

        for s in sorted({p["s"] for p in probs}):
            cur = [p for p in probs if p["s"] == s]
            for p in cur:
                p["s_old"] = s_scr[p["q"], p["h"]]
                lhs = jnp.concatenate([p["y"][:, dv:], p["qg"]], axis=0).astype(BF)
                p["ws"] = _dot(lhs, p["s_old"].astype(BF))
            for p in cur:
                p["u_bf"] = (p["y"][:, :dv] - p["ws"][:chunk]).astype(BF)
            for p in cur:
                s_scr[p["q"], p["h"]] = p["send"] * p["s_old"] + _dot_tn(p["kdec_bf"], p["u_bf"])
            for p in cur:
                o = p["ws"][chunk:] + _dot(p["a_bf"], pad_rows(p["u_bf"]))
                ms = jnp.mean(o * o, axis=-1, keepdims=True)
                zc = slice(p["h"] * dv, (p["h"] + 1) * dv)
                rows = slice(p["r0"], p["r0"] + chunk)
                o_ref[rows, zc] = (o * lax.rsqrt(ms + EPS) * onorm) * _silu(z_ref[rows, zc])

    for g0 in range(0, kheads, group_kh):
        run_group([(q, s, kh) for q in range(nseq) for s in range(nsub)
                   for kh in range(g0, min(g0 + group_kh, kheads))])

    @pl.when(c == nchunks - 1)
    def _():
        so_ref[...] = s_scr[...]


def _gdn(qkv, z, bg, bgt3, conv_w, onorm, s0, conv0, *, row0, seq_len, rows_per_step, chunk, seqs_per_step):
    nseq_total, vheads, dk, dv = s0.shape
    (qkv, qkv_cb), (z, z_cb) = qkv, z
    conv_dim = conv_w.shape[1]
    kheads = (conv_dim - vheads * dv) // (2 * dk)
    cg = rows_per_step
    nseq = seqs_per_step
    nchunks = seq_len // cg
    assert nseq == 1 or nchunks == 1, "several sequences per step only when a step covers whole sequences"
    rows = nseq * cg
    b0 = row0 // rows
    conv0p = jnp.pad(conv0, ((0, 0), (8 - (CONV_K - 1), 0), (0, 0)))
    rmap = lambda b, c: (b0 + b * nchunks + c, 0)
    st_spec = pl.BlockSpec((nseq, vheads, dk, dv), lambda b, c: (b, 0, 0, 0))
    kern = functools.partial(_gdn_kernel, kheads=kheads, vheads=vheads, dk=dk, dv=dv, chunk=chunk,
                             nsub=cg // chunk, nseq=nseq, nchunks=nchunks, group_kh=kheads)
    return pl.pallas_call(
        kern,
        grid=(nseq_total // nseq, nchunks),
        in_specs=[pl.BlockSpec((rows, conv_dim), lambda b, c: (b0 + b * nchunks + c, qkv_cb)),
                  pl.BlockSpec((rows, vheads * dv), lambda b, c: (b0 + b * nchunks + c, z_cb)),
                  pl.BlockSpec((rows, 2 * vheads), rmap),
                  pl.BlockSpec((1, 2 * vheads, rows), lambda b, c: (b * nchunks + c, 0, 0)),
                  pl.BlockSpec((CONV_K, conv_dim), lambda b, c: (0, 0)),
                  pl.BlockSpec((1, dv), lambda b, c: (0, 0)),
                  st_spec,
                  pl.BlockSpec((nseq, 8, conv_dim), lambda b, c: (b, 0, 0))],
        out_specs=[pl.BlockSpec((rows, vheads * dv), lambda b, c: (b * nchunks + c, 0)),
                   st_spec,
                   pl.BlockSpec((nseq, CONV_K - 1, conv_dim), lambda b, c: (b, 0, 0))],
        out_shape=[jax.ShapeDtypeStruct((nseq_total * seq_len, vheads * dv), F32),
                   jax.ShapeDtypeStruct(s0.shape, F32),
                   jax.ShapeDtypeStruct((nseq_total, CONV_K - 1, conv_dim), F32)],
        scratch_shapes=[pltpu.VMEM((nseq, vheads, dk, dv), F32), pltpu.VMEM((nseq, 8 + cg, conv_dim), F32),
                        pltpu.VMEM((nseq, cg, conv_dim), F32)],
        compiler_params=_cp(("arbitrary", "arbitrary"), 28),
        name="gdn_recurrence",
    )(qkv, z, bg, bgt3, conv_w, onorm.reshape(1, dv), s0, conv0p)


def _merge_kernel(xn_ref, oa1_ref, oa2_ref, ob1_ref, ob2_ref, wga_ref, wgb_ref, wa_ref, wb_ref, o_ref,
                  oa_bf, ob_bf, *, na):
    i = pl.program_id(0)
    j = pl.program_id(1)

    @pl.when((j == 0) & (i < na))
    def _():
        oa_bf[...] = oa1_ref[...].astype(BF)
        ob_bf[...] = ob1_ref[...].astype(BF)

    @pl.when((j == 0) & (i >= na))
    def _():
        oa_bf[...] = oa2_ref[...].astype(BF)
        ob_bf[...] = ob2_ref[...].astype(BF)

    xn = xn_ref[...]
    ya = _dot(oa_bf[...], wa_ref[...])
    yb = _dot(ob_bf[...], wb_ref[...])
    ga = _dot(xn, wga_ref[...])
    gb = _dot(xn, wgb_ref[...])
    o_ref[...] = (_sigmoid(ga) * ya + _sigmoid(gb) * yb).astype(o_ref.dtype)


def _merge(xn, oa_p, oa_s, ob_p, ob_s, w_gates_bf, w_pa, w_pb):
    rp, rs = oa_p.shape[0], oa_s.shape[0]
    ka, kb = oa_p.shape[1], ob_p.shape[1]
    d = w_pa.shape[1]
    k = xn.shape[1]
    tm = _pick(math.gcd(rp, rs), 512)
    tn = _pick(d, 512, LANES)
    na, nb = rp // tm, rs // tm
    nj = d // tn
    amap = lambda i, j: (jnp.minimum(i, na - 1), 0)
    bmap = lambda i, j: (jnp.maximum(i - na, 0), 0)
    return pl.pallas_call(
        functools.partial(_merge_kernel, na=na),
        grid=(na + nb, nj),
        in_specs=[pl.BlockSpec((tm, k), lambda i, j: (i, 0)),
                  pl.BlockSpec((tm, ka), amap), pl.BlockSpec((tm, ka), bmap),
                  pl.BlockSpec((tm, kb), amap), pl.BlockSpec((tm, kb), bmap),
                  pl.BlockSpec((k, tn), lambda i, j: (0, j)),
                  pl.BlockSpec((k, tn), lambda i, j: (0, nj + j)),
                  pl.BlockSpec((ka, tn), lambda i, j: (0, j)),
                  pl.BlockSpec((kb, tn), lambda i, j: (0, j))],
        out_specs=pl.BlockSpec((tm, tn), lambda i, j: (i, j)),
        out_shape=jax.ShapeDtypeStruct((rp + rs, d), BF),
        scratch_shapes=[pltpu.VMEM((tm, ka), BF), pltpu.VMEM((tm, kb), BF)],
        compiler_params=_cp(("arbitrary", "arbitrary"), 40),
        name="gates_branch_proj_merge",
    )(xn, oa_p, oa_s, ob_p, ob_s, w_gates_bf, w_gates_bf, w_pa.astype(BF), w_pb.astype(BF))


def _outproj_kernel(m_ref, xa_ref, xb_ref, wo_ref, nw_ref, wr_ref, h_ref, xf_ref, lg_ref, *, na):
    i = pl.program_id(0)

    def body(x_ref):
        h = x_ref[...] + _dot(m_ref[...], wo_ref[...])
        h_ref[...] = h
        ms = jnp.mean(h * h, axis=-1, keepdims=True)
        xf = h * lax.rsqrt(ms + EPS) * nw_ref[...]
        xf_ref[...] = xf
        lg_ref[...] = _dot(xf.astype(BF), wr_ref[...])

    @pl.when(i < na)
    def _():
        body(xa_ref)

    @pl.when(i >= na)
    def _():
        body(xb_ref)


def _outproj(merged, xa, xb, w_out, norm_ffn, w_router):
    t, d = merged.shape
    tm = _pick(math.gcd(xa.shape[0], xb.shape[0]), 512)
    na, nb, sa, sb = _two_source_specs(xa.shape[0], xb.shape[0], tm, d)
    nr = w_router.shape[1]
    const = lambda i: (0, 0)
    return pl.pallas_call(
        functools.partial(_outproj_kernel, na=na),
        grid=(na + nb,),
        in_specs=[pl.BlockSpec((tm, d), lambda i: (i, 0)), sa, sb,
                  pl.BlockSpec((d, d), const, pipeline_mode=pl.Buffered(1)),
                  pl.BlockSpec((1, d), const),
                  pl.BlockSpec((d, nr), const, pipeline_mode=pl.Buffered(1))],
        out_specs=[pl.BlockSpec((tm, d), lambda i: (i, 0)),
                   pl.BlockSpec((tm, d), lambda i: (i, 0)),
                   pl.BlockSpec((tm, nr), lambda i: (i, 0))],
        out_shape=[jax.ShapeDtypeStruct((t, d), F32), jax.ShapeDtypeStruct((t, d), F32),
                   jax.ShapeDtypeStruct((t, nr), F32)],
        compiler_params=_cp(("arbitrary",), 48),
        name="out_proj_ffn_norm",
    )(merged, xa, xb, w_out.astype(BF), norm_ffn.reshape(1, d), w_router)


def _route_kernel(lg_ref, b_ref, eid_ref, gate_ref, rank_ref, cnt_ref, carry,
                  *, ngroups, nexp, tm):
    i = pl.program_id(0)

    @pl.when(i == 0)
    def _():
        carry[...] = jnp.zeros_like(carry)

    per_group = nexp // ngroups
    lg = lg_ref[...] + b_ref[...]
    lane = _iota(lg.shape, 1)
    big = jnp.int32(1 << 20)
    is_g = lane < ngroups
    gl = jnp.where(is_g, lg, NEG)
    gmax = jnp.max(gl, axis=-1, keepdims=True)
    gidx = jnp.min(jnp.where(gl == gmax, lane, big), axis=-1, keepdims=True)
    gsum = jnp.sum(jnp.where(is_g, jnp.exp(gl - gmax), 0.0), axis=-1, keepdims=True)
    gw = 1.0 / gsum
    elane = lane - ngroups
    in_grp = (elane >= gidx * per_group) & (elane < (gidx + 1) * per_group)
    el = jnp.where(in_grp, lg, NEG)
    v1 = jnp.max(el, axis=-1, keepdims=True)
    i1 = jnp.min(jnp.where(in_grp & (el == v1), elane, big), axis=-1, keepdims=True)
    in2 = in_grp & (elane != i1)
    el2 = jnp.where(in2, lg, NEG)
    v2 = jnp.max(el2, axis=-1, keepdims=True)
    i2 = jnp.min(jnp.where(in2 & (el2 == v2), elane, big), axis=-1, keepdims=True)
    p2 = jnp.exp(v2 - v1)
    den = 1.0 + p2
    lane2 = _iota((tm, TOP_K), 1)
    eid_ref[...] = jnp.where(lane2 == 0, i1, i2)
    gate_ref[...] = jnp.where(lane2 == 0, gw / den, gw * p2 / den)

    oh1 = (elane == i1).astype(F32)
    oh2 = (elane == i2).astype(F32)
    lower = (_iota((tm, tm), 0) > _iota((tm, tm), 1)).astype(BF)
    cs1 = _dot(lower, oh1.astype(BF))
    cs2 = _dot(lower, oh2.astype(BF))
    tot1 = jnp.sum(oh1, axis=0, keepdims=True)
    tot2 = jnp.sum(oh2, axis=0, keepdims=True)
    base = carry[0:1, :]
    r1 = jnp.sum(oh1 * (base + cs1), axis=-1, keepdims=True)
    r2 = jnp.sum(oh2 * (base + tot1 + cs2), axis=-1, keepdims=True)
    rank_ref[...] = jnp.where(lane2 == 0, r1, r2).astype(I32)
    new = base + tot1 + tot2
    carry[...] = jnp.broadcast_to(new, carry.shape)
    cnt_ref[...] = jnp.broadcast_to(new, cnt_ref.shape)


def _route(logits, bias_row, ngroups, nexp):
    t, nr = logits.shape
    tm = _pick(t, 512)
    kern = functools.partial(_route_kernel, ngroups=ngroups, nexp=nexp, tm=tm)
    return pl.pallas_call(
        kern,
        grid=(t // tm,),
        in_specs=[pl.BlockSpec((tm, nr), lambda i: (i, 0)), pl.BlockSpec((1, nr), lambda i: (0, 0))],
        out_specs=[pl.BlockSpec((tm, TOP_K), lambda i: (i, 0)),
                   pl.BlockSpec((tm, TOP_K), lambda i: (i, 0)),
                   pl.BlockSpec((tm, TOP_K), lambda i: (i, 0)),
                   pl.BlockSpec((8, nr), lambda i: (0, 0))],
        out_shape=[jax.ShapeDtypeStruct((t, TOP_K), I32), jax.ShapeDtypeStruct((t, TOP_K), F32),
                   jax.ShapeDtypeStruct((t, TOP_K), I32), jax.ShapeDtypeStruct((8, nr), F32)],
        scratch_shapes=[pltpu.VMEM((8, nr), F32)],
        compiler_params=_cp(("arbitrary",), 16),
        name="route_topk_rank",
    )(logits, bias_row)


def _plan_kernel(cnt_ref, eid_ref, rank_ref, pos_ref, be_ref, *, ngroups, nexp, nblocks_pad, tm):
    nr = cnt_ref.shape[1]
    cnt = cnt_ref[0:1, :]
    padded = jnp.floor((cnt + (MOE_BLOCK - 1)) * (1.0 / MOE_BLOCK)) * MOE_BLOCK
    r_i = _iota((nr, nr), 0)
    c_i = _iota((nr, nr), 1)
    padded_col = jnp.sum(jnp.where(r_i == c_i, jnp.broadcast_to(padded, (nr, nr)), 0.0), axis=1, keepdims=True)
    start = jnp.sum(jnp.where(r_i < c_i, jnp.broadcast_to(padded_col, (nr, nr)), 0.0), axis=0, keepdims=True)
    end = start + padded
    lane = _iota((tm, nr), 1)
    eid = eid_ref[...]
    rank = rank_ref[...]
    lane2 = _iota((tm, TOP_K), 1)
    pos = jnp.zeros((tm, TOP_K), F32)
    for k in range(TOP_K):
        oh = (lane - ngroups) == eid[:, k:k + 1]
        st = jnp.sum(jnp.where(oh, start, 0.0), axis=-1, keepdims=True)
        pos = jnp.where(lane2 == k, st, pos)
    pos_ref[...] = pos.astype(I32) + rank

    @pl.when(pl.program_id(0) == 0)
    def _():
        blk_row = _iota((nblocks_pad, nr), 0).astype(F32) * MOE_BLOCK
        lane_b = _iota((nblocks_pad, nr), 1)
        is_e = (lane_b >= ngroups) & (lane_b < ngroups + nexp)
        n_le = jnp.sum(jnp.where(is_e & (end <= blk_row), 1.0, 0.0), axis=-1, keepdims=True)
        e_lane = (lane_b - ngroups).astype(F32)
        nonempty = is_e & (padded > 0.0)
        e_last = jnp.max(jnp.where(nonempty, e_lane, -1.0), axis=-1, keepdims=True)
        be = jnp.minimum(n_le, e_last)
        nxt = jnp.min(jnp.where(nonempty & (e_lane > be), e_lane, 1e9), axis=-1, keepdims=True)
        nxt = jnp.where(nxt > 1e8, -1.0, nxt)
        total = jnp.sum(padded, axis=-1, keepdims=True)
        valid = jnp.where(blk_row[:, 0:1] < total, 1.0, 0.0)
        col = _iota((nblocks_pad, 4), 1)
        meta = jnp.where(col == 0, be, jnp.where(col == 1, nxt, jnp.where(col == 2, valid, 0.0)))
        be_ref[...] = meta.astype(I32)


def _plan(cnt, eid, rank, ngroups, nexp, nblocks):
    t = eid.shape[0]
    nr = cnt.shape[1]
    tm = _pick(t, 512)
    nblocks_pad = -(-nblocks // 8) * 8
    kern = functools.partial(_plan_kernel, ngroups=ngroups, nexp=nexp, nblocks_pad=nblocks_pad, tm=tm)
    return pl.pallas_call(
        kern,
        grid=(t // tm,),
        in_specs=[pl.BlockSpec((8, nr), lambda i: (0, 0)),
                  pl.BlockSpec((tm, TOP_K), lambda i: (i, 0)),
                  pl.BlockSpec((tm, TOP_K), lambda i: (i, 0))],
        out_specs=[pl.BlockSpec((tm, TOP_K), lambda i: (i, 0)),
                   pl.BlockSpec((nblocks_pad, 4), lambda i: (0, 0))],
        out_shape=[jax.ShapeDtypeStruct((t, TOP_K), I32), jax.ShapeDtypeStruct((nblocks_pad, 4), I32)],
        compiler_params=_cp(("arbitrary",), 16),
        name="route_plan",
    )(cnt, eid, rank)


def _invert_kernel(pos_ref, tok_ref, *, n_assign, n_rows):
    def clear(r, c):
        tok_ref[r] = 0
        return c

    lax.fori_loop(0, n_rows, clear, 0, unroll=16)

    def put(t, c):
        for k in range(TOP_K):
            tok_ref[pos_ref[t * TOP_K + k]] = t
        return c

    lax.fori_loop(0, n_assign // TOP_K, put, 0, unroll=8)


def _invert(pos_flat, n_rows):
    n_assign = pos_flat.shape[0]
    return pl.pallas_call(
        functools.partial(_invert_kernel, n_assign=n_assign, n_rows=n_rows),
        in_specs=[pl.BlockSpec(memory_space=pltpu.SMEM)],
        out_specs=pl.BlockSpec(memory_space=pltpu.SMEM),
        out_shape=jax.ShapeDtypeStruct((n_rows,), I32),
        name="route_invert",
    )(pos_flat)


def _expert_kernel(be_ref, nxt_ref, valid_ref, tok_ref, x_hbm, wg_hbm, wu_hbm, wd_hbm, y_ref,
                   xbuf0, xbuf1, xsem, wg_st, wu_st, wd_st, wsem, wg_bf, wu_bf, wd_bf):
    i = pl.program_id(0)
    n = pl.num_programs(0)
    xbufs = (xbuf0, xbuf1)

    def row_copy(blk, s, r):
        t = tok_ref[blk * MOE_BLOCK + r]
        return pltpu.make_async_copy(x_hbm.at[pl.ds(t, 1), :], xbufs[s].at[pl.ds(r, 1), :], xsem.at[s])

    def block_wait(s):
        pltpu.make_async_copy(x_hbm.at[pl.ds(0, MOE_BLOCK), :], xbufs[s], xsem.at[s]).wait()

    def weight_copies(e):
        return (pltpu.make_async_copy(wg_hbm.at[e], wg_st, wsem.at[0]),
                pltpu.make_async_copy(wu_hbm.at[e], wu_st, wsem.at[1]),
                pltpu.make_async_copy(wd_hbm.at[e], wd_st, wsem.at[2]))

    @pl.when(i == 0)
    def _():
        for r in range(MOE_BLOCK):
            row_copy(0, 0, r).start()
        for cp in weight_copies(be_ref[0]):
            cp.start(priority=1)

    prev = be_ref[jnp.maximum(i - 1, 0)]

    @pl.when((i == 0) | (be_ref[i] != prev))
    def _():
        for cp in weight_copies(be_ref[i]):
            cp.wait()
        wg_bf[...] = wg_st[...].astype(BF)
        wu_bf[...] = wu_st[...].astype(BF)
        wd_bf[...] = wd_st[...].astype(BF)

        @pl.when(nxt_ref[i] >= 0)
        def _():
            for cp in weight_copies(nxt_ref[i]):
                cp.start(priority=1)

    ahead = jnp.minimum(i + 1, n - 1)
    used = valid_ref[i] != 0
    used_ahead = (i + 1 < n) & (valid_ref[ahead] != 0)

    def step(cur, oth):
        @pl.when(used_ahead)
        def _():
            for r in range(MOE_BLOCK):
                row_copy(ahead, oth, r).start()

        block_wait(cur)
        x = xbufs[cur][...].astype(BF)
        hid = (_silu(_dot(x, wg_bf[...])) * _dot(x, wu_bf[...])).astype(BF)
        y_ref[...] = _dot(hid, wd_bf[...])

    @pl.when(used & (i % 2 == 0))
    def _():
        step(0, 1)

    @pl.when(used & (i % 2 == 1))
    def _():
        step(1, 0)

    @pl.when(jnp.logical_not(used))
    def _():
        y_ref[...] = jnp.zeros_like(y_ref)


def _experts(block_e, block_next, block_valid, row_tok, xf, w_eg, w_eu, w_ed):
    nrows = row_tok.shape[0]
    d = w_eg.shape[1]
    nblocks = nrows // MOE_BLOCK
    de = w_eg.shape[2]
    any_spec = pl.BlockSpec(memory_space=pl.ANY)
    grid_spec = pltpu.PrefetchScalarGridSpec(
        num_scalar_prefetch=4,
        grid=(nblocks,),
        in_specs=[any_spec, any_spec, any_spec, any_spec],
        out_specs=pl.BlockSpec((MOE_BLOCK, d), lambda i, *_: (i, 0)),
        scratch_shapes=[pltpu.VMEM((MOE_BLOCK, d), F32), pltpu.VMEM((MOE_BLOCK, d), F32),
                        pltpu.SemaphoreType.DMA((2,)),
                        pltpu.VMEM((d, de), F32), pltpu.VMEM((d, de), F32), pltpu.VMEM((de, d), F32),
                        pltpu.SemaphoreType.DMA((3,)),
                        pltpu.VMEM((d, de), BF), pltpu.VMEM((d, de), BF), pltpu.VMEM((de, d), BF)],
    )
    return pl.pallas_call(
        _expert_kernel,
        grid_spec=grid_spec,
        out_shape=jax.ShapeDtypeStruct((nrows, d), F32),
        compiler_params=_cp(("arbitrary",), 28),
        name="expert_blocks",
    )(block_e, block_next, block_valid, row_tok, xf, w_eg, w_eu, w_ed)


def _final_kernel(pos_ref, h_ref, gate_ref, w_ref, yb_hbm, ya_ref, ys_ref, gbuf, sem, *, na, tm):
    i = pl.program_id(0)
    n = pl.num_programs(0)
    slot = i % 2

    def row_copy(tile, slot_, r, k):
        p = pos_ref[(tile * tm + r) * TOP_K + k]
        return pltpu.make_async_copy(yb_hbm.at[pl.ds(p, 1), :], gbuf.at[slot_, k, pl.ds(r, 1), :], sem.at[slot_])

    def start_tile(tile, slot_):
        for r in range(tm):
            for k in range(TOP_K):
                row_copy(tile, slot_, r, k).start()

    @pl.when(i == 0)
    def _():
        start_tile(0, 0)

    @pl.when(i + 1 < n)
    def _():
        start_tile(i + 1, 1 - slot)

    for r in range(tm):
        for k in range(TOP_K):
            row_copy(i, slot, r, k).wait()

    gate = gate_ref[...]
    y = h_ref[...]
    ffn = gbuf[slot, 0] * gate[:, 0:1]
    for k in range(1, TOP_K):
        ffn = ffn + gbuf[slot, k] * gate[:, k:k + 1]
    y = y + ffn
    ms = jnp.mean(y * y, axis=-1, keepdims=True)
    out = y * lax.rsqrt(ms + EPS) * w_ref[...]

    @pl.when(i < na)
    def _():
        ya_ref[...] = out

    @pl.when(i >= na)
    def _():
        ys_ref[...] = out


def _final(pos_flat, h, gate, yb, w, rows_a, rows_b):
    t, d = h.shape
    tm = _pick(math.gcd(rows_a, rows_b), 128)
    na, nb = rows_a // tm, rows_b // tm
    grid_spec = pltpu.PrefetchScalarGridSpec(
        num_scalar_prefetch=1,
        grid=(na + nb,),
        in_specs=[pl.BlockSpec((tm, d), lambda i, pos: (i, 0)),
                  pl.BlockSpec((tm, TOP_K), lambda i, pos: (i, 0)),
                  pl.BlockSpec((1, d), lambda i, pos: (0, 0)),
                  pl.BlockSpec(memory_space=pl.ANY)],
        out_specs=[pl.BlockSpec((tm, d), lambda i, pos: (jnp.minimum(i, na - 1), 0)),
                   pl.BlockSpec((tm, d), lambda i, pos: (jnp.maximum(i - na, 0), 0))],
        scratch_shapes=[pltpu.VMEM((2, TOP_K, tm, d), F32), pltpu.SemaphoreType.DMA((2,))],
    )
    return pl.pallas_call(
        functools.partial(_final_kernel, na=na, tm=tm),
        grid_spec=grid_spec,
        out_shape=[jax.ShapeDtypeStruct((rows_a, d), F32), jax.ShapeDtypeStruct((rows_b, d), F32)],
        compiler_params=_cp(("arbitrary",), 16),
        name="combine_final_norm",
    )(pos_flat, h, gate, w.reshape(1, d), yb)


def kernel(x_prompt, x_sample, state_hgrn, state_gdn, state_conv, norm_mix, w_in, conv_w, ha_lb_logits, ha_onorm, w_pa, gd_A_log, gd_dt_bias, gd_onorm, w_pb, w_out, norm_ffn, w_router_group, b_router_group, w_router_expert, b_router_expert, w_exp_gate, w_exp_up, w_exp_down, norm_final):
    depth = state_hgrn.shape[0]
    assert depth == 1, "one decoder layer"
    bp, lp, d = x_prompt.shape
    bs, ls, _ = x_sample.shape
    _, _, ha_heads, ha_dk, ha_dv = state_hgrn.shape
    _, _, vheads, gd_dk, gd_dv = state_gdn.shape
    conv_dim = state_conv.shape[3]
    ha_kdim, ha_vdim = ha_heads * ha_dk, ha_heads * ha_dv
    gd_vdim = vheads * gd_dv
    ngroups = w_router_group.shape[2]
    nexp = w_router_expert.shape[2]
    rp, rs = bp * lp, bs * ls
    t = rp + rs

    off_f = ha_kdim
    off_i = off_f + ha_kdim
    off_g = off_i + ha_vdim
    off_qkv = off_g + ha_vdim
    off_z = off_qkv + conv_dim
    off_b = off_z + gd_vdim
    off_gate = off_b + 2 * vheads

    xn = _rmsnorm_bf16(x_prompt.reshape(rp, d), x_sample.reshape(rs, d), norm_mix[0])

    assert ha_kdim == ha_vdim and off_qkv % conv_dim == 0 and off_z % gd_vdim == 0
    w_bf = w_in[0].astype(BF)
    proj, k_a, bg, bgt = _inproj(xn, w_bf, ha_lb_logits, off_b, ha_kdim, ha_dk ** -0.5,
                                 w_bf[:, off_b:off_gate], gd_A_log[0], gd_dt_bias[0])
    q_a, logf, v_a, g_a = ((proj, cb) for cb in (0, off_f // ha_kdim, off_i // ha_kdim, off_g // ha_kdim))
    k_a = (k_a, 0)
    qkv = (proj, off_qkv // conv_dim)
    z_b = (proj, off_z // gd_vdim)
    w_gates_bf = w_bf[:, off_gate:]

    cg_p = _pick(lp, HGRN_ROWS, GDN_CHUNK) if lp >= GDN_CHUNK else lp
    sub_p = min(HGRN_SUB, lp)
    chunk_p = min(GDN_CHUNK, lp)
    cg_s = ls
    zeros_h = jnp.zeros((bp, ha_heads, ha_dk, ha_dv), F32)
    zeros_g = jnp.zeros((bp, vheads, gd_dk, gd_dv), F32)
    zeros_c = jnp.zeros((bp, CONV_K - 1, conv_dim), F32)

    oa_p, sh_p = _hgrn2(q_a, logf, k_a, v_a, g_a, ha_onorm[0], zeros_h,
                        row0=0, seq_len=lp, rows_per_step=cg_p, sub=sub_p, seqs_per_step=1)
    oa_s, sh_s = _hgrn2(q_a, logf, k_a, v_a, g_a, ha_onorm[0], state_hgrn[0],
                        row0=rp, seq_len=ls, rows_per_step=cg_s, sub=min(HGRN_SUB, ls),
                        seqs_per_step=math.gcd(_pick(bs, HGRN_SAMPLE_SEQS, 1), max(rp // cg_s, 1)))

    def time_on_lanes(rows0, nrows, cg):
        part = bgt[:, rows0:rows0 + nrows].reshape(2 * vheads, nrows // cg, cg)
        return jnp.transpose(part, (1, 0, 2))

    seqs_s = _pick(bs, GDN_SAMPLE_SEQS, 1)
    cg_g = _pick(lp, GDN_ROWS, GDN_CHUNK) if lp >= GDN_CHUNK else lp
    bgt_p = time_on_lanes(0, rp, cg_g)
    bgt_s = time_on_lanes(rp, rs, seqs_s * cg_s)
    ob_p, sg_p, sc_p = _gdn(qkv, z_b, bg, bgt_p, conv_w[0], gd_onorm[0], zeros_g, zeros_c,
                            row0=0, seq_len=lp, rows_per_step=cg_g, chunk=chunk_p, seqs_per_step=1)
    ob_s, sg_s, sc_s = _gdn(qkv, z_b, bg, bgt_s, conv_w[0], gd_onorm[0], state_gdn[0],
                            state_conv[0], row0=rp, seq_len=ls, rows_per_step=cg_s, chunk=min(GDN_CHUNK, ls),
                            seqs_per_step=seqs_s)

    merged = _merge(xn, oa_p, oa_s, ob_p, ob_s, w_gates_bf, w_pa[0], w_pb[0])

    nr = LANES
    w_router = jnp.concatenate([w_router_group[0], w_router_expert[0]], axis=1)
    w_router = jnp.pad(w_router, ((0, 0), (0, nr - ngroups - nexp))).astype(BF)
    b_router = jnp.pad(jnp.concatenate([b_router_group[0], b_router_expert[0]]), (0, nr - ngroups - nexp))
    h, xf, logits = _outproj(merged, x_prompt.reshape(rp, d), x_sample.reshape(rs, d), w_out[0],
                             norm_ffn[0], w_router)

    eid, gate, rank, cnt = _route(logits, b_router.reshape(1, nr), ngroups, nexp)
    nblocks = -(-(t * TOP_K) // MOE_BLOCK) + nexp
    pos, block_meta = _plan(cnt, eid, rank, ngroups, nexp, nblocks)
    block_e, block_next, block_valid = (block_meta[:nblocks, j] for j in range(3))

    pos_flat = pos.reshape(-1)
    row_tok = _invert(pos_flat, nblocks * MOE_BLOCK)
    yb = _experts(block_e, block_next, block_valid, row_tok, xf, w_exp_gate[0], w_exp_up[0], w_exp_down[0])
    y_p, y_s = _final(pos_flat, h, gate, yb, norm_final, rp, rs)
    return (y_p.reshape(bp, lp, d), y_s.reshape(bs, ls, d),
            sh_p[None], sg_p[None], sc_p[None], sh_s[None], sg_s[None], sc_s[None])
```

```python
import functools
import math

import jax
import jax.numpy as jnp
from jax import lax
from jax.experimental import pallas as pl
from jax.experimental.pallas import tpu as pltpu

F32 = jnp.float32
BF = jnp.bfloat16
I32 = jnp.int32
EPS = 1e-6
CONV_K = 4
TOP_K = 2
MOE_BLOCK = 128
GDN_CHUNK = 64
GDN_ROWS = 256
GDN_SAMPLE_SEQS = 8
HGRN_SUB = 16
HGRN_ROWS = 128
HGRN_SAMPLE_SEQS = 8
LOG2E = 1.4426950408889634
LANES = 128
NEG = -3.0e38
MIB = 1024 * 1024


def _cp(sem, vmem_mib=48):
    return pltpu.CompilerParams(dimension_semantics=sem, vmem_limit_bytes=vmem_mib * MIB)


def _pick(n, pref, mult=8):
    best = None
    for t in range(mult, min(n, pref) + 1, mult):
        if n % t == 0:
            best = t
    return best if best is not None else n


def _sigmoid(x):
    return 1.0 / (1.0 + jnp.exp(-x))


def _silu(x):
    return x * _sigmoid(x)


def _iota(shape, dim):
    return lax.broadcasted_iota(I32, shape, dim)


def _dot(a, b):
    return jnp.dot(a, b, preferred_element_type=F32)


def _dot_nt(a, b):
    return lax.dot_general(a, b, (((1,), (1,)), ((), ())), preferred_element_type=F32)


def _dot_tn(a, b):
    return lax.dot_general(a, b, (((0,), (0,)), ((), ())), preferred_element_type=F32)


def _split(a):
    hi = a.astype(BF)
    lo = (a - hi.astype(F32)).astype(BF)
    return hi, lo


def _split3(a, axis=0):
    hi = a.astype(BF)
    r1 = a - hi.astype(F32)
    mid = r1.astype(BF)
    lo = (r1 - mid.astype(F32)).astype(BF)
    return jnp.concatenate([hi, mid, lo], axis=axis)


def _two_source_specs(rows_a, rows_b, tm, width):
    na, nb = rows_a // tm, rows_b // tm
    spec_a = pl.BlockSpec((tm, width), lambda i: (jnp.minimum(i, na - 1), 0))
    spec_b = pl.BlockSpec((tm, width), lambda i: (jnp.maximum(i - na, 0), 0))
    return na, nb, spec_a, spec_b


def _rmsnorm_kernel(xa_ref, xb_ref, w_ref, o_ref, *, na):
    i = pl.program_id(0)

    def body(x_ref):
        x = x_ref[...]
        ms = jnp.mean(x * x, axis=-1, keepdims=True)
        o_ref[...] = (x * lax.rsqrt(ms + EPS) * w_ref[...]).astype(o_ref.dtype)

    @pl.when(i < na)
    def _():
        body(xa_ref)

    @pl.when(i >= na)
    def _():
        body(xb_ref)


def _rmsnorm_bf16(xa, xb, w):
    d = xa.shape[1]
    tm = _pick(math.gcd(xa.shape[0], xb.shape[0]), 512)
    na, nb, sa, sb = _two_source_specs(xa.shape[0], xb.shape[0], tm, d)
    return pl.pallas_call(
        functools.partial(_rmsnorm_kernel, na=na),
        grid=(na + nb,),
        in_specs=[sa, sb, pl.BlockSpec((1, d), lambda i: (0, 0))],
        out_specs=pl.BlockSpec((tm, d), lambda i: (i, 0)),
        out_shape=jax.ShapeDtypeStruct((xa.shape[0] + xb.shape[0], d), BF),
        compiler_params=_cp(("arbitrary",), 24),
        name="rmsnorm_mix",
    )(xa, xb, w.reshape(1, d))


def _inproj_kernel(x_ref, w_ref, lb_ref, wba_ref, prow_ref, p_ref, kf_ref, bg_ref, bgt_ref,
                   *, jq, jf0, jf1, q_scale, vh):
    j = pl.program_id(0)

    @pl.when(j == 0)
    def _():
        ba = _dot(x_ref[...], wba_ref[...])
        prow = prow_ref[...]
        z = ba + prow[1:2, :]
        softplus = jnp.maximum(z, 0.0) + jnp.log(1.0 + jnp.exp(-jnp.abs(z)))
        bg = jnp.where(_iota(ba.shape, 1) < vh, _sigmoid(ba), prow[0:1, :] * softplus)
        bg_ref[...] = bg
        wide = jnp.concatenate([bg, jnp.zeros((bg.shape[0], LANES - 2 * vh), F32)], axis=1)
        bgt_ref[...] = jnp.transpose(wide)[:2 * vh, :]

    acc = _dot(x_ref[...], w_ref[...])
    is_forget = (j >= jf0) & (j < jf1)

    @pl.when(is_forget)
    def _():
        logf, kf = _forget_epilogue(acc, lb_ref[...])
        p_ref[...] = logf
        kf_ref[...] = kf

    @pl.when(jnp.logical_not(is_forget))
    def _():
        p_ref[...] = acc * jnp.where(j < jq, q_scale, 1.0)


def _inproj(x_bf, w, lb_logits, ncols, kdim, q_scale, w_ba, a_log, dt_bias):
    t, k = x_bf.shape
    vh = a_log.shape[0]
    zeros = jnp.zeros((vh,), F32)
    prow = jnp.stack([jnp.concatenate([zeros, -jnp.exp(a_log)]), jnp.concatenate([zeros, dt_bias])])
    const = lambda j, i: (0, 0)
    tn = _pick(math.gcd(ncols, kdim), 1024, LANES)
    tm = _pick(t, 1024)
    ni = t // tm
    jq, jf0, jf1 = kdim // tn, kdim // tn, 2 * kdim // tn
    nf = jf1 - jf0
    fcol = lambda j: jnp.clip(j - jf0, 0, nf - 1)
    frow = lambda j, i: jnp.where(j < jf0, 0, jnp.where(j >= jf1, ni - 1, i))
    brow = lambda j, i: jnp.where(j == 0, i, ni - 1)
    kern = functools.partial(_inproj_kernel, jq=jq, jf0=jf0, jf1=jf1, q_scale=q_scale, vh=vh)
    return pl.pallas_call(
        kern,
        grid=(ncols // tn, ni),
        in_specs=[pl.BlockSpec((tm, k), lambda j, i: (i, 0)),
                  pl.BlockSpec((k, tn), lambda j, i: (0, j)),
                  pl.BlockSpec((lb_logits.shape[0], tn), lambda j, i: (0, fcol(j))),
                  pl.BlockSpec((k, 2 * vh), const),
                  pl.BlockSpec((2, 2 * vh), const)],
        out_specs=[pl.BlockSpec((tm, tn), lambda j, i: (i, j)),
                   pl.BlockSpec((tm, tn), lambda j, i: (frow(j, i), fcol(j))),
                   pl.BlockSpec((tm, 2 * vh), lambda j, i: (brow(j, i), 0)),
                   pl.BlockSpec((2 * vh, tm), lambda j, i: (0, brow(j, i)))],
        out_shape=[jax.ShapeDtypeStruct((t, ncols), F32), jax.ShapeDtypeStruct((t, kdim), F32),
                   jax.ShapeDtypeStruct((t, 2 * vh), F32), jax.ShapeDtypeStruct((2 * vh, t), F32)],
        compiler_params=_cp(("arbitrary", "arbitrary"), 44),
        name="in_proj",
    )(x_bf, w, lb_logits, w_ba.astype(BF), prow)


def _forget_epilogue(acc, lb_logits):
    m = jnp.max(lb_logits, axis=0, keepdims=True)
    e = jnp.exp(lb_logits - m)
    lb = e[0:1, :] / jnp.sum(e, axis=0, keepdims=True)
    f = lb + (1.0 - lb) * _sigmoid(acc)
    return jnp.log(f), 1.0 - f


def _hgrn2_kernel(q_ref, lf_ref, k_ref, v_ref, g_ref, on_ref, s0_ref, o_ref, so_ref, s_scr, gall_scr,
                  g_scr, k_scr, v_scr, *, heads, dk, dv, sub, nsub, nseq, nchunks, group, unroll):
    c = pl.program_id(1)
    cg = sub * nsub

    @pl.when(c == 0)
    def _():
        s_scr[...] = s0_ref[...]

    rows_all = nseq * cg
    shift = sub.bit_length() - 1
    r_i = _iota((rows_all, 3 * rows_all), 0)
    c_i = _iota((rows_all, 3 * rows_all), 1) & (rows_all - 1)
    same = lax.shift_right_logical(r_i, shift) == lax.shift_right_logical(c_i, shift)
    btril3 = jnp.where((r_i >= c_i) & same, 1.0, 0.0).astype(BF)
    gall_scr[...] = _dot(btril3, _split3(lf_ref[...])) * LOG2E
    onorm = on_ref[...]
    n_aug = 16 - sub % 16
    aug_rhs = jnp.concatenate([jnp.zeros((n_aug, dv), BF), jnp.ones((n_aug, dv), BF)], axis=1)
    zeros_v = jnp.zeros((sub, dv), BF)
    zeros_aug = jnp.zeros((n_aug - 3, dk), F32)

    nh = sub // 8
    row8 = _iota((8, 1), 0)

    def block(it, carry):
        for u in range(unroll):
            sub_chunk(it * unroll + u, g_scr.at[u], k_scr.at[u], v_scr.at[u])
        return carry

    def sub_chunk(sb, g_scr, k_scr, v_scr):
        for q in range(nseq):
            rows = pl.ds(pl.multiple_of(q * cg + sb * sub, sub), sub)
            g_scr[q] = gall_scr[rows, :]
            k_scr[q] = k_ref[rows, :]
            v_scr[q] = v_ref[rows, :]
        problems = [(q, h) for q in range(nseq) for h in range(heads)]
        for g0 in range(0, len(problems), group):
            ps = []
            for (q, h) in problems[g0:g0 + group]:
                r0 = pl.multiple_of(q * cg + sb * sub, sub)
                rows = pl.ds(r0, sub)
                kc = pl.ds(h * dk, dk)
                vc = pl.ds(h * dv, dv)
                ps.append(dict(q=q, h=h, r0=r0, rows=rows, kc=kc, vc=vc,
                               gc=g_scr[q, :, kc],
                               qv=q_ref[rows, kc], k=k_ref[rows, kc], v=v_ref[rows, vc]))
            for p in ps:
                p["s_old"] = s_scr[p["q"], p["h"]]
                p["o_state"] = _dot((p["qv"] * jnp.exp2(p["gc"])).astype(BF), p["s_old"].astype(BF))
                p["o8"] = [jnp.zeros((8, dv), F32) for _ in range(nh)]
                p["q8"] = [p["qv"][8 * i:8 * i + 8] for i in range(nh)]
                p["g8"] = [p["gc"][8 * i:8 * i + 8] for i in range(nh)]
            for j in range(sub):
                for p in ps:
                    kj = k_scr[p["q"], pl.ds(j, 1), p["kc"]]
                    vj = v_scr[p["q"], pl.ds(j, 1), p["vc"]]
                    gj = g_scr[p["q"], pl.ds(j, 1), p["kc"]]
                    for i in range(j // 8, nh):
                        pr = (p["q8"][i] * kj) * jnp.exp2(p["g8"][i] - gj)
                        a_col = jnp.sum(pr, axis=-1, keepdims=True)
                        if i == j // 8:
                            a_col = jnp.where(row8 >= j % 8, a_col, 0.0)
                        p["o8"][i] = p["o8"][i] + a_col * vj
            for p in ps:
                p["o"] = p["o_state"] + (jnp.concatenate(p["o8"], axis=0) if nh > 1 else p["o8"][0])
            for p in ps:
                gend = p["gc"][sub - 1:sub, :]
                dend = jnp.exp2(gend)
                d_hi = dend.astype(BF).astype(F32)
                d_mid = (dend - d_hi).astype(BF).astype(F32)
                d_lo = (dend - d_hi - d_mid).astype(BF).astype(F32)
                kdec = p["k"] * jnp.exp2(gend - p["gc"])
                lhs = jnp.concatenate([kdec, d_hi, d_mid, d_lo, zeros_aug], axis=0).astype(BF)
                rhs = jnp.concatenate([jnp.concatenate([p["v"].astype(BF), zeros_v], axis=1), aug_rhs], axis=0)
                p["kv"] = _dot_tn(lhs, rhs)
            for p in ps:
                s_scr[p["q"], p["h"]] = p["kv"][:, dv:] * p["s_old"] + p["kv"][:, :dv]
            for p in ps:
                o = p["o"]
                ms = jnp.mean(o * o, axis=-1, keepdims=True)
                o_ref[p["rows"], p["vc"]] = (o * lax.rsqrt(ms + EPS) * onorm) * _silu(g_ref[p["rows"], p["vc"]])

    lax.fori_loop(0, nsub // unroll, block, 0)

    @pl.when(c == nchunks - 1)
    def _():
        so_ref[...] = s_scr[...]


def _hgrn2(q, lf, kf, v, gate, onorm, s0, *, row0, seq_len, rows_per_step, sub, seqs_per_step):
    nseq_total, heads, dk, dv = s0.shape
    assert dk == dv, "state decay is applied with a (dk, dv) tile"
    width = heads * dk
    cg = rows_per_step
    nseq = seqs_per_step
    nchunks = seq_len // cg
    assert nseq == 1 or nchunks == 1, "several sequences per step only when a step covers whole sequences"
    rows = nseq * cg
    b0 = row0 // rows
    assert sub & (sub - 1) == 0 and sub % 8 == 0 and rows & (rows - 1) == 0
    unroll = next(u for u in (8, 4, 2, 1) if (cg // sub) % u == 0)
    def row_spec(col_block):
        return pl.BlockSpec((rows, width), lambda b, c: (b0 + b * nchunks + c, col_block))

    st_spec = pl.BlockSpec((nseq, heads, dk, dv), lambda b, c: (b, 0, 0, 0))
    kern = functools.partial(_hgrn2_kernel, heads=heads, dk=dk, dv=dv, sub=sub, nsub=cg // sub, nseq=nseq,
                             nchunks=nchunks, group=8, unroll=unroll)
    operands = (q, lf, kf, v, gate)
    return pl.pallas_call(
        kern,
        grid=(nseq_total // nseq, nchunks),
        in_specs=[row_spec(cb) for _, cb in operands]
                 + [pl.BlockSpec((1, dv), lambda b, c: (0, 0)), st_spec],
        out_specs=[pl.BlockSpec((rows, heads * dv), lambda b, c: (b * nchunks + c, 0)), st_spec],
        out_shape=[jax.ShapeDtypeStruct((nseq_total * seq_len, heads * dv), F32),
                   jax.ShapeDtypeStruct(s0.shape, F32)],
        scratch_shapes=[pltpu.VMEM((nseq, heads, dk, dv), F32), pltpu.VMEM((rows, width), F32),
                        pltpu.VMEM((unroll, nseq, sub, width), F32), pltpu.VMEM((unroll, nseq, sub, width), F32),
                        pltpu.VMEM((unroll, nseq, sub, heads * dv), F32)],
        compiler_params=_cp(("arbitrary", "arbitrary"), 24),
        name="hgrn2_recurrence",
    )(*[a for a, _ in operands], onorm.reshape(1, dv), s0)


def _gdn_kernel(qkv_ref, z_ref, bg_ref, bgt_ref, cw_ref, on_ref, s0_ref, c0_ref,
                o_ref, so_ref, co_ref, s_scr, xx, xs,
                *, kheads, vheads, dk, dv, chunk, nsub, nseq, nchunks, group_kh):
    c = pl.program_id(1)
    cg = chunk * nsub
    kdim = kheads * dk
    rep = vheads // kheads
    tail0 = 8 - (CONV_K - 1)

    @pl.when(c == 0)
    def _():
        s_scr[...] = s0_ref[...]
        xx[:, 0:8, :] = c0_ref[...]

    for q in range(nseq):
        xx[q, 8:8 + cg, :] = qkv_ref[q * cg:(q + 1) * cg, :]
        conv = xx[q, pl.ds(tail0, cg), :] * cw_ref[0:1, :]
        for j in range(1, CONV_K):
            conv = conv + xx[q, pl.ds(tail0 + j, cg), :] * cw_ref[j:j + 1, :]
        xs[q] = _silu(conv)

    @pl.when(c == nchunks - 1)
    def _():
        for q in range(nseq):
            co_ref[q] = xx[q, pl.ds(8 + cg - (CONV_K - 1), CONV_K - 1), :]

    for q in range(nseq):
        xx[q, 0:8, :] = xx[q, cg:cg + 8, :]

    wp = chunk
    r_i = _iota((chunk, wp), 0)
    c_i = _iota((chunk, wp), 1)
    causal = r_i >= c_i
    strict = r_i > c_i

    def pad_rows(a):
        if a.shape[0] == wp:
            return a
        if a.dtype == BF and a.shape[0] % 16:
            return pad_rows(a.astype(F32)).astype(BF)
        return jnp.concatenate([a, jnp.zeros((wp - a.shape[0],) + a.shape[1:], a.dtype)], axis=0)

    def split_lhs(a):
        a_hi = a.astype(BF).astype(F32)
        return jnp.concatenate([a_hi, a_hi, a - a_hi], axis=1).astype(BF)

    def split_rhs(b):
        bh, bl = (pad_rows(t) for t in _split(b))
        return jnp.concatenate([bh, bl, bh], axis=0)
    cm = chunk - 1
    tril3 = jnp.where(_iota((chunk, 3 * chunk), 0) >= (_iota((chunk, 3 * chunk), 1) & cm), 1.0, 0.0).astype(BF)
    t_r = _iota((3 * chunk, wp), 0) & cm
    t_c = _iota((3 * chunk, wp), 1)
    triu3 = jnp.where((t_r <= t_c) & (t_c < chunk), 1.0, 0.0).astype(BF)
    onorm = on_ref[...]

    cums = {}

    def cumulative_decay(q, s):
        if (q, s) not in cums:
            r0 = q * cg + s * chunk
            bg = bg_ref[r0:r0 + chunk, :]
            gt = bgt_ref[0, vheads:2 * vheads, r0:r0 + chunk]
            cums[(q, s)] = (bg, _dot(tril3, _split3(bg[:, vheads:2 * vheads], axis=0)),
                            _dot(_split3(gt, axis=1), triu3))
        return cums[(q, s)]

    def run_group(items):
        probs = []
        for (q, s, kh) in items:
            r0 = q * cg + s * chunk
            x0 = s * chunk
            bg, gcol_all, grow_all = cumulative_decay(q, s)
            for kh in (kh,):
                qh = xs[q, x0:x0 + chunk, kh * dk:(kh + 1) * dk]
                kk_ = xs[q, x0:x0 + chunk, kdim + kh * dk:kdim + (kh + 1) * dk]
                qn = qh * lax.rsqrt(jnp.sum(qh * qh, axis=-1, keepdims=True) + EPS) * (dk ** -0.5)
                kn = kk_ * lax.rsqrt(jnp.sum(kk_ * kk_, axis=-1, keepdims=True) + EPS)
                kn_bf = kn.astype(BF)
                kn_pad = pad_rows(kn_bf)
                kk = _dot_nt(kn_bf, kn_pad)
                qk = _dot_nt(qn.astype(BF), kn_pad)
                for r in range(rep):
                    h = kh * rep + r
                    vh_ = xs[q, x0:x0 + chunk, 2 * kdim + h * dv:2 * kdim + (h + 1) * dv]
                    gcol = gcol_all[:, h:h + 1]
                    grow = grow_all[h:h + 1, :]
                    beta = bg[:, h:h + 1]
                    decay = jnp.where(causal, jnp.exp(jnp.minimum(gcol - grow, 0.0)), 0.0)
                    eg = jnp.exp(gcol)
                    gend = gcol[chunk - 1:chunk, :]
                    probs.append(dict(
                        q=q, s=s, h=h, r0=r0,
                        x=jnp.where(strict, -(beta * kk * decay), 0.0),
                        y=jnp.concatenate([vh_ * beta, kn * (beta * eg)], axis=1),
                        a_bf=(qk * decay).astype(BF),
                        qg=qn * eg,
                        kdec_bf=(kn * jnp.exp(gend - gcol)).astype(BF),
                        send=jnp.exp(gend)))

        if chunk <= 16:
            for j in range(chunk - 1):
                for p in probs:
                    p["y"] = p["y"] + p["x"][:, j:j + 1] * p["y"][j:j + 1, :]
        else:
            n_fac = int(math.log2(chunk))
            for k in range(n_fac):
                for p in probs:
                    p["x_lhs"] = split_lhs(p["x"])
                    p["y"] = p["y"] + _dot(p["x_lhs"], split_rhs(p["y"]))
                if k + 1 < n_fac:
                    for p in probs:
                        p["x"] = _dot(p["x_lhs"], split_rhs(p["x"]))

        for s in sorted({p["s"] for p in probs}):
            cur = [p for p in probs if p["s"] == s]
            for p in cur:
                p["s_old"] = s_scr[p["q"], p["h"]]
                lhs = jnp.concatenate([p["y"][:, dv:], p["qg"]], axis=0).astype(BF)
                p["ws"] = _dot(lhs, p["s_old"].astype(BF))
            for p in cur:
                p["u_bf"] = (p["y"][:, :dv] - p["ws"][:chunk]).astype(BF)
            for p in cur:
                s_scr[p["q"], p["h"]] = p["send"] * p["s_old"] + _dot_tn(p["kdec_bf"], p["u_bf"])
            for p in cur:
                o = p["ws"][chunk:] + _dot(p["a_bf"], pad_rows(p["u_bf"]))
                ms = jnp.mean(o * o, axis=-1, keepdims=True)
                zc = slice(p["h"] * dv, (p["h"] + 1) * dv)
                rows = slice(p["r0"], p["r0"] + chunk)
                o_ref[rows, zc] = (o * lax.rsqrt(ms + EPS) * onorm) * _silu(z_ref[rows, zc])

    for g0 in range(0, kheads, group_kh):
        run_group([(q, s, kh) for q in range(nseq) for s in range(nsub)
                   for kh in range(g0, min(g0 + group_kh, kheads))])

    @pl.when(c == nchunks - 1)
    def _():
        so_ref[...] = s_scr[...]


def _gdn(qkv, z, bg, bgt3, conv_w, onorm, s0, conv0, *, row0, seq_len, rows_per_step, chunk, seqs_per_step):
    nseq_total, vheads, dk, dv = s0.shape
    (qkv, qkv_cb), (z, z_cb) = qkv, z
    conv_dim = conv_w.shape[1]
    kheads = (conv_dim - vheads * dv) // (2 * dk)
    cg = rows_per_step
    nseq = seqs_per_step
    nchunks = seq_len // cg
    assert nseq == 1 or nchunks == 1, "several sequences per step only when a step covers whole sequences"
    rows = nseq * cg
    b0 = row0 // rows
    conv0p = jnp.pad(conv0, ((0, 0), (8 - (CONV_K - 1), 0), (0, 0)))
    rmap = lambda b, c: (b0 + b * nchunks + c, 0)
    st_spec = pl.BlockSpec((nseq, vheads, dk, dv), lambda b, c: (b, 0, 0, 0))
    kern = functools.partial(_gdn_kernel, kheads=kheads, vheads=vheads, dk=dk, dv=dv, chunk=chunk,
                             nsub=cg // chunk, nseq=nseq, nchunks=nchunks, group_kh=kheads)
    return pl.pallas_call(
        kern,
        grid=(nseq_total // nseq, nchunks),
        in_specs=[pl.BlockSpec((rows, conv_dim), lambda b, c: (b0 + b * nchunks + c, qkv_cb)),
                  pl.BlockSpec((rows, vheads * dv), lambda b, c: (b0 + b * nchunks + c, z_cb)),
                  pl.BlockSpec((rows, 2 * vheads), rmap),
                  pl.BlockSpec((1, 2 * vheads, rows), lambda b, c: (b * nchunks + c, 0, 0)),
                  pl.BlockSpec((CONV_K, conv_dim), lambda b, c: (0, 0)),
                  pl.BlockSpec((1, dv), lambda b, c: (0, 0)),
                  st_spec,
                  pl.BlockSpec((nseq, 8, conv_dim), lambda b, c: (b, 0, 0))],
        out_specs=[pl.BlockSpec((rows, vheads * dv), lambda b, c: (b * nchunks + c, 0)),
                   st_spec,
                   pl.BlockSpec((nseq, CONV_K - 1, conv_dim), lambda b, c: (b, 0, 0))],
        out_shape=[jax.ShapeDtypeStruct((nseq_total * seq_len, vheads * dv), F32),
                   jax.ShapeDtypeStruct(s0.shape, F32),
                   jax.ShapeDtypeStruct((nseq_total, CONV_K - 1, conv_dim), F32)],
        scratch_shapes=[pltpu.VMEM((nseq, vheads, dk, dv), F32), pltpu.VMEM((nseq, 8 + cg, conv_dim), F32),
                        pltpu.VMEM((nseq, cg, conv_dim), F32)],
        compiler_params=_cp(("arbitrary", "arbitrary"), 28),
        name="gdn_recurrence",
    )(qkv, z, bg, bgt3, conv_w, onorm.reshape(1, dv), s0, conv0p)


def _merge_kernel(xn_ref, oa1_ref, oa2_ref, ob1_ref, ob2_ref, wga_ref, wgb_ref, wa_ref, wb_ref, o_ref,
                  oa_bf, ob_bf, *, na):
    i = pl.program_id(0)
    j = pl.program_id(1)

    @pl.when((j == 0) & (i < na))
    def _():
        oa_bf[...] = oa1_ref[...].astype(BF)
        ob_bf[...] = ob1_ref[...].astype(BF)

    @pl.when((j == 0) & (i >= na))
    def _():
        oa_bf[...] = oa2_ref[...].astype(BF)
        ob_bf[...] = ob2_ref[...].astype(BF)

    xn = xn_ref[...]
    ya = _dot(oa_bf[...], wa_ref[...])
    yb = _dot(ob_bf[...], wb_ref[...])
    ga = _dot(xn, wga_ref[...])
    gb = _dot(xn, wgb_ref[...])
    o_ref[...] = (_sigmoid(ga) * ya + _sigmoid(gb) * yb).astype(o_ref.dtype)


def _merge(xn, oa_p, oa_s, ob_p, ob_s, w_gates_bf, w_pa, w_pb):
    rp, rs = oa_p.shape[0], oa_s.shape[0]
    ka, kb = oa_p.shape[1], ob_p.shape[1]
    d = w_pa.shape[1]
    k = xn.shape[1]
    tm = _pick(math.gcd(rp, rs), 512)
    tn = _pick(d, 512, LANES)
    na, nb = rp // tm, rs // tm
    nj = d // tn
    amap = lambda i, j: (jnp.minimum(i, na - 1), 0)
    bmap = lambda i, j: (jnp.maximum(i - na, 0), 0)
    return pl.pallas_call(
        functools.partial(_merge_kernel, na=na),
        grid=(na + nb, nj),
        in_specs=[pl.BlockSpec((tm, k), lambda i, j: (i, 0)),
                  pl.BlockSpec((tm, ka), amap), pl.BlockSpec((tm, ka), bmap),
                  pl.BlockSpec((tm, kb), amap), pl.BlockSpec((tm, kb), bmap),
                  pl.BlockSpec((k, tn), lambda i, j: (0, j)),
                  pl.BlockSpec((k, tn), lambda i, j: (0, nj + j)),
                  pl.BlockSpec((ka, tn), lambda i, j: (0, j)),
                  pl.BlockSpec((kb, tn), lambda i, j: (0, j))],
        out_specs=pl.BlockSpec((tm, tn), lambda i, j: (i, j)),
        out_shape=jax.ShapeDtypeStruct((rp + rs, d), BF),
        scratch_shapes=[pltpu.VMEM((tm, ka), BF), pltpu.VMEM((tm, kb), BF)],
        compiler_params=_cp(("arbitrary", "arbitrary"), 40),
        name="gates_branch_proj_merge",
    )(xn, oa_p, oa_s, ob_p, ob_s, w_gates_bf, w_gates_bf, w_pa.astype(BF), w_pb.astype(BF))


def _outproj_kernel(m_ref, xa_ref, xb_ref, wo_ref, nw_ref, wr_ref, h_ref, xf_ref, lg_ref, *, na):
    i = pl.program_id(0)

    def body(x_ref):
        h = x_ref[...] + _dot(m_ref[...], wo_ref[...])
        h_ref[...] = h
        ms = jnp.mean(h * h, axis=-1, keepdims=True)
        xf = h * lax.rsqrt(ms + EPS) * nw_ref[...]
        xf_ref[...] = xf
        lg_ref[...] = _dot(xf.astype(BF), wr_ref[...])

    @pl.when(i < na)
    def _():
        body(xa_ref)

    @pl.when(i >= na)
    def _():
        body(xb_ref)


def _outproj(merged, xa, xb, w_out, norm_ffn, w_router):
    t, d = merged.shape
    tm = _pick(math.gcd(xa.shape[0], xb.shape[0]), 512)
    na, nb, sa, sb = _two_source_specs(xa.shape[0], xb.shape[0], tm, d)
    nr = w_router.shape[1]
    const = lambda i: (0, 0)
    return pl.pallas_call(
        functools.partial(_outproj_kernel, na=na),
        grid=(na + nb,),
        in_specs=[pl.BlockSpec((tm, d), lambda i: (i, 0)), sa, sb,
                  pl.BlockSpec((d, d), const, pipeline_mode=pl.Buffered(1)),
                  pl.BlockSpec((1, d), const),
                  pl.BlockSpec((d, nr), const, pipeline_mode=pl.Buffered(1))],
        out_specs=[pl.BlockSpec((tm, d), lambda i: (i, 0)),
                   pl.BlockSpec((tm, d), lambda i: (i, 0)),
                   pl.BlockSpec((tm, nr), lambda i: (i, 0))],
        out_shape=[jax.ShapeDtypeStruct((t, d), F32), jax.ShapeDtypeStruct((t, d), F32),
                   jax.ShapeDtypeStruct((t, nr), F32)],
        compiler_params=_cp(("arbitrary",), 48),
        name="out_proj_ffn_norm",
    )(merged, xa, xb, w_out.astype(BF), norm_ffn.reshape(1, d), w_router)


def _route_kernel(lg_ref, b_ref, eid_ref, gate_ref, rank_ref, cnt_ref, carry,
                  *, ngroups, nexp, tm):
    i = pl.program_id(0)

    @pl.when(i == 0)
    def _():
        carry[...] = jnp.zeros_like(carry)

    per_group = nexp // ngroups
    lg = lg_ref[...] + b_ref[...]
    lane = _iota(lg.shape, 1)
    big = jnp.int32(1 << 20)
    is_g = lane < ngroups
    gl = jnp.where(is_g, lg, NEG)
    gmax = jnp.max(gl, axis=-1, keepdims=True)
    gidx = jnp.min(jnp.where(gl == gmax, lane, big), axis=-1, keepdims=True)
    gsum = jnp.sum(jnp.where(is_g, jnp.exp(gl - gmax), 0.0), axis=-1, keepdims=True)
    gw = 1.0 / gsum
    elane = lane - ngroups
    in_grp = (elane >= gidx * per_group) & (elane < (gidx + 1) * per_group)
    el = jnp.where(in_grp, lg, NEG)
    v1 = jnp.max(el, axis=-1, keepdims=True)
    i1 = jnp.min(jnp.where(in_grp & (el == v1), elane, big), axis=-1, keepdims=True)
    in2 = in_grp & (elane != i1)
    el2 = jnp.where(in2, lg, NEG)
    v2 = jnp.max(el2, axis=-1, keepdims=True)
    i2 = jnp.min(jnp.where(in2 & (el2 == v2), elane, big), axis=-1, keepdims=True)
    p2 = jnp.exp(v2 - v1)
    den = 1.0 + p2
    lane2 = _iota((tm, TOP_K), 1)
    eid_ref[...] = jnp.where(lane2 == 0, i1, i2)
    gate_ref[...] = jnp.where(lane2 == 0, gw / den, gw * p2 / den)

    oh1 = (elane == i1).astype(F32)
    oh2 = (elane == i2).astype(F32)
    lower = (_iota((tm, tm), 0) > _iota((tm, tm), 1)).astype(BF)
    cs1 = _dot(lower, oh1.astype(BF))
    cs2 = _dot(lower, oh2.astype(BF))
    tot1 = jnp.sum(oh1, axis=0, keepdims=True)
    tot2 = jnp.sum(oh2, axis=0, keepdims=True)
    base = carry[0:1, :]
    r1 = jnp.sum(oh1 * (base + cs1), axis=-1, keepdims=True)
    r2 = jnp.sum(oh2 * (base + tot1 + cs2), axis=-1, keepdims=True)
    rank_ref[...] = jnp.where(lane2 == 0, r1, r2).astype(I32)
    new = base + tot1 + tot2
    carry[...] = jnp.broadcast_to(new, carry.shape)
    cnt_ref[...] = jnp.broadcast_to(new, cnt_ref.shape)


def _route(logits, bias_row, ngroups, nexp):
    t, nr = logits.shape
    tm = _pick(t, 512)
    kern = functools.partial(_route_kernel, ngroups=ngroups, nexp=nexp, tm=tm)
    return pl.pallas_call(
        kern,
        grid=(t // tm,),
        in_specs=[pl.BlockSpec((tm, nr), lambda i: (i, 0)), pl.BlockSpec((1, nr), lambda i: (0, 0))],
        out_specs=[pl.BlockSpec((tm, TOP_K), lambda i: (i, 0)),
                   pl.BlockSpec((tm, TOP_K), lambda i: (i, 0)),
                   pl.BlockSpec((tm, TOP_K), lambda i: (i, 0)),
                   pl.BlockSpec((8, nr), lambda i: (0, 0))],
        out_shape=[jax.ShapeDtypeStruct((t, TOP_K), I32), jax.ShapeDtypeStruct((t, TOP_K), F32),
                   jax.ShapeDtypeStruct((t, TOP_K), I32), jax.ShapeDtypeStruct((8, nr), F32)],
        scratch_shapes=[pltpu.VMEM((8, nr), F32)],
        compiler_params=_cp(("arbitrary",), 16),
        name="route_topk_rank",
    )(logits, bias_row)


def _plan_kernel(cnt_ref, eid_ref, rank_ref, pos_ref, be_ref, *, ngroups, nexp, nblocks_pad, tm):
    nr = cnt_ref.shape[1]
    cnt = cnt_ref[0:1, :]
    padded = jnp.floor((cnt + (MOE_BLOCK - 1)) * (1.0 / MOE_BLOCK)) * MOE_BLOCK
    r_i = _iota((nr, nr), 0)
    c_i = _iota((nr, nr), 1)
    padded_col = jnp.sum(jnp.where(r_i == c_i, jnp.broadcast_to(padded, (nr, nr)), 0.0), axis=1, keepdims=True)
    start = jnp.sum(jnp.where(r_i < c_i, jnp.broadcast_to(padded_col, (nr, nr)), 0.0), axis=0, keepdims=True)
    end = start + padded
    lane = _iota((tm, nr), 1)
    eid = eid_ref[...]
    rank = rank_ref[...]
    lane2 = _iota((tm, TOP_K), 1)
    pos = jnp.zeros((tm, TOP_K), F32)
    for k in range(TOP_K):
        oh = (lane - ngroups) == eid[:, k:k + 1]
        st = jnp.sum(jnp.where(oh, start, 0.0), axis=-1, keepdims=True)
        pos = jnp.where(lane2 == k, st, pos)
    pos_ref[...] = pos.astype(I32) + rank

    @pl.when(pl.program_id(0) == 0)
    def _():
        blk_row = _iota((nblocks_pad, nr), 0).astype(F32) * MOE_BLOCK
        lane_b = _iota((nblocks_pad, nr), 1)
        is_e = (lane_b >= ngroups) & (lane_b < ngroups + nexp)
        n_le = jnp.sum(jnp.where(is_e & (end <= blk_row), 1.0, 0.0), axis=-1, keepdims=True)
        e_lane = (lane_b - ngroups).astype(F32)
        nonempty = is_e & (padded > 0.0)
        e_last = jnp.max(jnp.where(nonempty, e_lane, -1.0), axis=-1, keepdims=True)
        be = jnp.minimum(n_le, e_last)
        nxt = jnp.min(jnp.where(nonempty & (e_lane > be), e_lane, 1e9), axis=-1, keepdims=True)
        nxt = jnp.where(nxt > 1e8, -1.0, nxt)
        total = jnp.sum(padded, axis=-1, keepdims=True)
        valid = jnp.where(blk_row[:, 0:1] < total, 1.0, 0.0)
        col = _iota((nblocks_pad, 4), 1)
        meta = jnp.where(col == 0, be, jnp.where(col == 1, nxt, jnp.where(col == 2, valid, 0.0)))
        be_ref[...] = meta.astype(I32)


def _plan(cnt, eid, rank, ngroups, nexp, nblocks):
    t = eid.shape[0]
    nr = cnt.shape[1]
    tm = _pick(t, 512)
    nblocks_pad = -(-nblocks // 8) * 8
    kern = functools.partial(_plan_kernel, ngroups=ngroups, nexp=nexp, nblocks_pad=nblocks_pad, tm=tm)
    return pl.pallas_call(
        kern,
        grid=(t // tm,),
        in_specs=[pl.BlockSpec((8, nr), lambda i: (0, 0)),
                  pl.BlockSpec((tm, TOP_K), lambda i: (i, 0)),
                  pl.BlockSpec((tm, TOP_K), lambda i: (i, 0))],
        out_specs=[pl.BlockSpec((tm, TOP_K), lambda i: (i, 0)),
                   pl.BlockSpec((nblocks_pad, 4), lambda i: (0, 0))],
        out_shape=[jax.ShapeDtypeStruct((t, TOP_K), I32), jax.ShapeDtypeStruct((nblocks_pad, 4), I32)],
        compiler_params=_cp(("arbitrary",), 16),
        name="route_plan",
    )(cnt, eid, rank)


def _invert_kernel(pos_ref, tok_ref, *, n_assign, n_rows):
    def clear(r, c):
        tok_ref[r] = 0
        return c

    lax.fori_loop(0, n_rows, clear, 0, unroll=16)

    def put(t, c):
        for k in range(TOP_K):
            tok_ref[pos_ref[t * TOP_K + k]] = t
        return c

    lax.fori_loop(0, n_assign // TOP_K, put, 0, unroll=8)


def _invert(pos_flat, n_rows):
    n_assign = pos_flat.shape[0]
    return pl.pallas_call(
        functools.partial(_invert_kernel, n_assign=n_assign, n_rows=n_rows),
        in_specs=[pl.BlockSpec(memory_space=pltpu.SMEM)],
        out_specs=pl.BlockSpec(memory_space=pltpu.SMEM),
        out_shape=jax.ShapeDtypeStruct((n_rows,), I32),
        name="route_invert",
    )(pos_flat)


def _expert_kernel(be_ref, nxt_ref, valid_ref, tok_ref, x_hbm, wg_hbm, wu_hbm, wd_hbm, y_ref,
                   xbuf0, xbuf1, xsem, wg_st, wu_st, wd_st, wsem, wg_bf, wu_bf, wd_bf):
    i = pl.program_id(0)
    n = pl.num_programs(0)
    xbufs = (xbuf0, xbuf1)

    def row_copy(blk, s, r):
        t = tok_ref[blk * MOE_BLOCK + r]
        return pltpu.make_async_copy(x_hbm.at[pl.ds(t, 1), :], xbufs[s].at[pl.ds(r, 1), :], xsem.at[s])

    def block_wait(s):
        pltpu.make_async_copy(x_hbm.at[pl.ds(0, MOE_BLOCK), :], xbufs[s], xsem.at[s]).wait()

    def weight_copies(e):
        return (pltpu.make_async_copy(wg_hbm.at[e], wg_st, wsem.at[0]),
                pltpu.make_async_copy(wu_hbm.at[e], wu_st, wsem.at[1]),
                pltpu.make_async_copy(wd_hbm.at[e], wd_st, wsem.at[2]))

    @pl.when(i == 0)
    def _():
        for r in range(MOE_BLOCK):
            row_copy(0, 0, r).start()
        for cp in weight_copies(be_ref[0]):
            cp.start(priority=1)

    prev = be_ref[jnp.maximum(i - 1, 0)]

    @pl.when((i == 0) | (be_ref[i] != prev))
    def _():
        for cp in weight_copies(be_ref[i]):
            cp.wait()
        wg_bf[...] = wg_st[...].astype(BF)
        wu_bf[...] = wu_st[...].astype(BF)
        wd_bf[...] = wd_st[...].astype(BF)

        @pl.when(nxt_ref[i] >= 0)
        def _():
            for cp in weight_copies(nxt_ref[i]):
                cp.start(priority=1)

    ahead = jnp.minimum(i + 1, n - 1)
    used = valid_ref[i] != 0
    used_ahead = (i + 1 < n) & (valid_ref[ahead] != 0)

    def step(cur, oth):
        @pl.when(used_ahead)
        def _():
            for r in range(MOE_BLOCK):
                row_copy(ahead, oth, r).start()

        block_wait(cur)
        x = xbufs[cur][...].astype(BF)
        hid = (_silu(_dot(x, wg_bf[...])) * _dot(x, wu_bf[...])).astype(BF)
        y_ref[...] = _dot(hid, wd_bf[...])

    @pl.when(used & (i % 2 == 0))
    def _():
        step(0, 1)

    @pl.when(used & (i % 2 == 1))
    def _():
        step(1, 0)

    @pl.when(jnp.logical_not(used))
    def _():
        y_ref[...] = jnp.zeros_like(y_ref)


def _experts(block_e, block_next, block_valid, row_tok, xf, w_eg, w_eu, w_ed):
    nrows = row_tok.shape[0]
    d = w_eg.shape[1]
    nblocks = nrows // MOE_BLOCK
    de = w_eg.shape[2]
    any_spec = pl.BlockSpec(memory_space=pl.ANY)
    grid_spec = pltpu.PrefetchScalarGridSpec(
        num_scalar_prefetch=4,
        grid=(nblocks,),
        in_specs=[any_spec, any_spec, any_spec, any_spec],
        out_specs=pl.BlockSpec((MOE_BLOCK, d), lambda i, *_: (i, 0)),
        scratch_shapes=[pltpu.VMEM((MOE_BLOCK, d), F32), pltpu.VMEM((MOE_BLOCK, d), F32),
                        pltpu.SemaphoreType.DMA((2,)),
                        pltpu.VMEM((d, de), F32), pltpu.VMEM((d, de), F32), pltpu.VMEM((de, d), F32),
                        pltpu.SemaphoreType.DMA((3,)),
                        pltpu.VMEM((d, de), BF), pltpu.VMEM((d, de), BF), pltpu.VMEM((de, d), BF)],
    )
    return pl.pallas_call(
        _expert_kernel,
        grid_spec=grid_spec,
        out_shape=jax.ShapeDtypeStruct((nrows, d), F32),
        compiler_params=_cp(("arbitrary",), 28),
        name="expert_blocks",
    )(block_e, block_next, block_valid, row_tok, xf, w_eg, w_eu, w_ed)


def _final_kernel(pos_ref, h_ref, gate_ref, w_ref, yb_hbm, ya_ref, ys_ref, gbuf, sem, *, na, tm):
    i = pl.program_id(0)
    n = pl.num_programs(0)
    slot = i % 2

    def row_copy(tile, slot_, r, k):
        p = pos_ref[(tile * tm + r) * TOP_K + k]
        return pltpu.make_async_copy(yb_hbm.at[pl.ds(p, 1), :], gbuf.at[slot_, k, pl.ds(r, 1), :], sem.at[slot_])

    def start_tile(tile, slot_):
        for r in range(tm):
            for k in range(TOP_K):
                row_copy(tile, slot_, r, k).start()

    @pl.when(i == 0)
    def _():
        start_tile(0, 0)

    @pl.when(i + 1 < n)
    def _():
        start_tile(i + 1, 1 - slot)

    for r in range(tm):
        for k in range(TOP_K):
            row_copy(i, slot, r, k).wait()

    gate = gate_ref[...]
    y = h_ref[...]
    ffn = gbuf[slot, 0] * gate[:, 0:1]
    for k in range(1, TOP_K):
        ffn = ffn + gbuf[slot, k] * gate[:, k:k + 1]
    y = y + ffn
    ms = jnp.mean(y * y, axis=-1, keepdims=True)
    out = y * lax.rsqrt(ms + EPS) * w_ref[...]

    @pl.when(i < na)
    def _():
        ya_ref[...] = out

    @pl.when(i >= na)
    def _():
        ys_ref[...] = out


def _final(pos_flat, h, gate, yb, w, rows_a, rows_b):
    t, d = h.shape
    tm = _pick(math.gcd(rows_a, rows_b), 128)
    na, nb = rows_a // tm, rows_b // tm
    grid_spec = pltpu.PrefetchScalarGridSpec(
        num_scalar_prefetch=1,
        grid=(na + nb,),
        in_specs=[pl.BlockSpec((tm, d), lambda i, pos: (i, 0)),
                  pl.BlockSpec((tm, TOP_K), lambda i, pos: (i, 0)),
                  pl.BlockSpec((1, d), lambda i, pos: (0, 0)),
                  pl.BlockSpec(memory_space=pl.ANY)],
        out_specs=[pl.BlockSpec((tm, d), lambda i, pos: (jnp.minimum(i, na - 1), 0)),
                   pl.BlockSpec((tm, d), lambda i, pos: (jnp.maximum(i - na, 0), 0))],
        scratch_shapes=[pltpu.VMEM((2, TOP_K, tm, d), F32), pltpu.SemaphoreType.DMA((2,))],
    )
    return pl.pallas_call(
        functools.partial(_final_kernel, na=na, tm=tm),
        grid_spec=grid_spec,
        out_shape=[jax.ShapeDtypeStruct((rows_a, d), F32), jax.ShapeDtypeStruct((rows_b, d), F32)],
        compiler_params=_cp(("arbitrary",), 16),
        name="combine_final_norm",
    )(pos_flat, h, gate, w.reshape(1, d), yb)


def kernel(x_prompt, x_sample, state_hgrn, state_gdn, state_conv, norm_mix, w_in, conv_w, ha_lb_logits, ha_onorm, w_pa, gd_A_log, gd_dt_bias, gd_onorm, w_pb, w_out, norm_ffn, w_router_group, b_router_group, w_router_expert, b_router_expert, w_exp_gate, w_exp_up, w_exp_down, norm_final):
    depth = state_hgrn.shape[0]
    assert depth == 1, "one decoder layer"
    bp, lp, d = x_prompt.shape
    bs, ls, _ = x_sample.shape
    _, _, ha_heads, ha_dk, ha_dv = state_hgrn.shape
    _, _, vheads, gd_dk, gd_dv = state_gdn.shape
    conv_dim = state_conv.shape[3]
    ha_kdim, ha_vdim = ha_heads * ha_dk, ha_heads * ha_dv
    gd_vdim = vheads * gd_dv
    ngroups = w_router_group.shape[2]
    nexp = w_router_expert.shape[2]
    rp, rs = bp * lp, bs * ls
    t = rp + rs

    off_f = ha_kdim
    off_i = off_f + ha_kdim
    off_g = off_i + ha_vdim
    off_qkv = off_g + ha_vdim
    off_z = off_qkv + conv_dim
    off_b = off_z + gd_vdim
    off_gate = off_b + 2 * vheads

    xn = _rmsnorm_bf16(x_prompt.reshape(rp, d), x_sample.reshape(rs, d), norm_mix[0])

    assert ha_kdim == ha_vdim and off_qkv % conv_dim == 0 and off_z % gd_vdim == 0
    w_bf = w_in[0].astype(BF)
    proj, k_a, bg, bgt = _inproj(xn, w_bf, ha_lb_logits, off_b, ha_kdim, ha_dk ** -0.5,
                                 w_bf[:, off_b:off_gate], gd_A_log[0], gd_dt_bias[0])
    q_a, logf, v_a, g_a = ((proj, cb) for cb in (0, off_f // ha_kdim, off_i // ha_kdim, off_g // ha_kdim))
    k_a = (k_a, 0)
    qkv = (proj, off_qkv // conv_dim)
    z_b = (proj, off_z // gd_vdim)
    w_gates_bf = w_bf[:, off_gate:]

    cg_p = _pick(lp, HGRN_ROWS, GDN_CHUNK) if lp >= GDN_CHUNK else lp
    sub_p = min(HGRN_SUB, lp)
    chunk_p = min(GDN_CHUNK, lp)
    cg_s = ls
    zeros_h = jnp.zeros((bp, ha_heads, ha_dk, ha_dv), F32)
    zeros_g = jnp.zeros((bp, vheads, gd_dk, gd_dv), F32)
    zeros_c = jnp.zeros((bp, CONV_K - 1, conv_dim), F32)

    oa_p, sh_p = _hgrn2(q_a, logf, k_a, v_a, g_a, ha_onorm[0], zeros_h,
                        row0=0, seq_len=lp, rows_per_step=cg_p, sub=sub_p, seqs_per_step=1)
    oa_s, sh_s = _hgrn2(q_a, logf, k_a, v_a, g_a, ha_onorm[0], state_hgrn[0],
                        row0=rp, seq_len=ls, rows_per_step=cg_s, sub=min(HGRN_SUB, ls),
                        seqs_per_step=math.gcd(_pick(bs, HGRN_SAMPLE_SEQS, 1), max(rp // cg_s, 1)))

    def time_on_lanes(rows0, nrows, cg):
        part = bgt[:, rows0:rows0 + nrows].reshape(2 * vheads, nrows // cg, cg)
        return jnp.transpose(part, (1, 0, 2))

    seqs_s = _pick(bs, GDN_SAMPLE_SEQS, 1)
    cg_g = _pick(lp, GDN_ROWS, GDN_CHUNK) if lp >= GDN_CHUNK else lp
    bgt_p = time_on_lanes(0, rp, cg_g)
    bgt_s = time_on_lanes(rp, rs, seqs_s * cg_s)
    ob_p, sg_p, sc_p = _gdn(qkv, z_b, bg, bgt_p, conv_w[0], gd_onorm[0], zeros_g, zeros_c,
                            row0=0, seq_len=lp, rows_per_step=cg_g, chunk=chunk_p, seqs_per_step=1)
    ob_s, sg_s, sc_s = _gdn(qkv, z_b, bg, bgt_s, conv_w[0], gd_onorm[0], state_gdn[0],
                            state_conv[0], row0=rp, seq_len=ls, rows_per_step=cg_s, chunk=min(GDN_CHUNK, ls),
                            seqs_per_step=seqs_s)

    merged = _merge(xn, oa_p, oa_s, ob_p, ob_s, w_gates_bf, w_pa[0], w_pb[0])

    nr = LANES
    w_router = jnp.concatenate([w_router_group[0], w_router_expert[0]], axis=1)
    w_router = jnp.pad(w_router, ((0, 0), (0, nr - ngroups - nexp))).astype(BF)
    b_router = jnp.pad(jnp.concatenate([b_router_group[0], b_router_expert[0]]), (0, nr - ngroups - nexp))
    h, xf, logits = _outproj(merged, x_prompt.reshape(rp, d), x_sample.reshape(rs, d), w_out[0],
                             norm_ffn[0], w_router)

    eid, gate, rank, cnt = _route(logits, b_router.reshape(1, nr), ngroups, nexp)
    nblocks = -(-(t * TOP_K) // MOE_BLOCK) + nexp
    pos, block_meta = _plan(cnt, eid, rank, ngroups, nexp, nblocks)
    block_e, block_next, block_valid = (block_meta[:nblocks, j] for j in range(3))

    pos_flat = pos.reshape(-1)
    row_tok = _invert(pos_flat, nblocks * MOE_BLOCK)
    yb = _experts(block_e, block_next, block_valid, row_tok, xf, w_exp_gate[0], w_exp_up[0], w_exp_down[0])
    y_p, y_s = _final(pos_flat, h, gate, yb, norm_final, rp, rs)
    return (y_p.reshape(bp, lp, d), y_s.reshape(bs, ls, d),
            sh_p[None], sg_p[None], sc_p[None], sh_s[None], sg_s[None], sc_s[None])
```
